```python
import math
import jax, jax.numpy as jnp
from jax import lax
import numpy as np

D_MODEL = 1024
BATCH = 8
SEQ = 8192
DEPTH = 4

PLE_DIM = 256
BRANCH_WIDTH = 512
N_BRANCH = 3
SSM_WIDTH = BRANCH_WIDTH
SSM_GROUP = 16
SSM_GROUPS = SSM_WIDTH // SSM_GROUP
SSM_STATE = 64
DT_MIN = 1e-3
DT_MAX = 1e-1
CONV_WIDTH = BRANCH_WIDTH
CONV_TAPS = 3
HEAD_DIM = 64
N_Q_HEADS = BRANCH_WIDTH // HEAD_DIM
N_KV_HEADS = 2
GQA_GROUP = N_Q_HEADS // N_KV_HEADS
ATTN_WIDTH = N_Q_HEADS * HEAD_DIM
KV_WIDTH = N_KV_HEADS * HEAD_DIM
WINDOW = 128
BLOCK = WINDOW
ATTN_SCALE = 1.0 / math.sqrt(HEAD_DIM)
REL_BUCKETS = 32
REL_MAX_DIST = 128
FFN_HIDDEN = -(-8 * D_MODEL // (3 * 256)) * 256
RMS_EPS = 1e-6

IN_SIZES = (SSM_WIDTH, CONV_WIDTH, CONV_WIDTH, CONV_WIDTH, ATTN_WIDTH, KV_WIDTH, KV_WIDTH, N_BRANCH * D_MODEL)
IN_WIDTH = SSM_WIDTH + 3 * CONV_WIDTH + ATTN_WIDTH + 2 * KV_WIDTH + N_BRANCH * D_MODEL

kernel_name = "hybrid_s5_shortconv_swa_gated_trunk"


def rms_norm(x, g):
    xf = x.astype(jnp.float32)
    y = xf * lax.rsqrt(jnp.mean(xf * xf, axis=-1, keepdims=True) + RMS_EPS)
    return (y * g.astype(jnp.float32)).astype(x.dtype)


def split_columns(z):
    offs, acc = [], 0
    for s in IN_SIZES[:-1]:
        acc += s
        offs.append(acc)
    return jnp.split(z, offs, axis=-1)


def t5_bucket(dist):
    exact = REL_BUCKETS // 2
    df = jnp.maximum(dist, 1).astype(jnp.float32)
    large = exact + (jnp.log(df / exact) / math.log(REL_MAX_DIST / exact) * (REL_BUCKETS - exact)).astype(jnp.int32)
    large = jnp.minimum(large, REL_BUCKETS - 1)
    return jnp.where(dist < exact, dist, large)


def band_bias_and_mask(rel_table, n_blocks):
    qi = jnp.arange(BLOCK)[:, None]
    kj = jnp.arange(2 * BLOCK)[None, :]
    dist = qi + BLOCK - kj
    band = (dist >= 0) & (dist < WINDOW)
    bucket = t5_bucket(jnp.clip(dist, 0, REL_MAX_DIST - 1))
    bias = jnp.transpose(rel_table[bucket], (2, 0, 1)).astype(jnp.float32)
    blk = jnp.arange(n_blocks)[:, None, None]
    valid = band[None] & ((blk > 0) | (kj[None] >= BLOCK))
    return bias, valid


def s5_ssm(u, lam_re, lam_im, b_re, b_im, c_re, c_im, d_skip, log_dt, w_glu):
    bsz, seq, _ = u.shape
    ug = u.reshape(bsz, seq, SSM_GROUPS, SSM_GROUP)
    dt = jnp.exp(log_dt)[:, None]
    mag = jnp.exp(lam_re * dt)
    ang = lam_im * dt
    a_re = mag * jnp.cos(ang)
    a_im = mag * jnp.sin(ang)
    den = lam_re * lam_re + lam_im * lam_im
    nr = a_re - 1.0
    coef_re = (nr * lam_re + a_im * lam_im) / den
    coef_im = (a_im * lam_re - nr * lam_im) / den
    bb_re = coef_re[..., None] * b_re - coef_im[..., None] * b_im
    bb_im = coef_re[..., None] * b_im + coef_im[..., None] * b_re
    bu_re = jnp.einsum('bsgp,gnp->bsgn', ug, bb_re)
    bu_im = jnp.einsum('bsgp,gnp->bsgn', ug, bb_im)
    a_re_t = jnp.broadcast_to(a_re[None, None], (1, seq, SSM_GROUPS, SSM_STATE))
    a_im_t = jnp.broadcast_to(a_im[None, None], (1, seq, SSM_GROUPS, SSM_STATE))

    def combine(left, right):
        a1r, a1i, b1r, b1i = left
        a2r, a2i, b2r, b2i = right
        return (a2r * a1r - a2i * a1i,
                a2r * a1i + a2i * a1r,
                a2r * b1r - a2i * b1i + b2r,
                a2r * b1i + a2i * b1r + b2i)

    _, _, h_re, h_im = lax.associative_scan(combine, (a_re_t, a_im_t, bu_re, bu_im), axis=1)
    y = jnp.einsum('gpn,bsgn->bsgp', c_re, h_re) - jnp.einsum('gpn,bsgn->bsgp', c_im, h_im)
    y = y.reshape(bsz, seq, SSM_WIDTH) + d_skip * u
    y = jax.nn.gelu(y)
    return y * jax.nn.sigmoid(y @ w_glu)


def short_conv(b_gate, c_gate, xc, conv_w):
    v = c_gate * xc
    vp = jnp.pad(v, ((0, 0), (CONV_TAPS - 1, 0), (0, 0)))
    seq = v.shape[1]
    y = conv_w[0] * vp[:, 0:seq] + conv_w[1] * vp[:, 1:seq + 1] + conv_w[2] * vp[:, 2:seq + 2]
    return b_gate * y


def swa_attention(q, k, v, sinks, bias, valid):
    bsz, seq, _ = q.shape
    nb = seq // BLOCK
    qb = q.reshape(bsz, nb, BLOCK, N_KV_HEADS, GQA_GROUP, HEAD_DIM)

    def with_prev(t):
        tb = t.reshape(bsz, nb, BLOCK, N_KV_HEADS, HEAD_DIM)
        prev = jnp.pad(tb, ((0, 0), (1, 0), (0, 0), (0, 0), (0, 0)))[:, :-1]
        return jnp.concatenate([prev, tb], axis=2)

    kb = with_prev(k)
    vb = with_prev(v)
    s = jnp.einsum('bnqhgd,bnkhd->bnhgqk', qb, kb).astype(jnp.float32) * ATTN_SCALE
    s = s + bias.reshape(N_KV_HEADS, GQA_GROUP, BLOCK, 2 * BLOCK)
    s = jnp.where(valid[None, :, None, None], s, -jnp.inf)
    sink = sinks.astype(jnp.float32).reshape(N_KV_HEADS, GQA_GROUP)[None, None, :, :, None, None]
    m = jnp.maximum(jnp.max(s, axis=-1, keepdims=True), sink)
    pexp = jnp.exp(s - m)
    w = pexp / (jnp.sum(pexp, axis=-1, keepdims=True) + jnp.exp(sink - m))
    o = jnp.einsum('bnhgqk,bnkhd->bnqhgd', w.astype(v.dtype), vb)
    return o.reshape(bsz, seq, ATTN_WIDTH)


def _fwd_setup_inputs(seed: int = 0) -> dict:
    key = jax.random.key(seed)
    ks = jax.random.split(key, 26)
    f32 = jnp.float32

    def nrm(k, shape, scale):
        return jax.random.normal(k, shape, f32) * scale

    n_idx = jnp.arange(SSM_STATE, dtype=f32)
    log_dt = jax.random.uniform(ks[10], (DEPTH, SSM_GROUPS), f32, math.log(DT_MIN), math.log(DT_MAX))
    return {
        "x": nrm(ks[0], (BATCH, SEQ, D_MODEL), 1.0),
        "p": nrm(ks[1], (DEPTH, BATCH, SEQ, PLE_DIM), 1.0),
        "rel_bias": nrm(ks[2], (REL_BUCKETS, N_Q_HEADS), 0.1),
        "norm_mix": 1.0 + nrm(ks[3], (DEPTH, D_MODEL), 0.02),
        "w_in": nrm(ks[4], (DEPTH, D_MODEL, IN_WIDTH), D_MODEL ** -0.5),
        "ssm_lambda_re": -0.5 + nrm(ks[5], (DEPTH, SSM_GROUPS, SSM_STATE), 0.01),
        "ssm_lambda_im": jnp.pi * n_idx + nrm(ks[6], (DEPTH, SSM_GROUPS, SSM_STATE), 0.01),
        "ssm_b_re": nrm(ks[7], (DEPTH, SSM_GROUPS, SSM_STATE, SSM_GROUP), (2 * SSM_GROUP) ** -0.5),
        "ssm_b_im": nrm(ks[8], (DEPTH, SSM_GROUPS, SSM_STATE, SSM_GROUP), (2 * SSM_GROUP) ** -0.5),
        "ssm_c_re": nrm(ks[9], (DEPTH, SSM_GROUPS, SSM_GROUP, SSM_STATE), SSM_STATE ** -0.5),
        "ssm_c_im": nrm(ks[11], (DEPTH, SSM_GROUPS, SSM_GROUP, SSM_STATE), SSM_STATE ** -0.5),
        "ssm_d": nrm(ks[12], (DEPTH, SSM_WIDTH), 1.0),
        "ssm_log_dt": log_dt,
        "ssm_w_glu": nrm(ks[13], (DEPTH, SSM_WIDTH, SSM_WIDTH), SSM_WIDTH ** -0.5),
        "conv_w": nrm(ks[14], (DEPTH, CONV_TAPS, CONV_WIDTH), CONV_TAPS ** -0.5),
        "attn_sinks": nrm(ks[15], (DEPTH, N_Q_HEADS), 0.5),
        "w_branch": nrm(ks[16], (DEPTH, N_BRANCH, BRANCH_WIDTH, D_MODEL), BRANCH_WIDTH ** -0.5),
        "w_out": nrm(ks[17], (DEPTH, D_MODEL, D_MODEL), D_MODEL ** -0.5),
        "norm_ffn": 1.0 + nrm(ks[18], (DEPTH, D_MODEL), 0.02),
        "w_ffn_in": nrm(ks[19], (DEPTH, D_MODEL, 2 * FFN_HIDDEN), D_MODEL ** -0.5),
        "w_ffn_out": nrm(ks[20], (DEPTH, FFN_HIDDEN, D_MODEL), FFN_HIDDEN ** -0.5),
        "norm_ple": 1.0 + nrm(ks[21], (DEPTH, D_MODEL), 0.02),
        "w_ple_gate": nrm(ks[22], (DEPTH, D_MODEL, D_MODEL), D_MODEL ** -0.5),
        "w_ple_proj": nrm(ks[23], (DEPTH, PLE_DIM, D_MODEL), PLE_DIM ** -0.5),
        "norm_final": 1.0 + nrm(ks[24], (D_MODEL,), 0.02),
    }


def _fwd_reference(x, p, rel_bias, norm_mix, w_in, ssm_lambda_re, ssm_lambda_im, ssm_b_re, ssm_b_im,
              ssm_c_re, ssm_c_im, ssm_d, ssm_log_dt, ssm_w_glu, conv_w, attn_sinks, w_branch, w_out,
              norm_ffn, w_ffn_in, w_ffn_out, norm_ple, w_ple_gate, w_ple_proj, norm_final):
    seq = x.shape[1]
    bias, valid = band_bias_and_mask(rel_bias, seq // BLOCK)
    for i in range(DEPTH):
        h = rms_norm(x, norm_mix[i])
        z = h @ w_in[i]
        u, cb, cc, cx, q, k, v, gates = split_columns(z)
        y_ssm = s5_ssm(u, ssm_lambda_re[i], ssm_lambda_im[i], ssm_b_re[i], ssm_b_im[i],
                       ssm_c_re[i], ssm_c_im[i], ssm_d[i], ssm_log_dt[i], ssm_w_glu[i])
        y_conv = short_conv(cb, cc, cx, conv_w[i])
        y_attn = swa_attention(q, k, v, attn_sinks[i], bias, valid)
        g = jax.nn.sigmoid(gates)
        merged = (g[..., 0:D_MODEL] * (y_ssm @ w_branch[i, 0])
                  + g[..., D_MODEL:2 * D_MODEL] * (y_conv @ w_branch[i, 1])
                  + g[..., 2 * D_MODEL:3 * D_MODEL] * (y_attn @ w_branch[i, 2]))
        x = x + merged @ w_out[i]
        hf = rms_norm(x, norm_ffn[i]) @ w_ffn_in[i]
        x = x + (jax.nn.silu(hf[..., :FFN_HIDDEN]) * hf[..., FFN_HIDDEN:]) @ w_ffn_out[i]
        pg = jax.nn.sigmoid(rms_norm(x, norm_ple[i]) @ w_ple_gate[i])
        x = x + pg * (p[i] @ w_ple_proj[i])
    return rms_norm(x, norm_final)


import jax as _jax
import jax.numpy as _jnp

TWIN_FORMAT = 'train_step'
FWD_PARAMS = ['x', 'p', 'rel_bias', 'norm_mix', 'w_in', 'ssm_lambda_re', 'ssm_lambda_im', 'ssm_b_re', 'ssm_b_im', 'ssm_c_re', 'ssm_c_im', 'ssm_d', 'ssm_log_dt', 'ssm_w_glu', 'conv_w', 'attn_sinks', 'w_branch', 'w_out', 'norm_ffn', 'w_ffn_in', 'w_ffn_out', 'norm_ple', 'w_ple_gate', 'w_ple_proj', 'norm_final']
TWIN_WEIGHTS = ['rel_bias', 'norm_mix', 'w_in', 'ssm_lambda_re', 'ssm_lambda_im', 'ssm_b_re', 'ssm_b_im', 'ssm_c_re', 'ssm_c_im', 'ssm_d', 'ssm_log_dt', 'ssm_w_glu', 'conv_w', 'attn_sinks', 'w_branch', 'w_out', 'norm_ffn', 'w_ffn_in', 'w_ffn_out', 'norm_ple', 'w_ple_gate', 'w_ple_proj', 'norm_final']
TWIN_DIFF_INPUT = 'x'
TWIN_INPUTS = ['x', 'p', 'rel_bias', 'norm_mix', 'w_in', 'ssm_lambda_re', 'ssm_lambda_im', 'ssm_b_re', 'ssm_b_im', 'ssm_c_re', 'ssm_c_im', 'ssm_d', 'ssm_log_dt', 'ssm_w_glu', 'conv_w', 'attn_sinks', 'w_branch', 'w_out', 'norm_ffn', 'w_ffn_in', 'w_ffn_out', 'norm_ple', 'w_ple_gate', 'w_ple_proj', 'norm_final', 'loss_target', 'm_rel_bias', 'm_norm_mix', 'm_w_in', 'm_ssm_lambda_re', 'm_ssm_lambda_im', 'm_ssm_b_re', 'm_ssm_b_im', 'm_ssm_c_re', 'm_ssm_c_im', 'm_ssm_d', 'm_ssm_log_dt', 'm_ssm_w_glu', 'm_conv_w', 'm_attn_sinks', 'm_w_branch', 'm_w_out', 'm_norm_ffn', 'm_w_ffn_in', 'm_w_ffn_out', 'm_norm_ple', 'm_w_ple_gate', 'm_w_ple_proj', 'm_norm_final', 'v_rel_bias', 'v_norm_mix', 'v_w_in', 'v_ssm_lambda_re', 'v_ssm_lambda_im', 'v_ssm_b_re', 'v_ssm_b_im', 'v_ssm_c_re', 'v_ssm_c_im', 'v_ssm_d', 'v_ssm_log_dt', 'v_ssm_w_glu', 'v_conv_w', 'v_attn_sinks', 'v_w_branch', 'v_w_out', 'v_norm_ffn', 'v_w_ffn_in', 'v_w_ffn_out', 'v_norm_ple', 'v_w_ple_gate', 'v_w_ple_proj', 'v_norm_final']
TWIN_OUTPUTS = ['loss', 'grad_x', 'grad_rel_bias', 'grad_norm_mix', 'grad_w_in', 'grad_ssm_lambda_re', 'grad_ssm_lambda_im', 'grad_ssm_b_re', 'grad_ssm_b_im', 'grad_ssm_c_re', 'grad_ssm_c_im', 'grad_ssm_d', 'grad_ssm_log_dt', 'grad_ssm_w_glu', 'grad_conv_w', 'grad_attn_sinks', 'grad_w_branch', 'grad_w_out', 'grad_norm_ffn', 'grad_w_ffn_in', 'grad_w_ffn_out', 'grad_norm_ple', 'grad_w_ple_gate', 'grad_w_ple_proj', 'grad_norm_final', 'delta_rel_bias', 'delta_norm_mix', 'delta_w_in', 'delta_ssm_lambda_re', 'delta_ssm_lambda_im', 'delta_ssm_b_re', 'delta_ssm_b_im', 'delta_ssm_c_re', 'delta_ssm_c_im', 'delta_ssm_d', 'delta_ssm_log_dt', 'delta_ssm_w_glu', 'delta_conv_w', 'delta_attn_sinks', 'delta_w_branch', 'delta_w_out', 'delta_norm_ffn', 'delta_w_ffn_in', 'delta_w_ffn_out', 'delta_norm_ple', 'delta_w_ple_gate', 'delta_w_ple_proj', 'delta_norm_final', 'new_m_rel_bias', 'new_m_norm_mix', 'new_m_w_in', 'new_m_ssm_lambda_re', 'new_m_ssm_lambda_im', 'new_m_ssm_b_re', 'new_m_ssm_b_im', 'new_m_ssm_c_re', 'new_m_ssm_c_im', 'new_m_ssm_d', 'new_m_ssm_log_dt', 'new_m_ssm_w_glu', 'new_m_conv_w', 'new_m_attn_sinks', 'new_m_w_branch', 'new_m_w_out', 'new_m_norm_ffn', 'new_m_w_ffn_in', 'new_m_w_ffn_out', 'new_m_norm_ple', 'new_m_w_ple_gate', 'new_m_w_ple_proj', 'new_m_norm_final', 'new_v_rel_bias', 'new_v_norm_mix', 'new_v_w_in', 'new_v_ssm_lambda_re', 'new_v_ssm_lambda_im', 'new_v_ssm_b_re', 'new_v_ssm_b_im', 'new_v_ssm_c_re', 'new_v_ssm_c_im', 'new_v_ssm_d', 'new_v_ssm_log_dt', 'new_v_ssm_w_glu', 'new_v_conv_w', 'new_v_attn_sinks', 'new_v_w_branch', 'new_v_w_out', 'new_v_norm_ffn', 'new_v_w_ffn_in', 'new_v_w_ffn_out', 'new_v_norm_ple', 'new_v_w_ple_gate', 'new_v_w_ple_proj', 'new_v_norm_final']
TWIN_LEAF_KINDS = {'loss': 'loss', 'grad_x': 'grad_x', 'grad_rel_bias': 'grad_w', 'grad_norm_mix': 'grad_w', 'grad_w_in': 'grad_w', 'grad_ssm_lambda_re': 'grad_w', 'grad_ssm_lambda_im': 'grad_w', 'grad_ssm_b_re': 'grad_w', 'grad_ssm_b_im': 'grad_w', 'grad_ssm_c_re': 'grad_w', 'grad_ssm_c_im': 'grad_w', 'grad_ssm_d': 'grad_w', 'grad_ssm_log_dt': 'grad_w', 'grad_ssm_w_glu': 'grad_w', 'grad_conv_w': 'grad_w', 'grad_attn_sinks': 'grad_w', 'grad_w_branch': 'grad_w', 'grad_w_out': 'grad_w', 'grad_norm_ffn': 'grad_w', 'grad_w_ffn_in': 'grad_w', 'grad_w_ffn_out': 'grad_w', 'grad_norm_ple': 'grad_w', 'grad_w_ple_gate': 'grad_w', 'grad_w_ple_proj': 'grad_w', 'grad_norm_final': 'grad_w', 'delta_rel_bias': 'delta_w', 'delta_norm_mix': 'delta_w', 'delta_w_in': 'delta_w', 'delta_ssm_lambda_re': 'delta_w', 'delta_ssm_lambda_im': 'delta_w', 'delta_ssm_b_re': 'delta_w', 'delta_ssm_b_im': 'delta_w', 'delta_ssm_c_re': 'delta_w', 'delta_ssm_c_im': 'delta_w', 'delta_ssm_d': 'delta_w', 'delta_ssm_log_dt': 'delta_w', 'delta_ssm_w_glu': 'delta_w', 'delta_conv_w': 'delta_w', 'delta_attn_sinks': 'delta_w', 'delta_w_branch': 'delta_w', 'delta_w_out': 'delta_w', 'delta_norm_ffn': 'delta_w', 'delta_w_ffn_in': 'delta_w', 'delta_w_ffn_out': 'delta_w', 'delta_norm_ple': 'delta_w', 'delta_w_ple_gate': 'delta_w', 'delta_w_ple_proj': 'delta_w', 'delta_norm_final': 'delta_w', 'new_m_rel_bias': 'new_m', 'new_m_norm_mix': 'new_m', 'new_m_w_in': 'new_m', 'new_m_ssm_lambda_re': 'new_m', 'new_m_ssm_lambda_im': 'new_m', 'new_m_ssm_b_re': 'new_m', 'new_m_ssm_b_im': 'new_m', 'new_m_ssm_c_re': 'new_m', 'new_m_ssm_c_im': 'new_m', 'new_m_ssm_d': 'new_m', 'new_m_ssm_log_dt': 'new_m', 'new_m_ssm_w_glu': 'new_m', 'new_m_conv_w': 'new_m', 'new_m_attn_sinks': 'new_m', 'new_m_w_branch': 'new_m', 'new_m_w_out': 'new_m', 'new_m_norm_ffn': 'new_m', 'new_m_w_ffn_in': 'new_m', 'new_m_w_ffn_out': 'new_m', 'new_m_norm_ple': 'new_m', 'new_m_w_ple_gate': 'new_m', 'new_m_w_ple_proj': 'new_m', 'new_m_norm_final': 'new_m', 'new_v_rel_bias': 'new_v', 'new_v_norm_mix': 'new_v', 'new_v_w_in': 'new_v', 'new_v_ssm_lambda_re': 'new_v', 'new_v_ssm_lambda_im': 'new_v', 'new_v_ssm_b_re': 'new_v', 'new_v_ssm_b_im': 'new_v', 'new_v_ssm_c_re': 'new_v', 'new_v_ssm_c_im': 'new_v', 'new_v_ssm_d': 'new_v', 'new_v_ssm_log_dt': 'new_v', 'new_v_ssm_w_glu': 'new_v', 'new_v_conv_w': 'new_v', 'new_v_attn_sinks': 'new_v', 'new_v_w_branch': 'new_v', 'new_v_w_out': 'new_v', 'new_v_norm_ffn': 'new_v', 'new_v_w_ffn_in': 'new_v', 'new_v_w_ffn_out': 'new_v', 'new_v_norm_ple': 'new_v', 'new_v_w_ple_gate': 'new_v', 'new_v_w_ple_proj': 'new_v', 'new_v_norm_final': 'new_v'}


def _forward(args):
    return _fwd_reference(*[args[k] for k in FWD_PARAMS])


def _output_shape():
    def fwd():
        inp = _fwd_setup_inputs(0)
        return _fwd_reference(*[inp[k] for k in FWD_PARAMS])
    out = _jax.eval_shape(fwd)
    return out.shape, out.dtype

N_MICROBATCH = 1
ADAM_LR = 0.001
ADAM_B1 = 0.9
ADAM_B2 = 0.999
ADAM_EPS = 1e-08
ADAM_WD = 0.01
ADAM_STEP = 10
PER_EXAMPLE_BATCH_AXIS = {'x': 0, 'p': 1, 'loss_target': 0}
SHARED_INPUTS = []
_WEIGHT_DTYPES = {'rel_bias': _jnp.float32, 'norm_mix': _jnp.float32, 'w_in': _jnp.float32, 'ssm_lambda_re': _jnp.float32, 'ssm_lambda_im': _jnp.float32, 'ssm_b_re': _jnp.float32, 'ssm_b_im': _jnp.float32, 'ssm_c_re': _jnp.float32, 'ssm_c_im': _jnp.float32, 'ssm_d': _jnp.float32, 'ssm_log_dt': _jnp.float32, 'ssm_w_glu': _jnp.float32, 'conv_w': _jnp.float32, 'attn_sinks': _jnp.float32, 'w_branch': _jnp.float32, 'w_out': _jnp.float32, 'norm_ffn': _jnp.float32, 'w_ffn_in': _jnp.float32, 'w_ffn_out': _jnp.float32, 'norm_ple': _jnp.float32, 'w_ple_gate': _jnp.float32, 'w_ple_proj': _jnp.float32, 'norm_final': _jnp.float32}
MOMENT_SCALE = {'rel_bias': 8.046151e-02, 'norm_mix': 2.120709e-01, 'w_in': 8.771812e-02, 'ssm_lambda_re': 4.295642e-03, 'ssm_lambda_im': 4.785501e-03, 'ssm_b_re': 2.698595e-03, 'ssm_b_im': 2.759668e-03, 'ssm_c_re': 3.912812e-03, 'ssm_c_im': 4.028836e-03, 'ssm_d': 6.169962e-02, 'ssm_log_dt': 3.491034e+00, 'ssm_w_glu': 1.641419e-02, 'conv_w': 1.652885e-01, 'attn_sinks': 2.706126e-02, 'w_branch': 7.073818e-02, 'w_out': 1.225823e-01, 'norm_ffn': 1.556760e-01, 'w_ffn_in': 6.552438e-02, 'w_ffn_out': 1.068518e-01, 'norm_ple': 3.741834e-02, 'w_ple_gate': 3.755565e-02, 'w_ple_proj': 9.613159e-02, 'norm_final': 6.398402e+01}


def _to_microbatches(a, axis):
    t = _jnp.moveaxis(a, axis, 0)
    t = t.reshape((N_MICROBATCH, t.shape[0] // N_MICROBATCH) + t.shape[1:])
    return _jnp.moveaxis(t, 1, axis + 1)


def setup_inputs(seed: int = 0) -> dict:
    inp = _fwd_setup_inputs(seed)
    key = _jax.random.fold_in(_jax.random.key(seed), 7919)
    shape, _ = _output_shape()
    out = dict(inp)
    out["loss_target"] = _jax.random.normal(_jax.random.fold_in(key, 0), shape, _jnp.float32)
    for i, name in enumerate(TWIN_WEIGHTS):
        w = inp[name].astype(_jnp.float32)
        if MOMENT_SCALE is None:
            s = _jnp.sqrt(_jnp.mean(_jnp.square(w)) + 1e-30)
        else:
            s = MOMENT_SCALE[name]
        km, kv = _jax.random.split(_jax.random.fold_in(key, i + 1))
        out[name] = w
        out["m_" + name] = s * _jax.random.normal(km, w.shape, _jnp.float32)
        out["v_" + name] = (s * s) * _jax.random.uniform(kv, w.shape, _jnp.float32, 0.5, 1.5)
    if N_MICROBATCH > 1:
        for name, axis in PER_EXAMPLE_BATCH_AXIS.items():
            out[name] = _to_microbatches(out[name], axis)
    return {'x': out['x'], 'p': out['p'], 'rel_bias': out['rel_bias'], 'norm_mix': out['norm_mix'], 'w_in': out['w_in'], 'ssm_lambda_re': out['ssm_lambda_re'], 'ssm_lambda_im': out['ssm_lambda_im'], 'ssm_b_re': out['ssm_b_re'], 'ssm_b_im': out['ssm_b_im'], 'ssm_c_re': out['ssm_c_re'], 'ssm_c_im': out['ssm_c_im'], 'ssm_d': out['ssm_d'], 'ssm_log_dt': out['ssm_log_dt'], 'ssm_w_glu': out['ssm_w_glu'], 'conv_w': out['conv_w'], 'attn_sinks': out['attn_sinks'], 'w_branch': out['w_branch'], 'w_out': out['w_out'], 'norm_ffn': out['norm_ffn'], 'w_ffn_in': out['w_ffn_in'], 'w_ffn_out': out['w_ffn_out'], 'norm_ple': out['norm_ple'], 'w_ple_gate': out['w_ple_gate'], 'w_ple_proj': out['w_ple_proj'], 'norm_final': out['norm_final'], 'loss_target': out['loss_target'], 'm_rel_bias': out['m_rel_bias'], 'm_norm_mix': out['m_norm_mix'], 'm_w_in': out['m_w_in'], 'm_ssm_lambda_re': out['m_ssm_lambda_re'], 'm_ssm_lambda_im': out['m_ssm_lambda_im'], 'm_ssm_b_re': out['m_ssm_b_re'], 'm_ssm_b_im': out['m_ssm_b_im'], 'm_ssm_c_re': out['m_ssm_c_re'], 'm_ssm_c_im': out['m_ssm_c_im'], 'm_ssm_d': out['m_ssm_d'], 'm_ssm_log_dt': out['m_ssm_log_dt'], 'm_ssm_w_glu': out['m_ssm_w_glu'], 'm_conv_w': out['m_conv_w'], 'm_attn_sinks': out['m_attn_sinks'], 'm_w_branch': out['m_w_branch'], 'm_w_out': out['m_w_out'], 'm_norm_ffn': out['m_norm_ffn'], 'm_w_ffn_in': out['m_w_ffn_in'], 'm_w_ffn_out': out['m_w_ffn_out'], 'm_norm_ple': out['m_norm_ple'], 'm_w_ple_gate': out['m_w_ple_gate'], 'm_w_ple_proj': out['m_w_ple_proj'], 'm_norm_final': out['m_norm_final'], 'v_rel_bias': out['v_rel_bias'], 'v_norm_mix': out['v_norm_mix'], 'v_w_in': out['v_w_in'], 'v_ssm_lambda_re': out['v_ssm_lambda_re'], 'v_ssm_lambda_im': out['v_ssm_lambda_im'], 'v_ssm_b_re': out['v_ssm_b_re'], 'v_ssm_b_im': out['v_ssm_b_im'], 'v_ssm_c_re': out['v_ssm_c_re'], 'v_ssm_c_im': out['v_ssm_c_im'], 'v_ssm_d': out['v_ssm_d'], 'v_ssm_log_dt': out['v_ssm_log_dt'], 'v_ssm_w_glu': out['v_ssm_w_glu'], 'v_conv_w': out['v_conv_w'], 'v_attn_sinks': out['v_attn_sinks'], 'v_w_branch': out['v_w_branch'], 'v_w_out': out['v_w_out'], 'v_norm_ffn': out['v_norm_ffn'], 'v_w_ffn_in': out['v_w_ffn_in'], 'v_w_ffn_out': out['v_w_ffn_out'], 'v_norm_ple': out['v_norm_ple'], 'v_w_ple_gate': out['v_w_ple_gate'], 'v_w_ple_proj': out['v_w_ple_proj'], 'v_norm_final': out['v_norm_final']}


def _loss(weights, diff, rest, loss_target):
    with _jax.named_scope("forward"):
        args = {**rest, TWIN_DIFF_INPUT: diff, **{k: w.astype(_WEIGHT_DTYPES[k]) for k, w in weights.items()}}
        y = _forward(args)
    with _jax.named_scope("loss_head"):
        err = _jnp.square(y.astype(_jnp.float32) - loss_target)
        return 0.5 * _jnp.sum(_jnp.mean(err, axis=-1)) if err.ndim else 0.5 * err


def _adamw(w, g, m, v):
    m = ADAM_B1 * m + (1.0 - ADAM_B1) * g
    v = ADAM_B2 * v + (1.0 - ADAM_B2) * _jnp.square(g)
    m_hat = m / (1.0 - ADAM_B1 ** ADAM_STEP)
    v_hat = v / (1.0 - ADAM_B2 ** ADAM_STEP)
    delta = -ADAM_LR * (m_hat / (_jnp.sqrt(v_hat) + ADAM_EPS) + ADAM_WD * w)
    return delta, m, v


def reference(x, p, rel_bias, norm_mix, w_in, ssm_lambda_re, ssm_lambda_im, ssm_b_re, ssm_b_im, ssm_c_re, ssm_c_im, ssm_d, ssm_log_dt, ssm_w_glu, conv_w, attn_sinks, w_branch, w_out, norm_ffn, w_ffn_in, w_ffn_out, norm_ple, w_ple_gate, w_ple_proj, norm_final, loss_target, m_rel_bias, m_norm_mix, m_w_in, m_ssm_lambda_re, m_ssm_lambda_im, m_ssm_b_re, m_ssm_b_im, m_ssm_c_re, m_ssm_c_im, m_ssm_d, m_ssm_log_dt, m_ssm_w_glu, m_conv_w, m_attn_sinks, m_w_branch, m_w_out, m_norm_ffn, m_w_ffn_in, m_w_ffn_out, m_norm_ple, m_w_ple_gate, m_w_ple_proj, m_norm_final, v_rel_bias, v_norm_mix, v_w_in, v_ssm_lambda_re, v_ssm_lambda_im, v_ssm_b_re, v_ssm_b_im, v_ssm_c_re, v_ssm_c_im, v_ssm_d, v_ssm_log_dt, v_ssm_w_glu, v_conv_w, v_attn_sinks, v_w_branch, v_w_out, v_norm_ffn, v_w_ffn_in, v_w_ffn_out, v_norm_ple, v_w_ple_gate, v_w_ple_proj, v_norm_final):
    given = dict(x=x, p=p, rel_bias=rel_bias, norm_mix=norm_mix, w_in=w_in, ssm_lambda_re=ssm_lambda_re, ssm_lambda_im=ssm_lambda_im, ssm_b_re=ssm_b_re, ssm_b_im=ssm_b_im, ssm_c_re=ssm_c_re, ssm_c_im=ssm_c_im, ssm_d=ssm_d, ssm_log_dt=ssm_log_dt, ssm_w_glu=ssm_w_glu, conv_w=conv_w, attn_sinks=attn_sinks, w_branch=w_branch, w_out=w_out, norm_ffn=norm_ffn, w_ffn_in=w_ffn_in, w_ffn_out=w_ffn_out, norm_ple=norm_ple, w_ple_gate=w_ple_gate, w_ple_proj=w_ple_proj, norm_final=norm_final, loss_target=loss_target, m_rel_bias=m_rel_bias, m_norm_mix=m_norm_mix, m_w_in=m_w_in, m_ssm_lambda_re=m_ssm_lambda_re, m_ssm_lambda_im=m_ssm_lambda_im, m_ssm_b_re=m_ssm_b_re, m_ssm_b_im=m_ssm_b_im, m_ssm_c_re=m_ssm_c_re, m_ssm_c_im=m_ssm_c_im, m_ssm_d=m_ssm_d, m_ssm_log_dt=m_ssm_log_dt, m_ssm_w_glu=m_ssm_w_glu, m_conv_w=m_conv_w, m_attn_sinks=m_attn_sinks, m_w_branch=m_w_branch, m_w_out=m_w_out, m_norm_ffn=m_norm_ffn, m_w_ffn_in=m_w_ffn_in, m_w_ffn_out=m_w_ffn_out, m_norm_ple=m_norm_ple, m_w_ple_gate=m_w_ple_gate, m_w_ple_proj=m_w_ple_proj, m_norm_final=m_norm_final, v_rel_bias=v_rel_bias, v_norm_mix=v_norm_mix, v_w_in=v_w_in, v_ssm_lambda_re=v_ssm_lambda_re, v_ssm_lambda_im=v_ssm_lambda_im, v_ssm_b_re=v_ssm_b_re, v_ssm_b_im=v_ssm_b_im, v_ssm_c_re=v_ssm_c_re, v_ssm_c_im=v_ssm_c_im, v_ssm_d=v_ssm_d, v_ssm_log_dt=v_ssm_log_dt, v_ssm_w_glu=v_ssm_w_glu, v_conv_w=v_conv_w, v_attn_sinks=v_attn_sinks, v_w_branch=v_w_branch, v_w_out=v_w_out, v_norm_ffn=v_norm_ffn, v_w_ffn_in=v_w_ffn_in, v_w_ffn_out=v_w_ffn_out, v_norm_ple=v_norm_ple, v_w_ple_gate=v_w_ple_gate, v_w_ple_proj=v_w_ple_proj, v_norm_final=v_norm_final)
    weights = {n: given[n] for n in TWIN_WEIGHTS}
    shared = {n: given[n] for n in SHARED_INPUTS}
    per_example = {n: given[n] for n in ['x', 'p']}
    grad_fn = _jax.value_and_grad(_loss, argnums=(0, 1))

    def one_microbatch(ex, loss_target):
        ex = dict(ex)
        diff = ex.pop(TWIN_DIFF_INPUT)
        return grad_fn(weights, diff, {**shared, **ex}, loss_target)

    if N_MICROBATCH == 1:
        loss, (grad_w, grad_x) = one_microbatch(per_example, given["loss_target"])
    else:
        def body(carry, xs):
            loss_sum, grad_sum = carry
            l_k, (gw_k, gx_k) = one_microbatch(xs[0], xs[1])
            with _jax.named_scope("update"):
                return (loss_sum + l_k, _jax.tree.map(_jnp.add, grad_sum, gw_k)), gx_k

        init = (_jnp.zeros((), _jnp.float32), _jax.tree.map(_jnp.zeros_like, weights))
        (loss, grad_w), grad_x = _jax.lax.scan(body, init, (per_example, given["loss_target"]))
    with _jax.named_scope("update"):
        delta_w, new_m, new_v = {}, {}, {}
        for n in TWIN_WEIGHTS:
            delta_w[n], new_m[n], new_v[n] = _adamw(weights[n], grad_w[n], given["m_" + n], given["v_" + n])
    return (loss, grad_x, *[grad_w[n] for n in TWIN_WEIGHTS], *[delta_w[n] for n in TWIN_WEIGHTS],
            *[new_m[n] for n in TWIN_WEIGHTS], *[new_v[n] for n in TWIN_WEIGHTS])
```

```python
import functools
import math

import numpy as np
import jax
import jax.numpy as jnp
from jax import lax
from jax.experimental import pallas as pl
from jax.experimental.pallas import tpu as pltpu

F32 = jnp.float32
BF16 = jnp.bfloat16
MESH = pl.DeviceIdType.MESH

D_MODEL = 1024
DEPTH = 4
PLE_DIM = 256
BRANCH = 512
SSM_GROUPS = 32
SSM_GROUP = 16
SSM_STATE = 64
SSM_LANES = SSM_GROUPS * SSM_STATE
SSM_SUB = 4
SUB_IN = BRANCH // SSM_SUB
SUB_ST = SSM_LANES // SSM_SUB
HEAD_DIM = 64
N_Q_HEADS = 8
N_KV_HEADS = 2
GQA_GROUP = 4
KV_WIDTH = 2 * N_KV_HEADS * HEAD_DIM
WINDOW = 128
BLOCK = 128
ATTN_SCALE = 1.0 / math.sqrt(HEAD_DIM)
REL_BUCKETS = 32
REL_MAX_DIST = 128
FFN_HIDDEN = 2816
RMS_EPS = 1e-6
IN_WIDTH = 5888
N_DEV = 8

ADAM_LR = 0.001
ADAM_B1 = 0.9
ADAM_B2 = 0.999
ADAM_EPS = 1e-08
ADAM_WD = 0.01
ADAM_STEP = 10

COL_U = 3072
COL_KV = 5632
NEG = -1e30

SCAN_T = 256
TM = 512
VMEM_LIMIT = 52 * 1024 * 1024


def _cparams(*sem):
    return pltpu.CompilerParams(dimension_semantics=sem, vmem_limit_bytes=VMEM_LIMIT)


def _full(shape):
    n = len(shape)
    return pl.BlockSpec(shape, lambda *_: (0,) * n)


def _pick(n, cands):
    for c in cands:
        if n % c == 0:
            return c
    return n


def _dot(a, b):
    return jnp.dot(a, b, preferred_element_type=F32)


def _dot_tn(a, b):
    return lax.dot_general(a, b, (((0,), (0,)), ((), ())), preferred_element_type=F32)


def _dot_nt(a, b):
    return lax.dot_general(a, b, (((1,), (1,)), ((), ())), preferred_element_type=F32)


def _rms(x, g):
    r = lax.rsqrt(jnp.mean(x * x, axis=-1, keepdims=True) + RMS_EPS)
    return x * r * g


def fused_mm(name, ins, in_specs, prologue, w, *, tn, out_dtype, res=None, extras=()):
    S = ins[0].shape[0]
    K, N = w.shape
    n_in, n_ex = len(ins), len(extras)

    def body(*refs):
        in_refs = refs[:n_in]
        w_ref = refs[n_in]
        pos = n_in + 1
        res_ref = None
        if res is not None:
            res_ref = refs[pos]
            pos += 1
        o_ref = refs[pos]
        ex_refs = refs[pos + 1:pos + 1 + n_ex]
        a_scr = refs[-1]

        @pl.when(pl.program_id(1) == 0)
        def _():
            out = prologue(*[r[...] for r in in_refs])
            a_scr[...] = out[0]
            for r, e in zip(ex_refs, out[1:]):
                r[...] = e.astype(r.dtype)

        acc = _dot(a_scr[...], w_ref[...])
        if res_ref is not None:
            acc = acc + res_ref[...]
        o_ref[...] = acc.astype(o_ref.dtype)

    specs = list(in_specs) + [pl.BlockSpec((K, tn), lambda i, j: (0, j))]
    args = list(ins) + [w]
    if res is not None:
        specs.append(pl.BlockSpec((TM, tn), lambda i, j: (i, j)))
        args.append(res)
    out_shape = [jax.ShapeDtypeStruct((S, N), out_dtype)]
    out_specs = [pl.BlockSpec((TM, tn), lambda i, j: (i, j))]
    for cols, dt in extras:
        out_shape.append(jax.ShapeDtypeStruct((S, cols), dt))
        out_specs.append(pl.BlockSpec((TM, cols), lambda i, j: (i, 0)))
    outs = pl.pallas_call(
        body, name=name, grid=(S // TM, N // tn), in_specs=specs, out_specs=out_specs, out_shape=out_shape,
        scratch_shapes=[pltpu.VMEM((TM, K), BF16)], compiler_params=_cparams("parallel", "arbitrary"),
    )(*args)
    return outs if n_ex else outs[0]


def _row_spec(cols, blk=0):
    return pl.BlockSpec((TM, cols), lambda i, j: (i, blk))


def mm_tn(name, a, b):
    S, K = a.shape
    N = b.shape[1]
    tk = _pick(K, (1024, 1408, 512, 256))
    tn = _pick(N, (1024, 1408, 1536, 512, 256))

    def body(a_ref, b_ref, o_ref):
        @pl.when(pl.program_id(2) == 0)
        def _():
            o_ref[...] = jnp.zeros_like(o_ref)

        o_ref[...] += _dot_tn(a_ref[...].astype(BF16), b_ref[...].astype(BF16))

    return pl.pallas_call(
        body, name=name, grid=(K // tk, N // tn, S // TM),
        in_specs=[pl.BlockSpec((TM, tk), lambda k, n, s: (s, k)), pl.BlockSpec((TM, tn), lambda k, n, s: (s, n))],
        out_specs=pl.BlockSpec((tk, tn), lambda k, n, s: (k, n)),
        out_shape=jax.ShapeDtypeStruct((K, N), F32),
        compiler_params=_cparams("parallel", "parallel", "arbitrary"),
    )(a, b)


def norm_bwd(name, dh, x, g, dres):
    S = x.shape[0]

    def body(dh_ref, x_ref, g_ref, dres_ref, dx_ref, dg_ref):
        @pl.when(pl.program_id(0) == 0)
        def _():
            dg_ref[...] = jnp.zeros_like(dg_ref)

        xv = x_ref[...]
        dhv = dh_ref[...].astype(F32)
        r = lax.rsqrt(jnp.mean(xv * xv, axis=-1, keepdims=True) + RMS_EPS)
        xhat = xv * r
        dxhat = dhv * g_ref[...]
        dx = r * (dxhat - xhat * jnp.mean(dxhat * xhat, axis=-1, keepdims=True))
        dx_ref[...] = dres_ref[...] + dx
        dg_ref[...] += jnp.sum((dhv * xhat).reshape(TM // 8, 8, D_MODEL), axis=0)

    row = pl.BlockSpec((TM, D_MODEL), lambda i: (i, 0))
    return pl.pallas_call(
        body, name=name, grid=(S // TM,), in_specs=[row, row, _full((1, D_MODEL)), row],
        out_specs=[row, _full((8, D_MODEL))],
        out_shape=[jax.ShapeDtypeStruct((S, D_MODEL), F32), jax.ShapeDtypeStruct((8, D_MODEL), F32)],
        compiler_params=_cparams("arbitrary"),
    )(dh, x, g, dres)


def swiglu_bwd(dact, hf):
    S = hf.shape[0]

    def body(da_ref, h1_ref, h2_ref, o1_ref, o2_ref):
        h1 = h1_ref[...].astype(F32)
        h2 = h2_ref[...].astype(F32)
        da = da_ref[...].astype(F32)
        sg = jax.nn.sigmoid(h1)
        o1_ref[...] = (da * h2 * sg * (1.0 + h1 * (1.0 - sg))).astype(BF16)
        o2_ref[...] = (da * h1 * sg).astype(BF16)

    tn = 1408
    nn = FFN_HIDDEN // tn
    return pl.pallas_call(
        body, name="swiglu_bwd", grid=(S // TM, nn),
        in_specs=[pl.BlockSpec((TM, tn), lambda i, j: (i, j)), pl.BlockSpec((TM, tn), lambda i, j: (i, j)),
                  pl.BlockSpec((TM, tn), lambda i, j: (i, nn + j))],
        out_specs=[pl.BlockSpec((TM, tn), lambda i, j: (i, j)), pl.BlockSpec((TM, tn), lambda i, j: (i, j))],
        out_shape=[jax.ShapeDtypeStruct((S, FFN_HIDDEN), BF16)] * 2,
        compiler_params=_cparams("parallel", "parallel"),
    )(dact, hf, hf)


def ple_combine(x2, a_pre, pp):
    S = x2.shape[0]

    def body(x_ref, a_ref, p_ref, o_ref):
        o_ref[...] = x_ref[...] + jax.nn.sigmoid(a_ref[...].astype(F32)) * p_ref[...].astype(F32)

    row = pl.BlockSpec((TM, D_MODEL), lambda i: (i, 0))
    return pl.pallas_call(
        body, name="ple_combine", grid=(S // TM,), in_specs=[row, row, row], out_specs=row,
        out_shape=jax.ShapeDtypeStruct((S, D_MODEL), F32), compiler_params=_cparams("parallel"),
    )(x2, a_pre, pp)


def ple_bwd(dx3, a_pre, pp):
    S = dx3.shape[0]

    def body(dx_ref, a_ref, p_ref, da_ref, dpp_ref):
        dx = dx_ref[...]
        pg = jax.nn.sigmoid(a_ref[...].astype(F32))
        dpp_ref[...] = (dx * pg).astype(BF16)
        da_ref[...] = (dx * p_ref[...].astype(F32) * pg * (1.0 - pg)).astype(BF16)

    row = pl.BlockSpec((TM, D_MODEL), lambda i: (i, 0))
    return pl.pallas_call(
        body, name="ple_bwd", grid=(S // TM,), in_specs=[row, row, row], out_specs=[row, row],
        out_shape=[jax.ShapeDtypeStruct((S, D_MODEL), BF16)] * 2, compiler_params=_cparams("parallel"),
    )(dx3, a_pre, pp)


def _gate_specs(tn, nn):
    return [pl.BlockSpec((TM, tn), functools.partial(lambda i, j, r: (i, r * nn + j), r=r)) for r in range(3)]


def merge_fwd(z, y_ssm, y_conv, y_attn, wb):
    S = z.shape[0]
    tn = 512
    nn = D_MODEL // tn

    def body(g0, g1, g2, y0, y1, y2, w_ref, o_ref):
        acc = jnp.zeros((TM, tn), F32)
        for r, (g_ref, y_ref) in enumerate(((g0, y0), (g1, y1), (g2, y2))):
            acc += jax.nn.sigmoid(g_ref[...].astype(F32)) * _dot(y_ref[...], w_ref[r])
        o_ref[...] = acc.astype(BF16)

    y_spec = pl.BlockSpec((TM, BRANCH), lambda i, j: (i, 0))
    return pl.pallas_call(
        body, name="merge_fwd", grid=(S // TM, nn),
        in_specs=_gate_specs(tn, nn) + [y_spec] * 3 + [pl.BlockSpec((3, BRANCH, tn), lambda i, j: (0, 0, j))],
        out_specs=pl.BlockSpec((TM, tn), lambda i, j: (i, j)),
        out_shape=jax.ShapeDtypeStruct((S, D_MODEL), BF16), compiler_params=_cparams("parallel", "parallel"),
    )(z, z, z, y_ssm, y_conv, y_attn, wb)


def merge_bwd(dmerged, z, y_ssm, y_conv, y_attn, wb):
    S = z.shape[0]
    tn = 512
    nn = D_MODEL // tn

    def body(dm_ref, g0, g1, g2, y0, y1, y2, w_ref, dg0, dg1, dg2, db0, db1, db2):
        dm = dm_ref[...].astype(F32)
        for r, (g_ref, y_ref, dg_ref, db_ref) in enumerate(((g0, y0, dg0, db0), (g1, y1, dg1, db1), (g2, y2, dg2, db2))):
            sg = jax.nn.sigmoid(g_ref[...].astype(F32))
            b = _dot(y_ref[...], w_ref[r])
            dg_ref[...] = (dm * b * sg * (1.0 - sg)).astype(BF16)
            db_ref[...] = (dm * sg).astype(BF16)

    y_spec = pl.BlockSpec((TM, BRANCH), lambda i, j: (i, 0))
    outs = pl.pallas_call(
        body, name="merge_bwd", grid=(S // TM, nn),
        in_specs=[pl.BlockSpec((TM, tn), lambda i, j: (i, j))] + _gate_specs(tn, nn) + [y_spec] * 3
        + [pl.BlockSpec((3, BRANCH, tn), lambda i, j: (0, 0, j))],
        out_specs=[pl.BlockSpec((TM, tn), lambda i, j: (i, j))] * 6,
        out_shape=[jax.ShapeDtypeStruct((S, D_MODEL), BF16)] * 6, compiler_params=_cparams("parallel", "parallel"),
    )(dmerged, z, z, z, y_ssm, y_conv, y_attn, wb)
    return outs[:3], outs[3:]


def _shift_down(v, halo, k):
    rolled = pltpu.roll(v, k, 0)
    h = pltpu.roll(halo, k, 0)
    row = lax.broadcasted_iota(jnp.int32, v.shape, 0)
    head = jnp.concatenate([h, jnp.zeros((v.shape[0] - 8, v.shape[1]), v.dtype)], axis=0)
    return jnp.where(row < k, head, rolled)


def _shift_up(v, halo, k):
    n = v.shape[0]
    rolled = pltpu.roll(v, n - k, 0)
    h = pltpu.roll(halo, 8 - k, 0)
    row = lax.broadcasted_iota(jnp.int32, v.shape, 0)
    tail = jnp.concatenate([jnp.zeros((n - 8, v.shape[1]), v.dtype), h], axis=0)
    return jnp.where(row >= n - k, tail, rolled)


def _conv_specs():
    rb = TM // 8
    c0 = COL_U // BRANCH

    def cur(k):
        return pl.BlockSpec((TM, BRANCH), lambda i: (i, c0 + k))

    def prev(k):
        return pl.BlockSpec((8, BRANCH), lambda i: (jnp.maximum(i * rb - 1, 0), c0 + k))

    return [cur(1), cur(2), cur(3), prev(2), prev(3)]


def conv_fwd(z, conv_w):
    S = z.shape[0]

    def body(cb_ref, cc_ref, cx_ref, pc_ref, px_ref, w_ref, o_ref):
        first = pl.program_id(0) == 0
        v = cc_ref[...].astype(F32) * cx_ref[...].astype(F32)
        pv = jnp.where(first, 0.0, pc_ref[...].astype(F32) * px_ref[...].astype(F32))
        w = w_ref[...]
        y = w[2:3] * v + w[1:2] * _shift_down(v, pv, 1) + w[0:1] * _shift_down(v, pv, 2)
        o_ref[...] = (cb_ref[...].astype(F32) * y).astype(BF16)

    return pl.pallas_call(
        body, name="conv_fwd", grid=(S // TM,), in_specs=_conv_specs() + [_full((3, BRANCH))],
        out_specs=pl.BlockSpec((TM, BRANCH), lambda i: (i, 0)),
        out_shape=jax.ShapeDtypeStruct((S, BRANCH), BF16), compiler_params=_cparams("parallel"),
    )(z, z, z, z, z, conv_w)


def conv_bwd(dy, z, conv_w):
    S = z.shape[0]
    rb = TM // 8
    nt = S // TM
    c0 = COL_U // BRANCH

    def body(dy_ref, cb_ref, cc_ref, cx_ref, pc_ref, px_ref, ndy_ref, ncb_ref, w_ref, o_ref, dw_ref):
        i = pl.program_id(0)

        @pl.when(i == 0)
        def _():
            dw_ref[...] = jnp.zeros_like(dw_ref)

        cb = cb_ref[...].astype(F32)
        cc = cc_ref[...].astype(F32)
        cx = cx_ref[...].astype(F32)
        dyv = dy_ref[...].astype(F32)
        v = cc * cx
        pv = jnp.where(i == 0, 0.0, pc_ref[...].astype(F32) * px_ref[...].astype(F32))
        v1 = _shift_down(v, pv, 1)
        v2 = _shift_down(v, pv, 2)
        w = w_ref[...]
        conv = w[2:3] * v + w[1:2] * v1 + w[0:1] * v2
        dc = dyv * cb
        ndc = jnp.where(i == nt - 1, 0.0, ndy_ref[...].astype(F32) * ncb_ref[...].astype(F32))
        dv = w[2:3] * dc + w[1:2] * _shift_up(dc, ndc, 1) + w[0:1] * _shift_up(dc, ndc, 2)
        o_ref[:, 0:BRANCH] = (dyv * conv).astype(BF16)
        o_ref[:, BRANCH:2 * BRANCH] = (dv * cx).astype(BF16)
        o_ref[:, 2 * BRANCH:3 * BRANCH] = (dv * cc).astype(BF16)
        for k, vk in enumerate((v2, v1, v)):
            dw_ref[8 * k:8 * k + 8, :] += jnp.sum((dc * vk).reshape(rb, 8, BRANCH), axis=0)

    nxt = jnp.minimum

    return pl.pallas_call(
        body, name="conv_bwd", grid=(nt,),
        in_specs=[pl.BlockSpec((TM, BRANCH), lambda i: (i, 0))] + _conv_specs()
        + [pl.BlockSpec((8, BRANCH), lambda i: (nxt((i + 1) * rb, S // 8 - 1), 0)),
           pl.BlockSpec((8, BRANCH), lambda i: (nxt((i + 1) * rb, S // 8 - 1), c0 + 1)),
           _full((3, BRANCH))],
        out_specs=[pl.BlockSpec((TM, 3 * BRANCH), lambda i: (i, 0)), _full((24, BRANCH))],
        out_shape=[jax.ShapeDtypeStruct((S, 3 * BRANCH), BF16), jax.ShapeDtypeStruct((24, BRANCH), F32)],
        compiler_params=_cparams("arbitrary"),
    )(dy, z, z, z, z, z, dy, z, conv_w)


def _bucket_onehot_t():
    qi = np.arange(BLOCK)[:, None]
    kj = np.arange(2 * BLOCK)[None, :]
    dist = np.clip(qi + BLOCK - kj, 0, REL_MAX_DIST - 1)
    exact = REL_BUCKETS // 2
    df = np.maximum(dist, 1).astype(np.float32)
    large = exact + (np.log(df / np.float32(exact)) / np.float32(math.log(REL_MAX_DIST / exact))
                     * np.float32(REL_BUCKETS - exact)).astype(np.int32)
    large = np.minimum(large, REL_BUCKETS - 1)
    bucket = np.where(dist < exact, dist, large).reshape(-1)
    return (np.arange(REL_BUCKETS)[:, None] == bucket[None, :]).astype(np.float32)


def rel_bias_fwd(rel_bias_t):
    n = BLOCK * 2 * BLOCK

    def body(r_ref, oh_ref, o_ref):
        o_ref[...] = jnp.dot(r_ref[...], oh_ref[...], precision=lax.Precision.HIGHEST, preferred_element_type=F32)

    return pl.pallas_call(
        body, name="rel_bias_fwd", grid=(1,), in_specs=[_full((N_Q_HEADS, REL_BUCKETS)), _full((REL_BUCKETS, n))],
        out_specs=_full((N_Q_HEADS, n)), out_shape=jax.ShapeDtypeStruct((N_Q_HEADS, n), F32),
        compiler_params=_cparams("arbitrary"),
    )(rel_bias_t, jnp.asarray(_bucket_onehot_t()))


def rel_bias_bwd(dbias):
    n_l = dbias.shape[0]
    n = BLOCK * 2 * BLOCK

    def body(d_ref, oh_ref, o_ref):
        tot = d_ref[0]
        for l in range(1, n_l):
            tot = tot + d_ref[l]
        o_ref[...] = lax.dot_general(tot, oh_ref[...], (((1,), (1,)), ((), ())), precision=lax.Precision.HIGHEST,
                                     preferred_element_type=F32)

    return pl.pallas_call(
        body, name="rel_bias_bwd", grid=(1,), in_specs=[_full((n_l, N_Q_HEADS, n)), _full((REL_BUCKETS, n))],
        out_specs=_full((N_Q_HEADS, REL_BUCKETS)), out_shape=jax.ShapeDtypeStruct((N_Q_HEADS, REL_BUCKETS), F32),
        compiler_params=_cparams("arbitrary"),
    )(dbias, jnp.asarray(_bucket_onehot_t()))


def _attn_valid(first):
    qi = lax.broadcasted_iota(jnp.int32, (BLOCK, 2 * BLOCK), 0)
    kj = lax.broadcasted_iota(jnp.int32, (BLOCK, 2 * BLOCK), 1)
    dist = qi + BLOCK - kj
    return (dist >= 0) & (dist < WINDOW) & (jnp.logical_not(first) | (kj >= BLOCK))


def _attn_weights(qh, kcat, bias_h, valid, sink):
    s = _dot_nt(qh, kcat) * ATTN_SCALE + bias_h
    s = jnp.where(valid, s, NEG)
    m = jnp.maximum(jnp.max(s, axis=-1, keepdims=True), sink)
    p = jnp.exp(s - m)
    esink = jnp.exp(sink - m)
    inv = 1.0 / (jnp.sum(p, axis=-1, keepdims=True) + esink)
    return p * inv, esink * inv


def _kv_heads(kvp, kvc, hk):
    ks = slice(hk * HEAD_DIM, (hk + 1) * HEAD_DIM)
    vs = slice(KV_WIDTH // 2 + hk * HEAD_DIM, KV_WIDTH // 2 + (hk + 1) * HEAD_DIM)
    return jnp.concatenate([kvp[:, ks], kvc[:, ks]], axis=0), jnp.concatenate([kvp[:, vs], kvc[:, vs]], axis=0)


def _attn_specs():
    cq = (COL_U + 4 * BRANCH) // BRANCH
    ckv = COL_KV // KV_WIDTH
    return [pl.BlockSpec((BLOCK, BRANCH), lambda n: (n, cq)),
            pl.BlockSpec((BLOCK, KV_WIDTH), lambda n: (n, ckv)),
            pl.BlockSpec((BLOCK, KV_WIDTH), lambda n: (jnp.maximum(n - 1, 0), ckv)),
            _full((N_Q_HEADS, BLOCK, 2 * BLOCK)),
            pl.BlockSpec(memory_space=pltpu.SMEM)]


def attn_fwd(z, bias, sinks):
    S = z.shape[0]

    def body(q_ref, kvc_ref, kvp_ref, b_ref, sink_ref, o_ref):
        valid = _attn_valid(pl.program_id(0) == 0)
        q = q_ref[...]
        kvc = kvc_ref[...]
        kvp = kvp_ref[...]
        outs = []
        for hk in range(N_KV_HEADS):
            kcat, vcat = _kv_heads(kvp, kvc, hk)
            for g in range(GQA_GROUP):
                h = hk * GQA_GROUP + g
                w, _ = _attn_weights(q[:, h * HEAD_DIM:(h + 1) * HEAD_DIM], kcat, b_ref[h], valid, sink_ref[h])
                outs.append(_dot(w.astype(BF16), vcat))
        o_ref[...] = jnp.concatenate(outs, axis=1).astype(BF16)

    return pl.pallas_call(
        body, name="attn_fwd", grid=(S // BLOCK,), in_specs=_attn_specs(),
        out_specs=pl.BlockSpec((BLOCK, BRANCH), lambda n: (n, 0)),
        out_shape=jax.ShapeDtypeStruct((S, BRANCH), BF16), compiler_params=_cparams("parallel"),
    )(z, z, z, bias, sinks)


def attn_bwd(do, z, bias, sinks):
    S = z.shape[0]

    def body(do_ref, q_ref, kvc_ref, kvp_ref, b_ref, sink_ref, dq_ref, dc_ref, dp_ref, db_ref, ds_ref):
        first = pl.program_id(0) == 0

        @pl.when(first)
        def _():
            db_ref[...] = jnp.zeros_like(db_ref)
            ds_ref[...] = jnp.zeros_like(ds_ref)

        valid = _attn_valid(first)
        q = q_ref[...]
        kvc = kvc_ref[...]
        kvp = kvp_ref[...]
        dov = do_ref[...]
        dqs, dks, dvs = [], [], []
        for hk in range(N_KV_HEADS):
            kcat, vcat = _kv_heads(kvp, kvc, hk)
            dk = jnp.zeros((2 * BLOCK, HEAD_DIM), F32)
            dv = jnp.zeros((2 * BLOCK, HEAD_DIM), F32)
            for g in range(GQA_GROUP):
                h = hk * GQA_GROUP + g
                qh = q[:, h * HEAD_DIM:(h + 1) * HEAD_DIM]
                doh = dov[:, h * HEAD_DIM:(h + 1) * HEAD_DIM]
                w, wsink = _attn_weights(qh, kcat, b_ref[h], valid, sink_ref[h])
                dv += _dot_tn(w.astype(BF16), doh)
                dw = _dot_nt(doh, vcat)
                delta = jnp.sum(w * dw, axis=-1, keepdims=True)
                ds = w * (dw - delta)
                db_ref[h] += ds
                tot = jnp.sum(-wsink * delta, axis=0, keepdims=True)
                ds_ref[h:h + 1, :] += jnp.broadcast_to(tot, (1, BLOCK))
                dsb = (ds * ATTN_SCALE).astype(BF16)
                dqs.append(_dot(dsb, kcat))
                dk += _dot_tn(dsb, qh)
            dks.append(dk)
            dvs.append(dv)
        dq_ref[...] = jnp.concatenate(dqs, axis=1).astype(BF16)
        both = jnp.concatenate(dks + dvs, axis=1)
        dp_ref[...] = both[:BLOCK]
        dc_ref[...] = both[BLOCK:]

    blk = pl.BlockSpec((BLOCK, BRANCH), lambda n: (n, 0))
    kvb = pl.BlockSpec((BLOCK, KV_WIDTH), lambda n: (n, 0))
    return pl.pallas_call(
        body, name="attn_bwd", grid=(S // BLOCK,), in_specs=[blk] + _attn_specs(),
        out_specs=[blk, kvb, kvb, _full((N_Q_HEADS, BLOCK, 2 * BLOCK)), _full((N_Q_HEADS, BLOCK))],
        out_shape=[jax.ShapeDtypeStruct((S, BRANCH), BF16), jax.ShapeDtypeStruct((S, KV_WIDTH), F32),
                   jax.ShapeDtypeStruct((S, KV_WIDTH), F32), jax.ShapeDtypeStruct((N_Q_HEADS, BLOCK, 2 * BLOCK), F32),
                   jax.ShapeDtypeStruct((N_Q_HEADS, BLOCK), F32)],
        compiler_params=_cparams("arbitrary"),
    )(do, z, z, z, bias, sinks)


def kv_shift_add(dcur, dprev):
    S = dcur.shape[0]
    nb = S // BLOCK

    def body(c_ref, p_ref, o_ref):
        nxt = jnp.where(pl.program_id(0) == nb - 1, 0.0, p_ref[...])
        o_ref[...] = (c_ref[...] + nxt).astype(BF16)

    return pl.pallas_call(
        body, name="kv_shift_add", grid=(nb,),
        in_specs=[pl.BlockSpec((BLOCK, KV_WIDTH), lambda n: (n, 0)),
                  pl.BlockSpec((BLOCK, KV_WIDTH), lambda n: (jnp.minimum(n + 1, nb - 1), 0))],
        out_specs=pl.BlockSpec((BLOCK, KV_WIDTH), lambda n: (n, 0)),
        out_shape=jax.ShapeDtypeStruct((S, KV_WIDTH), BF16), compiler_params=_cparams("parallel"),
    )(dcur, dprev)


def _ssm_disc(lam_re, lam_im, log_dt, bt_re, bt_im):
    dt = jnp.exp(log_dt)
    mag = jnp.exp(lam_re * dt)
    ang = lam_im * dt
    a_re = mag * jnp.cos(ang)
    a_im = mag * jnp.sin(ang)
    den = lam_re * lam_re + lam_im * lam_im
    nr = a_re - 1.0
    coef_re = (nr * lam_re + a_im * lam_im) / den
    coef_im = (a_im * lam_re - nr * lam_im) / den
    bb_re = coef_re[:, None, :] * bt_re - coef_im[:, None, :] * bt_im
    bb_im = coef_re[:, None, :] * bt_im + coef_im[:, None, :] * bt_re
    return a_re, a_im, bb_re, bb_im


_GN = (SSM_GROUPS, SSM_STATE)
_GPN = (SSM_GROUPS, SSM_GROUP, SSM_STATE)


def ssm_disc_fwd(lam_re, lam_im, log_dt, bt_re, bt_im):
    def body(lr_ref, li_ref, dt_ref, br_ref, bi_ref, ar_ref, ai_ref, bbr_ref, bbi_ref):
        a_re, a_im, bb_re, bb_im = _ssm_disc(lr_ref[...], li_ref[...], dt_ref[...], br_ref[...], bi_ref[...])
        ar_ref[...] = a_re
        ai_ref[...] = a_im
        bbr_ref[...] = bb_re
        bbi_ref[...] = bb_im

    return pl.pallas_call(
        body, name="ssm_disc_fwd", grid=(1,),
        in_specs=[_full(_GN), _full(_GN), _full((SSM_GROUPS, 1)), _full(_GPN), _full(_GPN)],
        out_specs=[_full(_GN), _full(_GN), _full(_GPN), _full(_GPN)],
        out_shape=[jax.ShapeDtypeStruct(s, F32) for s in (_GN, _GN, _GPN, _GPN)],
        compiler_params=_cparams("arbitrary"),
    )(lam_re, lam_im, log_dt, bt_re, bt_im)


def ssm_disc_bwd(lam_re, lam_im, log_dt, bt_re, bt_im, da_re, da_im, dbb_re, dbb_im):
    def body(lr_ref, li_ref, dt_ref, br_ref, bi_ref, dar_ref, dai_ref, dbr_ref, dbi_ref, o_lr, o_li, o_dt, o_br, o_bi):
        prim = (lr_ref[...], li_ref[...], dt_ref[...], br_ref[...], bi_ref[...])
        _, vjp = jax.vjp(_ssm_disc, *prim)
        grads = vjp((dar_ref[...], dai_ref[...], dbr_ref[...], dbi_ref[...]))
        for r, v in zip((o_lr, o_li, o_dt, o_br, o_bi), grads):
            r[...] = v

    shapes = (_GN, _GN, (SSM_GROUPS, 1), _GPN, _GPN)
    return pl.pallas_call(
        body, name="ssm_disc_bwd", grid=(1,),
        in_specs=[_full(s) for s in shapes + (_GN, _GN, _GPN, _GPN)],
        out_specs=[_full(s) for s in shapes], out_shape=[jax.ShapeDtypeStruct(s, F32) for s in shapes],
        compiler_params=_cparams("arbitrary"),
    )(lam_re, lam_im, log_dt, bt_re, bt_im, da_re, da_im, dbb_re, dbb_im)


LANE_GROUPS = SSM_LANES // 128
SUB_GROUPS = SUB_ST // 128
_TM_SHAPE = (LANE_GROUPS, 128)


def _step_rows(t):
    return pl.ds(pl.multiple_of(t * LANE_GROUPS, LANE_GROUPS), LANE_GROUPS)


def _group_rows(j):
    return pl.ds(j, SCAN_T, stride=LANE_GROUPS)


def _store_sub(ref, j, val):
    for k in range(SUB_GROUPS):
        ref[_group_rows(j * SUB_GROUPS + k), :] = val[:, k * 128:(k + 1) * 128]


def _load_sub(ref, j):
    return jnp.concatenate([ref[_group_rows(j * SUB_GROUPS + k), :] for k in range(SUB_GROUPS)], axis=1)


_SUB_SHAPE_IN = (SSM_SUB, SUB_IN, SUB_ST)
_SUB_SHAPE_OUT = (SSM_SUB, SUB_ST, SUB_IN)


def ssm_fwd(z, bb_re, bb_im, ct_re, ct_im, a_re, a_im, d_skip, wglu):
    S = z.shape[0]
    cu = COL_U // BRANCH

    def body(u_ref, bbr_ref, bbi_ref, ctr_ref, cti_ref, ar_ref, ai_ref, d_ref, wg_ref,
             y_ref, ypre_ref, hr_ref, hi_ref, bur, bui, car_r, car_i):
        @pl.when(pl.program_id(0) == 0)
        def _():
            car_r[...] = jnp.zeros_like(car_r)
            car_i[...] = jnp.zeros_like(car_i)

        u = u_ref[...]
        for j in range(SSM_SUB):
            uj = u[:, j * SUB_IN:(j + 1) * SUB_IN]
            _store_sub(bur, j, _dot(uj, bbr_ref[j]))
            _store_sub(bui, j, _dot(uj, bbi_ref[j]))
        ar = ar_ref[...]
        ai = ai_ref[...]

        def step(t, carry):
            hr, hi = carry
            rows = _step_rows(t)
            nhr = ar * hr - ai * hi + bur[rows, :]
            nhi = ar * hi + ai * hr + bui[rows, :]
            hr_ref[rows, :] = nhr
            hi_ref[rows, :] = nhi
            return nhr, nhi

        hr, hi = lax.fori_loop(0, SCAN_T, step, (car_r[...], car_i[...]), unroll=8)
        car_r[...] = hr
        car_i[...] = hi
        ys = []
        for j in range(SSM_SUB):
            ys.append(_dot(_load_sub(hr_ref, j).astype(BF16), ctr_ref[j])
                      - _dot(_load_sub(hi_ref, j).astype(BF16), cti_ref[j]))
        ypre = jnp.concatenate(ys, axis=1) + d_ref[...] * u.astype(F32)
        ypre_ref[...] = ypre
        g = jax.nn.gelu(ypre)
        y_ref[...] = (g * jax.nn.sigmoid(_dot(g.astype(BF16), wg_ref[...]))).astype(BF16)

    row = pl.BlockSpec((SCAN_T, BRANCH), lambda i: (i, 0))
    st = pl.BlockSpec((SCAN_T * LANE_GROUPS, 128), lambda i: (i, 0))
    return pl.pallas_call(
        body, name="ssm_fwd", grid=(S // SCAN_T,),
        in_specs=[pl.BlockSpec((SCAN_T, BRANCH), lambda i: (i, cu)), _full(_SUB_SHAPE_IN), _full(_SUB_SHAPE_IN),
                  _full(_SUB_SHAPE_OUT), _full(_SUB_SHAPE_OUT), _full(_TM_SHAPE), _full(_TM_SHAPE), _full((1, BRANCH)),
                  _full((BRANCH, BRANCH))],
        out_specs=[row, row, st, st],
        out_shape=[jax.ShapeDtypeStruct((S, BRANCH), BF16), jax.ShapeDtypeStruct((S, BRANCH), F32),
                   jax.ShapeDtypeStruct((S * LANE_GROUPS, 128), F32), jax.ShapeDtypeStruct((S * LANE_GROUPS, 128), F32)],
        scratch_shapes=[pltpu.VMEM((SCAN_T * LANE_GROUPS, 128), F32), pltpu.VMEM((SCAN_T * LANE_GROUPS, 128), F32),
                        pltpu.VMEM(_TM_SHAPE, F32), pltpu.VMEM(_TM_SHAPE, F32)],
        compiler_params=_cparams("arbitrary"),
    )(z, bb_re, bb_im, ct_re, ct_im, a_re, a_im, d_skip, wglu)


def ssm_bwd(dy, z, ypre, h_re, h_im, bbt_re, bbt_im, c_re, c_im, a_re, a_im, d_skip, wglu, wglu_t):
    S = z.shape[0]
    nt = S // SCAN_T
    cu = COL_U // BRANCH

    def body(dy_ref, u_ref, ypre_ref, hr_ref, hi_ref, hpr_ref, hpi_ref, bbr_ref, bbi_ref, cr_ref, ci_ref, ar_ref, ai_ref,
             d_ref, wg_ref, wgt_ref,
             du_ref, dbbr_ref, dbbi_ref, dctr_ref, dcti_ref, dar_ref, dai_ref, dd_ref, dwg_ref,
             lr_scr, li_scr, car_r, car_i):
        step = pl.program_id(0)

        @pl.when(step == 0)
        def _():
            for r in (dbbr_ref, dbbi_ref, dctr_ref, dcti_ref, dar_ref, dai_ref, dd_ref, dwg_ref, car_r, car_i):
                r[...] = jnp.zeros_like(r)

        u = u_ref[...]
        uf = u.astype(F32)
        dyv = dy_ref[...].astype(F32)
        g, gelu_vjp = jax.vjp(jax.nn.gelu, ypre_ref[...])
        gb = g.astype(BF16)
        sg = jax.nn.sigmoid(_dot(gb, wg_ref[...]))
        dgl = (dyv * g * sg * (1.0 - sg)).astype(BF16)
        dwg_ref[...] += _dot_tn(gb, dgl)
        dg = dyv * sg + _dot(dgl, wgt_ref[...])
        dypre = gelu_vjp(dg)[0]
        dd_ref[...] += jnp.sum(dypre * uf, axis=0, keepdims=True)
        dyb = dypre.astype(BF16)
        for j in range(SSM_SUB):
            dyj = dyb[:, j * SUB_IN:(j + 1) * SUB_IN]
            _store_sub(lr_scr, j, _dot(dyj, cr_ref[j]))
            _store_sub(li_scr, j, -_dot(dyj, ci_ref[j]))
            dctr_ref[j] += _dot_tn(_load_sub(hr_ref, j).astype(BF16), dyj)
            dcti_ref[j] -= _dot_tn(_load_sub(hi_ref, j).astype(BF16), dyj)

        ar = ar_ref[...]
        ai = ai_ref[...]

        def adjoint(lr, li, rows):
            nlr = ar * lr + ai * li + lr_scr[rows, :]
            nli = ar * li - ai * lr + li_scr[rows, :]
            lr_scr[rows, :] = nlr
            li_scr[rows, :] = nli
            return nlr, nli

        def back(k, carry):
            lr, li, acc_r, acc_i = carry
            t = SCAN_T - 1 - k
            lr, li = adjoint(lr, li, _step_rows(t))
            hpr = hr_ref[_step_rows(t - 1), :]
            hpi = hi_ref[_step_rows(t - 1), :]
            return lr, li, acc_r + lr * hpr + li * hpi, acc_i + li * hpr - lr * hpi

        zero = jnp.zeros(_TM_SHAPE, F32)
        lr, li, acc_r, acc_i = lax.fori_loop(0, SCAN_T - 1, back, (car_r[...], car_i[...], zero, zero), unroll=8)
        lr, li = adjoint(lr, li, pl.ds(0, LANE_GROUPS))
        car_r[...] = lr
        car_i[...] = li
        first_tile = step == nt - 1
        hpr = jnp.where(first_tile, 0.0, hpr_ref[...])
        hpi = jnp.where(first_tile, 0.0, hpi_ref[...])
        dar_ref[...] += acc_r + lr * hpr + li * hpi
        dai_ref[...] += acc_i + li * hpr - lr * hpi

        dus = []
        for j in range(SSM_SUB):
            lrb = _load_sub(lr_scr, j).astype(BF16)
            lib = _load_sub(li_scr, j).astype(BF16)
            uj = u[:, j * SUB_IN:(j + 1) * SUB_IN]
            dus.append(_dot(lrb, bbr_ref[j]) + _dot(lib, bbi_ref[j]))
            dbbr_ref[j] += _dot_tn(uj, lrb)
            dbbi_ref[j] += _dot_tn(uj, lib)
        du_ref[...] = (jnp.concatenate(dus, axis=1) + dypre * d_ref[...]).astype(BF16)

    def rev(i):
        return nt - 1 - i

    row = pl.BlockSpec((SCAN_T, BRANCH), lambda i: (rev(i), 0))
    st = pl.BlockSpec((SCAN_T * LANE_GROUPS, 128), lambda i: (rev(i), 0))
    before = pl.BlockSpec(_TM_SHAPE, lambda i: (jnp.maximum(rev(i) * SCAN_T - 1, 0), 0))
    tm = _full(_TM_SHAPE)
    return pl.pallas_call(
        body, name="ssm_bwd", grid=(nt,),
        in_specs=[row, pl.BlockSpec((SCAN_T, BRANCH), lambda i: (rev(i), cu)), row, st, st, before, before,
                  _full(_SUB_SHAPE_OUT), _full(_SUB_SHAPE_OUT), _full(_SUB_SHAPE_IN), _full(_SUB_SHAPE_IN),
                  tm, tm, _full((1, BRANCH)), _full((BRANCH, BRANCH)), _full((BRANCH, BRANCH))],
        out_specs=[row, _full(_SUB_SHAPE_IN), _full(_SUB_SHAPE_IN), _full(_SUB_SHAPE_OUT), _full(_SUB_SHAPE_OUT),
                   tm, tm, _full((1, BRANCH)), _full((BRANCH, BRANCH))],
        out_shape=[jax.ShapeDtypeStruct((S, BRANCH), BF16), jax.ShapeDtypeStruct(_SUB_SHAPE_IN, F32),
                   jax.ShapeDtypeStruct(_SUB_SHAPE_IN, F32), jax.ShapeDtypeStruct(_SUB_SHAPE_OUT, F32),
                   jax.ShapeDtypeStruct(_SUB_SHAPE_OUT, F32), jax.ShapeDtypeStruct(_TM_SHAPE, F32),
                   jax.ShapeDtypeStruct(_TM_SHAPE, F32), jax.ShapeDtypeStruct((1, BRANCH), F32),
                   jax.ShapeDtypeStruct((BRANCH, BRANCH), F32)],
        scratch_shapes=[pltpu.VMEM((SCAN_T * LANE_GROUPS, 128), F32), pltpu.VMEM((SCAN_T * LANE_GROUPS, 128), F32),
                        pltpu.VMEM(_TM_SHAPE, F32), pltpu.VMEM(_TM_SHAPE, F32)],
        compiler_params=_cparams("arbitrary"),
    )(dy, z, ypre, h_re, h_im, h_re, h_im, bbt_re, bbt_im, c_re, c_im, a_re, a_im, d_skip, wglu, wglu_t)


def _blockdiag(x):
    gs = SSM_GROUPS // SSM_SUB
    x = x.reshape(SSM_SUB, gs, SSM_GROUP, SSM_STATE)
    eye = jnp.eye(gs, dtype=x.dtype)
    return (x[:, :, :, None, :] * eye[None, :, None, :, None]).reshape(SSM_SUB, SUB_IN, SUB_ST)


def _blockdiag_extract(x):
    gs = SSM_GROUPS // SSM_SUB
    x = x.reshape(SSM_SUB, gs, SSM_GROUP, gs, SSM_STATE)
    eye = jnp.eye(gs, dtype=x.dtype)
    return jnp.sum(x * eye[None, :, None, :, None], axis=3).reshape(SSM_GROUPS, SSM_GROUP, SSM_STATE)


def loss_head(x, g, target):
    S = x.shape[0]

    def body(x_ref, g_ref, t_ref, dx_ref, loss_ref, dg_ref):
        @pl.when(pl.program_id(0) == 0)
        def _():
            loss_ref[...] = jnp.zeros_like(loss_ref)
            dg_ref[...] = jnp.zeros_like(dg_ref)

        xv = x_ref[...]
        gv = g_ref[...]
        r = lax.rsqrt(jnp.mean(xv * xv, axis=-1, keepdims=True) + RMS_EPS)
        xhat = xv * r
        err = xhat * gv - t_ref[...]
        loss_ref[...] += jnp.sum((err * err).reshape(TM // 8, 8, D_MODEL), axis=0) * (0.5 / D_MODEL)
        dy = err * (1.0 / D_MODEL)
        dxhat = dy * gv
        dx_ref[...] = r * (dxhat - xhat * jnp.mean(dxhat * xhat, axis=-1, keepdims=True))
        dg_ref[...] += jnp.sum((dy * xhat).reshape(TM // 8, 8, D_MODEL), axis=0)

    row = pl.BlockSpec((TM, D_MODEL), lambda i: (i, 0))
    acc = _full((8, D_MODEL))
    return pl.pallas_call(
        body, name="loss_head", grid=(S // TM,), in_specs=[row, _full((1, D_MODEL)), row],
        out_specs=[row, acc, acc],
        out_shape=[jax.ShapeDtypeStruct((S, D_MODEL), F32), jax.ShapeDtypeStruct((8, D_MODEL), F32),
                   jax.ShapeDtypeStruct((8, D_MODEL), F32)],
        compiler_params=_cparams("arbitrary"),
    )(x, g, target)


def _x_spec():
    return pl.BlockSpec((TM, D_MODEL), lambda i, j: (i, 0))


def _g_spec():
    return pl.BlockSpec((1, D_MODEL), lambda i, j: (0, 0))


def _norm_prologue(x, g):
    h = _rms(x, g).astype(BF16)
    return h, h


def _cast_prologue(x):
    return (x.astype(BF16),)


def _swiglu_prologue(h1, h2):
    a = h1.astype(F32)
    act = (a * jax.nn.sigmoid(a) * h2.astype(F32)).astype(BF16)
    return act, act


def _concat_prologue(*pieces):
    return (jnp.concatenate(pieces, axis=1),)


def _ssm_consts(lw):
    a_re, a_im, bbt_re, bbt_im = ssm_disc_fwd(
        lw["ssm_lambda_re"], lw["ssm_lambda_im"], lw["ssm_log_dt"].reshape(SSM_GROUPS, 1), lw["bt_re"], lw["bt_im"])
    bb_re = _blockdiag(bbt_re).astype(BF16)
    bb_im = _blockdiag(bbt_im).astype(BF16)
    c_re = _blockdiag(lw["ssm_c_re"]).astype(BF16)
    c_im = _blockdiag(lw["ssm_c_im"]).astype(BF16)
    return dict(
        a_re=a_re.reshape(_TM_SHAPE), a_im=a_im.reshape(_TM_SHAPE),
        bb_re=bb_re, bb_im=bb_im, bbt_re=jnp.swapaxes(bb_re, 1, 2), bbt_im=jnp.swapaxes(bb_im, 1, 2),
        c_re=c_re, c_im=c_im, ct_re=jnp.swapaxes(c_re, 1, 2), ct_im=jnp.swapaxes(c_im, 1, 2))


def layer_fwd(x, lw, bias):
    z, h = fused_mm("in_proj", [x, lw["norm_mix"]], [_x_spec(), _g_spec()], _norm_prologue, lw["w_in"], tn=2944,
                    out_dtype=BF16, extras=((D_MODEL, BF16),))
    sc = _ssm_consts(lw)
    y_ssm, ypre, h_re, h_im = ssm_fwd(z, sc["bb_re"], sc["bb_im"], sc["ct_re"], sc["ct_im"], sc["a_re"], sc["a_im"],
                                      lw["ssm_d"], lw["ssm_w_glu"])
    y_conv = conv_fwd(z, lw["conv_w"])
    y_attn = attn_fwd(z, bias, lw["attn_sinks"])
    merged = merge_fwd(z, y_ssm, y_conv, y_attn, lw["w_branch"])
    x1 = fused_mm("out_proj", [merged], [_x_spec()], _cast_prologue, lw["w_out"], tn=512, out_dtype=F32, res=x)
    hf, hn1 = fused_mm("ffn_in", [x1, lw["norm_ffn"]], [_x_spec(), _g_spec()], _norm_prologue, lw["w_ffn_in"], tn=2816,
                       out_dtype=BF16, extras=((D_MODEL, BF16),))
    x2, act = fused_mm("ffn_out", [hf, hf], [_row_spec(FFN_HIDDEN, 0), _row_spec(FFN_HIDDEN, 1)], _swiglu_prologue,
                       lw["w_ffn_out"], tn=512, out_dtype=F32, res=x1, extras=((FFN_HIDDEN, BF16),))
    a_pre, hn2 = fused_mm("ple_gate", [x2, lw["norm_ple"]], [_x_spec(), _g_spec()], _norm_prologue, lw["w_ple_gate"],
                          tn=512, out_dtype=BF16, extras=((D_MODEL, BF16),))
    pp = fused_mm("ple_proj", [lw["p"]], [_row_spec(PLE_DIM)], _cast_prologue, lw["w_ple_proj"], tn=512, out_dtype=BF16)
    x3 = ple_combine(x2, a_pre, pp)
    res = dict(x=x, z=z, h=h, y_ssm=y_ssm, ypre=ypre, h_re=h_re, h_im=h_im, y_conv=y_conv, y_attn=y_attn, merged=merged,
               x1=x1, hf=hf, hn1=hn1, act=act, x2=x2, a_pre=a_pre, hn2=hn2, pp=pp)
    return x3, res


def layer_bwd(dx3, lw, res, bias):
    g = {}
    da, dpp = ple_bwd(dx3, res["a_pre"], res["pp"])
    g["w_ple_proj"] = mm_tn("d_w_ple_proj", lw["p"], dpp)
    g["w_ple_gate"] = mm_tn("d_w_ple_gate", res["hn2"], da)
    dhn2 = fused_mm("d_ple_gate", [da], [_x_spec()], _cast_prologue, lw["w_ple_gate_t"], tn=512, out_dtype=BF16)
    dx2, g["norm_ple"] = norm_bwd("ple_norm_bwd", dhn2, res["x2"], lw["norm_ple"], dx3)
    dact = fused_mm("d_ffn_out", [dx2], [_x_spec()], _cast_prologue, lw["w_ffn_out_t"], tn=1408, out_dtype=BF16)
    g["w_ffn_out"] = mm_tn("d_w_ffn_out", res["act"], dx2)
    dh1, dh2 = swiglu_bwd(dact, res["hf"])
    g["w_ffn_in"] = jnp.concatenate([mm_tn("d_w_ffn_in_a", res["hn1"], dh1), mm_tn("d_w_ffn_in_b", res["hn1"], dh2)],
                                    axis=1)
    dhn1 = fused_mm("d_ffn_in", [dh1, dh2], [_row_spec(FFN_HIDDEN), _row_spec(FFN_HIDDEN)], _concat_prologue,
                    lw["w_ffn_in_t"], tn=512, out_dtype=BF16)
    dx1, g["norm_ffn"] = norm_bwd("ffn_norm_bwd", dhn1, res["x1"], lw["norm_ffn"], dx2)
    dmerged = fused_mm("d_out_proj", [dx1], [_x_spec()], _cast_prologue, lw["w_out_t"], tn=512, out_dtype=BF16)
    g["w_out"] = mm_tn("d_w_out", res["merged"], dx1)
    z = res["z"]
    ys = (res["y_ssm"], res["y_conv"], res["y_attn"])
    dgates, dbs = merge_bwd(dmerged, z, *ys, lw["w_branch"])
    dys, dwb = [], []
    for r in range(3):
        dys.append(fused_mm(f"d_branch_{r}", [dbs[r]], [_x_spec()], _cast_prologue, lw["w_branch_t"][r], tn=512,
                            out_dtype=BF16))
        dwb.append(mm_tn(f"d_w_branch_{r}", ys[r], dbs[r]))
    g["w_branch"] = jnp.stack(dwb)
    sc = _ssm_consts(lw)
    (du, dbb_re, dbb_im, dct_re, dct_im, da_re, da_im, g["ssm_d"], g["ssm_w_glu"]) = ssm_bwd(
        dys[0], z, res["ypre"], res["h_re"], res["h_im"], sc["bbt_re"], sc["bbt_im"], sc["c_re"], sc["c_im"],
        sc["a_re"], sc["a_im"], lw["ssm_d"], lw["ssm_w_glu"], lw["ssm_w_glu_t"])
    g["ssm_c_re"] = _blockdiag_extract(jnp.swapaxes(dct_re, 1, 2))
    g["ssm_c_im"] = _blockdiag_extract(jnp.swapaxes(dct_im, 1, 2))
    (g["ssm_lambda_re"], g["ssm_lambda_im"], dlog_dt, g["bt_re"], g["bt_im"]) = ssm_disc_bwd(
        lw["ssm_lambda_re"], lw["ssm_lambda_im"], lw["ssm_log_dt"].reshape(SSM_GROUPS, 1), lw["bt_re"], lw["bt_im"],
        da_re.reshape(_GN), da_im.reshape(_GN),
        _blockdiag_extract(dbb_re), _blockdiag_extract(dbb_im))
    g["ssm_log_dt"] = dlog_dt.reshape(SSM_GROUPS)
    dconv, g["conv_w"] = conv_bwd(dys[1], z, lw["conv_w"])
    dq, dkv_cur, dkv_prev, g["dbias"], g["attn_sinks"] = attn_bwd(dys[2], z, bias, lw["attn_sinks"])
    dkv = kv_shift_add(dkv_cur, dkv_prev)
    pieces = [dgates[0], dgates[1], dgates[2], du, dconv, dq, dkv]
    widths = [D_MODEL, D_MODEL, D_MODEL, BRANCH, 3 * BRANCH, BRANCH, KV_WIDTH]
    g["w_in"] = jnp.concatenate([mm_tn(f"d_w_in_{k}", res["h"], pc) for k, pc in enumerate(pieces)], axis=1)
    dhn0 = fused_mm("d_in_proj", pieces, [_row_spec(w) for w in widths], _concat_prologue, lw["w_in_t"], tn=512,
                    out_dtype=BF16)
    dx0, g["norm_mix"] = norm_bwd("mix_norm_bwd", dhn0, res["x"], lw["norm_mix"], dx1)
    return dx0, g


def adamw(name, parts, w, m, v):
    n, R, C = parts.shape
    tr = _pick(R, (512, 256, 128, 64, 32, 16, 8))

    def body(p_ref, w_ref, m_ref, v_ref, g_ref, d_ref, nm_ref, nv_ref):
        gsum = p_ref[0].astype(F32)
        for k in range(1, n):
            gsum = gsum + p_ref[k].astype(F32)
        mn = ADAM_B1 * m_ref[...] + (1.0 - ADAM_B1) * gsum
        vn = ADAM_B2 * v_ref[...] + (1.0 - ADAM_B2) * jnp.square(gsum)
        m_hat = mn / (1.0 - ADAM_B1 ** ADAM_STEP)
        v_hat = vn / (1.0 - ADAM_B2 ** ADAM_STEP)
        g_ref[...] = gsum
        d_ref[...] = -ADAM_LR * (m_hat / (jnp.sqrt(v_hat) + ADAM_EPS) + ADAM_WD * w_ref[...])
        nm_ref[...] = mn
        nv_ref[...] = vn

    blk = pl.BlockSpec((tr, C), lambda i: (i, 0))
    return pl.pallas_call(
        body, name=name, grid=(R // tr,), in_specs=[pl.BlockSpec((n, tr, C), lambda i: (0, i, 0)), blk, blk, blk],
        out_specs=[blk] * 4, out_shape=[jax.ShapeDtypeStruct((R, C), F32)] * 4, compiler_params=_cparams("parallel"),
    )(parts, w, m, v)


_ANY = pl.BlockSpec(memory_space=pl.ANY)


def _coords():
    return lax.axis_index("x"), lax.axis_index("y"), lax.axis_index("c")


def _chip_peers(x, y):
    return [(1 - x, y), (x, 1 - y), (1 - x, 1 - y)]


def _comm_call(name, body, blks, out_shapes, copies_per_block):
    n = len(blks)
    return pl.pallas_call(
        functools.partial(body, n), name=name, in_specs=[_ANY] * n, out_specs=[_ANY] * n,
        out_shape=[jax.ShapeDtypeStruct(s, b.dtype) for s, b in zip(out_shapes, blks)],
        scratch_shapes=[pltpu.SemaphoreType.DMA((copies_per_block * n,)), pltpu.SemaphoreType.DMA((copies_per_block * n,))],
    )(*blks)


def gather_chips(name, blks):
    def body(n, *refs):
        x_refs, out_refs, send_sems, recv_sems = refs[:n], refs[n:2 * n], refs[2 * n], refs[2 * n + 1]
        x, y, c = _coords()
        me = 2 * x + y
        peers = _chip_peers(x, y)

        def copy(i, k, slot):
            return pltpu.make_async_remote_copy(
                src_ref=x_refs[i], dst_ref=out_refs[i].at[slot], send_sem=send_sems.at[3 * i + k],
                recv_sem=recv_sems.at[3 * i + k], device_id=(*peers[k], c), device_id_type=MESH)

        sends = [copy(i, k, me) for i in range(n) for k in range(3)]
        for cp in sends:
            cp.start()
        for i in range(n):
            for k, (px, py) in enumerate(peers):
                copy(i, k, 2 * px + py).wait_recv()
        for cp in sends:
            cp.wait_send()

    return _comm_call(name, body, blks, [(4,) + b.shape for b in blks], 3)


def gather_cores(name, blks):
    def body(n, *refs):
        x_refs, out_refs, send_sems, recv_sems = refs[:n], refs[n:2 * n], refs[2 * n], refs[2 * n + 1]
        x, y, c = _coords()

        def copy(i, slot):
            return pltpu.make_async_remote_copy(
                src_ref=x_refs[i], dst_ref=out_refs[i].at[slot], send_sem=send_sems.at[i], recv_sem=recv_sems.at[i],
                device_id=(x, y, 1 - c), device_id_type=MESH)

        sends = [copy(i, c) for i in range(n)]
        for cp in sends:
            cp.start()
        for i in range(n):
            copy(i, 1 - c).wait_recv()
        for cp in sends:
            cp.wait_send()

    return _comm_call(name, body, blks, [(2,) + b.shape for b in blks], 1)


def scatter_cores(name, blks):
    def body(n, *refs):
        x_refs, out_refs, send_sems, recv_sems = refs[:n], refs[n:2 * n], refs[2 * n], refs[2 * n + 1]
        x, y, c = _coords()
        sends = [pltpu.make_async_remote_copy(
            src_ref=x_refs[i].at[1 - c], dst_ref=out_refs[i], send_sem=send_sems.at[i], recv_sem=recv_sems.at[i],
            device_id=(x, y, 1 - c), device_id_type=MESH) for i in range(n)]
        for cp in sends:
            cp.start()
        for cp in sends:
            cp.wait_recv()
        for cp in sends:
            cp.wait_send()

    return _comm_call(name, body, blks, [b.shape[1:] for b in blks], 1)


def scatter_chips(name, blks):
    def body(n, *refs):
        x_refs, out_refs, send_sems, recv_sems = refs[:n], refs[n:2 * n], refs[2 * n], refs[2 * n + 1]
        x, y, c = _coords()
        me = 2 * x + y
        peers = _chip_peers(x, y)

        def copy(i, k, src_slot, dst_slot):
            return pltpu.make_async_remote_copy(
                src_ref=x_refs[i].at[src_slot], dst_ref=out_refs[i].at[dst_slot], send_sem=send_sems.at[3 * i + k],
                recv_sem=recv_sems.at[3 * i + k], device_id=(*peers[k], c), device_id_type=MESH)

        sends = [copy(i, k, 2 * px + py, me) for i in range(n) for k, (px, py) in enumerate(peers)]
        for cp in sends:
            cp.start()
        for i in range(n):
            for k, (px, py) in enumerate(peers):
                copy(i, k, me, 2 * px + py).wait_recv()
        for cp in sends:
            cp.wait_send()

    return _comm_call(name, body, blks, [b.shape for b in blks], 3)


def _put_slot(buf, block, idx):
    return lax.dynamic_update_slice(buf, block[None].astype(buf.dtype), (idx,) + (0,) * block.ndim)


def pair_sum(name, mine, theirs):
    _, R, C = mine.shape
    tr = _pick(R, (1024, 512, 256, 128, 64, 32, 16))
    c_idx = lax.axis_index("c").astype(jnp.int32).reshape(1)

    def body(c_ref, a_ref, b_ref, o_ref):
        o_ref[...] = (a_ref[0].astype(F32) + b_ref[...].astype(F32)).astype(BF16)

    return pl.pallas_call(
        body, name=name,
        grid_spec=pltpu.PrefetchScalarGridSpec(
            num_scalar_prefetch=1, grid=(R // tr,),
            in_specs=[pl.BlockSpec((1, tr, C), lambda i, c: (c[0], i, 0)), pl.BlockSpec((tr, C), lambda i, c: (i, 0))],
            out_specs=pl.BlockSpec((tr, C), lambda i, c: (i, 0))),
        out_shape=jax.ShapeDtypeStruct(theirs.shape, BF16), compiler_params=_cparams("parallel"),
    )(c_idx, mine, theirs)


SHARDED = {
    "w_in": ((D_MODEL, IN_WIDTH), 2), "ssm_w_glu": ((BRANCH, BRANCH), 1), "conv_w": ((3, BRANCH), 2),
    "w_branch": ((3, BRANCH, D_MODEL), 3), "w_out": ((D_MODEL, D_MODEL), 1), "w_ffn_in": ((D_MODEL, 2 * FFN_HIDDEN), 2),
    "w_ffn_out": ((FFN_HIDDEN, D_MODEL), 1), "w_ple_gate": ((D_MODEL, D_MODEL), 1), "w_ple_proj": ((PLE_DIM, D_MODEL), 2),
}
SMALL = ["rel_bias", "norm_mix", "ssm_lambda_re", "ssm_lambda_im", "ssm_b_re", "ssm_b_im", "ssm_c_re", "ssm_c_im", "ssm_d",
         "ssm_log_dt", "attn_sinks", "norm_ffn", "norm_ple", "norm_final"]
WEIGHTS = ["rel_bias", "norm_mix", "w_in", "ssm_lambda_re", "ssm_lambda_im", "ssm_b_re", "ssm_b_im", "ssm_c_re", "ssm_c_im",
           "ssm_d", "ssm_log_dt", "ssm_w_glu", "conv_w", "attn_sinks", "w_branch", "w_out", "norm_ffn", "w_ffn_in",
           "w_ffn_out", "norm_ple", "w_ple_gate", "w_ple_proj", "norm_final"]


def _pad_to(flat, n):
    return jnp.pad(flat, [(0, 0)] * (flat.ndim - 1) + [(0, n - flat.shape[-1])])


def _unshard(g8, name):
    _, axis = SHARDED[name]
    shard = g8.shape[2:]
    b = g8.reshape((2, 2, 2) + shard)
    b = jnp.moveaxis(b, (1, 2, 0), (axis, axis + 1, axis + 2))
    full = list(shard)
    full[axis] *= N_DEV
    return b.reshape(full)


def _shard_split(full, name):
    _, axis = SHARDED[name]
    dims = list(full.shape)
    dims[axis:axis + 1] = [2, 2, 2, dims[axis] // N_DEV]
    b = jnp.moveaxis(full.reshape(dims), (axis, axis + 1, axis + 2), (1, 2, 0))
    return b.reshape((2, 4) + b.shape[3:])


def _small_sizes(shapes):
    return [-(-int(np.prod(shapes[n])) // 128) * 128 for n in SMALL]


def pack_small(vals, shapes, extra):
    segs = [_pad_to(vals[n].reshape(-1).astype(F32), s) for n, s in zip(SMALL, _small_sizes(shapes))]
    segs.append(_pad_to(extra.reshape(-1), 128))
    flat = jnp.concatenate(segs)
    rows = -(-flat.shape[0] // (128 * 8)) * 8
    return _pad_to(flat, rows * 128).reshape(rows, 128)


def unpack_small(packed, shapes):
    flat = packed.reshape(-1)
    out, off = {}, 0
    for n, s in zip(SMALL, _small_sizes(shapes)):
        out[n] = flat[off:off + int(np.prod(shapes[n]))].reshape(shapes[n])
        off += s
    return out, flat[off]


def _layer_weights(full, small):
    w_in = full["w_in"]
    w_in_p = jnp.concatenate([w_in[..., 2816:], w_in[..., :2560], w_in[..., 2560:2816]], axis=-1)
    return dict(
        w_in=w_in_p, w_in_t=jnp.swapaxes(w_in_p, 1, 2),
        ssm_w_glu=full["ssm_w_glu"], ssm_w_glu_t=jnp.swapaxes(full["ssm_w_glu"], 1, 2),
        conv_w=full["conv_w"],
        w_branch=full["w_branch"], w_branch_t=jnp.swapaxes(full["w_branch"], 2, 3),
        w_out=full["w_out"], w_out_t=jnp.swapaxes(full["w_out"], 1, 2),
        w_ffn_in=full["w_ffn_in"], w_ffn_in_t=jnp.swapaxes(full["w_ffn_in"], 1, 2),
        w_ffn_out=full["w_ffn_out"], w_ffn_out_t=jnp.swapaxes(full["w_ffn_out"], 1, 2),
        w_ple_gate=full["w_ple_gate"], w_ple_gate_t=jnp.swapaxes(full["w_ple_gate"], 1, 2),
        w_ple_proj=full["w_ple_proj"],
        norm_mix=small["norm_mix"][:, None, :], norm_ffn=small["norm_ffn"][:, None, :],
        norm_ple=small["norm_ple"][:, None, :],
        ssm_lambda_re=small["ssm_lambda_re"], ssm_lambda_im=small["ssm_lambda_im"], ssm_log_dt=small["ssm_log_dt"],
        bt_re=jnp.swapaxes(small["ssm_b_re"], 2, 3), bt_im=jnp.swapaxes(small["ssm_b_im"], 2, 3),
        ssm_c_re=small["ssm_c_re"], ssm_c_im=small["ssm_c_im"], ssm_d=small["ssm_d"][:, None, :],
        attn_sinks=small["attn_sinks"],
    )


def local_step(x, p, full, small, target):
    bias = rel_bias_fwd(small["rel_bias"].T).reshape(N_Q_HEADS, BLOCK, 2 * BLOCK)
    lw = _layer_weights(full, small)
    lw["p"] = p

    def fwd(xc, lw_i):
        return layer_fwd(xc, lw_i, bias)

    x_out, res = lax.scan(fwd, x, lw)
    dx, loss_parts, dg_final = loss_head(x_out, small["norm_final"][None, :], target)

    def bwd(dxc, inp):
        lw_i, res_i = inp
        return layer_bwd(dxc, lw_i, res_i, bias)

    grad_x, g = lax.scan(bwd, dx, (lw, res), reverse=True)
    drel = rel_bias_bwd(g["dbias"].reshape(DEPTH, N_Q_HEADS, BLOCK * 2 * BLOCK)).T
    w_in_g = g["w_in"]
    gfull = dict(
        w_in=jnp.concatenate([w_in_g[..., 3072:5632], w_in_g[..., 5632:], w_in_g[..., :3072]], axis=-1),
        ssm_w_glu=g["ssm_w_glu"], conv_w=jnp.sum(g["conv_w"].reshape(DEPTH, 3, 8, BRANCH), axis=2),
        w_branch=g["w_branch"], w_out=g["w_out"], w_ffn_in=g["w_ffn_in"], w_ffn_out=g["w_ffn_out"],
        w_ple_gate=g["w_ple_gate"], w_ple_proj=g["w_ple_proj"])
    gsmall = dict(
        rel_bias=drel, norm_mix=jnp.sum(g["norm_mix"], axis=1), ssm_lambda_re=g["ssm_lambda_re"],
        ssm_lambda_im=g["ssm_lambda_im"], ssm_b_re=jnp.swapaxes(g["bt_re"], 2, 3), ssm_b_im=jnp.swapaxes(g["bt_im"], 2, 3),
        ssm_c_re=g["ssm_c_re"], ssm_c_im=g["ssm_c_im"], ssm_d=g["ssm_d"][:, 0, :], ssm_log_dt=g["ssm_log_dt"],
        attn_sinks=g["attn_sinks"][:, :, 0], norm_ffn=jnp.sum(g["norm_ffn"], axis=1), norm_ple=jnp.sum(g["norm_ple"], axis=1),
        norm_final=jnp.sum(dg_final, axis=0))
    return loss_parts, grad_x, gfull, gsmall


def kernel(x, p, rel_bias, norm_mix, w_in, ssm_lambda_re, ssm_lambda_im, ssm_b_re, ssm_b_im, ssm_c_re, ssm_c_im, ssm_d, ssm_log_dt, ssm_w_glu, conv_w, attn_sinks, w_branch, w_out, norm_ffn, w_ffn_in, w_ffn_out, norm_ple, w_ple_gate, w_ple_proj, norm_final, loss_target, m_rel_bias, m_norm_mix, m_w_in, m_ssm_lambda_re, m_ssm_lambda_im, m_ssm_b_re, m_ssm_b_im, m_ssm_c_re, m_ssm_c_im, m_ssm_d, m_ssm_log_dt, m_ssm_w_glu, m_conv_w, m_attn_sinks, m_w_branch, m_w_out, m_norm_ffn, m_w_ffn_in, m_w_ffn_out, m_norm_ple, m_w_ple_gate, m_w_ple_proj, m_norm_final, v_rel_bias, v_norm_mix, v_w_in, v_ssm_lambda_re, v_ssm_lambda_im, v_ssm_b_re, v_ssm_b_im, v_ssm_c_re, v_ssm_c_im, v_ssm_d, v_ssm_log_dt, v_ssm_w_glu, v_conv_w, v_attn_sinks, v_w_branch, v_w_out, v_norm_ffn, v_w_ffn_in, v_w_ffn_out, v_norm_ple, v_w_ple_gate, v_w_ple_proj, v_norm_final):
    args = dict(locals())
    w = {n: args[n] for n in WEIGHTS}
    m = {n: args["m_" + n] for n in WEIGHTS}
    v = {n: args["v_" + n] for n in WEIGHTS}
    shapes = {n: w[n].shape for n in SMALL}

    sharded = list(SHARDED)
    x_i, y_i, c_i = _coords()
    chip = 2 * x_i + y_i

    def all_gather(tag, blks):
        g4 = gather_chips(f"gather_{tag}_chips", blks)
        g4 = [_put_slot(g, b, chip) for g, b in zip(g4, blks)]
        g8 = gather_cores(f"gather_{tag}_cores", g4)
        return [_put_slot(g, b, c_i) for g, b in zip(g8, g4)]

    mine = [w[n] if n == "conv_w" else w[n].astype(BF16) for n in sharded]
    full = {n: _unshard(g, n) for n, g in zip(sharded, all_gather("w", mine))}

    loss_parts, grad_x, gfull, gsmall = local_step(x[0], p[:, 0], full, {n: w[n] for n in SMALL}, loss_target[0])

    gp = [_shard_split(gfull[n], n).astype(BF16) for n in sharded]
    from_sibling = scatter_cores("scatter_g_cores", gp)
    sums = [pair_sum("pair_sum_" + n, a.reshape(2, -1, a.shape[-1]), b.reshape(-1, b.shape[-1])).reshape(b.shape)
            for n, a, b in zip(sharded, gp, from_sibling)]
    received = scatter_chips("scatter_g_chips", sums)

    outs = ({}, {}, {}, {})
    for name, r, s in zip(sharded, received, sums):
        parts = _put_slot(r, lax.dynamic_index_in_dim(s, chip, 0, keepdims=False), chip)
        cols = parts.shape[-1]
        res4 = adamw("adamw_" + name, parts.reshape(4, -1, cols), w[name].reshape(-1, cols), m[name].reshape(-1, cols),
                     v[name].reshape(-1, cols))
        for d, o in zip(outs, res4):
            d[name] = o.reshape(w[name].shape)

    small_local = pack_small(gsmall, shapes, jnp.sum(loss_parts))
    small_all = all_gather("s", [small_local])[0]
    zero = jnp.zeros((1,), F32)
    res4 = adamw("adamw_small", small_all.reshape(N_DEV, small_local.shape[0], 128),
                 pack_small({n: w[n] for n in SMALL}, shapes, zero), pack_small({n: m[n] for n in SMALL}, shapes, zero),
                 pack_small({n: v[n] for n in SMALL}, shapes, zero))
    loss = None
    for d, r in zip(outs, res4):
        vals, extra = unpack_small(r, shapes)
        d.update(vals)
        if loss is None:
            loss = extra

    return (loss, grad_x[None], *[d[n] for d in outs for n in WEIGHTS])
```

```python
import functools
import math

import numpy as np
import jax
import jax.numpy as jnp
from jax import lax
from jax.experimental import pallas as pl
from jax.experimental.pallas import tpu as pltpu

F32 = jnp.float32
BF16 = jnp.bfloat16
MESH = pl.DeviceIdType.MESH

D_MODEL = 1024
DEPTH = 4
PLE_DIM = 256
BRANCH = 512
SSM_GROUPS = 32
SSM_GROUP = 16
SSM_STATE = 64
SSM_LANES = SSM_GROUPS * SSM_STATE
SSM_SUB = 4
SUB_IN = BRANCH // SSM_SUB
SUB_ST = SSM_LANES // SSM_SUB
HEAD_DIM = 64
N_Q_HEADS = 8
N_KV_HEADS = 2
GQA_GROUP = 4
KV_WIDTH = 2 * N_KV_HEADS * HEAD_DIM
WINDOW = 128
BLOCK = 128
ATTN_SCALE = 1.0 / math.sqrt(HEAD_DIM)
REL_BUCKETS = 32
REL_MAX_DIST = 128
FFN_HIDDEN = 2816
RMS_EPS = 1e-6
IN_WIDTH = 5888
N_DEV = 8

ADAM_LR = 0.001
ADAM_B1 = 0.9
ADAM_B2 = 0.999
ADAM_EPS = 1e-08
ADAM_WD = 0.01
ADAM_STEP = 10

COL_U = 3072
COL_KV = 5632
NEG = -1e30

SCAN_T = 256
TM = 512
VMEM_LIMIT = 52 * 1024 * 1024


def _cparams(*sem):
    return pltpu.CompilerParams(dimension_semantics=sem, vmem_limit_bytes=VMEM_LIMIT)


def _full(shape):
    n = len(shape)
    return pl.BlockSpec(shape, lambda *_: (0,) * n)


def _pick(n, cands):
    for c in cands:
        if n % c == 0:
            return c
    return n


def _dot(a, b):
    return jnp.dot(a, b, preferred_element_type=F32)


def _dot_tn(a, b):
    return lax.dot_general(a, b, (((0,), (0,)), ((), ())), preferred_element_type=F32)


def _dot_nt(a, b):
    return lax.dot_general(a, b, (((1,), (1,)), ((), ())), preferred_element_type=F32)


def _rms(x, g):
    r = lax.rsqrt(jnp.mean(x * x, axis=-1, keepdims=True) + RMS_EPS)
    return x * r * g


def fused_mm(name, ins, in_specs, prologue, w, *, tn, out_dtype, res=None, extras=()):
    S = ins[0].shape[0]
    K, N = w.shape
    n_in, n_ex = len(ins), len(extras)

    def body(*refs):
        in_refs = refs[:n_in]
        w_ref = refs[n_in]
        pos = n_in + 1
        res_ref = None
        if res is not None:
            res_ref = refs[pos]
            pos += 1
        o_ref = refs[pos]
        ex_refs = refs[pos + 1:pos + 1 + n_ex]
        a_scr = refs[-1]

        @pl.when(pl.program_id(1) == 0)
        def _():
            out = prologue(*[r[...] for r in in_refs])
            a_scr[...] = out[0]
            for r, e in zip(ex_refs, out[1:]):
                r[...] = e.astype(r.dtype)

        acc = _dot(a_scr[...], w_ref[...])
        if res_ref is not None:
            acc = acc + res_ref[...]
        o_ref[...] = acc.astype(o_ref.dtype)

    specs = list(in_specs) + [pl.BlockSpec((K, tn), lambda i, j: (0, j))]
    args = list(ins) + [w]
    if res is not None:
        specs.append(pl.BlockSpec((TM, tn), lambda i, j: (i, j)))
        args.append(res)
    out_shape = [jax.ShapeDtypeStruct((S, N), out_dtype)]
    out_specs = [pl.BlockSpec((TM, tn), lambda i, j: (i, j))]
    for cols, dt in extras:
        out_shape.append(jax.ShapeDtypeStruct((S, cols), dt))
        out_specs.append(pl.BlockSpec((TM, cols), lambda i, j: (i, 0)))
    outs = pl.pallas_call(
        body, name=name, grid=(S // TM, N // tn), in_specs=specs, out_specs=out_specs, out_shape=out_shape,
        scratch_shapes=[pltpu.VMEM((TM, K), BF16)], compiler_params=_cparams("parallel", "arbitrary"),
    )(*args)
    return outs if n_ex else outs[0]


def _row_spec(cols, blk=0):
    return pl.BlockSpec((TM, cols), lambda i, j: (i, blk))


def mm_tn(name, a, b):
    S, K = a.shape
    N = b.shape[1]
    tk = _pick(K, (1024, 1408, 512, 256))
    tn = _pick(N, (1024, 1408, 1536, 512, 256))

    def body(a_ref, b_ref, o_ref):
        @pl.when(pl.program_id(2) == 0)
        def _():
            o_ref[...] = jnp.zeros_like(o_ref)

        o_ref[...] += _dot_tn(a_ref[...].astype(BF16), b_ref[...].astype(BF16))

    return pl.pallas_call(
        body, name=name, grid=(K // tk, N // tn, S // TM),
        in_specs=[pl.BlockSpec((TM, tk), lambda k, n, s: (s, k)), pl.BlockSpec((TM, tn), lambda k, n, s: (s, n))],
        out_specs=pl.BlockSpec((tk, tn), lambda k, n, s: (k, n)),
        out_shape=jax.ShapeDtypeStruct((K, N), F32),
        compiler_params=_cparams("parallel", "parallel", "arbitrary"),
    )(a, b)


def norm_bwd(name, dh, x, g, dres):
    S = x.shape[0]

    def body(dh_ref, x_ref, g_ref, dres_ref, dx_ref, dg_ref):
        @pl.when(pl.program_id(0) == 0)
        def _():
            dg_ref[...] = jnp.zeros_like(dg_ref)

        xv = x_ref[...]
        dhv = dh_ref[...].astype(F32)
        r = lax.rsqrt(jnp.mean(xv * xv, axis=-1, keepdims=True) + RMS_EPS)
        xhat = xv * r
        dxhat = dhv * g_ref[...]
        dx = r * (dxhat - xhat * jnp.mean(dxhat * xhat, axis=-1, keepdims=True))
        dx_ref[...] = dres_ref[...] + dx
        dg_ref[...] += jnp.sum((dhv * xhat).reshape(TM // 8, 8, D_MODEL), axis=0)

    row = pl.BlockSpec((TM, D_MODEL), lambda i: (i, 0))
    return pl.pallas_call(
        body, name=name, grid=(S // TM,), in_specs=[row, row, _full((1, D_MODEL)), row],
        out_specs=[row, _full((8, D_MODEL))],
        out_shape=[jax.ShapeDtypeStruct((S, D_MODEL), F32), jax.ShapeDtypeStruct((8, D_MODEL), F32)],
        compiler_params=_cparams("arbitrary"),
    )(dh, x, g, dres)


def swiglu_bwd(dact, hf):
    S = hf.shape[0]

    def body(da_ref, h1_ref, h2_ref, o1_ref, o2_ref):
        h1 = h1_ref[...].astype(F32)
        h2 = h2_ref[...].astype(F32)
        da = da_ref[...].astype(F32)
        sg = jax.nn.sigmoid(h1)
        o1_ref[...] = (da * h2 * sg * (1.0 + h1 * (1.0 - sg))).astype(BF16)
        o2_ref[...] = (da * h1 * sg).astype(BF16)

    tn = 1408
    nn = FFN_HIDDEN // tn
    return pl.pallas_call(
        body, name="swiglu_bwd", grid=(S // TM, nn),
        in_specs=[pl.BlockSpec((TM, tn), lambda i, j: (i, j)), pl.BlockSpec((TM, tn), lambda i, j: (i, j)),
                  pl.BlockSpec((TM, tn), lambda i, j: (i, nn + j))],
        out_specs=[pl.BlockSpec((TM, tn), lambda i, j: (i, j)), pl.BlockSpec((TM, tn), lambda i, j: (i, j))],
        out_shape=[jax.ShapeDtypeStruct((S, FFN_HIDDEN), BF16)] * 2,
        compiler_params=_cparams("parallel", "parallel"),
    )(dact, hf, hf)


def ple_combine(x2, a_pre, pp):
    S = x2.shape[0]

    def body(x_ref, a_ref, p_ref, o_ref):
        o_ref[...] = x_ref[...] + jax.nn.sigmoid(a_ref[...].astype(F32)) * p_ref[...].astype(F32)

    row = pl.BlockSpec((TM, D_MODEL), lambda i: (i, 0))
    return pl.pallas_call(
        body, name="ple_combine", grid=(S // TM,), in_specs=[row, row, row], out_specs=row,
        out_shape=jax.ShapeDtypeStruct((S, D_MODEL), F32), compiler_params=_cparams("parallel"),
    )(x2, a_pre, pp)


def ple_bwd(dx3, a_pre, pp):
    S = dx3.shape[0]

    def body(dx_ref, a_ref, p_ref, da_ref, dpp_ref):
        dx = dx_ref[...]
        pg = jax.nn.sigmoid(a_ref[...].astype(F32))
        dpp_ref[...] = (dx * pg).astype(BF16)
        da_ref[...] = (dx * p_ref[...].astype(F32) * pg * (1.0 - pg)).astype(BF16)

    row = pl.BlockSpec((TM, D_MODEL), lambda i: (i, 0))
    return pl.pallas_call(
        body, name="ple_bwd", grid=(S // TM,), in_specs=[row, row, row], out_specs=[row, row],
        out_shape=[jax.ShapeDtypeStruct((S, D_MODEL), BF16)] * 2, compiler_params=_cparams("parallel"),
    )(dx3, a_pre, pp)


def _gate_specs(tn, nn):
    return [pl.BlockSpec((TM, tn), functools.partial(lambda i, j, r: (i, r * nn + j), r=r)) for r in range(3)]


def merge_fwd(z, y_ssm, y_conv, y_attn, wb):
    S = z.shape[0]
    tn = 512
    nn = D_MODEL // tn

    def body(g0, g1, g2, y0, y1, y2, w_ref, o_ref):
        acc = jnp.zeros((TM, tn), F32)
        for r, (g_ref, y_ref) in enumerate(((g0, y0), (g1, y1), (g2, y2))):
            acc += jax.nn.sigmoid(g_ref[...].astype(F32)) * _dot(y_ref[...], w_ref[r])
        o_ref[...] = acc.astype(BF16)

    y_spec = pl.BlockSpec((TM, BRANCH), lambda i, j: (i, 0))
    return pl.pallas_call(
        body, name="merge_fwd", grid=(S // TM, nn),
        in_specs=_gate_specs(tn, nn) + [y_spec] * 3 + [pl.BlockSpec((3, BRANCH, tn), lambda i, j: (0, 0, j))],
        out_specs=pl.BlockSpec((TM, tn), lambda i, j: (i, j)),
        out_shape=jax.ShapeDtypeStruct((S, D_MODEL), BF16), compiler_params=_cparams("parallel", "parallel"),
    )(z, z, z, y_ssm, y_conv, y_attn, wb)


def merge_bwd(dmerged, z, y_ssm, y_conv, y_attn, wb):
    S = z.shape[0]
    tn = 512
    nn = D_MODEL // tn

    def body(dm_ref, g0, g1, g2, y0, y1, y2, w_ref, dg0, dg1, dg2, db0, db1, db2):
        dm = dm_ref[...].astype(F32)
        for r, (g_ref, y_ref, dg_ref, db_ref) in enumerate(((g0, y0, dg0, db0), (g1, y1, dg1, db1), (g2, y2, dg2, db2))):
            sg = jax.nn.sigmoid(g_ref[...].astype(F32))
            b = _dot(y_ref[...], w_ref[r])
            dg_ref[...] = (dm * b * sg * (1.0 - sg)).astype(BF16)
            db_ref[...] = (dm * sg).astype(BF16)

    y_spec = pl.BlockSpec((TM, BRANCH), lambda i, j: (i, 0))
    outs = pl.pallas_call(
        body, name="merge_bwd", grid=(S // TM, nn),
        in_specs=[pl.BlockSpec((TM, tn), lambda i, j: (i, j))] + _gate_specs(tn, nn) + [y_spec] * 3
        + [pl.BlockSpec((3, BRANCH, tn), lambda i, j: (0, 0, j))],
        out_specs=[pl.BlockSpec((TM, tn), lambda i, j: (i, j))] * 6,
        out_shape=[jax.ShapeDtypeStruct((S, D_MODEL), BF16)] * 6, compiler_params=_cparams("parallel", "parallel"),
    )(dmerged, z, z, z, y_ssm, y_conv, y_attn, wb)
    return outs[:3], outs[3:]


def _shift_down(v, halo, k):
    rolled = pltpu.roll(v, k, 0)
    h = pltpu.roll(halo, k, 0)
    row = lax.broadcasted_iota(jnp.int32, v.shape, 0)
    head = jnp.concatenate([h, jnp.zeros((v.shape[0] - 8, v.shape[1]), v.dtype)], axis=0)
    return jnp.where(row < k, head, rolled)


def _shift_up(v, halo, k):
    n = v.shape[0]
    rolled = pltpu.roll(v, n - k, 0)
    h = pltpu.roll(halo, 8 - k, 0)
    row = lax.broadcasted_iota(jnp.int32, v.shape, 0)
    tail = jnp.concatenate([jnp.zeros((n - 8, v.shape[1]), v.dtype), h], axis=0)
    return jnp.where(row >= n - k, tail, rolled)


def _conv_specs():
    rb = TM // 8
    c0 = COL_U // BRANCH

    def cur(k):
        return pl.BlockSpec((TM, BRANCH), lambda i: (i, c0 + k))

    def prev(k):
        return pl.BlockSpec((8, BRANCH), lambda i: (jnp.maximum(i * rb - 1, 0), c0 + k))

    return [cur(1), cur(2), cur(3), prev(2), prev(3)]


def conv_fwd(z, conv_w):
    S = z.shape[0]

    def body(cb_ref, cc_ref, cx_ref, pc_ref, px_ref, w_ref, o_ref):
        first = pl.program_id(0) == 0
        v = cc_ref[...].astype(F32) * cx_ref[...].astype(F32)
        pv = jnp.where(first, 0.0, pc_ref[...].astype(F32) * px_ref[...].astype(F32))
        w = w_ref[...]
        y = w[2:3] * v + w[1:2] * _shift_down(v, pv, 1) + w[0:1] * _shift_down(v, pv, 2)
        o_ref[...] = (cb_ref[...].astype(F32) * y).astype(BF16)

    return pl.pallas_call(
        body, name="conv_fwd", grid=(S // TM,), in_specs=_conv_specs() + [_full((3, BRANCH))],
        out_specs=pl.BlockSpec((TM, BRANCH), lambda i: (i, 0)),
        out_shape=jax.ShapeDtypeStruct((S, BRANCH), BF16), compiler_params=_cparams("parallel"),
    )(z, z, z, z, z, conv_w)


def conv_bwd(dy, z, conv_w):
    S = z.shape[0]
    rb = TM // 8
    nt = S // TM
    c0 = COL_U // BRANCH

    def body(dy_ref, cb_ref, cc_ref, cx_ref, pc_ref, px_ref, ndy_ref, ncb_ref, w_ref, o_ref, dw_ref):
        i = pl.program_id(0)

        @pl.when(i == 0)
        def _():
            dw_ref[...] = jnp.zeros_like(dw_ref)

        cb = cb_ref[...].astype(F32)
        cc = cc_ref[...].astype(F32)
        cx = cx_ref[...].astype(F32)
        dyv = dy_ref[...].astype(F32)
        v = cc * cx
        pv = jnp.where(i == 0, 0.0, pc_ref[...].astype(F32) * px_ref[...].astype(F32))
        v1 = _shift_down(v, pv, 1)
        v2 = _shift_down(v, pv, 2)
        w = w_ref[...]
        conv = w[2:3] * v + w[1:2] * v1 + w[0:1] * v2
        dc = dyv * cb
        ndc = jnp.where(i == nt - 1, 0.0, ndy_ref[...].astype(F32) * ncb_ref[...].astype(F32))
        dv = w[2:3] * dc + w[1:2] * _shift_up(dc, ndc, 1) + w[0:1] * _shift_up(dc, ndc, 2)
        o_ref[:, 0:BRANCH] = (dyv * conv).astype(BF16)
        o_ref[:, BRANCH:2 * BRANCH] = (dv * cx).astype(BF16)
        o_ref[:, 2 * BRANCH:3 * BRANCH] = (dv * cc).astype(BF16)
        for k, vk in enumerate((v2, v1, v)):
            dw_ref[8 * k:8 * k + 8, :] += jnp.sum((dc * vk).reshape(rb, 8, BRANCH), axis=0)

    nxt = jnp.minimum

    return pl.pallas_call(
        body, name="conv_bwd", grid=(nt,),
        in_specs=[pl.BlockSpec((TM, BRANCH), lambda i: (i, 0))] + _conv_specs()
        + [pl.BlockSpec((8, BRANCH), lambda i: (nxt((i + 1) * rb, S // 8 - 1), 0)),
           pl.BlockSpec((8, BRANCH), lambda i: (nxt((i + 1) * rb, S // 8 - 1), c0 + 1)),
           _full((3, BRANCH))],
        out_specs=[pl.BlockSpec((TM, 3 * BRANCH), lambda i: (i, 0)), _full((24, BRANCH))],
        out_shape=[jax.ShapeDtypeStruct((S, 3 * BRANCH), BF16), jax.ShapeDtypeStruct((24, BRANCH), F32)],
        compiler_params=_cparams("arbitrary"),
    )(dy, z, z, z, z, z, dy, z, conv_w)


def _bucket_onehot_t():
    qi = np.arange(BLOCK)[:, None]
    kj = np.arange(2 * BLOCK)[None, :]
    dist = np.clip(qi + BLOCK - kj, 0, REL_MAX_DIST - 1)
    exact = REL_BUCKETS // 2
    df = np.maximum(dist, 1).astype(np.float32)
    large = exact + (np.log(df / np.float32(exact)) / np.float32(math.log(REL_MAX_DIST / exact))
                     * np.float32(REL_BUCKETS - exact)).astype(np.int32)
    large = np.minimum(large, REL_BUCKETS - 1)
    bucket = np.where(dist < exact, dist, large).reshape(-1)
    return (np.arange(REL_BUCKETS)[:, None] == bucket[None, :]).astype(np.float32)


def rel_bias_fwd(rel_bias_t):
    n = BLOCK * 2 * BLOCK

    def body(r_ref, oh_ref, o_ref):
        o_ref[...] = jnp.dot(r_ref[...], oh_ref[...], precision=lax.Precision.HIGHEST, preferred_element_type=F32)

    return pl.pallas_call(
        body, name="rel_bias_fwd", grid=(1,), in_specs=[_full((N_Q_HEADS, REL_BUCKETS)), _full((REL_BUCKETS, n))],
        out_specs=_full((N_Q_HEADS, n)), out_shape=jax.ShapeDtypeStruct((N_Q_HEADS, n), F32),
        compiler_params=_cparams("arbitrary"),
    )(rel_bias_t, jnp.asarray(_bucket_onehot_t()))


def rel_bias_bwd(dbias):
    n_l = dbias.shape[0]
    n = BLOCK * 2 * BLOCK

    def body(d_ref, oh_ref, o_ref):
        tot = d_ref[0]
        for l in range(1, n_l):
            tot = tot + d_ref[l]
        o_ref[...] = lax.dot_general(tot, oh_ref[...], (((1,), (1,)), ((), ())), precision=lax.Precision.HIGHEST,
                                     preferred_element_type=F32)

    return pl.pallas_call(
        body, name="rel_bias_bwd", grid=(1,), in_specs=[_full((n_l, N_Q_HEADS, n)), _full((REL_BUCKETS, n))],
        out_specs=_full((N_Q_HEADS, REL_BUCKETS)), out_shape=jax.ShapeDtypeStruct((N_Q_HEADS, REL_BUCKETS), F32),
        compiler_params=_cparams("arbitrary"),
    )(dbias, jnp.asarray(_bucket_onehot_t()))


def _attn_valid(first):
    qi = lax.broadcasted_iota(jnp.int32, (BLOCK, 2 * BLOCK), 0)
    kj = lax.broadcasted_iota(jnp.int32, (BLOCK, 2 * BLOCK), 1)
    dist = qi + BLOCK - kj
    return (dist >= 0) & (dist < WINDOW) & (jnp.logical_not(first) | (kj >= BLOCK))


def _attn_weights(qh, kcat, bias_h, valid, sink):
    s = _dot_nt(qh, kcat) * ATTN_SCALE + bias_h
    s = jnp.where(valid, s, NEG)
    m = jnp.maximum(jnp.max(s, axis=-1, keepdims=True), sink)
    p = jnp.exp(s - m)
    esink = jnp.exp(sink - m)
    inv = 1.0 / (jnp.sum(p, axis=-1, keepdims=True) + esink)
    return p * inv, esink * inv


def _kv_heads(kvp, kvc, hk):
    ks = slice(hk * HEAD_DIM, (hk + 1) * HEAD_DIM)
    vs = slice(KV_WIDTH // 2 + hk * HEAD_DIM, KV_WIDTH // 2 + (hk + 1) * HEAD_DIM)
    return jnp.concatenate([kvp[:, ks], kvc[:, ks]], axis=0), jnp.concatenate([kvp[:, vs], kvc[:, vs]], axis=0)


def _attn_specs():
    cq = (COL_U + 4 * BRANCH) // BRANCH
    ckv = COL_KV // KV_WIDTH
    return [pl.BlockSpec((BLOCK, BRANCH), lambda n: (n, cq)),
            pl.BlockSpec((BLOCK, KV_WIDTH), lambda n: (n, ckv)),
            pl.BlockSpec((BLOCK, KV_WIDTH), lambda n: (jnp.maximum(n - 1, 0), ckv)),
            _full((N_Q_HEADS, BLOCK, 2 * BLOCK)),
            pl.BlockSpec(memory_space=pltpu.SMEM)]


def attn_fwd(z, bias, sinks):
    S = z.shape[0]

    def body(q_ref, kvc_ref, kvp_ref, b_ref, sink_ref, o_ref):
        valid = _attn_valid(pl.program_id(0) == 0)
        q = q_ref[...]
        kvc = kvc_ref[...]
        kvp = kvp_ref[...]
        outs = []
        for hk in range(N_KV_HEADS):
            kcat, vcat = _kv_heads(kvp, kvc, hk)
            for g in range(GQA_GROUP):
                h = hk * GQA_GROUP + g
                w, _ = _attn_weights(q[:, h * HEAD_DIM:(h + 1) * HEAD_DIM], kcat, b_ref[h], valid, sink_ref[h])
                outs.append(_dot(w.astype(BF16), vcat))
        o_ref[...] = jnp.concatenate(outs, axis=1).astype(BF16)

    return pl.pallas_call(
        body, name="attn_fwd", grid=(S // BLOCK,), in_specs=_attn_specs(),
        out_specs=pl.BlockSpec((BLOCK, BRANCH), lambda n: (n, 0)),
        out_shape=jax.ShapeDtypeStruct((S, BRANCH), BF16), compiler_params=_cparams("parallel"),
    )(z, z, z, bias, sinks)


def attn_bwd(do, z, bias, sinks):
    S = z.shape[0]

    def body(do_ref, q_ref, kvc_ref, kvp_ref, b_ref, sink_ref, dq_ref, dc_ref, dp_ref, db_ref, ds_ref):
        first = pl.program_id(0) == 0

        @pl.when(first)
        def _():
            db_ref[...] = jnp.zeros_like(db_ref)
            ds_ref[...] = jnp.zeros_like(ds_ref)

        valid = _attn_valid(first)
        q = q_ref[...]
        kvc = kvc_ref[...]
        kvp = kvp_ref[...]
        dov = do_ref[...]
        dqs, dks, dvs = [], [], []
        for hk in range(N_KV_HEADS):
            kcat, vcat = _kv_heads(kvp, kvc, hk)
            dk = jnp.zeros((2 * BLOCK, HEAD_DIM), F32)
            dv = jnp.zeros((2 * BLOCK, HEAD_DIM), F32)
            for g in range(GQA_GROUP):
                h = hk * GQA_GROUP + g
                qh = q[:, h * HEAD_DIM:(h + 1) * HEAD_DIM]
                doh = dov[:, h * HEAD_DIM:(h + 1) * HEAD_DIM]
                w, wsink = _attn_weights(qh, kcat, b_ref[h], valid, sink_ref[h])
                dv += _dot_tn(w.astype(BF16), doh)
                dw = _dot_nt(doh, vcat)
                delta = jnp.sum(w * dw, axis=-1, keepdims=True)
                ds = w * (dw - delta)
                db_ref[h] += ds
                tot = jnp.sum(-wsink * delta, axis=0, keepdims=True)
                ds_ref[h:h + 1, :] += jnp.broadcast_to(tot, (1, BLOCK))
                dsb = (ds * ATTN_SCALE).astype(BF16)
                dqs.append(_dot(dsb, kcat))
                dk += _dot_tn(dsb, qh)
            dks.append(dk)
            dvs.append(dv)
        dq_ref[...] = jnp.concatenate(dqs, axis=1).astype(BF16)
        both = jnp.concatenate(dks + dvs, axis=1)
        dp_ref[...] = both[:BLOCK]
        dc_ref[...] = both[BLOCK:]

    blk = pl.BlockSpec((BLOCK, BRANCH), lambda n: (n, 0))
    kvb = pl.BlockSpec((BLOCK, KV_WIDTH), lambda n: (n, 0))
    return pl.pallas_call(
        body, name="attn_bwd", grid=(S // BLOCK,), in_specs=[blk] + _attn_specs(),
        out_specs=[blk, kvb, kvb, _full((N_Q_HEADS, BLOCK, 2 * BLOCK)), _full((N_Q_HEADS, BLOCK))],
        out_shape=[jax.ShapeDtypeStruct((S, BRANCH), BF16), jax.ShapeDtypeStruct((S, KV_WIDTH), F32),
                   jax.ShapeDtypeStruct((S, KV_WIDTH), F32), jax.ShapeDtypeStruct((N_Q_HEADS, BLOCK, 2 * BLOCK), F32),
                   jax.ShapeDtypeStruct((N_Q_HEADS, BLOCK), F32)],
        compiler_params=_cparams("arbitrary"),
    )(do, z, z, z, bias, sinks)


def kv_shift_add(dcur, dprev):
    S = dcur.shape[0]
    nt = S // TM
    per_tile = TM // BLOCK

    def body(c_ref, p_ref, n_ref, o_ref):
        nxt = jnp.where(pl.program_id(0) == nt - 1, 0.0, n_ref[...])
        o_ref[...] = (c_ref[...] + jnp.concatenate([p_ref[BLOCK:, :], nxt], axis=0)).astype(BF16)

    tile = pl.BlockSpec((TM, KV_WIDTH), lambda i: (i, 0))
    return pl.pallas_call(
        body, name="kv_shift_add", grid=(nt,),
        in_specs=[tile, tile,
                  pl.BlockSpec((BLOCK, KV_WIDTH), lambda i: (jnp.minimum((i + 1) * per_tile, S // BLOCK - 1), 0))],
        out_specs=tile, out_shape=jax.ShapeDtypeStruct((S, KV_WIDTH), BF16), compiler_params=_cparams("parallel"),
    )(dcur, dprev, dprev)


def _ssm_disc(lam_re, lam_im, log_dt, bt_re, bt_im):
    dt = jnp.exp(log_dt)
    mag = jnp.exp(lam_re * dt)
    ang = lam_im * dt
    a_re = mag * jnp.cos(ang)
    a_im = mag * jnp.sin(ang)
    den = lam_re * lam_re + lam_im * lam_im
    nr = a_re - 1.0
    coef_re = (nr * lam_re + a_im * lam_im) / den
    coef_im = (a_im * lam_re - nr * lam_im) / den
    bb_re = coef_re[:, None, :] * bt_re - coef_im[:, None, :] * bt_im
    bb_im = coef_re[:, None, :] * bt_im + coef_im[:, None, :] * bt_re
    return a_re, a_im, bb_re, bb_im


_GN = (SSM_GROUPS, SSM_STATE)
_GPN = (SSM_GROUPS, SSM_GROUP, SSM_STATE)


def ssm_disc_fwd(lam_re, lam_im, log_dt, bt_re, bt_im):
    def body(lr_ref, li_ref, dt_ref, br_ref, bi_ref, ar_ref, ai_ref, bbr_ref, bbi_ref):
        a_re, a_im, bb_re, bb_im = _ssm_disc(lr_ref[...], li_ref[...], dt_ref[...], br_ref[...], bi_ref[...])
        ar_ref[...] = a_re
        ai_ref[...] = a_im
        bbr_ref[...] = bb_re
        bbi_ref[...] = bb_im

    return pl.pallas_call(
        body, name="ssm_disc_fwd", grid=(1,),
        in_specs=[_full(_GN), _full(_GN), _full((SSM_GROUPS, 1)), _full(_GPN), _full(_GPN)],
        out_specs=[_full(_GN), _full(_GN), _full(_GPN), _full(_GPN)],
        out_shape=[jax.ShapeDtypeStruct(s, F32) for s in (_GN, _GN, _GPN, _GPN)],
        compiler_params=_cparams("arbitrary"),
    )(lam_re, lam_im, log_dt, bt_re, bt_im)


def ssm_disc_bwd(lam_re, lam_im, log_dt, bt_re, bt_im, da_re, da_im, dbb_re, dbb_im):
    def body(lr_ref, li_ref, dt_ref, br_ref, bi_ref, dar_ref, dai_ref, dbr_ref, dbi_ref, o_lr, o_li, o_dt, o_br, o_bi):
        prim = (lr_ref[...], li_ref[...], dt_ref[...], br_ref[...], bi_ref[...])
        _, vjp = jax.vjp(_ssm_disc, *prim)
        grads = vjp((dar_ref[...], dai_ref[...], dbr_ref[...], dbi_ref[...]))
        for r, v in zip((o_lr, o_li, o_dt, o_br, o_bi), grads):
            r[...] = v

    shapes = (_GN, _GN, (SSM_GROUPS, 1), _GPN, _GPN)
    return pl.pallas_call(
        body, name="ssm_disc_bwd", grid=(1,),
        in_specs=[_full(s) for s in shapes + (_GN, _GN, _GPN, _GPN)],
        out_specs=[_full(s) for s in shapes], out_shape=[jax.ShapeDtypeStruct(s, F32) for s in shapes],
        compiler_params=_cparams("arbitrary"),
    )(lam_re, lam_im, log_dt, bt_re, bt_im, da_re, da_im, dbb_re, dbb_im)


LANE_GROUPS = SSM_LANES // 128
SUB_GROUPS = SUB_ST // 128
_TM_SHAPE = (LANE_GROUPS, 128)


def _step_rows(t):
    return pl.ds(pl.multiple_of(t * LANE_GROUPS, LANE_GROUPS), LANE_GROUPS)


def _group_rows(j):
    return pl.ds(j, SCAN_T, stride=LANE_GROUPS)


def _store_sub(ref, j, val):
    for k in range(SUB_GROUPS):
        ref[_group_rows(j * SUB_GROUPS + k), :] = val[:, k * 128:(k + 1) * 128]


def _load_sub(ref, j):
    return jnp.concatenate([ref[_group_rows(j * SUB_GROUPS + k), :] for k in range(SUB_GROUPS)], axis=1)


_SUB_SHAPE_IN = (SSM_SUB, SUB_IN, SUB_ST)
_SUB_SHAPE_OUT = (SSM_SUB, SUB_ST, SUB_IN)


def ssm_fwd(z, bb_re, bb_im, ct_re, ct_im, a_re, a_im, d_skip, wglu):
    S = z.shape[0]
    cu = COL_U // BRANCH

    def body(u_ref, bbr_ref, bbi_ref, ctr_ref, cti_ref, ar_ref, ai_ref, d_ref, wg_ref,
             y_ref, ypre_ref, hr_ref, hi_ref, bur, bui, car_r, car_i):
        @pl.when(pl.program_id(0) == 0)
        def _():
            car_r[...] = jnp.zeros_like(car_r)
            car_i[...] = jnp.zeros_like(car_i)

        u = u_ref[...]
        for j in range(SSM_SUB):
            uj = u[:, j * SUB_IN:(j + 1) * SUB_IN]
            _store_sub(bur, j, _dot(uj, bbr_ref[j]))
            _store_sub(bui, j, _dot(uj, bbi_ref[j]))
        ar = ar_ref[...]
        ai = ai_ref[...]

        def step(t, carry):
            hr, hi = carry
            rows = _step_rows(t)
            nhr = ar * hr - ai * hi + bur[rows, :]
            nhi = ar * hi + ai * hr + bui[rows, :]
            hr_ref[rows, :] = nhr
            hi_ref[rows, :] = nhi
            return nhr, nhi

        hr, hi = lax.fori_loop(0, SCAN_T, step, (car_r[...], car_i[...]), unroll=8)
        car_r[...] = hr
        car_i[...] = hi
        ys = []
        for j in range(SSM_SUB):
            ys.append(_dot(_load_sub(hr_ref, j).astype(BF16), ctr_ref[j])
                      - _dot(_load_sub(hi_ref, j).astype(BF16), cti_ref[j]))
        ypre = jnp.concatenate(ys, axis=1) + d_ref[...] * u.astype(F32)
        ypre_ref[...] = ypre
        g = jax.nn.gelu(ypre)
        y_ref[...] = (g * jax.nn.sigmoid(_dot(g.astype(BF16), wg_ref[...]))).astype(BF16)

    row = pl.BlockSpec((SCAN_T, BRANCH), lambda i: (i, 0))
    st = pl.BlockSpec((SCAN_T * LANE_GROUPS, 128), lambda i: (i, 0))
    return pl.pallas_call(
        body, name="ssm_fwd", grid=(S // SCAN_T,),
        in_specs=[pl.BlockSpec((SCAN_T, BRANCH), lambda i: (i, cu)), _full(_SUB_SHAPE_IN), _full(_SUB_SHAPE_IN),
                  _full(_SUB_SHAPE_OUT), _full(_SUB_SHAPE_OUT), _full(_TM_SHAPE), _full(_TM_SHAPE), _full((1, BRANCH)),
                  _full((BRANCH, BRANCH))],
        out_specs=[row, row, st, st],
        out_shape=[jax.ShapeDtypeStruct((S, BRANCH), BF16), jax.ShapeDtypeStruct((S, BRANCH), F32),
                   jax.ShapeDtypeStruct((S * LANE_GROUPS, 128), F32), jax.ShapeDtypeStruct((S * LANE_GROUPS, 128), F32)],
        scratch_shapes=[pltpu.VMEM((SCAN_T * LANE_GROUPS, 128), F32), pltpu.VMEM((SCAN_T * LANE_GROUPS, 128), F32),
                        pltpu.VMEM(_TM_SHAPE, F32), pltpu.VMEM(_TM_SHAPE, F32)],
        compiler_params=_cparams("arbitrary"),
    )(z, bb_re, bb_im, ct_re, ct_im, a_re, a_im, d_skip, wglu)


def ssm_bwd(dy, z, ypre, h_re, h_im, bbt_re, bbt_im, c_re, c_im, a_re, a_im, d_skip, wglu, wglu_t):
    S = z.shape[0]
    nt = S // SCAN_T
    cu = COL_U // BRANCH

    def body(dy_ref, u_ref, ypre_ref, hr_ref, hi_ref, hpr_ref, hpi_ref, bbr_ref, bbi_ref, cr_ref, ci_ref, ar_ref, ai_ref,
             d_ref, wg_ref, wgt_ref,
             du_ref, dbbr_ref, dbbi_ref, dctr_ref, dcti_ref, dar_ref, dai_ref, dd_ref, dwg_ref,
             lr_scr, li_scr, car_r, car_i):
        step = pl.program_id(0)

        @pl.when(step == 0)
        def _():
            for r in (dbbr_ref, dbbi_ref, dctr_ref, dcti_ref, dar_ref, dai_ref, dd_ref, dwg_ref, car_r, car_i):
                r[...] = jnp.zeros_like(r)

        u = u_ref[...]
        uf = u.astype(F32)
        dyv = dy_ref[...].astype(F32)
        g, gelu_vjp = jax.vjp(jax.nn.gelu, ypre_ref[...])
        gb = g.astype(BF16)
        sg = jax.nn.sigmoid(_dot(gb, wg_ref[...]))
        dgl = (dyv * g * sg * (1.0 - sg)).astype(BF16)
        dwg_ref[...] += _dot_tn(gb, dgl)
        dg = dyv * sg + _dot(dgl, wgt_ref[...])
        dypre = gelu_vjp(dg)[0]
        dd_ref[...] += jnp.sum(dypre * uf, axis=0, keepdims=True)
        dyb = dypre.astype(BF16)
        for j in range(SSM_SUB):
            dyj = dyb[:, j * SUB_IN:(j + 1) * SUB_IN]
            _store_sub(lr_scr, j, _dot(dyj, cr_ref[j]))
            _store_sub(li_scr, j, -_dot(dyj, ci_ref[j]))
            dctr_ref[j] += _dot_tn(_load_sub(hr_ref, j).astype(BF16), dyj)
            dcti_ref[j] -= _dot_tn(_load_sub(hi_ref, j).astype(BF16), dyj)

        ar = ar_ref[...]
        ai = ai_ref[...]

        def adjoint(lr, li, rows):
            nlr = ar * lr + ai * li + lr_scr[rows, :]
            nli = ar * li - ai * lr + li_scr[rows, :]
            lr_scr[rows, :] = nlr
            li_scr[rows, :] = nli
            return nlr, nli

        def back(k, carry):
            lr, li, acc_r, acc_i = carry
            t = SCAN_T - 1 - k
            lr, li = adjoint(lr, li, _step_rows(t))
            hpr = hr_ref[_step_rows(t - 1), :]
            hpi = hi_ref[_step_rows(t - 1), :]
            return lr, li, acc_r + lr * hpr + li * hpi, acc_i + li * hpr - lr * hpi

        zero = jnp.zeros(_TM_SHAPE, F32)
        lr, li, acc_r, acc_i = lax.fori_loop(0, SCAN_T - 1, back, (car_r[...], car_i[...], zero, zero), unroll=8)
        lr, li = adjoint(lr, li, pl.ds(0, LANE_GROUPS))
        car_r[...] = lr
        car_i[...] = li
        first_tile = step == nt - 1
        hpr = jnp.where(first_tile, 0.0, hpr_ref[...])
        hpi = jnp.where(first_tile, 0.0, hpi_ref[...])
        dar_ref[...] += acc_r + lr * hpr + li * hpi
        dai_ref[...] += acc_i + li * hpr - lr * hpi

        dus = []
        for j in range(SSM_SUB):
            lrb = _load_sub(lr_scr, j).astype(BF16)
            lib = _load_sub(li_scr, j).astype(BF16)
            uj = u[:, j * SUB_IN:(j + 1) * SUB_IN]
            dus.append(_dot(lrb, bbr_ref[j]) + _dot(lib, bbi_ref[j]))
            dbbr_ref[j] += _dot_tn(uj, lrb)
            dbbi_ref[j] += _dot_tn(uj, lib)
        du_ref[...] = (jnp.concatenate(dus, axis=1) + dypre * d_ref[...]).astype(BF16)

    def rev(i):
        return nt - 1 - i

    row = pl.BlockSpec((SCAN_T, BRANCH), lambda i: (rev(i), 0))
    st = pl.BlockSpec((SCAN_T * LANE_GROUPS, 128), lambda i: (rev(i), 0))
    before = pl.BlockSpec(_TM_SHAPE, lambda i: (jnp.maximum(rev(i) * SCAN_T - 1, 0), 0))
    tm = _full(_TM_SHAPE)
    return pl.pallas_call(
        body, name="ssm_bwd", grid=(nt,),
        in_specs=[row, pl.BlockSpec((SCAN_T, BRANCH), lambda i: (rev(i), cu)), row, st, st, before, before,
                  _full(_SUB_SHAPE_OUT), _full(_SUB_SHAPE_OUT), _full(_SUB_SHAPE_IN), _full(_SUB_SHAPE_IN),
                  tm, tm, _full((1, BRANCH)), _full((BRANCH, BRANCH)), _full((BRANCH, BRANCH))],
        out_specs=[row, _full(_SUB_SHAPE_IN), _full(_SUB_SHAPE_IN), _full(_SUB_SHAPE_OUT), _full(_SUB_SHAPE_OUT),
                   tm, tm, _full((1, BRANCH)), _full((BRANCH, BRANCH))],
        out_shape=[jax.ShapeDtypeStruct((S, BRANCH), BF16), jax.ShapeDtypeStruct(_SUB_SHAPE_IN, F32),
                   jax.ShapeDtypeStruct(_SUB_SHAPE_IN, F32), jax.ShapeDtypeStruct(_SUB_SHAPE_OUT, F32),
                   jax.ShapeDtypeStruct(_SUB_SHAPE_OUT, F32), jax.ShapeDtypeStruct(_TM_SHAPE, F32),
                   jax.ShapeDtypeStruct(_TM_SHAPE, F32), jax.ShapeDtypeStruct((1, BRANCH), F32),
                   jax.ShapeDtypeStruct((BRANCH, BRANCH), F32)],
        scratch_shapes=[pltpu.VMEM((SCAN_T * LANE_GROUPS, 128), F32), pltpu.VMEM((SCAN_T * LANE_GROUPS, 128), F32),
                        pltpu.VMEM(_TM_SHAPE, F32), pltpu.VMEM(_TM_SHAPE, F32)],
        compiler_params=_cparams("arbitrary"),
    )(dy, z, ypre, h_re, h_im, h_re, h_im, bbt_re, bbt_im, c_re, c_im, a_re, a_im, d_skip, wglu, wglu_t)


def _blockdiag(x):
    gs = SSM_GROUPS // SSM_SUB
    x = x.reshape(SSM_SUB, gs, SSM_GROUP, SSM_STATE)
    eye = jnp.eye(gs, dtype=x.dtype)
    return (x[:, :, :, None, :] * eye[None, :, None, :, None]).reshape(SSM_SUB, SUB_IN, SUB_ST)


def _blockdiag_extract(x):
    gs = SSM_GROUPS // SSM_SUB
    x = x.reshape(SSM_SUB, gs, SSM_GROUP, gs, SSM_STATE)
    eye = jnp.eye(gs, dtype=x.dtype)
    return jnp.sum(x * eye[None, :, None, :, None], axis=3).reshape(SSM_GROUPS, SSM_GROUP, SSM_STATE)


def loss_head(x, g, target):
    S = x.shape[0]

    def body(x_ref, g_ref, t_ref, dx_ref, loss_ref, dg_ref):
        @pl.when(pl.program_id(0) == 0)
        def _():
            loss_ref[...] = jnp.zeros_like(loss_ref)
            dg_ref[...] = jnp.zeros_like(dg_ref)

        xv = x_ref[...]
        gv = g_ref[...]
        r = lax.rsqrt(jnp.mean(xv * xv, axis=-1, keepdims=True) + RMS_EPS)
        xhat = xv * r
        err = xhat * gv - t_ref[...]
        loss_ref[...] += jnp.sum((err * err).reshape(TM // 8, 8, D_MODEL), axis=0) * (0.5 / D_MODEL)
        dy = err * (1.0 / D_MODEL)
        dxhat = dy * gv
        dx_ref[...] = r * (dxhat - xhat * jnp.mean(dxhat * xhat, axis=-1, keepdims=True))
        dg_ref[...] += jnp.sum((dy * xhat).reshape(TM // 8, 8, D_MODEL), axis=0)

    row = pl.BlockSpec((TM, D_MODEL), lambda i: (i, 0))
    acc = _full((8, D_MODEL))
    return pl.pallas_call(
        body, name="loss_head", grid=(S // TM,), in_specs=[row, _full((1, D_MODEL)), row],
        out_specs=[row, acc, acc],
        out_shape=[jax.ShapeDtypeStruct((S, D_MODEL), F32), jax.ShapeDtypeStruct((8, D_MODEL), F32),
                   jax.ShapeDtypeStruct((8, D_MODEL), F32)],
        compiler_params=_cparams("arbitrary"),
    )(x, g, target)


def _x_spec():
    return pl.BlockSpec((TM, D_MODEL), lambda i, j: (i, 0))


def _g_spec():
    return pl.BlockSpec((1, D_MODEL), lambda i, j: (0, 0))


def _norm_prologue(x, g):
    h = _rms(x, g).astype(BF16)
    return h, h


def _cast_prologue(x):
    return (x.astype(BF16),)


def _swiglu_prologue(h1, h2):
    a = h1.astype(F32)
    act = (a * jax.nn.sigmoid(a) * h2.astype(F32)).astype(BF16)
    return act, act


def _concat_prologue(*pieces):
    return (jnp.concatenate(pieces, axis=1),)


def _ssm_consts(lw):
    a_re, a_im, bbt_re, bbt_im = ssm_disc_fwd(
        lw["ssm_lambda_re"], lw["ssm_lambda_im"], lw["ssm_log_dt"].reshape(SSM_GROUPS, 1), lw["bt_re"], lw["bt_im"])
    bb_re = _blockdiag(bbt_re).astype(BF16)
    bb_im = _blockdiag(bbt_im).astype(BF16)
    c_re = _blockdiag(lw["ssm_c_re"]).astype(BF16)
    c_im = _blockdiag(lw["ssm_c_im"]).astype(BF16)
    return dict(
        a_re=a_re.reshape(_TM_SHAPE), a_im=a_im.reshape(_TM_SHAPE),
        bb_re=bb_re, bb_im=bb_im, bbt_re=jnp.swapaxes(bb_re, 1, 2), bbt_im=jnp.swapaxes(bb_im, 1, 2),
        c_re=c_re, c_im=c_im, ct_re=jnp.swapaxes(c_re, 1, 2), ct_im=jnp.swapaxes(c_im, 1, 2))


def layer_fwd(x, lw, bias):
    z, h = fused_mm("in_proj", [x, lw["norm_mix"]], [_x_spec(), _g_spec()], _norm_prologue, lw["w_in"], tn=2944,
                    out_dtype=BF16, extras=((D_MODEL, BF16),))
    sc = _ssm_consts(lw)
    y_ssm, ypre, h_re, h_im = ssm_fwd(z, sc["bb_re"], sc["bb_im"], sc["ct_re"], sc["ct_im"], sc["a_re"], sc["a_im"],
                                      lw["ssm_d"], lw["ssm_w_glu"])
    y_conv = conv_fwd(z, lw["conv_w"])
    y_attn = attn_fwd(z, bias, lw["attn_sinks"])
    merged = merge_fwd(z, y_ssm, y_conv, y_attn, lw["w_branch"])
    x1 = fused_mm("out_proj", [merged], [_x_spec()], _cast_prologue, lw["w_out"], tn=512, out_dtype=F32, res=x)
    hf, hn1 = fused_mm("ffn_in", [x1, lw["norm_ffn"]], [_x_spec(), _g_spec()], _norm_prologue, lw["w_ffn_in"], tn=2816,
                       out_dtype=BF16, extras=((D_MODEL, BF16),))
    x2, act = fused_mm("ffn_out", [hf, hf], [_row_spec(FFN_HIDDEN, 0), _row_spec(FFN_HIDDEN, 1)], _swiglu_prologue,
                       lw["w_ffn_out"], tn=512, out_dtype=F32, res=x1, extras=((FFN_HIDDEN, BF16),))
    a_pre, hn2 = fused_mm("ple_gate", [x2, lw["norm_ple"]], [_x_spec(), _g_spec()], _norm_prologue, lw["w_ple_gate"],
                          tn=512, out_dtype=BF16, extras=((D_MODEL, BF16),))
    pp = fused_mm("ple_proj", [lw["p"]], [_row_spec(PLE_DIM)], _cast_prologue, lw["w_ple_proj"], tn=512, out_dtype=BF16)
    x3 = ple_combine(x2, a_pre, pp)
    res = dict(x=x, z=z, h=h, y_ssm=y_ssm, ypre=ypre, h_re=h_re, h_im=h_im, y_conv=y_conv, y_attn=y_attn, merged=merged,
               x1=x1, hf=hf, hn1=hn1, act=act, x2=x2, a_pre=a_pre, hn2=hn2, pp=pp)
    return x3, res


def layer_bwd(dx3, lw, res, bias):
    g = {}
    da, dpp = ple_bwd(dx3, res["a_pre"], res["pp"])
    g["w_ple_proj"] = mm_tn("d_w_ple_proj", lw["p"], dpp)
    g["w_ple_gate"] = mm_tn("d_w_ple_gate", res["hn2"], da)
    dhn2 = fused_mm("d_ple_gate", [da], [_x_spec()], _cast_prologue, lw["w_ple_gate_t"], tn=512, out_dtype=BF16)
    dx2, g["norm_ple"] = norm_bwd("ple_norm_bwd", dhn2, res["x2"], lw["norm_ple"], dx3)
    dact = fused_mm("d_ffn_out", [dx2], [_x_spec()], _cast_prologue, lw["w_ffn_out_t"], tn=1408, out_dtype=BF16)
    g["w_ffn_out"] = mm_tn("d_w_ffn_out", res["act"], dx2)
    dh1, dh2 = swiglu_bwd(dact, res["hf"])
    g["w_ffn_in"] = jnp.concatenate([mm_tn("d_w_ffn_in_a", res["hn1"], dh1), mm_tn("d_w_ffn_in_b", res["hn1"], dh2)],
                                    axis=1)
    dhn1 = fused_mm("d_ffn_in", [dh1, dh2], [_row_spec(FFN_HIDDEN), _row_spec(FFN_HIDDEN)], _concat_prologue,
                    lw["w_ffn_in_t"], tn=512, out_dtype=BF16)
    dx1, g["norm_ffn"] = norm_bwd("ffn_norm_bwd", dhn1, res["x1"], lw["norm_ffn"], dx2)
    dmerged = fused_mm("d_out_proj", [dx1], [_x_spec()], _cast_prologue, lw["w_out_t"], tn=512, out_dtype=BF16)
    g["w_out"] = mm_tn("d_w_out", res["merged"], dx1)
    z = res["z"]
    ys = (res["y_ssm"], res["y_conv"], res["y_attn"])
    dgates, dbs = merge_bwd(dmerged, z, *ys, lw["w_branch"])
    dys, dwb = [], []
    for r in range(3):
        dys.append(fused_mm(f"d_branch_{r}", [dbs[r]], [_x_spec()], _cast_prologue, lw["w_branch_t"][r], tn=512,
                            out_dtype=BF16))
        dwb.append(mm_tn(f"d_w_branch_{r}", ys[r], dbs[r]))
    g["w_branch"] = jnp.stack(dwb)
    sc = _ssm_consts(lw)
    (du, dbb_re, dbb_im, dct_re, dct_im, da_re, da_im, g["ssm_d"], g["ssm_w_glu"]) = ssm_bwd(
        dys[0], z, res["ypre"], res["h_re"], res["h_im"], sc["bbt_re"], sc["bbt_im"], sc["c_re"], sc["c_im"],
        sc["a_re"], sc["a_im"], lw["ssm_d"], lw["ssm_w_glu"], lw["ssm_w_glu_t"])
    g["ssm_c_re"] = _blockdiag_extract(jnp.swapaxes(dct_re, 1, 2))
    g["ssm_c_im"] = _blockdiag_extract(jnp.swapaxes(dct_im, 1, 2))
    (g["ssm_lambda_re"], g["ssm_lambda_im"], dlog_dt, g["bt_re"], g["bt_im"]) = ssm_disc_bwd(
        lw["ssm_lambda_re"], lw["ssm_lambda_im"], lw["ssm_log_dt"].reshape(SSM_GROUPS, 1), lw["bt_re"], lw["bt_im"],
        da_re.reshape(_GN), da_im.reshape(_GN),
        _blockdiag_extract(dbb_re), _blockdiag_extract(dbb_im))
    g["ssm_log_dt"] = dlog_dt.reshape(SSM_GROUPS)
    dconv, g["conv_w"] = conv_bwd(dys[1], z, lw["conv_w"])
    dq, dkv_cur, dkv_prev, g["dbias"], g["attn_sinks"] = attn_bwd(dys[2], z, bias, lw["attn_sinks"])
    dkv = kv_shift_add(dkv_cur, dkv_prev)
    pieces = [dgates[0], dgates[1], dgates[2], du, dconv, dq, dkv]
    widths = [D_MODEL, D_MODEL, D_MODEL, BRANCH, 3 * BRANCH, BRANCH, KV_WIDTH]
    g["w_in"] = jnp.concatenate([mm_tn(f"d_w_in_{k}", res["h"], pc) for k, pc in enumerate(pieces)], axis=1)
    dhn0 = fused_mm("d_in_proj", pieces, [_row_spec(w) for w in widths], _concat_prologue, lw["w_in_t"], tn=512,
                    out_dtype=BF16)
    dx0, g["norm_mix"] = norm_bwd("mix_norm_bwd", dhn0, res["x"], lw["norm_mix"], dx1)
    return dx0, g


def adamw(name, parts, w, m, v):
    n, R, C = parts.shape
    tr = _pick(R, (512, 256, 128, 64, 32, 16, 8))

    def body(p_ref, w_ref, m_ref, v_ref, g_ref, d_ref, nm_ref, nv_ref):
        gsum = p_ref[0].astype(F32)
        for k in range(1, n):
            gsum = gsum + p_ref[k].astype(F32)
        mn = ADAM_B1 * m_ref[...] + (1.0 - ADAM_B1) * gsum
        vn = ADAM_B2 * v_ref[...] + (1.0 - ADAM_B2) * jnp.square(gsum)
        m_hat = mn / (1.0 - ADAM_B1 ** ADAM_STEP)
        v_hat = vn / (1.0 - ADAM_B2 ** ADAM_STEP)
        g_ref[...] = gsum
        d_ref[...] = -ADAM_LR * (m_hat / (jnp.sqrt(v_hat) + ADAM_EPS) + ADAM_WD * w_ref[...])
        nm_ref[...] = mn
        nv_ref[...] = vn

    blk = pl.BlockSpec((tr, C), lambda i: (i, 0))
    return pl.pallas_call(
        body, name=name, grid=(R // tr,), in_specs=[pl.BlockSpec((n, tr, C), lambda i: (0, i, 0)), blk, blk, blk],
        out_specs=[blk] * 4, out_shape=[jax.ShapeDtypeStruct((R, C), F32)] * 4, compiler_params=_cparams("parallel"),
    )(parts, w, m, v)


_ANY = pl.BlockSpec(memory_space=pl.ANY)


def _coords():
    return lax.axis_index("x"), lax.axis_index("y"), lax.axis_index("c")


def _chip_peers(x, y):
    return [(1 - x, y), (x, 1 - y), (1 - x, 1 - y)]


def _comm_call(name, body, blks, out_shapes, copies_per_block):
    n = len(blks)
    return pl.pallas_call(
        functools.partial(body, n), name=name, in_specs=[_ANY] * n, out_specs=[_ANY] * n,
        out_shape=[jax.ShapeDtypeStruct(s, b.dtype) for s, b in zip(out_shapes, blks)],
        scratch_shapes=[pltpu.SemaphoreType.DMA((copies_per_block * n,)), pltpu.SemaphoreType.DMA((copies_per_block * n,))],
    )(*blks)


def gather_chips(name, blks):
    def body(n, *refs):
        x_refs, out_refs, send_sems, recv_sems = refs[:n], refs[n:2 * n], refs[2 * n], refs[2 * n + 1]
        x, y, c = _coords()
        me = 2 * x + y
        peers = _chip_peers(x, y)

        def copy(i, k, slot):
            return pltpu.make_async_remote_copy(
                src_ref=x_refs[i], dst_ref=out_refs[i].at[slot], send_sem=send_sems.at[3 * i + k],
                recv_sem=recv_sems.at[3 * i + k], device_id=(*peers[k], c), device_id_type=MESH)

        sends = [copy(i, k, me) for i in range(n) for k in range(3)]
        for cp in sends:
            cp.start()
        for i in range(n):
            for k, (px, py) in enumerate(peers):
                copy(i, k, 2 * px + py).wait_recv()
        for cp in sends:
            cp.wait_send()

    return _comm_call(name, body, blks, [(4,) + b.shape for b in blks], 3)


def gather_cores(name, blks):
    def body(n, *refs):
        x_refs, out_refs, send_sems, recv_sems = refs[:n], refs[n:2 * n], refs[2 * n], refs[2 * n + 1]
        x, y, c = _coords()

        def copy(i, slot):
            return pltpu.make_async_remote_copy(
                src_ref=x_refs[i], dst_ref=out_refs[i].at[slot], send_sem=send_sems.at[i], recv_sem=recv_sems.at[i],
                device_id=(x, y, 1 - c), device_id_type=MESH)

        sends = [copy(i, c) for i in range(n)]
        for cp in sends:
            cp.start()
        for i in range(n):
            copy(i, 1 - c).wait_recv()
        for cp in sends:
            cp.wait_send()

    return _comm_call(name, body, blks, [(2,) + b.shape for b in blks], 1)


def scatter_cores(name, blks):
    def body(n, *refs):
        x_refs, out_refs, send_sems, recv_sems = refs[:n], refs[n:2 * n], refs[2 * n], refs[2 * n + 1]
        x, y, c = _coords()
        sends = [pltpu.make_async_remote_copy(
            src_ref=x_refs[i].at[1 - c], dst_ref=out_refs[i], send_sem=send_sems.at[i], recv_sem=recv_sems.at[i],
            device_id=(x, y, 1 - c), device_id_type=MESH) for i in range(n)]
        for cp in sends:
            cp.start()
        for cp in sends:
            cp.wait_recv()
        for cp in sends:
            cp.wait_send()

    return _comm_call(name, body, blks, [b.shape[1:] for b in blks], 1)


def scatter_chips(name, blks):
    def body(n, *refs):
        x_refs, out_refs, send_sems, recv_sems = refs[:n], refs[n:2 * n], refs[2 * n], refs[2 * n + 1]
        x, y, c = _coords()
        me = 2 * x + y
        peers = _chip_peers(x, y)

        def copy(i, k, src_slot, dst_slot):
            return pltpu.make_async_remote_copy(
                src_ref=x_refs[i].at[src_slot], dst_ref=out_refs[i].at[dst_slot], send_sem=send_sems.at[3 * i + k],
                recv_sem=recv_sems.at[3 * i + k], device_id=(*peers[k], c), device_id_type=MESH)

        sends = [copy(i, k, 2 * px + py, me) for i in range(n) for k, (px, py) in enumerate(peers)]
        for cp in sends:
            cp.start()
        for i in range(n):
            for k, (px, py) in enumerate(peers):
                copy(i, k, me, 2 * px + py).wait_recv()
        for cp in sends:
            cp.wait_send()

    return _comm_call(name, body, blks, [b.shape for b in blks], 3)


def _put_slot(buf, block, idx):
    return lax.dynamic_update_slice(buf, block[None].astype(buf.dtype), (idx,) + (0,) * block.ndim)


def pair_sum(name, mine, theirs):
    _, R, C = mine.shape
    tr = _pick(R, (1024, 512, 256, 128, 64, 32, 16))
    c_idx = lax.axis_index("c").astype(jnp.int32).reshape(1)

    def body(c_ref, a_ref, b_ref, o_ref):
        o_ref[...] = (a_ref[0].astype(F32) + b_ref[...].astype(F32)).astype(BF16)

    return pl.pallas_call(
        body, name=name,
        grid_spec=pltpu.PrefetchScalarGridSpec(
            num_scalar_prefetch=1, grid=(R // tr,),
            in_specs=[pl.BlockSpec((1, tr, C), lambda i, c: (c[0], i, 0)), pl.BlockSpec((tr, C), lambda i, c: (i, 0))],
            out_specs=pl.BlockSpec((tr, C), lambda i, c: (i, 0))),
        out_shape=jax.ShapeDtypeStruct(theirs.shape, BF16), compiler_params=_cparams("parallel"),
    )(c_idx, mine, theirs)


SHARDED = {
    "w_in": ((D_MODEL, IN_WIDTH), 2), "ssm_w_glu": ((BRANCH, BRANCH), 1), "conv_w": ((3, BRANCH), 2),
    "w_branch": ((3, BRANCH, D_MODEL), 3), "w_out": ((D_MODEL, D_MODEL), 1), "w_ffn_in": ((D_MODEL, 2 * FFN_HIDDEN), 2),
    "w_ffn_out": ((FFN_HIDDEN, D_MODEL), 1), "w_ple_gate": ((D_MODEL, D_MODEL), 1), "w_ple_proj": ((PLE_DIM, D_MODEL), 2),
}
SMALL = ["rel_bias", "norm_mix", "ssm_lambda_re", "ssm_lambda_im", "ssm_b_re", "ssm_b_im", "ssm_c_re", "ssm_c_im", "ssm_d",
         "ssm_log_dt", "attn_sinks", "norm_ffn", "norm_ple", "norm_final"]
WEIGHTS = ["rel_bias", "norm_mix", "w_in", "ssm_lambda_re", "ssm_lambda_im", "ssm_b_re", "ssm_b_im", "ssm_c_re", "ssm_c_im",
           "ssm_d", "ssm_log_dt", "ssm_w_glu", "conv_w", "attn_sinks", "w_branch", "w_out", "norm_ffn", "w_ffn_in",
           "w_ffn_out", "norm_ple", "w_ple_gate", "w_ple_proj", "norm_final"]


def _pad_to(flat, n):
    return jnp.pad(flat, [(0, 0)] * (flat.ndim - 1) + [(0, n - flat.shape[-1])])


def _unshard(g8, name):
    _, axis = SHARDED[name]
    shard = g8.shape[2:]
    b = g8.reshape((2, 2, 2) + shard)
    b = jnp.moveaxis(b, (1, 2, 0), (axis, axis + 1, axis + 2))
    full = list(shard)
    full[axis] *= N_DEV
    return b.reshape(full)


def _shard_split(full, name):
    _, axis = SHARDED[name]
    dims = list(full.shape)
    dims[axis:axis + 1] = [2, 2, 2, dims[axis] // N_DEV]
    b = jnp.moveaxis(full.reshape(dims), (axis, axis + 1, axis + 2), (1, 2, 0))
    return b.reshape((2, 4) + b.shape[3:])


def _small_sizes(shapes):
    return [-(-int(np.prod(shapes[n])) // 128) * 128 for n in SMALL]


def pack_small(vals, shapes, extra):
    segs = [_pad_to(vals[n].reshape(-1).astype(F32), s) for n, s in zip(SMALL, _small_sizes(shapes))]
    segs.append(_pad_to(extra.reshape(-1), 128))
    flat = jnp.concatenate(segs)
    rows = -(-flat.shape[0] // (128 * 8)) * 8
    return _pad_to(flat, rows * 128).reshape(rows, 128)


def unpack_small(packed, shapes):
    flat = packed.reshape(-1)
    out, off = {}, 0
    for n, s in zip(SMALL, _small_sizes(shapes)):
        out[n] = flat[off:off + int(np.prod(shapes[n]))].reshape(shapes[n])
        off += s
    return out, flat[off]


def _layer_weights(full, small):
    w_in = full["w_in"]
    w_in_p = jnp.concatenate([w_in[..., 2816:], w_in[..., :2560], w_in[..., 2560:2816]], axis=-1)
    return dict(
        w_in=w_in_p, w_in_t=jnp.swapaxes(w_in_p, 1, 2),
        ssm_w_glu=full["ssm_w_glu"], ssm_w_glu_t=jnp.swapaxes(full["ssm_w_glu"], 1, 2),
        conv_w=full["conv_w"],
        w_branch=full["w_branch"], w_branch_t=jnp.swapaxes(full["w_branch"], 2, 3),
        w_out=full["w_out"], w_out_t=jnp.swapaxes(full["w_out"], 1, 2),
        w_ffn_in=full["w_ffn_in"], w_ffn_in_t=jnp.swapaxes(full["w_ffn_in"], 1, 2),
        w_ffn_out=full["w_ffn_out"], w_ffn_out_t=jnp.swapaxes(full["w_ffn_out"], 1, 2),
        w_ple_gate=full["w_ple_gate"], w_ple_gate_t=jnp.swapaxes(full["w_ple_gate"], 1, 2),
        w_ple_proj=full["w_ple_proj"],
        norm_mix=small["norm_mix"][:, None, :], norm_ffn=small["norm_ffn"][:, None, :],
        norm_ple=small["norm_ple"][:, None, :],
        ssm_lambda_re=small["ssm_lambda_re"], ssm_lambda_im=small["ssm_lambda_im"], ssm_log_dt=small["ssm_log_dt"],
        bt_re=jnp.swapaxes(small["ssm_b_re"], 2, 3), bt_im=jnp.swapaxes(small["ssm_b_im"], 2, 3),
        ssm_c_re=small["ssm_c_re"], ssm_c_im=small["ssm_c_im"], ssm_d=small["ssm_d"][:, None, :],
        attn_sinks=small["attn_sinks"],
    )


def local_step(x, p, full, small, target):
    bias = rel_bias_fwd(small["rel_bias"].T).reshape(N_Q_HEADS, BLOCK, 2 * BLOCK)
    lw = _layer_weights(full, small)
    lw["p"] = p

    layers = [{k: a[i] for k, a in lw.items()} for i in range(DEPTH)]
    res = []
    for lw_i in layers:
        x, res_i = layer_fwd(x, lw_i, bias)
        res.append(res_i)
    grad_x, loss_parts, dg_final = loss_head(x, small["norm_final"][None, :], target)
    per_layer = [None] * DEPTH
    for i in reversed(range(DEPTH)):
        grad_x, per_layer[i] = layer_bwd(grad_x, layers[i], res[i], bias)
    g = {k: jnp.stack([gl[k] for gl in per_layer]) for k in per_layer[0]}
    drel = rel_bias_bwd(g["dbias"].reshape(DEPTH, N_Q_HEADS, BLOCK * 2 * BLOCK)).T
    w_in_g = g["w_in"]
    gfull = dict(
        w_in=jnp.concatenate([w_in_g[..., 3072:5632], w_in_g[..., 5632:], w_in_g[..., :3072]], axis=-1),
        ssm_w_glu=g["ssm_w_glu"], conv_w=jnp.sum(g["conv_w"].reshape(DEPTH, 3, 8, BRANCH), axis=2),
        w_branch=g["w_branch"], w_out=g["w_out"], w_ffn_in=g["w_ffn_in"], w_ffn_out=g["w_ffn_out"],
        w_ple_gate=g["w_ple_gate"], w_ple_proj=g["w_ple_proj"])
    gsmall = dict(
        rel_bias=drel, norm_mix=jnp.sum(g["norm_mix"], axis=1), ssm_lambda_re=g["ssm_lambda_re"],
        ssm_lambda_im=g["ssm_lambda_im"], ssm_b_re=jnp.swapaxes(g["bt_re"], 2, 3), ssm_b_im=jnp.swapaxes(g["bt_im"], 2, 3),
        ssm_c_re=g["ssm_c_re"], ssm_c_im=g["ssm_c_im"], ssm_d=g["ssm_d"][:, 0, :], ssm_log_dt=g["ssm_log_dt"],
        attn_sinks=g["attn_sinks"][:, :, 0], norm_ffn=jnp.sum(g["norm_ffn"], axis=1), norm_ple=jnp.sum(g["norm_ple"], axis=1),
        norm_final=jnp.sum(dg_final, axis=0))
    return loss_parts, grad_x, gfull, gsmall


def kernel(x, p, rel_bias, norm_mix, w_in, ssm_lambda_re, ssm_lambda_im, ssm_b_re, ssm_b_im, ssm_c_re, ssm_c_im, ssm_d, ssm_log_dt, ssm_w_glu, conv_w, attn_sinks, w_branch, w_out, norm_ffn, w_ffn_in, w_ffn_out, norm_ple, w_ple_gate, w_ple_proj, norm_final, loss_target, m_rel_bias, m_norm_mix, m_w_in, m_ssm_lambda_re, m_ssm_lambda_im, m_ssm_b_re, m_ssm_b_im, m_ssm_c_re, m_ssm_c_im, m_ssm_d, m_ssm_log_dt, m_ssm_w_glu, m_conv_w, m_attn_sinks, m_w_branch, m_w_out, m_norm_ffn, m_w_ffn_in, m_w_ffn_out, m_norm_ple, m_w_ple_gate, m_w_ple_proj, m_norm_final, v_rel_bias, v_norm_mix, v_w_in, v_ssm_lambda_re, v_ssm_lambda_im, v_ssm_b_re, v_ssm_b_im, v_ssm_c_re, v_ssm_c_im, v_ssm_d, v_ssm_log_dt, v_ssm_w_glu, v_conv_w, v_attn_sinks, v_w_branch, v_w_out, v_norm_ffn, v_w_ffn_in, v_w_ffn_out, v_norm_ple, v_w_ple_gate, v_w_ple_proj, v_norm_final):
    args = dict(locals())
    w = {n: args[n] for n in WEIGHTS}
    m = {n: args["m_" + n] for n in WEIGHTS}
    v = {n: args["v_" + n] for n in WEIGHTS}
    shapes = {n: w[n].shape for n in SMALL}

    sharded = list(SHARDED)
    x_i, y_i, c_i = _coords()
    chip = 2 * x_i + y_i

    def all_gather(tag, blks):
        g4 = gather_chips(f"gather_{tag}_chips", blks)
        g4 = [_put_slot(g, b, chip) for g, b in zip(g4, blks)]
        g8 = gather_cores(f"gather_{tag}_cores", g4)
        return [_put_slot(g, b, c_i) for g, b in zip(g8, g4)]

    mine = [w[n] if n == "conv_w" else w[n].astype(BF16) for n in sharded]
    full = {n: _unshard(g, n) for n, g in zip(sharded, all_gather("w", mine))}

    loss_parts, grad_x, gfull, gsmall = local_step(x[0], p[:, 0], full, {n: w[n] for n in SMALL}, loss_target[0])

    gp = [_shard_split(gfull[n], n).astype(BF16) for n in sharded]
    from_sibling = scatter_cores("scatter_g_cores", gp)
    sums = [pair_sum("pair_sum_" + n, a.reshape(2, -1, a.shape[-1]), b.reshape(-1, b.shape[-1])).reshape(b.shape)
            for n, a, b in zip(sharded, gp, from_sibling)]
    received = scatter_chips("scatter_g_chips", sums)

    outs = ({}, {}, {}, {})
    for name, r, s in zip(sharded, received, sums):
        parts = _put_slot(r, lax.dynamic_index_in_dim(s, chip, 0, keepdims=False), chip)
        cols = parts.shape[-1]
        res4 = adamw("adamw_" + name, parts.reshape(4, -1, cols), w[name].reshape(-1, cols), m[name].reshape(-1, cols),
                     v[name].reshape(-1, cols))
        for d, o in zip(outs, res4):
            d[name] = o.reshape(w[name].shape)

    small_local = pack_small(gsmall, shapes, jnp.sum(loss_parts))
    small_all = all_gather("s", [small_local])[0]
    zero = jnp.zeros((1,), F32)
    res4 = adamw("adamw_small", small_all.reshape(N_DEV, small_local.shape[0], 128),
                 pack_small({n: w[n] for n in SMALL}, shapes, zero), pack_small({n: m[n] for n in SMALL}, shapes, zero),
                 pack_small({n: v[n] for n in SMALL}, shapes, zero))
    loss = None
    for d, r in zip(outs, res4):
        vals, extra = unpack_small(r, shapes)
        d.update(vals)
        if loss is None:
            loss = extra

    return (loss, grad_x[None], *[d[n] for d in outs for n in WEIGHTS])
```

```python
import functools
import math

import numpy as np
import jax
import jax.numpy as jnp
from jax import lax
from jax.experimental import pallas as pl
from jax.experimental.pallas import tpu as pltpu

F32 = jnp.float32
BF16 = jnp.bfloat16
MESH = pl.DeviceIdType.MESH

D_MODEL = 1024
DEPTH = 4
PLE_DIM = 256
BRANCH = 512
SSM_GROUPS = 32
SSM_GROUP = 16
SSM_STATE = 64
SSM_LANES = SSM_GROUPS * SSM_STATE
SSM_SUB = 4
SUB_IN = BRANCH // SSM_SUB
SUB_ST = SSM_LANES // SSM_SUB
HEAD_DIM = 64
N_Q_HEADS = 8
N_KV_HEADS = 2
GQA_GROUP = 4
KV_WIDTH = 2 * N_KV_HEADS * HEAD_DIM
WINDOW = 128
BLOCK = 128
ATTN_SCALE = 1.0 / math.sqrt(HEAD_DIM)
REL_BUCKETS = 32
REL_MAX_DIST = 128
FFN_HIDDEN = 2816
RMS_EPS = 1e-6
IN_WIDTH = 5888
N_DEV = 8

ADAM_LR = 0.001
ADAM_B1 = 0.9
ADAM_B2 = 0.999
ADAM_EPS = 1e-08
ADAM_WD = 0.01
ADAM_STEP = 10

COL_U = 3072
COL_KV = 5632
NEG = -1e30

SCAN_T = 256
TM = 512
TM_W = 1024
VMEM_LIMIT = 52 * 1024 * 1024


def _cparams(*sem):
    return pltpu.CompilerParams(dimension_semantics=sem, vmem_limit_bytes=VMEM_LIMIT)


def _full(shape):
    n = len(shape)
    return pl.BlockSpec(shape, lambda *_: (0,) * n)


def _pick(n, cands):
    for c in cands:
        if n % c == 0:
            return c
    return n


def _dot(a, b):
    return jnp.dot(a, b, preferred_element_type=F32)


def _dot_tn(a, b):
    return lax.dot_general(a, b, (((0,), (0,)), ((), ())), preferred_element_type=F32)


def _dot_nt(a, b):
    return lax.dot_general(a, b, (((1,), (1,)), ((), ())), preferred_element_type=F32)


def _rms(x, g):
    r = lax.rsqrt(jnp.mean(x * x, axis=-1, keepdims=True) + RMS_EPS)
    return x * r * g


def fused_mm(name, ins, in_specs, prologue, w, *, tn, out_dtype, res=None, extras=()):
    S = ins[0].shape[0]
    K, N = w.shape
    tn = min(tn, N)
    n_in, n_ex = len(ins), len(extras)

    def body(*refs):
        in_refs = refs[:n_in]
        w_ref = refs[n_in]
        pos = n_in + 1
        res_ref = None
        if res is not None:
            res_ref = refs[pos]
            pos += 1
        o_ref = refs[pos]
        ex_refs = refs[pos + 1:pos + 1 + n_ex]
        a_scr = refs[-1]
        out = prologue(*[r[...] for r in in_refs])
        a_scr[...] = out[0]
        for r, e in zip(ex_refs, out[1:]):
            r[...] = e.astype(r.dtype)
        for j in range(N // tn):
            cs = slice(j * tn, (j + 1) * tn)
            acc = _dot(a_scr[...], w_ref[:, cs])
            if res_ref is not None:
                acc = acc + res_ref[:, cs]
            o_ref[:, cs] = acc.astype(o_ref.dtype)

    specs = list(in_specs) + [pl.BlockSpec((K, N), lambda i: (0, 0), pipeline_mode=pl.Buffered(1))]
    args = list(ins) + [w]
    if res is not None:
        specs.append(pl.BlockSpec((TM, N), lambda i: (i, 0)))
        args.append(res)
    out_shape = [jax.ShapeDtypeStruct((S, N), out_dtype)]
    out_specs = [pl.BlockSpec((TM, N), lambda i: (i, 0))]
    for cols, dt in extras:
        out_shape.append(jax.ShapeDtypeStruct((S, cols), dt))
        out_specs.append(pl.BlockSpec((TM, cols), lambda i: (i, 0)))
    outs = pl.pallas_call(
        body, name=name, grid=(S // TM,), in_specs=specs, out_specs=out_specs, out_shape=out_shape,
        scratch_shapes=[pltpu.VMEM((TM, K), BF16)], compiler_params=_cparams("parallel"),
    )(*args)
    return outs if n_ex else outs[0]


def _row_spec(cols, blk=0):
    return pl.BlockSpec((TM, cols), lambda i: (i, blk))


def mm_tn(name, a, b):
    S, K = a.shape
    N = b.shape[1]
    tk = _pick(K, (1024, 1408, 512, 256))
    tn = _pick(N, (1024, 1408, 1536, 512, 256))

    def body(a_ref, b_ref, o_ref):
        @pl.when(pl.program_id(2) == 0)
        def _():
            o_ref[...] = jnp.zeros_like(o_ref)

        o_ref[...] += _dot_tn(a_ref[...].astype(BF16), b_ref[...].astype(BF16))

    return pl.pallas_call(
        body, name=name, grid=(K // tk, N // tn, S // TM_W),
        in_specs=[pl.BlockSpec((TM_W, tk), lambda k, n, s: (s, k)), pl.BlockSpec((TM_W, tn), lambda k, n, s: (s, n))],
        out_specs=pl.BlockSpec((tk, tn), lambda k, n, s: (k, n)),
        out_shape=jax.ShapeDtypeStruct((K, N), F32),
        compiler_params=_cparams("parallel", "parallel", "arbitrary"),
    )(a, b)


def norm_bwd(name, dh, x, g, dres):
    S = x.shape[0]

    def body(dh_ref, x_ref, g_ref, dres_ref, dx_ref, dg_ref):
        @pl.when(pl.program_id(0) == 0)
        def _():
            dg_ref[...] = jnp.zeros_like(dg_ref)

        xv = x_ref[...]
        dhv = dh_ref[...].astype(F32)
        r = lax.rsqrt(jnp.mean(xv * xv, axis=-1, keepdims=True) + RMS_EPS)
        xhat = xv * r
        dxhat = dhv * g_ref[...]
        dx = r * (dxhat - xhat * jnp.mean(dxhat * xhat, axis=-1, keepdims=True))
        dx_ref[...] = dres_ref[...] + dx
        dg_ref[...] += jnp.sum((dhv * xhat).reshape(TM // 8, 8, D_MODEL), axis=0)

    row = pl.BlockSpec((TM, D_MODEL), lambda i: (i, 0))
    return pl.pallas_call(
        body, name=name, grid=(S // TM,), in_specs=[row, row, _full((1, D_MODEL)), row],
        out_specs=[row, _full((8, D_MODEL))],
        out_shape=[jax.ShapeDtypeStruct((S, D_MODEL), F32), jax.ShapeDtypeStruct((8, D_MODEL), F32)],
        compiler_params=_cparams("arbitrary"),
    )(dh, x, g, dres)


def swiglu_bwd(dact, hf):
    S = hf.shape[0]

    def body(da_ref, h1_ref, h2_ref, o1_ref, o2_ref):
        h1 = h1_ref[...].astype(F32)
        h2 = h2_ref[...].astype(F32)
        da = da_ref[...].astype(F32)
        sg = jax.nn.sigmoid(h1)
        o1_ref[...] = (da * h2 * sg * (1.0 + h1 * (1.0 - sg))).astype(BF16)
        o2_ref[...] = (da * h1 * sg).astype(BF16)

    tn = 1408
    nn = FFN_HIDDEN // tn
    return pl.pallas_call(
        body, name="swiglu_bwd", grid=(S // TM, nn),
        in_specs=[pl.BlockSpec((TM, tn), lambda i, j: (i, j)), pl.BlockSpec((TM, tn), lambda i, j: (i, j)),
                  pl.BlockSpec((TM, tn), lambda i, j: (i, nn + j))],
        out_specs=[pl.BlockSpec((TM, tn), lambda i, j: (i, j)), pl.BlockSpec((TM, tn), lambda i, j: (i, j))],
        out_shape=[jax.ShapeDtypeStruct((S, FFN_HIDDEN), BF16)] * 2,
        compiler_params=_cparams("parallel", "parallel"),
    )(dact, hf, hf)


def ple_combine(x2, a_pre, pp):
    S = x2.shape[0]

    def body(x_ref, a_ref, p_ref, o_ref):
        o_ref[...] = x_ref[...] + jax.nn.sigmoid(a_ref[...].astype(F32)) * p_ref[...].astype(F32)

    row = pl.BlockSpec((TM, D_MODEL), lambda i: (i, 0))
    return pl.pallas_call(
        body, name="ple_combine", grid=(S // TM,), in_specs=[row, row, row], out_specs=row,
        out_shape=jax.ShapeDtypeStruct((S, D_MODEL), F32), compiler_params=_cparams("parallel"),
    )(x2, a_pre, pp)


def ple_bwd(dx3, a_pre, pp):
    S = dx3.shape[0]

    def body(dx_ref, a_ref, p_ref, da_ref, dpp_ref):
        dx = dx_ref[...]
        pg = jax.nn.sigmoid(a_ref[...].astype(F32))
        dpp_ref[...] = (dx * pg).astype(BF16)
        da_ref[...] = (dx * p_ref[...].astype(F32) * pg * (1.0 - pg)).astype(BF16)

    row = pl.BlockSpec((TM, D_MODEL), lambda i: (i, 0))
    return pl.pallas_call(
        body, name="ple_bwd", grid=(S // TM,), in_specs=[row, row, row], out_specs=[row, row],
        out_shape=[jax.ShapeDtypeStruct((S, D_MODEL), BF16)] * 2, compiler_params=_cparams("parallel"),
    )(dx3, a_pre, pp)


def _gate_specs(tn, nn):
    return [pl.BlockSpec((TM, tn), functools.partial(lambda i, j, r: (i, r * nn + j), r=r)) for r in range(3)]


def merge_fwd(z, y_ssm, y_conv, y_attn, wb):
    S = z.shape[0]
    tn = 512
    nn = D_MODEL // tn

    def body(g0, g1, g2, y0, y1, y2, w_ref, o_ref):
        acc = jnp.zeros((TM, tn), F32)
        for r, (g_ref, y_ref) in enumerate(((g0, y0), (g1, y1), (g2, y2))):
            acc += jax.nn.sigmoid(g_ref[...].astype(F32)) * _dot(y_ref[...], w_ref[r])
        o_ref[...] = acc.astype(BF16)

    y_spec = pl.BlockSpec((TM, BRANCH), lambda i, j: (i, 0))
    return pl.pallas_call(
        body, name="merge_fwd", grid=(S // TM, nn),
        in_specs=_gate_specs(tn, nn) + [y_spec] * 3 + [pl.BlockSpec((3, BRANCH, tn), lambda i, j: (0, 0, j))],
        out_specs=pl.BlockSpec((TM, tn), lambda i, j: (i, j)),
        out_shape=jax.ShapeDtypeStruct((S, D_MODEL), BF16), compiler_params=_cparams("parallel", "parallel"),
    )(z, z, z, y_ssm, y_conv, y_attn, wb)


def merge_bwd(dmerged, z, y_ssm, y_conv, y_attn, wb):
    S = z.shape[0]
    tn = 512
    nn = D_MODEL // tn

    def body(dm_ref, g0, g1, g2, y0, y1, y2, w_ref, dg0, dg1, dg2, db0, db1, db2):
        dm = dm_ref[...].astype(F32)
        for r, (g_ref, y_ref, dg_ref, db_ref) in enumerate(((g0, y0, dg0, db0), (g1, y1, dg1, db1), (g2, y2, dg2, db2))):
            sg = jax.nn.sigmoid(g_ref[...].astype(F32))
            b = _dot(y_ref[...], w_ref[r])
            dg_ref[...] = (dm * b * sg * (1.0 - sg)).astype(BF16)
            db_ref[...] = (dm * sg).astype(BF16)

    y_spec = pl.BlockSpec((TM, BRANCH), lambda i, j: (i, 0))
    outs = pl.pallas_call(
        body, name="merge_bwd", grid=(S // TM, nn),
        in_specs=[pl.BlockSpec((TM, tn), lambda i, j: (i, j))] + _gate_specs(tn, nn) + [y_spec] * 3
        + [pl.BlockSpec((3, BRANCH, tn), lambda i, j: (0, 0, j))],
        out_specs=[pl.BlockSpec((TM, tn), lambda i, j: (i, j))] * 6,
        out_shape=[jax.ShapeDtypeStruct((S, D_MODEL), BF16)] * 6, compiler_params=_cparams("parallel", "parallel"),
    )(dmerged, z, z, z, y_ssm, y_conv, y_attn, wb)
    return outs[:3], outs[3:]


def _shift_down(v, halo, k):
    rolled = pltpu.roll(v, k, 0)
    h = pltpu.roll(halo, k, 0)
    row = lax.broadcasted_iota(jnp.int32, v.shape, 0)
    head = jnp.concatenate([h, jnp.zeros((v.shape[0] - 8, v.shape[1]), v.dtype)], axis=0)
    return jnp.where(row < k, head, rolled)


def _shift_up(v, halo, k):
    n = v.shape[0]
    rolled = pltpu.roll(v, n - k, 0)
    h = pltpu.roll(halo, 8 - k, 0)
    row = lax.broadcasted_iota(jnp.int32, v.shape, 0)
    tail = jnp.concatenate([jnp.zeros((n - 8, v.shape[1]), v.dtype), h], axis=0)
    return jnp.where(row >= n - k, tail, rolled)


def _conv_specs():
    rb = TM // 8
    c0 = COL_U // BRANCH

    def cur(k):
        return pl.BlockSpec((TM, BRANCH), lambda i: (i, c0 + k))

    def prev(k):
        return pl.BlockSpec((8, BRANCH), lambda i: (jnp.maximum(i * rb - 1, 0), c0 + k))

    return [cur(1), cur(2), cur(3), prev(2), prev(3)]


def conv_fwd(z, conv_w):
    S = z.shape[0]

    def body(cb_ref, cc_ref, cx_ref, pc_ref, px_ref, w_ref, o_ref):
        first = pl.program_id(0) == 0
        v = cc_ref[...].astype(F32) * cx_ref[...].astype(F32)
        pv = jnp.where(first, 0.0, pc_ref[...].astype(F32) * px_ref[...].astype(F32))
        w = w_ref[...]
        y = w[2:3] * v + w[1:2] * _shift_down(v, pv, 1) + w[0:1] * _shift_down(v, pv, 2)
        o_ref[...] = (cb_ref[...].astype(F32) * y).astype(BF16)

    return pl.pallas_call(
        body, name="conv_fwd", grid=(S // TM,), in_specs=_conv_specs() + [_full((3, BRANCH))],
        out_specs=pl.BlockSpec((TM, BRANCH), lambda i: (i, 0)),
        out_shape=jax.ShapeDtypeStruct((S, BRANCH), BF16), compiler_params=_cparams("parallel"),
    )(z, z, z, z, z, conv_w)


def conv_bwd(dy, z, conv_w):
    S = z.shape[0]
    rb = TM // 8
    nt = S // TM
    c0 = COL_U // BRANCH

    def body(dy_ref, cb_ref, cc_ref, cx_ref, pc_ref, px_ref, ndy_ref, ncb_ref, w_ref, o_ref, dw_ref):
        i = pl.program_id(0)

        @pl.when(i == 0)
        def _():
            dw_ref[...] = jnp.zeros_like(dw_ref)

        cb = cb_ref[...].astype(F32)
        cc = cc_ref[...].astype(F32)
        cx = cx_ref[...].astype(F32)
        dyv = dy_ref[...].astype(F32)
        v = cc * cx
        pv = jnp.where(i == 0, 0.0, pc_ref[...].astype(F32) * px_ref[...].astype(F32))
        v1 = _shift_down(v, pv, 1)
        v2 = _shift_down(v, pv, 2)
        w = w_ref[...]
        conv = w[2:3] * v + w[1:2] * v1 + w[0:1] * v2
        dc = dyv * cb
        ndc = jnp.where(i == nt - 1, 0.0, ndy_ref[...].astype(F32) * ncb_ref[...].astype(F32))
        dv = w[2:3] * dc + w[1:2] * _shift_up(dc, ndc, 1) + w[0:1] * _shift_up(dc, ndc, 2)
        o_ref[:, 0:BRANCH] = (dyv * conv).astype(BF16)
        o_ref[:, BRANCH:2 * BRANCH] = (dv * cx).astype(BF16)
        o_ref[:, 2 * BRANCH:3 * BRANCH] = (dv * cc).astype(BF16)
        for k, vk in enumerate((v2, v1, v)):
            dw_ref[8 * k:8 * k + 8, :] += jnp.sum((dc * vk).reshape(rb, 8, BRANCH), axis=0)

    nxt = jnp.minimum

    return pl.pallas_call(
        body, name="conv_bwd", grid=(nt,),
        in_specs=[pl.BlockSpec((TM, BRANCH), lambda i: (i, 0))] + _conv_specs()
        + [pl.BlockSpec((8, BRANCH), lambda i: (nxt((i + 1) * rb, S // 8 - 1), 0)),
           pl.BlockSpec((8, BRANCH), lambda i: (nxt((i + 1) * rb, S // 8 - 1), c0 + 1)),
           _full((3, BRANCH))],
        out_specs=[pl.BlockSpec((TM, 3 * BRANCH), lambda i: (i, 0)), _full((24, BRANCH))],
        out_shape=[jax.ShapeDtypeStruct((S, 3 * BRANCH), BF16), jax.ShapeDtypeStruct((24, BRANCH), F32)],
        compiler_params=_cparams("arbitrary"),
    )(dy, z, z, z, z, z, dy, z, conv_w)


def _bucket_onehot_t():
    qi = np.arange(BLOCK)[:, None]
    kj = np.arange(2 * BLOCK)[None, :]
    dist = np.clip(qi + BLOCK - kj, 0, REL_MAX_DIST - 1)
    exact = REL_BUCKETS // 2
    df = np.maximum(dist, 1).astype(np.float32)
    large = exact + (np.log(df / np.float32(exact)) / np.float32(math.log(REL_MAX_DIST / exact))
                     * np.float32(REL_BUCKETS - exact)).astype(np.int32)
    large = np.minimum(large, REL_BUCKETS - 1)
    bucket = np.where(dist < exact, dist, large).reshape(-1)
    return (np.arange(REL_BUCKETS)[:, None] == bucket[None, :]).astype(np.float32)


def rel_bias_fwd(rel_bias_t):
    n = BLOCK * 2 * BLOCK

    def body(r_ref, oh_ref, o_ref):
        o_ref[...] = jnp.dot(r_ref[...], oh_ref[...], precision=lax.Precision.HIGHEST, preferred_element_type=F32)

    return pl.pallas_call(
        body, name="rel_bias_fwd", grid=(1,), in_specs=[_full((N_Q_HEADS, REL_BUCKETS)), _full((REL_BUCKETS, n))],
        out_specs=_full((N_Q_HEADS, n)), out_shape=jax.ShapeDtypeStruct((N_Q_HEADS, n), F32),
        compiler_params=_cparams("arbitrary"),
    )(rel_bias_t, jnp.asarray(_bucket_onehot_t()))


def rel_bias_bwd(dbias):
    n_l = dbias.shape[0]
    n = BLOCK * 2 * BLOCK

    def body(d_ref, oh_ref, o_ref):
        tot = d_ref[0]
        for l in range(1, n_l):
            tot = tot + d_ref[l]
        o_ref[...] = lax.dot_general(tot, oh_ref[...], (((1,), (1,)), ((), ())), precision=lax.Precision.HIGHEST,
                                     preferred_element_type=F32)

    return pl.pallas_call(
        body, name="rel_bias_bwd", grid=(1,), in_specs=[_full((n_l, N_Q_HEADS, n)), _full((REL_BUCKETS, n))],
        out_specs=_full((N_Q_HEADS, REL_BUCKETS)), out_shape=jax.ShapeDtypeStruct((N_Q_HEADS, REL_BUCKETS), F32),
        compiler_params=_cparams("arbitrary"),
    )(dbias, jnp.asarray(_bucket_onehot_t()))


def _attn_valid(first):
    qi = lax.broadcasted_iota(jnp.int32, (BLOCK, 2 * BLOCK), 0)
    kj = lax.broadcasted_iota(jnp.int32, (BLOCK, 2 * BLOCK), 1)
    dist = qi + BLOCK - kj
    return (dist >= 0) & (dist < WINDOW) & (jnp.logical_not(first) | (kj >= BLOCK))


def _attn_weights(qh, kcat, bias_h, valid, sink):
    s = _dot_nt(qh, kcat) * ATTN_SCALE + bias_h
    s = jnp.where(valid, s, NEG)
    m = jnp.maximum(jnp.max(s, axis=-1, keepdims=True), sink)
    p = jnp.exp(s - m)
    esink = jnp.exp(sink - m)
    inv = 1.0 / (jnp.sum(p, axis=-1, keepdims=True) + esink)
    return p * inv, esink * inv


def _kv_heads(kvp, kvc, hk):
    ks = slice(hk * HEAD_DIM, (hk + 1) * HEAD_DIM)
    vs = slice(KV_WIDTH // 2 + hk * HEAD_DIM, KV_WIDTH // 2 + (hk + 1) * HEAD_DIM)
    return jnp.concatenate([kvp[:, ks], kvc[:, ks]], axis=0), jnp.concatenate([kvp[:, vs], kvc[:, vs]], axis=0)


def _attn_specs():
    cq = (COL_U + 4 * BRANCH) // BRANCH
    ckv = COL_KV // KV_WIDTH
    return [pl.BlockSpec((BLOCK, BRANCH), lambda n: (n, cq)),
            pl.BlockSpec((BLOCK, KV_WIDTH), lambda n: (n, ckv)),
            pl.BlockSpec((BLOCK, KV_WIDTH), lambda n: (jnp.maximum(n - 1, 0), ckv)),
            _full((N_Q_HEADS, BLOCK, 2 * BLOCK)),
            pl.BlockSpec(memory_space=pltpu.SMEM)]


def attn_fwd(z, bias, sinks):
    S = z.shape[0]

    def body(q_ref, kvc_ref, kvp_ref, b_ref, sink_ref, o_ref):
        valid = _attn_valid(pl.program_id(0) == 0)
        q = q_ref[...]
        kvc = kvc_ref[...]
        kvp = kvp_ref[...]
        outs = []
        for hk in range(N_KV_HEADS):
            kcat, vcat = _kv_heads(kvp, kvc, hk)
            for g in range(GQA_GROUP):
                h = hk * GQA_GROUP + g
                w, _ = _attn_weights(q[:, h * HEAD_DIM:(h + 1) * HEAD_DIM], kcat, b_ref[h], valid, sink_ref[h])
                outs.append(_dot(w.astype(BF16), vcat))
        o_ref[...] = jnp.concatenate(outs, axis=1).astype(BF16)

    return pl.pallas_call(
        body, name="attn_fwd", grid=(S // BLOCK,), in_specs=_attn_specs(),
        out_specs=pl.BlockSpec((BLOCK, BRANCH), lambda n: (n, 0)),
        out_shape=jax.ShapeDtypeStruct((S, BRANCH), BF16), compiler_params=_cparams("parallel"),
    )(z, z, z, bias, sinks)


def attn_bwd(do, z, bias, sinks):
    S = z.shape[0]

    def body(do_ref, q_ref, kvc_ref, kvp_ref, b_ref, sink_ref, dq_ref, dc_ref, dp_ref, db_ref, ds_ref):
        first = pl.program_id(0) == 0

        @pl.when(first)
        def _():
            db_ref[...] = jnp.zeros_like(db_ref)
            ds_ref[...] = jnp.zeros_like(ds_ref)

        valid = _attn_valid(first)
        q = q_ref[...]
        kvc = kvc_ref[...]
        kvp = kvp_ref[...]
        dov = do_ref[...]
        dqs, dks, dvs = [], [], []
        for hk in range(N_KV_HEADS):
            kcat, vcat = _kv_heads(kvp, kvc, hk)
            dk = jnp.zeros((2 * BLOCK, HEAD_DIM), F32)
            dv = jnp.zeros((2 * BLOCK, HEAD_DIM), F32)
            for g in range(GQA_GROUP):
                h = hk * GQA_GROUP + g
                qh = q[:, h * HEAD_DIM:(h + 1) * HEAD_DIM]
                doh = dov[:, h * HEAD_DIM:(h + 1) * HEAD_DIM]
                w, wsink = _attn_weights(qh, kcat, b_ref[h], valid, sink_ref[h])
                dv += _dot_tn(w.astype(BF16), doh)
                dw = _dot_nt(doh, vcat)
                delta = jnp.sum(w * dw, axis=-1, keepdims=True)
                ds = w * (dw - delta)
                db_ref[h] += ds
                tot = jnp.sum(-wsink * delta, axis=0, keepdims=True)
                ds_ref[h:h + 1, :] += jnp.broadcast_to(tot, (1, BLOCK))
                dsb = (ds * ATTN_SCALE).astype(BF16)
                dqs.append(_dot(dsb, kcat))
                dk += _dot_tn(dsb, qh)
            dks.append(dk)
            dvs.append(dv)
        dq_ref[...] = jnp.concatenate(dqs, axis=1).astype(BF16)
        both = jnp.concatenate(dks + dvs, axis=1)
        dp_ref[...] = both[:BLOCK]
        dc_ref[...] = both[BLOCK:]

    blk = pl.BlockSpec((BLOCK, BRANCH), lambda n: (n, 0))
    kvb = pl.BlockSpec((BLOCK, KV_WIDTH), lambda n: (n, 0))
    return pl.pallas_call(
        body, name="attn_bwd", grid=(S // BLOCK,), in_specs=[blk] + _attn_specs(),
        out_specs=[blk, kvb, kvb, _full((N_Q_HEADS, BLOCK, 2 * BLOCK)), _full((N_Q_HEADS, BLOCK))],
        out_shape=[jax.ShapeDtypeStruct((S, BRANCH), BF16), jax.ShapeDtypeStruct((S, KV_WIDTH), F32),
                   jax.ShapeDtypeStruct((S, KV_WIDTH), F32), jax.ShapeDtypeStruct((N_Q_HEADS, BLOCK, 2 * BLOCK), F32),
                   jax.ShapeDtypeStruct((N_Q_HEADS, BLOCK), F32)],
        compiler_params=_cparams("arbitrary"),
    )(do, z, z, z, bias, sinks)


def kv_shift_add(dcur, dprev):
    S = dcur.shape[0]
    nt = S // TM
    per_tile = TM // BLOCK

    def body(c_ref, p_ref, n_ref, o_ref):
        nxt = jnp.where(pl.program_id(0) == nt - 1, 0.0, n_ref[...])
        o_ref[...] = (c_ref[...] + jnp.concatenate([p_ref[BLOCK:, :], nxt], axis=0)).astype(BF16)

    tile = pl.BlockSpec((TM, KV_WIDTH), lambda i: (i, 0))
    return pl.pallas_call(
        body, name="kv_shift_add", grid=(nt,),
        in_specs=[tile, tile,
                  pl.BlockSpec((BLOCK, KV_WIDTH), lambda i: (jnp.minimum((i + 1) * per_tile, S // BLOCK - 1), 0))],
        out_specs=tile, out_shape=jax.ShapeDtypeStruct((S, KV_WIDTH), BF16), compiler_params=_cparams("parallel"),
    )(dcur, dprev, dprev)


def _ssm_disc(lam_re, lam_im, log_dt, bt_re, bt_im):
    dt = jnp.exp(log_dt)
    mag = jnp.exp(lam_re * dt)
    ang = lam_im * dt
    a_re = mag * jnp.cos(ang)
    a_im = mag * jnp.sin(ang)
    den = lam_re * lam_re + lam_im * lam_im
    nr = a_re - 1.0
    coef_re = (nr * lam_re + a_im * lam_im) / den
    coef_im = (a_im * lam_re - nr * lam_im) / den
    bb_re = coef_re[:, None, :] * bt_re - coef_im[:, None, :] * bt_im
    bb_im = coef_re[:, None, :] * bt_im + coef_im[:, None, :] * bt_re
    return a_re, a_im, bb_re, bb_im


_GN = (SSM_GROUPS, SSM_STATE)
_GPN = (SSM_GROUPS, SSM_GROUP, SSM_STATE)


def ssm_disc_fwd(lam_re, lam_im, log_dt, bt_re, bt_im):
    def body(lr_ref, li_ref, dt_ref, br_ref, bi_ref, ar_ref, ai_ref, bbr_ref, bbi_ref):
        a_re, a_im, bb_re, bb_im = _ssm_disc(lr_ref[...], li_ref[...], dt_ref[...], br_ref[...], bi_ref[...])
        ar_ref[...] = a_re
        ai_ref[...] = a_im
        bbr_ref[...] = bb_re
        bbi_ref[...] = bb_im

    return pl.pallas_call(
        body, name="ssm_disc_fwd", grid=(1,),
        in_specs=[_full(_GN), _full(_GN), _full((SSM_GROUPS, 1)), _full(_GPN), _full(_GPN)],
        out_specs=[_full(_GN), _full(_GN), _full(_GPN), _full(_GPN)],
        out_shape=[jax.ShapeDtypeStruct(s, F32) for s in (_GN, _GN, _GPN, _GPN)],
        compiler_params=_cparams("arbitrary"),
    )(lam_re, lam_im, log_dt, bt_re, bt_im)


def ssm_disc_bwd(lam_re, lam_im, log_dt, bt_re, bt_im, da_re, da_im, dbb_re, dbb_im):
    def body(lr_ref, li_ref, dt_ref, br_ref, bi_ref, dar_ref, dai_ref, dbr_ref, dbi_ref, o_lr, o_li, o_dt, o_br, o_bi):
        prim = (lr_ref[...], li_ref[...], dt_ref[...], br_ref[...], bi_ref[...])
        _, vjp = jax.vjp(_ssm_disc, *prim)
        grads = vjp((dar_ref[...], dai_ref[...], dbr_ref[...], dbi_ref[...]))
        for r, v in zip((o_lr, o_li, o_dt, o_br, o_bi), grads):
            r[...] = v

    shapes = (_GN, _GN, (SSM_GROUPS, 1), _GPN, _GPN)
    return pl.pallas_call(
        body, name="ssm_disc_bwd", grid=(1,),
        in_specs=[_full(s) for s in shapes + (_GN, _GN, _GPN, _GPN)],
        out_specs=[_full(s) for s in shapes], out_shape=[jax.ShapeDtypeStruct(s, F32) for s in shapes],
        compiler_params=_cparams("arbitrary"),
    )(lam_re, lam_im, log_dt, bt_re, bt_im, da_re, da_im, dbb_re, dbb_im)


LANE_GROUPS = SSM_LANES // 128
SUB_GROUPS = SUB_ST // 128
_TM_SHAPE = (LANE_GROUPS, 128)


def _step_rows(t):
    return pl.ds(pl.multiple_of(t * LANE_GROUPS, LANE_GROUPS), LANE_GROUPS)


def _group_rows(j):
    return pl.ds(j, SCAN_T, stride=LANE_GROUPS)


def _store_sub(ref, j, val):
    for k in range(SUB_GROUPS):
        ref[_group_rows(j * SUB_GROUPS + k), :] = val[:, k * 128:(k + 1) * 128]


def _load_sub(ref, j):
    return jnp.concatenate([ref[_group_rows(j * SUB_GROUPS + k), :] for k in range(SUB_GROUPS)], axis=1)


_SUB_SHAPE_IN = (SSM_SUB, SUB_IN, SUB_ST)
_SUB_SHAPE_OUT = (SSM_SUB, SUB_ST, SUB_IN)


def ssm_fwd(z, bb_re, bb_im, ct_re, ct_im, a_re, a_im, d_skip, wglu):
    S = z.shape[0]
    cu = COL_U // BRANCH

    def body(u_ref, bbr_ref, bbi_ref, ctr_ref, cti_ref, ar_ref, ai_ref, d_ref, wg_ref,
             y_ref, ypre_ref, hr_ref, hi_ref, bur, bui, car_r, car_i):
        @pl.when(pl.program_id(0) == 0)
        def _():
            car_r[...] = jnp.zeros_like(car_r)
            car_i[...] = jnp.zeros_like(car_i)

        u = u_ref[...]
        for j in range(SSM_SUB):
            uj = u[:, j * SUB_IN:(j + 1) * SUB_IN]
            _store_sub(bur, j, _dot(uj, bbr_ref[j]))
            _store_sub(bui, j, _dot(uj, bbi_ref[j]))
        ar = ar_ref[...]
        ai = ai_ref[...]

        def step(t, carry):
            hr, hi = carry
            rows = _step_rows(t)
            nhr = ar * hr - ai * hi + bur[rows, :]
            nhi = ar * hi + ai * hr + bui[rows, :]
            hr_ref[rows, :] = nhr
            hi_ref[rows, :] = nhi
            return nhr, nhi

        hr, hi = lax.fori_loop(0, SCAN_T, step, (car_r[...], car_i[...]), unroll=8)
        car_r[...] = hr
        car_i[...] = hi
        ys = []
        for j in range(SSM_SUB):
            ys.append(_dot(_load_sub(hr_ref, j).astype(BF16), ctr_ref[j])
                      - _dot(_load_sub(hi_ref, j).astype(BF16), cti_ref[j]))
        ypre = jnp.concatenate(ys, axis=1) + d_ref[...] * u.astype(F32)
        ypre_ref[...] = ypre
        g = jax.nn.gelu(ypre)
        y_ref[...] = (g * jax.nn.sigmoid(_dot(g.astype(BF16), wg_ref[...]))).astype(BF16)

    row = pl.BlockSpec((SCAN_T, BRANCH), lambda i: (i, 0))
    st = pl.BlockSpec((SCAN_T * LANE_GROUPS, 128), lambda i: (i, 0))
    return pl.pallas_call(
        body, name="ssm_fwd", grid=(S // SCAN_T,),
        in_specs=[pl.BlockSpec((SCAN_T, BRANCH), lambda i: (i, cu)), _full(_SUB_SHAPE_IN), _full(_SUB_SHAPE_IN),
                  _full(_SUB_SHAPE_OUT), _full(_SUB_SHAPE_OUT), _full(_TM_SHAPE), _full(_TM_SHAPE), _full((1, BRANCH)),
                  _full((BRANCH, BRANCH))],
        out_specs=[row, row, st, st],
        out_shape=[jax.ShapeDtypeStruct((S, BRANCH), BF16), jax.ShapeDtypeStruct((S, BRANCH), F32),
                   jax.ShapeDtypeStruct((S * LANE_GROUPS, 128), F32), jax.ShapeDtypeStruct((S * LANE_GROUPS, 128), F32)],
        scratch_shapes=[pltpu.VMEM((SCAN_T * LANE_GROUPS, 128), F32), pltpu.VMEM((SCAN_T * LANE_GROUPS, 128), F32),
                        pltpu.VMEM(_TM_SHAPE, F32), pltpu.VMEM(_TM_SHAPE, F32)],
        compiler_params=_cparams("arbitrary"),
    )(z, bb_re, bb_im, ct_re, ct_im, a_re, a_im, d_skip, wglu)


def ssm_bwd(dy, z, ypre, h_re, h_im, bbt_re, bbt_im, c_re, c_im, a_re, a_im, d_skip, wglu, wglu_t):
    S = z.shape[0]
    nt = S // SCAN_T
    cu = COL_U // BRANCH

    def body(dy_ref, u_ref, ypre_ref, hr_ref, hi_ref, hpr_ref, hpi_ref, bbr_ref, bbi_ref, cr_ref, ci_ref, ar_ref, ai_ref,
             d_ref, wg_ref, wgt_ref,
             du_ref, dbbr_ref, dbbi_ref, dctr_ref, dcti_ref, dar_ref, dai_ref, dd_ref, dwg_ref,
             lr_scr, li_scr, car_r, car_i):
        step = pl.program_id(0)

        @pl.when(step == 0)
        def _():
            for r in (dbbr_ref, dbbi_ref, dctr_ref, dcti_ref, dar_ref, dai_ref, dd_ref, dwg_ref, car_r, car_i):
                r[...] = jnp.zeros_like(r)

        u = u_ref[...]
        uf = u.astype(F32)
        dyv = dy_ref[...].astype(F32)
        g, gelu_vjp = jax.vjp(jax.nn.gelu, ypre_ref[...])
        gb = g.astype(BF16)
        sg = jax.nn.sigmoid(_dot(gb, wg_ref[...]))
        dgl = (dyv * g * sg * (1.0 - sg)).astype(BF16)
        dwg_ref[...] += _dot_tn(gb, dgl)
        dg = dyv * sg + _dot(dgl, wgt_ref[...])
        dypre = gelu_vjp(dg)[0]
        dd_ref[...] += jnp.sum(dypre * uf, axis=0, keepdims=True)
        dyb = dypre.astype(BF16)
        for j in range(SSM_SUB):
            dyj = dyb[:, j * SUB_IN:(j + 1) * SUB_IN]
            _store_sub(lr_scr, j, _dot(dyj, cr_ref[j]))
            _store_sub(li_scr, j, -_dot(dyj, ci_ref[j]))
            dctr_ref[j] += _dot_tn(_load_sub(hr_ref, j).astype(BF16), dyj)
            dcti_ref[j] -= _dot_tn(_load_sub(hi_ref, j).astype(BF16), dyj)

        ar = ar_ref[...]
        ai = ai_ref[...]

        def adjoint(lr, li, rows):
            nlr = ar * lr + ai * li + lr_scr[rows, :]
            nli = ar * li - ai * lr + li_scr[rows, :]
            lr_scr[rows, :] = nlr
            li_scr[rows, :] = nli
            return nlr, nli

        def back(k, carry):
            lr, li, acc_r, acc_i = carry
            t = SCAN_T - 1 - k
            lr, li = adjoint(lr, li, _step_rows(t))
            hpr = hr_ref[_step_rows(t - 1), :]
            hpi = hi_ref[_step_rows(t - 1), :]
            return lr, li, acc_r + lr * hpr + li * hpi, acc_i + li * hpr - lr * hpi

        zero = jnp.zeros(_TM_SHAPE, F32)
        lr, li, acc_r, acc_i = lax.fori_loop(0, SCAN_T - 1, back, (car_r[...], car_i[...], zero, zero), unroll=8)
        lr, li = adjoint(lr, li, pl.ds(0, LANE_GROUPS))
        car_r[...] = lr
        car_i[...] = li
        first_tile = step == nt - 1
        hpr = jnp.where(first_tile, 0.0, hpr_ref[...])
        hpi = jnp.where(first_tile, 0.0, hpi_ref[...])
        dar_ref[...] += acc_r + lr * hpr + li * hpi
        dai_ref[...] += acc_i + li * hpr - lr * hpi

        dus = []
        for j in range(SSM_SUB):
            lrb = _load_sub(lr_scr, j).astype(BF16)
            lib = _load_sub(li_scr, j).astype(BF16)
            uj = u[:, j * SUB_IN:(j + 1) * SUB_IN]
            dus.append(_dot(lrb, bbr_ref[j]) + _dot(lib, bbi_ref[j]))
            dbbr_ref[j] += _dot_tn(uj, lrb)
            dbbi_ref[j] += _dot_tn(uj, lib)
        du_ref[...] = (jnp.concatenate(dus, axis=1) + dypre * d_ref[...]).astype(BF16)

    def rev(i):
        return nt - 1 - i

    row = pl.BlockSpec((SCAN_T, BRANCH), lambda i: (rev(i), 0))
    st = pl.BlockSpec((SCAN_T * LANE_GROUPS, 128), lambda i: (rev(i), 0))
    before = pl.BlockSpec(_TM_SHAPE, lambda i: (jnp.maximum(rev(i) * SCAN_T - 1, 0), 0))
    tm = _full(_TM_SHAPE)
    return pl.pallas_call(
        body, name="ssm_bwd", grid=(nt,),
        in_specs=[row, pl.BlockSpec((SCAN_T, BRANCH), lambda i: (rev(i), cu)), row, st, st, before, before,
                  _full(_SUB_SHAPE_OUT), _full(_SUB_SHAPE_OUT), _full(_SUB_SHAPE_IN), _full(_SUB_SHAPE_IN),
                  tm, tm, _full((1, BRANCH)), _full((BRANCH, BRANCH)), _full((BRANCH, BRANCH))],
        out_specs=[row, _full(_SUB_SHAPE_IN), _full(_SUB_SHAPE_IN), _full(_SUB_SHAPE_OUT), _full(_SUB_SHAPE_OUT),
                   tm, tm, _full((1, BRANCH)), _full((BRANCH, BRANCH))],
        out_shape=[jax.ShapeDtypeStruct((S, BRANCH), BF16), jax.ShapeDtypeStruct(_SUB_SHAPE_IN, F32),
                   jax.ShapeDtypeStruct(_SUB_SHAPE_IN, F32), jax.ShapeDtypeStruct(_SUB_SHAPE_OUT, F32),
                   jax.ShapeDtypeStruct(_SUB_SHAPE_OUT, F32), jax.ShapeDtypeStruct(_TM_SHAPE, F32),
                   jax.ShapeDtypeStruct(_TM_SHAPE, F32), jax.ShapeDtypeStruct((1, BRANCH), F32),
                   jax.ShapeDtypeStruct((BRANCH, BRANCH), F32)],
        scratch_shapes=[pltpu.VMEM((SCAN_T * LANE_GROUPS, 128), F32), pltpu.VMEM((SCAN_T * LANE_GROUPS, 128), F32),
                        pltpu.VMEM(_TM_SHAPE, F32), pltpu.VMEM(_TM_SHAPE, F32)],
        compiler_params=_cparams("arbitrary"),
    )(dy, z, ypre, h_re, h_im, h_re, h_im, bbt_re, bbt_im, c_re, c_im, a_re, a_im, d_skip, wglu, wglu_t)


def _blockdiag(x):
    gs = SSM_GROUPS // SSM_SUB
    x = x.reshape(SSM_SUB, gs, SSM_GROUP, SSM_STATE)
    eye = jnp.eye(gs, dtype=x.dtype)
    return (x[:, :, :, None, :] * eye[None, :, None, :, None]).reshape(SSM_SUB, SUB_IN, SUB_ST)


def _blockdiag_extract(x):
    gs = SSM_GROUPS // SSM_SUB
    x = x.reshape(SSM_SUB, gs, SSM_GROUP, gs, SSM_STATE)
    eye = jnp.eye(gs, dtype=x.dtype)
    return jnp.sum(x * eye[None, :, None, :, None], axis=3).reshape(SSM_GROUPS, SSM_GROUP, SSM_STATE)


def loss_head(x, g, target):
    S = x.shape[0]

    def body(x_ref, g_ref, t_ref, dx_ref, loss_ref, dg_ref):
        @pl.when(pl.program_id(0) == 0)
        def _():
            loss_ref[...] = jnp.zeros_like(loss_ref)
            dg_ref[...] = jnp.zeros_like(dg_ref)

        xv = x_ref[...]
        gv = g_ref[...]
        r = lax.rsqrt(jnp.mean(xv * xv, axis=-1, keepdims=True) + RMS_EPS)
        xhat = xv * r
        err = xhat * gv - t_ref[...]
        loss_ref[...] += jnp.sum((err * err).reshape(TM // 8, 8, D_MODEL), axis=0) * (0.5 / D_MODEL)
        dy = err * (1.0 / D_MODEL)
        dxhat = dy * gv
        dx_ref[...] = r * (dxhat - xhat * jnp.mean(dxhat * xhat, axis=-1, keepdims=True))
        dg_ref[...] += jnp.sum((dy * xhat).reshape(TM // 8, 8, D_MODEL), axis=0)

    row = pl.BlockSpec((TM, D_MODEL), lambda i: (i, 0))
    acc = _full((8, D_MODEL))
    return pl.pallas_call(
        body, name="loss_head", grid=(S // TM,), in_specs=[row, _full((1, D_MODEL)), row],
        out_specs=[row, acc, acc],
        out_shape=[jax.ShapeDtypeStruct((S, D_MODEL), F32), jax.ShapeDtypeStruct((8, D_MODEL), F32),
                   jax.ShapeDtypeStruct((8, D_MODEL), F32)],
        compiler_params=_cparams("arbitrary"),
    )(x, g, target)


def _x_spec():
    return pl.BlockSpec((TM, D_MODEL), lambda i: (i, 0))


def _g_spec():
    return pl.BlockSpec((1, D_MODEL), lambda i: (0, 0))


def _norm_prologue(x, g):
    h = _rms(x, g).astype(BF16)
    return h, h


def _cast_prologue(x):
    return (x.astype(BF16),)


def _swiglu_prologue(h1, h2):
    a = h1.astype(F32)
    act = (a * jax.nn.sigmoid(a) * h2.astype(F32)).astype(BF16)
    return act, act


def _concat_prologue(*pieces):
    return (jnp.concatenate(pieces, axis=1),)


def _ssm_consts(lw):
    a_re, a_im, bbt_re, bbt_im = ssm_disc_fwd(
        lw["ssm_lambda_re"], lw["ssm_lambda_im"], lw["ssm_log_dt"].reshape(SSM_GROUPS, 1), lw["bt_re"], lw["bt_im"])
    bb_re = _blockdiag(bbt_re).astype(BF16)
    bb_im = _blockdiag(bbt_im).astype(BF16)
    c_re = _blockdiag(lw["ssm_c_re"]).astype(BF16)
    c_im = _blockdiag(lw["ssm_c_im"]).astype(BF16)
    return dict(
        a_re=a_re.reshape(_TM_SHAPE), a_im=a_im.reshape(_TM_SHAPE),
        bb_re=bb_re, bb_im=bb_im, bbt_re=jnp.swapaxes(bb_re, 1, 2), bbt_im=jnp.swapaxes(bb_im, 1, 2),
        c_re=c_re, c_im=c_im, ct_re=jnp.swapaxes(c_re, 1, 2), ct_im=jnp.swapaxes(c_im, 1, 2))


def layer_fwd(x, lw, bias):
    z, h = fused_mm("in_proj", [x, lw["norm_mix"]], [_x_spec(), _g_spec()], _norm_prologue, lw["w_in"], tn=2944,
                    out_dtype=BF16, extras=((D_MODEL, BF16),))
    sc = _ssm_consts(lw)
    y_ssm, ypre, h_re, h_im = ssm_fwd(z, sc["bb_re"], sc["bb_im"], sc["ct_re"], sc["ct_im"], sc["a_re"], sc["a_im"],
                                      lw["ssm_d"], lw["ssm_w_glu"])
    y_conv = conv_fwd(z, lw["conv_w"])
    y_attn = attn_fwd(z, bias, lw["attn_sinks"])
    merged = merge_fwd(z, y_ssm, y_conv, y_attn, lw["w_branch"])
    x1 = fused_mm("out_proj", [merged], [_x_spec()], _cast_prologue, lw["w_out"], tn=1024, out_dtype=F32, res=x)
    hf, hn1 = fused_mm("ffn_in", [x1, lw["norm_ffn"]], [_x_spec(), _g_spec()], _norm_prologue, lw["w_ffn_in"], tn=2816,
                       out_dtype=BF16, extras=((D_MODEL, BF16),))
    x2, act = fused_mm("ffn_out", [hf, hf], [_row_spec(FFN_HIDDEN, 0), _row_spec(FFN_HIDDEN, 1)], _swiglu_prologue,
                       lw["w_ffn_out"], tn=1024, out_dtype=F32, res=x1, extras=((FFN_HIDDEN, BF16),))
    a_pre, hn2 = fused_mm("ple_gate", [x2, lw["norm_ple"]], [_x_spec(), _g_spec()], _norm_prologue, lw["w_ple_gate"],
                          tn=1024, out_dtype=BF16, extras=((D_MODEL, BF16),))
    pp = fused_mm("ple_proj", [lw["p"]], [_row_spec(PLE_DIM)], _cast_prologue, lw["w_ple_proj"], tn=1024, out_dtype=BF16)
    x3 = ple_combine(x2, a_pre, pp)
    res = dict(x=x, z=z, h=h, y_ssm=y_ssm, ypre=ypre, h_re=h_re, h_im=h_im, y_conv=y_conv, y_attn=y_attn, merged=merged,
               x1=x1, hf=hf, hn1=hn1, act=act, x2=x2, a_pre=a_pre, hn2=hn2, pp=pp)
    return x3, res


def layer_bwd(dx3, lw, res, bias):
    g = {}
    da, dpp = ple_bwd(dx3, res["a_pre"], res["pp"])
    g["w_ple_proj"] = mm_tn("d_w_ple_proj", lw["p"], dpp)
    g["w_ple_gate"] = mm_tn("d_w_ple_gate", res["hn2"], da)
    dhn2 = fused_mm("d_ple_gate", [da], [_x_spec()], _cast_prologue, lw["w_ple_gate_t"], tn=1024, out_dtype=BF16)
    dx2, g["norm_ple"] = norm_bwd("ple_norm_bwd", dhn2, res["x2"], lw["norm_ple"], dx3)
    dact = fused_mm("d_ffn_out", [dx2], [_x_spec()], _cast_prologue, lw["w_ffn_out_t"], tn=1408, out_dtype=BF16)
    g["w_ffn_out"] = mm_tn("d_w_ffn_out", res["act"], dx2)
    dh1, dh2 = swiglu_bwd(dact, res["hf"])
    g["w_ffn_in"] = jnp.concatenate([mm_tn("d_w_ffn_in_a", res["hn1"], dh1), mm_tn("d_w_ffn_in_b", res["hn1"], dh2)],
                                    axis=1)
    dhn1 = fused_mm("d_ffn_in", [dh1, dh2], [_row_spec(FFN_HIDDEN), _row_spec(FFN_HIDDEN)], _concat_prologue,
                    lw["w_ffn_in_t"], tn=1024, out_dtype=BF16)
    dx1, g["norm_ffn"] = norm_bwd("ffn_norm_bwd", dhn1, res["x1"], lw["norm_ffn"], dx2)
    dmerged = fused_mm("d_out_proj", [dx1], [_x_spec()], _cast_prologue, lw["w_out_t"], tn=1024, out_dtype=BF16)
    g["w_out"] = mm_tn("d_w_out", res["merged"], dx1)
    z = res["z"]
    ys = (res["y_ssm"], res["y_conv"], res["y_attn"])
    dgates, dbs = merge_bwd(dmerged, z, *ys, lw["w_branch"])
    dys, dwb = [], []
    for r in range(3):
        dys.append(fused_mm(f"d_branch_{r}", [dbs[r]], [_x_spec()], _cast_prologue, lw["w_branch_t"][r], tn=1024,
                            out_dtype=BF16))
        dwb.append(mm_tn(f"d_w_branch_{r}", ys[r], dbs[r]))
    g["w_branch"] = jnp.stack(dwb)
    sc = _ssm_consts(lw)
    (du, dbb_re, dbb_im, dct_re, dct_im, da_re, da_im, g["ssm_d"], g["ssm_w_glu"]) = ssm_bwd(
        dys[0], z, res["ypre"], res["h_re"], res["h_im"], sc["bbt_re"], sc["bbt_im"], sc["c_re"], sc["c_im"],
        sc["a_re"], sc["a_im"], lw["ssm_d"], lw["ssm_w_glu"], lw["ssm_w_glu_t"])
    g["ssm_c_re"] = _blockdiag_extract(jnp.swapaxes(dct_re, 1, 2))
    g["ssm_c_im"] = _blockdiag_extract(jnp.swapaxes(dct_im, 1, 2))
    (g["ssm_lambda_re"], g["ssm_lambda_im"], dlog_dt, g["bt_re"], g["bt_im"]) = ssm_disc_bwd(
        lw["ssm_lambda_re"], lw["ssm_lambda_im"], lw["ssm_log_dt"].reshape(SSM_GROUPS, 1), lw["bt_re"], lw["bt_im"],
        da_re.reshape(_GN), da_im.reshape(_GN),
        _blockdiag_extract(dbb_re), _blockdiag_extract(dbb_im))
    g["ssm_log_dt"] = dlog_dt.reshape(SSM_GROUPS)
    dconv, g["conv_w"] = conv_bwd(dys[1], z, lw["conv_w"])
    dq, dkv_cur, dkv_prev, g["dbias"], g["attn_sinks"] = attn_bwd(dys[2], z, bias, lw["attn_sinks"])
    dkv = kv_shift_add(dkv_cur, dkv_prev)
    pieces = [dgates[0], dgates[1], dgates[2], du, dconv, dq, dkv]
    widths = [D_MODEL, D_MODEL, D_MODEL, BRANCH, 3 * BRANCH, BRANCH, KV_WIDTH]
    g["w_in"] = jnp.concatenate([mm_tn(f"d_w_in_{k}", res["h"], pc) for k, pc in enumerate(pieces)], axis=1)
    dhn0 = fused_mm("d_in_proj", pieces, [_row_spec(w) for w in widths], _concat_prologue, lw["w_in_t"], tn=1024,
                    out_dtype=BF16)
    dx0, g["norm_mix"] = norm_bwd("mix_norm_bwd", dhn0, res["x"], lw["norm_mix"], dx1)
    return dx0, g


def adamw(name, parts, w, m, v):
    n, R, C = parts.shape
    tr = _pick(R, (512, 256, 128, 64, 32, 16, 8))

    def body(p_ref, w_ref, m_ref, v_ref, g_ref, d_ref, nm_ref, nv_ref):
        gsum = p_ref[0].astype(F32)
        for k in range(1, n):
            gsum = gsum + p_ref[k].astype(F32)
        mn = ADAM_B1 * m_ref[...] + (1.0 - ADAM_B1) * gsum
        vn = ADAM_B2 * v_ref[...] + (1.0 - ADAM_B2) * jnp.square(gsum)
        m_hat = mn / (1.0 - ADAM_B1 ** ADAM_STEP)
        v_hat = vn / (1.0 - ADAM_B2 ** ADAM_STEP)
        g_ref[...] = gsum
        d_ref[...] = -ADAM_LR * (m_hat / (jnp.sqrt(v_hat) + ADAM_EPS) + ADAM_WD * w_ref[...])
        nm_ref[...] = mn
        nv_ref[...] = vn

    blk = pl.BlockSpec((tr, C), lambda i: (i, 0))
    return pl.pallas_call(
        body, name=name, grid=(R // tr,), in_specs=[pl.BlockSpec((n, tr, C), lambda i: (0, i, 0)), blk, blk, blk],
        out_specs=[blk] * 4, out_shape=[jax.ShapeDtypeStruct((R, C), F32)] * 4, compiler_params=_cparams("parallel"),
    )(parts, w, m, v)


_ANY = pl.BlockSpec(memory_space=pl.ANY)


def _coords():
    return lax.axis_index("x"), lax.axis_index("y"), lax.axis_index("c")


def _chip_peers(x, y):
    return [(1 - x, y), (x, 1 - y), (1 - x, 1 - y)]


def _comm_call(name, body, blks, out_shapes, copies_per_block):
    n = len(blks)
    return pl.pallas_call(
        functools.partial(body, n), name=name, in_specs=[_ANY] * n, out_specs=[_ANY] * n,
        out_shape=[jax.ShapeDtypeStruct(s, b.dtype) for s, b in zip(out_shapes, blks)],
        scratch_shapes=[pltpu.SemaphoreType.DMA((copies_per_block * n,)), pltpu.SemaphoreType.DMA((copies_per_block * n,))],
    )(*blks)


def gather_chips(name, blks):
    def body(n, *refs):
        x_refs, out_refs, send_sems, recv_sems = refs[:n], refs[n:2 * n], refs[2 * n], refs[2 * n + 1]
        x, y, c = _coords()
        me = 2 * x + y
        peers = _chip_peers(x, y)

        def copy(i, k, slot):
            return pltpu.make_async_remote_copy(
                src_ref=x_refs[i], dst_ref=out_refs[i].at[slot], send_sem=send_sems.at[3 * i + k],
                recv_sem=recv_sems.at[3 * i + k], device_id=(*peers[k], c), device_id_type=MESH)

        sends = [copy(i, k, me) for i in range(n) for k in range(3)]
        for cp in sends:
            cp.start()
        for i in range(n):
            for k, (px, py) in enumerate(peers):
                copy(i, k, 2 * px + py).wait_recv()
        for cp in sends:
            cp.wait_send()

    return _comm_call(name, body, blks, [(4,) + b.shape for b in blks], 3)


def gather_cores(name, blks):
    def body(n, *refs):
        x_refs, out_refs, send_sems, recv_sems = refs[:n], refs[n:2 * n], refs[2 * n], refs[2 * n + 1]
        x, y, c = _coords()

        def copy(i, slot):
            return pltpu.make_async_remote_copy(
                src_ref=x_refs[i], dst_ref=out_refs[i].at[slot], send_sem=send_sems.at[i], recv_sem=recv_sems.at[i],
                device_id=(x, y, 1 - c), device_id_type=MESH)

        sends = [copy(i, c) for i in range(n)]
        for cp in sends:
            cp.start()
        for i in range(n):
            copy(i, 1 - c).wait_recv()
        for cp in sends:
            cp.wait_send()

    return _comm_call(name, body, blks, [(2,) + b.shape for b in blks], 1)


def scatter_cores(name, blks):
    def body(n, *refs):
        x_refs, out_refs, send_sems, recv_sems = refs[:n], refs[n:2 * n], refs[2 * n], refs[2 * n + 1]
        x, y, c = _coords()
        sends = [pltpu.make_async_remote_copy(
            src_ref=x_refs[i].at[1 - c], dst_ref=out_refs[i], send_sem=send_sems.at[i], recv_sem=recv_sems.at[i],
            device_id=(x, y, 1 - c), device_id_type=MESH) for i in range(n)]
        for cp in sends:
            cp.start()
        for cp in sends:
            cp.wait_recv()
        for cp in sends:
            cp.wait_send()

    return _comm_call(name, body, blks, [b.shape[1:] for b in blks], 1)


def scatter_chips(name, blks):
    def body(n, *refs):
        x_refs, out_refs, send_sems, recv_sems = refs[:n], refs[n:2 * n], refs[2 * n], refs[2 * n + 1]
        x, y, c = _coords()
        me = 2 * x + y
        peers = _chip_peers(x, y)

        def copy(i, k, src_slot, dst_slot):
            return pltpu.make_async_remote_copy(
                src_ref=x_refs[i].at[src_slot], dst_ref=out_refs[i].at[dst_slot], send_sem=send_sems.at[3 * i + k],
                recv_sem=recv_sems.at[3 * i + k], device_id=(*peers[k], c), device_id_type=MESH)

        sends = [copy(i, k, 2 * px + py, me) for i in range(n) for k, (px, py) in enumerate(peers)]
        for cp in sends:
            cp.start()
        for i in range(n):
            for k, (px, py) in enumerate(peers):
                copy(i, k, me, 2 * px + py).wait_recv()
        for cp in sends:
            cp.wait_send()

    return _comm_call(name, body, blks, [b.shape for b in blks], 3)


def _put_slot(buf, block, idx):
    return lax.dynamic_update_slice(buf, block[None].astype(buf.dtype), (idx,) + (0,) * block.ndim)


def pair_sum(name, mine, theirs):
    _, R, C = mine.shape
    tr = _pick(R, (1024, 512, 256, 128, 64, 32, 16))
    c_idx = lax.axis_index("c").astype(jnp.int32).reshape(1)

    def body(c_ref, a_ref, b_ref, o_ref):
        o_ref[...] = (a_ref[0].astype(F32) + b_ref[...].astype(F32)).astype(BF16)

    return pl.pallas_call(
        body, name=name,
        grid_spec=pltpu.PrefetchScalarGridSpec(
            num_scalar_prefetch=1, grid=(R // tr,),
            in_specs=[pl.BlockSpec((1, tr, C), lambda i, c: (c[0], i, 0)), pl.BlockSpec((tr, C), lambda i, c: (i, 0))],
            out_specs=pl.BlockSpec((tr, C), lambda i, c: (i, 0))),
        out_shape=jax.ShapeDtypeStruct(theirs.shape, BF16), compiler_params=_cparams("parallel"),
    )(c_idx, mine, theirs)


SHARDED = {
    "w_in": ((D_MODEL, IN_WIDTH), 2), "ssm_w_glu": ((BRANCH, BRANCH), 1), "conv_w": ((3, BRANCH), 2),
    "w_branch": ((3, BRANCH, D_MODEL), 3), "w_out": ((D_MODEL, D_MODEL), 1), "w_ffn_in": ((D_MODEL, 2 * FFN_HIDDEN), 2),
    "w_ffn_out": ((FFN_HIDDEN, D_MODEL), 1), "w_ple_gate": ((D_MODEL, D_MODEL), 1), "w_ple_proj": ((PLE_DIM, D_MODEL), 2),
}
SMALL = ["rel_bias", "norm_mix", "ssm_lambda_re", "ssm_lambda_im", "ssm_b_re", "ssm_b_im", "ssm_c_re", "ssm_c_im", "ssm_d",
         "ssm_log_dt", "attn_sinks", "norm_ffn", "norm_ple", "norm_final"]
WEIGHTS = ["rel_bias", "norm_mix", "w_in", "ssm_lambda_re", "ssm_lambda_im", "ssm_b_re", "ssm_b_im", "ssm_c_re", "ssm_c_im",
           "ssm_d", "ssm_log_dt", "ssm_w_glu", "conv_w", "attn_sinks", "w_branch", "w_out", "norm_ffn", "w_ffn_in",
           "w_ffn_out", "norm_ple", "w_ple_gate", "w_ple_proj", "norm_final"]


def _pad_to(flat, n):
    return jnp.pad(flat, [(0, 0)] * (flat.ndim - 1) + [(0, n - flat.shape[-1])])


def _unshard(g8, name, layer):
    axis = SHARDED[name][1] - 1
    b = g8[:, :, layer]
    shard = b.shape[2:]
    b = b.reshape((2, 2, 2) + shard)
    b = jnp.moveaxis(b, (1, 2, 0), (axis, axis + 1, axis + 2))
    full = list(shard)
    full[axis] *= N_DEV
    return b.reshape(full)


def _shard_split(full, name):
    _, axis = SHARDED[name]
    dims = list(full.shape)
    dims[axis:axis + 1] = [2, 2, 2, dims[axis] // N_DEV]
    b = jnp.moveaxis(full.reshape(dims), (axis, axis + 1, axis + 2), (1, 2, 0))
    return b.reshape((2, 4) + b.shape[3:])


def _small_sizes(shapes):
    return [-(-int(np.prod(shapes[n])) // 128) * 128 for n in SMALL]


def pack_small(vals, shapes, extra):
    segs = [_pad_to(vals[n].reshape(-1).astype(F32), s) for n, s in zip(SMALL, _small_sizes(shapes))]
    segs.append(_pad_to(extra.reshape(-1), 128))
    flat = jnp.concatenate(segs)
    rows = -(-flat.shape[0] // (128 * 8)) * 8
    return _pad_to(flat, rows * 128).reshape(rows, 128)


def unpack_small(packed, shapes):
    flat = packed.reshape(-1)
    out, off = {}, 0
    for n, s in zip(SMALL, _small_sizes(shapes)):
        out[n] = flat[off:off + int(np.prod(shapes[n]))].reshape(shapes[n])
        off += s
    return out, flat[off]


def _layer_weights(gathered, small, p, i):
    full = {n: _unshard(gathered[n], n, i) for n in SHARDED}
    w_in = full["w_in"]
    w_in_p = jnp.concatenate([w_in[:, 2816:], w_in[:, :2560], w_in[:, 2560:2816]], axis=-1)
    return dict(
        w_in=w_in_p, w_in_t=w_in_p.T,
        ssm_w_glu=full["ssm_w_glu"], ssm_w_glu_t=full["ssm_w_glu"].T,
        conv_w=full["conv_w"],
        w_branch=full["w_branch"], w_branch_t=jnp.swapaxes(full["w_branch"], 1, 2),
        w_out=full["w_out"], w_out_t=full["w_out"].T,
        w_ffn_in=full["w_ffn_in"], w_ffn_in_t=full["w_ffn_in"].T,
        w_ffn_out=full["w_ffn_out"], w_ffn_out_t=full["w_ffn_out"].T,
        w_ple_gate=full["w_ple_gate"], w_ple_gate_t=full["w_ple_gate"].T,
        w_ple_proj=full["w_ple_proj"],
        norm_mix=small["norm_mix"][i][None, :], norm_ffn=small["norm_ffn"][i][None, :],
        norm_ple=small["norm_ple"][i][None, :],
        ssm_lambda_re=small["ssm_lambda_re"][i], ssm_lambda_im=small["ssm_lambda_im"][i],
        ssm_log_dt=small["ssm_log_dt"][i],
        bt_re=jnp.swapaxes(small["ssm_b_re"][i], 1, 2), bt_im=jnp.swapaxes(small["ssm_b_im"][i], 1, 2),
        ssm_c_re=small["ssm_c_re"][i], ssm_c_im=small["ssm_c_im"][i], ssm_d=small["ssm_d"][i][None, :],
        attn_sinks=small["attn_sinks"][i], p=p[i],
    )


def local_step(x, p, gathered, small, target):
    bias = rel_bias_fwd(small["rel_bias"].T).reshape(N_Q_HEADS, BLOCK, 2 * BLOCK)
    layers = [_layer_weights(gathered, small, p, i) for i in range(DEPTH)]
    res = []
    for lw_i in layers:
        x, res_i = layer_fwd(x, lw_i, bias)
        res.append(res_i)
    grad_x, loss_parts, dg_final = loss_head(x, small["norm_final"][None, :], target)
    per_layer = [None] * DEPTH
    for i in reversed(range(DEPTH)):
        grad_x, per_layer[i] = layer_bwd(grad_x, layers[i], res[i], bias)
    g = {k: jnp.stack([gl[k] for gl in per_layer]) for k in per_layer[0]}
    drel = rel_bias_bwd(g["dbias"].reshape(DEPTH, N_Q_HEADS, BLOCK * 2 * BLOCK)).T
    w_in_g = g["w_in"]
    gfull = dict(
        w_in=jnp.concatenate([w_in_g[..., 3072:5632], w_in_g[..., 5632:], w_in_g[..., :3072]], axis=-1),
        ssm_w_glu=g["ssm_w_glu"], conv_w=jnp.sum(g["conv_w"].reshape(DEPTH, 3, 8, BRANCH), axis=2),
        w_branch=g["w_branch"], w_out=g["w_out"], w_ffn_in=g["w_ffn_in"], w_ffn_out=g["w_ffn_out"],
        w_ple_gate=g["w_ple_gate"], w_ple_proj=g["w_ple_proj"])
    gsmall = dict(
        rel_bias=drel, norm_mix=jnp.sum(g["norm_mix"], axis=1), ssm_lambda_re=g["ssm_lambda_re"],
        ssm_lambda_im=g["ssm_lambda_im"], ssm_b_re=jnp.swapaxes(g["bt_re"], 2, 3), ssm_b_im=jnp.swapaxes(g["bt_im"], 2, 3),
        ssm_c_re=g["ssm_c_re"], ssm_c_im=g["ssm_c_im"], ssm_d=g["ssm_d"][:, 0, :], ssm_log_dt=g["ssm_log_dt"],
        attn_sinks=g["attn_sinks"][:, :, 0], norm_ffn=jnp.sum(g["norm_ffn"], axis=1), norm_ple=jnp.sum(g["norm_ple"], axis=1),
        norm_final=jnp.sum(dg_final, axis=0))
    return loss_parts, grad_x, gfull, gsmall


def kernel(x, p, rel_bias, norm_mix, w_in, ssm_lambda_re, ssm_lambda_im, ssm_b_re, ssm_b_im, ssm_c_re, ssm_c_im, ssm_d, ssm_log_dt, ssm_w_glu, conv_w, attn_sinks, w_branch, w_out, norm_ffn, w_ffn_in, w_ffn_out, norm_ple, w_ple_gate, w_ple_proj, norm_final, loss_target, m_rel_bias, m_norm_mix, m_w_in, m_ssm_lambda_re, m_ssm_lambda_im, m_ssm_b_re, m_ssm_b_im, m_ssm_c_re, m_ssm_c_im, m_ssm_d, m_ssm_log_dt, m_ssm_w_glu, m_conv_w, m_attn_sinks, m_w_branch, m_w_out, m_norm_ffn, m_w_ffn_in, m_w_ffn_out, m_norm_ple, m_w_ple_gate, m_w_ple_proj, m_norm_final, v_rel_bias, v_norm_mix, v_w_in, v_ssm_lambda_re, v_ssm_lambda_im, v_ssm_b_re, v_ssm_b_im, v_ssm_c_re, v_ssm_c_im, v_ssm_d, v_ssm_log_dt, v_ssm_w_glu, v_conv_w, v_attn_sinks, v_w_branch, v_w_out, v_norm_ffn, v_w_ffn_in, v_w_ffn_out, v_norm_ple, v_w_ple_gate, v_w_ple_proj, v_norm_final):
    args = dict(locals())
    w = {n: args[n] for n in WEIGHTS}
    m = {n: args["m_" + n] for n in WEIGHTS}
    v = {n: args["v_" + n] for n in WEIGHTS}
    shapes = {n: w[n].shape for n in SMALL}

    sharded = list(SHARDED)
    x_i, y_i, c_i = _coords()
    chip = 2 * x_i + y_i

    def all_gather(tag, blks):
        g4 = gather_chips(f"gather_{tag}_chips", blks)
        g4 = [_put_slot(g, b, chip) for g, b in zip(g4, blks)]
        g8 = gather_cores(f"gather_{tag}_cores", g4)
        return [_put_slot(g, b, c_i) for g, b in zip(g8, g4)]

    mine = [w[n] if n == "conv_w" else w[n].astype(BF16) for n in sharded]
    gathered = dict(zip(sharded, all_gather("w", mine)))

    loss_parts, grad_x, gfull, gsmall = local_step(x[0], p[:, 0], gathered, {n: w[n] for n in SMALL}, loss_target[0])

    gp = [_shard_split(gfull[n], n).astype(BF16) for n in sharded]
    from_sibling = scatter_cores("scatter_g_cores", gp)
    sums = [pair_sum("pair_sum_" + n, a.reshape(2, -1, a.shape[-1]), b.reshape(-1, b.shape[-1])).reshape(b.shape)
            for n, a, b in zip(sharded, gp, from_sibling)]
    received = scatter_chips("scatter_g_chips", sums)

    outs = ({}, {}, {}, {})
    for name, r, s in zip(sharded, received, sums):
        parts = _put_slot(r, lax.dynamic_index_in_dim(s, chip, 0, keepdims=False), chip)
        cols = parts.shape[-1]
        res4 = adamw("adamw_" + name, parts.reshape(4, -1, cols), w[name].reshape(-1, cols), m[name].reshape(-1, cols),
                     v[name].reshape(-1, cols))
        for d, o in zip(outs, res4):
            d[name] = o.reshape(w[name].shape)

    small_local = pack_small(gsmall, shapes, jnp.sum(loss_parts))
    small_all = all_gather("s", [small_local])[0]
    zero = jnp.zeros((1,), F32)
    res4 = adamw("adamw_small", small_all.reshape(N_DEV, small_local.shape[0], 128),
                 pack_small({n: w[n] for n in SMALL}, shapes, zero), pack_small({n: m[n] for n in SMALL}, shapes, zero),
                 pack_small({n: v[n] for n in SMALL}, shapes, zero))
    loss = None
    for d, r in zip(outs, res4):
        vals, extra = unpack_small(r, shapes)
        d.update(vals)
        if loss is None:
            loss = extra

    return (loss, grad_x[None], *[d[n] for d in outs for n in WEIGHTS])
```

```python
import functools
import math

import numpy as np
import jax
import jax.numpy as jnp
from jax import lax
from jax.experimental import pallas as pl
from jax.experimental.pallas import tpu as pltpu

F32 = jnp.float32
BF16 = jnp.bfloat16
MESH = pl.DeviceIdType.MESH

D_MODEL = 1024
DEPTH = 4
PLE_DIM = 256
BRANCH = 512
SSM_GROUPS = 32
SSM_GROUP = 16
SSM_STATE = 64
SSM_LANES = SSM_GROUPS * SSM_STATE
SSM_SUB = 4
SUB_IN = BRANCH // SSM_SUB
SUB_ST = SSM_LANES // SSM_SUB
HEAD_DIM = 64
N_Q_HEADS = 8
N_KV_HEADS = 2
GQA_GROUP = 4
KV_WIDTH = 2 * N_KV_HEADS * HEAD_DIM
WINDOW = 128
BLOCK = 128
ATTN_SCALE = 1.0 / math.sqrt(HEAD_DIM)
REL_BUCKETS = 32
REL_MAX_DIST = 128
FFN_HIDDEN = 2816
RMS_EPS = 1e-6
IN_WIDTH = 5888
N_DEV = 8

ADAM_LR = 0.001
ADAM_B1 = 0.9
ADAM_B2 = 0.999
ADAM_EPS = 1e-08
ADAM_WD = 0.01
ADAM_STEP = 10

COL_U = 3072
COL_KV = 5632
NEG = -1e30

SCAN_T = 256
TM = 512
TM_W = 1024
VMEM_LIMIT = 52 * 1024 * 1024


def _cparams(*sem):
    return pltpu.CompilerParams(dimension_semantics=sem, vmem_limit_bytes=VMEM_LIMIT)


def _full(shape):
    n = len(shape)
    return pl.BlockSpec(shape, lambda *_: (0,) * n)


def _pick(n, cands):
    for c in cands:
        if n % c == 0:
            return c
    return n


def _call(body, *, name, steps, in_specs, out_specs, out_shape, scratch, args, side=None):
    in_specs, out_specs, out_shape = list(in_specs), list(out_specs), list(out_shape)
    scratch, args = list(scratch), list(args)
    n_in, n_out, n_scr = len(in_specs), len(out_specs), len(scratch)
    n = 0
    if side is not None:
        kind, blks = side
        out_shape_of, per_block, make_copies = _EXCHANGES[kind]
        n = len(blks)
        inner = body

        def body(*refs):
            ins, sx = refs[:n_in], refs[n_in:n_in + n]
            outs, so = refs[n_in + n:n_in + n + n_out], refs[n_in + n + n_out:n_in + 2 * n + n_out]
            scr = refs[n_in + 2 * n + n_out:n_in + 2 * n + n_out + n_scr]
            send_sems, recv_sems = refs[-2:]

            @pl.when(pl.program_id(0) == 0)
            def _():
                sends, _ = make_copies(sx, so, send_sems, recv_sems)
                for cp in sends:
                    cp.start()

            inner(*ins, *outs, *scr)

            @pl.when(pl.program_id(0) == steps - 1)
            def _():
                sends, recvs = make_copies(sx, so, send_sems, recv_sems)
                for cp in recvs:
                    cp.wait_recv()
                for cp in sends:
                    cp.wait_send()

        in_specs += [_ANY] * n
        args += list(blks)
        out_specs += [_ANY] * n
        out_shape += [jax.ShapeDtypeStruct(out_shape_of(b), b.dtype) for b in blks]
        scratch += [pltpu.SemaphoreType.DMA((per_block * n,)), pltpu.SemaphoreType.DMA((per_block * n,))]
    outs = pl.pallas_call(
        body, name=name, grid=(steps,), in_specs=in_specs, out_specs=out_specs, out_shape=out_shape,
        scratch_shapes=scratch, compiler_params=_cparams("arbitrary"),
    )(*args)
    return outs[:n_out], outs[n_out:]


def _dot(a, b):
    return jnp.dot(a, b, preferred_element_type=F32)


def _dot_tn(a, b):
    return lax.dot_general(a, b, (((0,), (0,)), ((), ())), preferred_element_type=F32)


def _dot_nt(a, b):
    return lax.dot_general(a, b, (((1,), (1,)), ((), ())), preferred_element_type=F32)


def _rms(x, g):
    r = lax.rsqrt(jnp.mean(x * x, axis=-1, keepdims=True) + RMS_EPS)
    return x * r * g


def fused_mm(name, ins, in_specs, prologue, w, *, tn, out_dtype, res=None, extras=(), side=None):
    S = ins[0].shape[0]
    K, N = w.shape
    tn = min(tn, N)
    n_in, n_ex = len(ins), len(extras)

    def body(*refs):
        in_refs = refs[:n_in]
        w_ref = refs[n_in]
        pos = n_in + 1
        res_ref = None
        if res is not None:
            res_ref = refs[pos]
            pos += 1
        o_ref = refs[pos]
        ex_refs = refs[pos + 1:pos + 1 + n_ex]
        a_scr = refs[-1]
        out = prologue(*[r[...] for r in in_refs])
        a_scr[...] = out[0]
        for r, e in zip(ex_refs, out[1:]):
            r[...] = e.astype(r.dtype)
        for j in range(N // tn):
            cs = slice(j * tn, (j + 1) * tn)
            acc = _dot(a_scr[...], w_ref[:, cs])
            if res_ref is not None:
                acc = acc + res_ref[:, cs]
            o_ref[:, cs] = acc.astype(o_ref.dtype)

    specs = list(in_specs) + [pl.BlockSpec((K, N), lambda i: (0, 0), pipeline_mode=pl.Buffered(1))]
    args = list(ins) + [w]
    if res is not None:
        specs.append(pl.BlockSpec((TM, N), lambda i: (i, 0)))
        args.append(res)
    out_shape = [jax.ShapeDtypeStruct((S, N), out_dtype)]
    out_specs = [pl.BlockSpec((TM, N), lambda i: (i, 0))]
    for cols, dt in extras:
        out_shape.append(jax.ShapeDtypeStruct((S, cols), dt))
        out_specs.append(pl.BlockSpec((TM, cols), lambda i: (i, 0)))
    outs, side_outs = _call(body, name=name, steps=S // TM, in_specs=specs, out_specs=out_specs, out_shape=out_shape,
                            scratch=[pltpu.VMEM((TM, K), BF16)], args=args, side=side)
    result = outs if n_ex else outs[0]
    return result if side is None else (result, side_outs)


def _row_spec(cols, blk=0):
    return pl.BlockSpec((TM, cols), lambda i: (i, blk))


def mm_tn(name, a, b):
    S, K = a.shape
    N = b.shape[1]
    tk = _pick(K, (1024, 1408, 512, 256))
    tn = _pick(N, (1024, 1408, 1536, 512, 256))

    def body(a_ref, b_ref, o_ref):
        @pl.when(pl.program_id(2) == 0)
        def _():
            o_ref[...] = jnp.zeros_like(o_ref)

        o_ref[...] += _dot_tn(a_ref[...].astype(BF16), b_ref[...].astype(BF16))

    return pl.pallas_call(
        body, name=name, grid=(K // tk, N // tn, S // TM_W),
        in_specs=[pl.BlockSpec((TM_W, tk), lambda k, n, s: (s, k)), pl.BlockSpec((TM_W, tn), lambda k, n, s: (s, n))],
        out_specs=pl.BlockSpec((tk, tn), lambda k, n, s: (k, n)),
        out_shape=jax.ShapeDtypeStruct((K, N), F32),
        compiler_params=_cparams("parallel", "parallel", "arbitrary"),
    )(a, b)


def norm_bwd(name, dh, x, g, dres):
    S = x.shape[0]

    def body(dh_ref, x_ref, g_ref, dres_ref, dx_ref, dg_ref):
        @pl.when(pl.program_id(0) == 0)
        def _():
            dg_ref[...] = jnp.zeros_like(dg_ref)

        xv = x_ref[...]
        dhv = dh_ref[...].astype(F32)
        r = lax.rsqrt(jnp.mean(xv * xv, axis=-1, keepdims=True) + RMS_EPS)
        xhat = xv * r
        dxhat = dhv * g_ref[...]
        dx = r * (dxhat - xhat * jnp.mean(dxhat * xhat, axis=-1, keepdims=True))
        dx_ref[...] = dres_ref[...] + dx
        dg_ref[...] += jnp.sum((dhv * xhat).reshape(TM // 8, 8, D_MODEL), axis=0)

    row = pl.BlockSpec((TM, D_MODEL), lambda i: (i, 0))
    return pl.pallas_call(
        body, name=name, grid=(S // TM,), in_specs=[row, row, _full((1, D_MODEL)), row],
        out_specs=[row, _full((8, D_MODEL))],
        out_shape=[jax.ShapeDtypeStruct((S, D_MODEL), F32), jax.ShapeDtypeStruct((8, D_MODEL), F32)],
        compiler_params=_cparams("arbitrary"),
    )(dh, x, g, dres)


def swiglu_bwd(dact, hf):
    S = hf.shape[0]

    def body(da_ref, h1_ref, h2_ref, o1_ref, o2_ref):
        h1 = h1_ref[...].astype(F32)
        h2 = h2_ref[...].astype(F32)
        da = da_ref[...].astype(F32)
        sg = jax.nn.sigmoid(h1)
        o1_ref[...] = (da * h2 * sg * (1.0 + h1 * (1.0 - sg))).astype(BF16)
        o2_ref[...] = (da * h1 * sg).astype(BF16)

    tn = 1408
    nn = FFN_HIDDEN // tn
    return pl.pallas_call(
        body, name="swiglu_bwd", grid=(S // TM, nn),
        in_specs=[pl.BlockSpec((TM, tn), lambda i, j: (i, j)), pl.BlockSpec((TM, tn), lambda i, j: (i, j)),
                  pl.BlockSpec((TM, tn), lambda i, j: (i, nn + j))],
        out_specs=[pl.BlockSpec((TM, tn), lambda i, j: (i, j)), pl.BlockSpec((TM, tn), lambda i, j: (i, j))],
        out_shape=[jax.ShapeDtypeStruct((S, FFN_HIDDEN), BF16)] * 2,
        compiler_params=_cparams("parallel", "parallel"),
    )(dact, hf, hf)


def ple_combine(x2, a_pre, pp):
    S = x2.shape[0]

    def body(x_ref, a_ref, p_ref, o_ref):
        o_ref[...] = x_ref[...] + jax.nn.sigmoid(a_ref[...].astype(F32)) * p_ref[...].astype(F32)

    row = pl.BlockSpec((TM, D_MODEL), lambda i: (i, 0))
    return pl.pallas_call(
        body, name="ple_combine", grid=(S // TM,), in_specs=[row, row, row], out_specs=row,
        out_shape=jax.ShapeDtypeStruct((S, D_MODEL), F32), compiler_params=_cparams("parallel"),
    )(x2, a_pre, pp)


def ple_bwd(dx3, a_pre, pp):
    S = dx3.shape[0]

    def body(dx_ref, a_ref, p_ref, da_ref, dpp_ref):
        dx = dx_ref[...]
        pg = jax.nn.sigmoid(a_ref[...].astype(F32))
        dpp_ref[...] = (dx * pg).astype(BF16)
        da_ref[...] = (dx * p_ref[...].astype(F32) * pg * (1.0 - pg)).astype(BF16)

    row = pl.BlockSpec((TM, D_MODEL), lambda i: (i, 0))
    return pl.pallas_call(
        body, name="ple_bwd", grid=(S // TM,), in_specs=[row, row, row], out_specs=[row, row],
        out_shape=[jax.ShapeDtypeStruct((S, D_MODEL), BF16)] * 2, compiler_params=_cparams("parallel"),
    )(dx3, a_pre, pp)


def _gate_specs(tn, nn):
    return [pl.BlockSpec((TM, tn), functools.partial(lambda i, j, r: (i, r * nn + j), r=r)) for r in range(3)]


def merge_fwd(z, y_ssm, y_conv, y_attn, wb):
    S = z.shape[0]
    tn = 512
    nn = D_MODEL // tn

    def body(g0, g1, g2, y0, y1, y2, w_ref, o_ref):
        acc = jnp.zeros((TM, tn), F32)
        for r, (g_ref, y_ref) in enumerate(((g0, y0), (g1, y1), (g2, y2))):
            acc += jax.nn.sigmoid(g_ref[...].astype(F32)) * _dot(y_ref[...], w_ref[r])
        o_ref[...] = acc.astype(BF16)

    y_spec = pl.BlockSpec((TM, BRANCH), lambda i, j: (i, 0))
    return pl.pallas_call(
        body, name="merge_fwd", grid=(S // TM, nn),
        in_specs=_gate_specs(tn, nn) + [y_spec] * 3 + [pl.BlockSpec((3, BRANCH, tn), lambda i, j: (0, 0, j))],
        out_specs=pl.BlockSpec((TM, tn), lambda i, j: (i, j)),
        out_shape=jax.ShapeDtypeStruct((S, D_MODEL), BF16), compiler_params=_cparams("parallel", "parallel"),
    )(z, z, z, y_ssm, y_conv, y_attn, wb)


def merge_bwd(dmerged, z, y_ssm, y_conv, y_attn, wb):
    S = z.shape[0]
    tn = 512
    nn = D_MODEL // tn

    def body(dm_ref, g0, g1, g2, y0, y1, y2, w_ref, dg0, dg1, dg2, db0, db1, db2):
        dm = dm_ref[...].astype(F32)
        for r, (g_ref, y_ref, dg_ref, db_ref) in enumerate(((g0, y0, dg0, db0), (g1, y1, dg1, db1), (g2, y2, dg2, db2))):
            sg = jax.nn.sigmoid(g_ref[...].astype(F32))
            b = _dot(y_ref[...], w_ref[r])
            dg_ref[...] = (dm * b * sg * (1.0 - sg)).astype(BF16)
            db_ref[...] = (dm * sg).astype(BF16)

    y_spec = pl.BlockSpec((TM, BRANCH), lambda i, j: (i, 0))
    outs = pl.pallas_call(
        body, name="merge_bwd", grid=(S // TM, nn),
        in_specs=[pl.BlockSpec((TM, tn), lambda i, j: (i, j))] + _gate_specs(tn, nn) + [y_spec] * 3
        + [pl.BlockSpec((3, BRANCH, tn), lambda i, j: (0, 0, j))],
        out_specs=[pl.BlockSpec((TM, tn), lambda i, j: (i, j))] * 6,
        out_shape=[jax.ShapeDtypeStruct((S, D_MODEL), BF16)] * 6, compiler_params=_cparams("parallel", "parallel"),
    )(dmerged, z, z, z, y_ssm, y_conv, y_attn, wb)
    return outs[:3], outs[3:]


def _shift_down(v, halo, k):
    rolled = pltpu.roll(v, k, 0)
    h = pltpu.roll(halo, k, 0)
    row = lax.broadcasted_iota(jnp.int32, v.shape, 0)
    head = jnp.concatenate([h, jnp.zeros((v.shape[0] - 8, v.shape[1]), v.dtype)], axis=0)
    return jnp.where(row < k, head, rolled)


def _shift_up(v, halo, k):
    n = v.shape[0]
    rolled = pltpu.roll(v, n - k, 0)
    h = pltpu.roll(halo, 8 - k, 0)
    row = lax.broadcasted_iota(jnp.int32, v.shape, 0)
    tail = jnp.concatenate([jnp.zeros((n - 8, v.shape[1]), v.dtype), h], axis=0)
    return jnp.where(row >= n - k, tail, rolled)


def _conv_specs():
    rb = TM // 8
    c0 = COL_U // BRANCH

    def cur(k):
        return pl.BlockSpec((TM, BRANCH), lambda i: (i, c0 + k))

    def prev(k):
        return pl.BlockSpec((8, BRANCH), lambda i: (jnp.maximum(i * rb - 1, 0), c0 + k))

    return [cur(1), cur(2), cur(3), prev(2), prev(3)]


def conv_fwd(z, conv_w):
    S = z.shape[0]

    def body(cb_ref, cc_ref, cx_ref, pc_ref, px_ref, w_ref, o_ref):
        first = pl.program_id(0) == 0
        v = cc_ref[...].astype(F32) * cx_ref[...].astype(F32)
        pv = jnp.where(first, 0.0, pc_ref[...].astype(F32) * px_ref[...].astype(F32))
        w = w_ref[...]
        y = w[2:3] * v + w[1:2] * _shift_down(v, pv, 1) + w[0:1] * _shift_down(v, pv, 2)
        o_ref[...] = (cb_ref[...].astype(F32) * y).astype(BF16)

    return pl.pallas_call(
        body, name="conv_fwd", grid=(S // TM,), in_specs=_conv_specs() + [_full((3, BRANCH))],
        out_specs=pl.BlockSpec((TM, BRANCH), lambda i: (i, 0)),
        out_shape=jax.ShapeDtypeStruct((S, BRANCH), BF16), compiler_params=_cparams("parallel"),
    )(z, z, z, z, z, conv_w)


def conv_bwd(dy, z, conv_w):
    S = z.shape[0]
    rb = TM // 8
    nt = S // TM
    c0 = COL_U // BRANCH

    def body(dy_ref, cb_ref, cc_ref, cx_ref, pc_ref, px_ref, ndy_ref, ncb_ref, w_ref, o_ref, dw_ref):
        i = pl.program_id(0)

        @pl.when(i == 0)
        def _():
            dw_ref[...] = jnp.zeros_like(dw_ref)

        cb = cb_ref[...].astype(F32)
        cc = cc_ref[...].astype(F32)
        cx = cx_ref[...].astype(F32)
        dyv = dy_ref[...].astype(F32)
        v = cc * cx
        pv = jnp.where(i == 0, 0.0, pc_ref[...].astype(F32) * px_ref[...].astype(F32))
        v1 = _shift_down(v, pv, 1)
        v2 = _shift_down(v, pv, 2)
        w = w_ref[...]
        conv = w[2:3] * v + w[1:2] * v1 + w[0:1] * v2
        dc = dyv * cb
        ndc = jnp.where(i == nt - 1, 0.0, ndy_ref[...].astype(F32) * ncb_ref[...].astype(F32))
        dv = w[2:3] * dc + w[1:2] * _shift_up(dc, ndc, 1) + w[0:1] * _shift_up(dc, ndc, 2)
        o_ref[:, 0:BRANCH] = (dyv * conv).astype(BF16)
        o_ref[:, BRANCH:2 * BRANCH] = (dv * cx).astype(BF16)
        o_ref[:, 2 * BRANCH:3 * BRANCH] = (dv * cc).astype(BF16)
        for k, vk in enumerate((v2, v1, v)):
            dw_ref[8 * k:8 * k + 8, :] += jnp.sum((dc * vk).reshape(rb, 8, BRANCH), axis=0)

    nxt = jnp.minimum

    return pl.pallas_call(
        body, name="conv_bwd", grid=(nt,),
        in_specs=[pl.BlockSpec((TM, BRANCH), lambda i: (i, 0))] + _conv_specs()
        + [pl.BlockSpec((8, BRANCH), lambda i: (nxt((i + 1) * rb, S // 8 - 1), 0)),
           pl.BlockSpec((8, BRANCH), lambda i: (nxt((i + 1) * rb, S // 8 - 1), c0 + 1)),
           _full((3, BRANCH))],
        out_specs=[pl.BlockSpec((TM, 3 * BRANCH), lambda i: (i, 0)), _full((24, BRANCH))],
        out_shape=[jax.ShapeDtypeStruct((S, 3 * BRANCH), BF16), jax.ShapeDtypeStruct((24, BRANCH), F32)],
        compiler_params=_cparams("arbitrary"),
    )(dy, z, z, z, z, z, dy, z, conv_w)


def _bucket_onehot_t():
    qi = np.arange(BLOCK)[:, None]
    kj = np.arange(2 * BLOCK)[None, :]
    dist = np.clip(qi + BLOCK - kj, 0, REL_MAX_DIST - 1)
    exact = REL_BUCKETS // 2
    df = np.maximum(dist, 1).astype(np.float32)
    large = exact + (np.log(df / np.float32(exact)) / np.float32(math.log(REL_MAX_DIST / exact))
                     * np.float32(REL_BUCKETS - exact)).astype(np.int32)
    large = np.minimum(large, REL_BUCKETS - 1)
    bucket = np.where(dist < exact, dist, large).reshape(-1)
    return (np.arange(REL_BUCKETS)[:, None] == bucket[None, :]).astype(np.float32)


def rel_bias_fwd(rel_bias_t):
    n = BLOCK * 2 * BLOCK

    def body(r_ref, oh_ref, o_ref):
        o_ref[...] = jnp.dot(r_ref[...], oh_ref[...], precision=lax.Precision.HIGHEST, preferred_element_type=F32)

    return pl.pallas_call(
        body, name="rel_bias_fwd", grid=(1,), in_specs=[_full((N_Q_HEADS, REL_BUCKETS)), _full((REL_BUCKETS, n))],
        out_specs=_full((N_Q_HEADS, n)), out_shape=jax.ShapeDtypeStruct((N_Q_HEADS, n), F32),
        compiler_params=_cparams("arbitrary"),
    )(rel_bias_t, jnp.asarray(_bucket_onehot_t()))


def rel_bias_bwd(dbias):
    n_l = dbias.shape[0]
    n = BLOCK * 2 * BLOCK

    def body(d_ref, oh_ref, o_ref):
        tot = d_ref[0]
        for l in range(1, n_l):
            tot = tot + d_ref[l]
        o_ref[...] = lax.dot_general(tot, oh_ref[...], (((1,), (1,)), ((), ())), precision=lax.Precision.HIGHEST,
                                     preferred_element_type=F32)

    return pl.pallas_call(
        body, name="rel_bias_bwd", grid=(1,), in_specs=[_full((n_l, N_Q_HEADS, n)), _full((REL_BUCKETS, n))],
        out_specs=_full((N_Q_HEADS, REL_BUCKETS)), out_shape=jax.ShapeDtypeStruct((N_Q_HEADS, REL_BUCKETS), F32),
        compiler_params=_cparams("arbitrary"),
    )(dbias, jnp.asarray(_bucket_onehot_t()))


def _attn_valid(first):
    qi = lax.broadcasted_iota(jnp.int32, (BLOCK, 2 * BLOCK), 0)
    kj = lax.broadcasted_iota(jnp.int32, (BLOCK, 2 * BLOCK), 1)
    dist = qi + BLOCK - kj
    return (dist >= 0) & (dist < WINDOW) & (jnp.logical_not(first) | (kj >= BLOCK))


def _attn_weights(qh, kcat, bias_h, valid, sink):
    s = _dot_nt(qh, kcat) * ATTN_SCALE + bias_h
    s = jnp.where(valid, s, NEG)
    m = jnp.maximum(jnp.max(s, axis=-1, keepdims=True), sink)
    p = jnp.exp(s - m)
    esink = jnp.exp(sink - m)
    inv = 1.0 / (jnp.sum(p, axis=-1, keepdims=True) + esink)
    return p * inv, esink * inv


def _kv_heads(kvp, kvc, hk):
    ks = slice(hk * HEAD_DIM, (hk + 1) * HEAD_DIM)
    vs = slice(KV_WIDTH // 2 + hk * HEAD_DIM, KV_WIDTH // 2 + (hk + 1) * HEAD_DIM)
    return jnp.concatenate([kvp[:, ks], kvc[:, ks]], axis=0), jnp.concatenate([kvp[:, vs], kvc[:, vs]], axis=0)


def _attn_specs():
    cq = (COL_U + 4 * BRANCH) // BRANCH
    ckv = COL_KV // KV_WIDTH
    return [pl.BlockSpec((BLOCK, BRANCH), lambda n: (n, cq)),
            pl.BlockSpec((BLOCK, KV_WIDTH), lambda n: (n, ckv)),
            pl.BlockSpec((BLOCK, KV_WIDTH), lambda n: (jnp.maximum(n - 1, 0), ckv)),
            _full((N_Q_HEADS, BLOCK, 2 * BLOCK)),
            pl.BlockSpec(memory_space=pltpu.SMEM)]


def attn_fwd(z, bias, sinks, side=None):
    S = z.shape[0]

    def body(q_ref, kvc_ref, kvp_ref, b_ref, sink_ref, o_ref):
        valid = _attn_valid(pl.program_id(0) == 0)
        q = q_ref[...]
        kvc = kvc_ref[...]
        kvp = kvp_ref[...]
        outs = []
        for hk in range(N_KV_HEADS):
            kcat, vcat = _kv_heads(kvp, kvc, hk)
            for g in range(GQA_GROUP):
                h = hk * GQA_GROUP + g
                w, _ = _attn_weights(q[:, h * HEAD_DIM:(h + 1) * HEAD_DIM], kcat, b_ref[h], valid, sink_ref[h])
                outs.append(_dot(w.astype(BF16), vcat))
        o_ref[...] = jnp.concatenate(outs, axis=1).astype(BF16)

    outs, side_outs = _call(
        body, name="attn_fwd", steps=S // BLOCK, in_specs=_attn_specs(),
        out_specs=[pl.BlockSpec((BLOCK, BRANCH), lambda n: (n, 0))],
        out_shape=[jax.ShapeDtypeStruct((S, BRANCH), BF16)], scratch=[], args=(z, z, z, bias, sinks), side=side)
    return outs[0], side_outs


def attn_bwd(do, z, bias, sinks):
    S = z.shape[0]

    def body(do_ref, q_ref, kvc_ref, kvp_ref, b_ref, sink_ref, dq_ref, dc_ref, dp_ref, db_ref, ds_ref):
        first = pl.program_id(0) == 0

        @pl.when(first)
        def _():
            db_ref[...] = jnp.zeros_like(db_ref)
            ds_ref[...] = jnp.zeros_like(ds_ref)

        valid = _attn_valid(first)
        q = q_ref[...]
        kvc = kvc_ref[...]
        kvp = kvp_ref[...]
        dov = do_ref[...]
        dqs, dks, dvs = [], [], []
        for hk in range(N_KV_HEADS):
            kcat, vcat = _kv_heads(kvp, kvc, hk)
            dk = jnp.zeros((2 * BLOCK, HEAD_DIM), F32)
            dv = jnp.zeros((2 * BLOCK, HEAD_DIM), F32)
            for g in range(GQA_GROUP):
                h = hk * GQA_GROUP + g
                qh = q[:, h * HEAD_DIM:(h + 1) * HEAD_DIM]
                doh = dov[:, h * HEAD_DIM:(h + 1) * HEAD_DIM]
                w, wsink = _attn_weights(qh, kcat, b_ref[h], valid, sink_ref[h])
                dv += _dot_tn(w.astype(BF16), doh)
                dw = _dot_nt(doh, vcat)
                delta = jnp.sum(w * dw, axis=-1, keepdims=True)
                ds = w * (dw - delta)
                db_ref[h] += ds
                tot = jnp.sum(-wsink * delta, axis=0, keepdims=True)
                ds_ref[h:h + 1, :] += jnp.broadcast_to(tot, (1, BLOCK))
                dsb = (ds * ATTN_SCALE).astype(BF16)
                dqs.append(_dot(dsb, kcat))
                dk += _dot_tn(dsb, qh)
            dks.append(dk)
            dvs.append(dv)
        dq_ref[...] = jnp.concatenate(dqs, axis=1).astype(BF16)
        both = jnp.concatenate(dks + dvs, axis=1)
        dp_ref[...] = both[:BLOCK]
        dc_ref[...] = both[BLOCK:]

    blk = pl.BlockSpec((BLOCK, BRANCH), lambda n: (n, 0))
    kvb = pl.BlockSpec((BLOCK, KV_WIDTH), lambda n: (n, 0))
    return pl.pallas_call(
        body, name="attn_bwd", grid=(S // BLOCK,), in_specs=[blk] + _attn_specs(),
        out_specs=[blk, kvb, kvb, _full((N_Q_HEADS, BLOCK, 2 * BLOCK)), _full((N_Q_HEADS, BLOCK))],
        out_shape=[jax.ShapeDtypeStruct((S, BRANCH), BF16), jax.ShapeDtypeStruct((S, KV_WIDTH), F32),
                   jax.ShapeDtypeStruct((S, KV_WIDTH), F32), jax.ShapeDtypeStruct((N_Q_HEADS, BLOCK, 2 * BLOCK), F32),
                   jax.ShapeDtypeStruct((N_Q_HEADS, BLOCK), F32)],
        compiler_params=_cparams("arbitrary"),
    )(do, z, z, z, bias, sinks)


def kv_shift_add(dcur, dprev):
    S = dcur.shape[0]
    nt = S // TM
    per_tile = TM // BLOCK

    def body(c_ref, p_ref, n_ref, o_ref):
        nxt = jnp.where(pl.program_id(0) == nt - 1, 0.0, n_ref[...])
        o_ref[...] = (c_ref[...] + jnp.concatenate([p_ref[BLOCK:, :], nxt], axis=0)).astype(BF16)

    tile = pl.BlockSpec((TM, KV_WIDTH), lambda i: (i, 0))
    return pl.pallas_call(
        body, name="kv_shift_add", grid=(nt,),
        in_specs=[tile, tile,
                  pl.BlockSpec((BLOCK, KV_WIDTH), lambda i: (jnp.minimum((i + 1) * per_tile, S // BLOCK - 1), 0))],
        out_specs=tile, out_shape=jax.ShapeDtypeStruct((S, KV_WIDTH), BF16), compiler_params=_cparams("parallel"),
    )(dcur, dprev, dprev)


def _ssm_disc(lam_re, lam_im, log_dt, bt_re, bt_im):
    dt = jnp.exp(log_dt)
    mag = jnp.exp(lam_re * dt)
    ang = lam_im * dt
    a_re = mag * jnp.cos(ang)
    a_im = mag * jnp.sin(ang)
    den = lam_re * lam_re + lam_im * lam_im
    nr = a_re - 1.0
    coef_re = (nr * lam_re + a_im * lam_im) / den
    coef_im = (a_im * lam_re - nr * lam_im) / den
    bb_re = coef_re[:, None, :] * bt_re - coef_im[:, None, :] * bt_im
    bb_im = coef_re[:, None, :] * bt_im + coef_im[:, None, :] * bt_re
    return a_re, a_im, bb_re, bb_im


_GN = (SSM_GROUPS, SSM_STATE)
_GPN = (SSM_GROUPS, SSM_GROUP, SSM_STATE)


def ssm_disc_fwd(lam_re, lam_im, log_dt, bt_re, bt_im):
    def body(lr_ref, li_ref, dt_ref, br_ref, bi_ref, ar_ref, ai_ref, bbr_ref, bbi_ref):
        a_re, a_im, bb_re, bb_im = _ssm_disc(lr_ref[...], li_ref[...], dt_ref[...], br_ref[...], bi_ref[...])
        ar_ref[...] = a_re
        ai_ref[...] = a_im
        bbr_ref[...] = bb_re
        bbi_ref[...] = bb_im

    return pl.pallas_call(
        body, name="ssm_disc_fwd", grid=(1,),
        in_specs=[_full(_GN), _full(_GN), _full((SSM_GROUPS, 1)), _full(_GPN), _full(_GPN)],
        out_specs=[_full(_GN), _full(_GN), _full(_GPN), _full(_GPN)],
        out_shape=[jax.ShapeDtypeStruct(s, F32) for s in (_GN, _GN, _GPN, _GPN)],
        compiler_params=_cparams("arbitrary"),
    )(lam_re, lam_im, log_dt, bt_re, bt_im)


def ssm_disc_bwd(lam_re, lam_im, log_dt, bt_re, bt_im, da_re, da_im, dbb_re, dbb_im):
    def body(lr_ref, li_ref, dt_ref, br_ref, bi_ref, dar_ref, dai_ref, dbr_ref, dbi_ref, o_lr, o_li, o_dt, o_br, o_bi):
        prim = (lr_ref[...], li_ref[...], dt_ref[...], br_ref[...], bi_ref[...])
        _, vjp = jax.vjp(_ssm_disc, *prim)
        grads = vjp((dar_ref[...], dai_ref[...], dbr_ref[...], dbi_ref[...]))
        for r, v in zip((o_lr, o_li, o_dt, o_br, o_bi), grads):
            r[...] = v

    shapes = (_GN, _GN, (SSM_GROUPS, 1), _GPN, _GPN)
    return pl.pallas_call(
        body, name="ssm_disc_bwd", grid=(1,),
        in_specs=[_full(s) for s in shapes + (_GN, _GN, _GPN, _GPN)],
        out_specs=[_full(s) for s in shapes], out_shape=[jax.ShapeDtypeStruct(s, F32) for s in shapes],
        compiler_params=_cparams("arbitrary"),
    )(lam_re, lam_im, log_dt, bt_re, bt_im, da_re, da_im, dbb_re, dbb_im)


LANE_GROUPS = SSM_LANES // 128
SUB_GROUPS = SUB_ST // 128
_TM_SHAPE = (LANE_GROUPS, 128)


def _step_rows(t):
    return pl.ds(pl.multiple_of(t * LANE_GROUPS, LANE_GROUPS), LANE_GROUPS)


def _group_rows(j):
    return pl.ds(j, SCAN_T, stride=LANE_GROUPS)


def _store_sub(ref, j, val):
    for k in range(SUB_GROUPS):
        ref[_group_rows(j * SUB_GROUPS + k), :] = val[:, k * 128:(k + 1) * 128]


def _load_sub(ref, j):
    return jnp.concatenate([ref[_group_rows(j * SUB_GROUPS + k), :] for k in range(SUB_GROUPS)], axis=1)


_SUB_SHAPE_IN = (SSM_SUB, SUB_IN, SUB_ST)
_SUB_SHAPE_OUT = (SSM_SUB, SUB_ST, SUB_IN)


def ssm_fwd(z, bb_re, bb_im, ct_re, ct_im, a_re, a_im, d_skip, wglu, side=None):
    S = z.shape[0]
    cu = COL_U // BRANCH

    def body(u_ref, bbr_ref, bbi_ref, ctr_ref, cti_ref, ar_ref, ai_ref, d_ref, wg_ref,
             y_ref, ypre_ref, hr_ref, hi_ref, bur, bui, car_r, car_i):
        @pl.when(pl.program_id(0) == 0)
        def _():
            car_r[...] = jnp.zeros_like(car_r)
            car_i[...] = jnp.zeros_like(car_i)

        u = u_ref[...]
        for j in range(SSM_SUB):
            uj = u[:, j * SUB_IN:(j + 1) * SUB_IN]
            _store_sub(bur, j, _dot(uj, bbr_ref[j]))
            _store_sub(bui, j, _dot(uj, bbi_ref[j]))
        ar = ar_ref[...]
        ai = ai_ref[...]

        def step(t, carry):
            hr, hi = carry
            rows = _step_rows(t)
            nhr = ar * hr - ai * hi + bur[rows, :]
            nhi = ar * hi + ai * hr + bui[rows, :]
            hr_ref[rows, :] = nhr
            hi_ref[rows, :] = nhi
            return nhr, nhi

        hr, hi = lax.fori_loop(0, SCAN_T, step, (car_r[...], car_i[...]), unroll=8)
        car_r[...] = hr
        car_i[...] = hi
        ys = []
        for j in range(SSM_SUB):
            ys.append(_dot(_load_sub(hr_ref, j).astype(BF16), ctr_ref[j])
                      - _dot(_load_sub(hi_ref, j).astype(BF16), cti_ref[j]))
        ypre = jnp.concatenate(ys, axis=1) + d_ref[...] * u.astype(F32)
        ypre_ref[...] = ypre
        g = jax.nn.gelu(ypre)
        y_ref[...] = (g * jax.nn.sigmoid(_dot(g.astype(BF16), wg_ref[...]))).astype(BF16)

    row = pl.BlockSpec((SCAN_T, BRANCH), lambda i: (i, 0))
    st = pl.BlockSpec((SCAN_T * LANE_GROUPS, 128), lambda i: (i, 0))
    return _call(
        body, name="ssm_fwd", steps=S // SCAN_T,
        in_specs=[pl.BlockSpec((SCAN_T, BRANCH), lambda i: (i, cu)), _full(_SUB_SHAPE_IN), _full(_SUB_SHAPE_IN),
                  _full(_SUB_SHAPE_OUT), _full(_SUB_SHAPE_OUT), _full(_TM_SHAPE), _full(_TM_SHAPE), _full((1, BRANCH)),
                  _full((BRANCH, BRANCH))],
        out_specs=[row, row, st, st],
        out_shape=[jax.ShapeDtypeStruct((S, BRANCH), BF16), jax.ShapeDtypeStruct((S, BRANCH), F32),
                   jax.ShapeDtypeStruct((S * LANE_GROUPS, 128), F32), jax.ShapeDtypeStruct((S * LANE_GROUPS, 128), F32)],
        scratch=[pltpu.VMEM((SCAN_T * LANE_GROUPS, 128), F32), pltpu.VMEM((SCAN_T * LANE_GROUPS, 128), F32),
                 pltpu.VMEM(_TM_SHAPE, F32), pltpu.VMEM(_TM_SHAPE, F32)],
        args=(z, bb_re, bb_im, ct_re, ct_im, a_re, a_im, d_skip, wglu), side=side)


def ssm_bwd(dy, z, ypre, h_re, h_im, bbt_re, bbt_im, c_re, c_im, a_re, a_im, d_skip, wglu, wglu_t, side=None):
    S = z.shape[0]
    nt = S // SCAN_T
    cu = COL_U // BRANCH

    def body(dy_ref, u_ref, ypre_ref, hr_ref, hi_ref, hpr_ref, hpi_ref, bbr_ref, bbi_ref, cr_ref, ci_ref, ar_ref, ai_ref,
             d_ref, wg_ref, wgt_ref,
             du_ref, dbbr_ref, dbbi_ref, dctr_ref, dcti_ref, dar_ref, dai_ref, dd_ref, dwg_ref,
             lr_scr, li_scr, car_r, car_i):
        step = pl.program_id(0)

        @pl.when(step == 0)
        def _():
            for r in (dbbr_ref, dbbi_ref, dctr_ref, dcti_ref, dar_ref, dai_ref, dd_ref, dwg_ref, car_r, car_i):
                r[...] = jnp.zeros_like(r)

        u = u_ref[...]
        uf = u.astype(F32)
        dyv = dy_ref[...].astype(F32)
        g, gelu_vjp = jax.vjp(jax.nn.gelu, ypre_ref[...])
        gb = g.astype(BF16)
        sg = jax.nn.sigmoid(_dot(gb, wg_ref[...]))
        dgl = (dyv * g * sg * (1.0 - sg)).astype(BF16)
        dwg_ref[...] += _dot_tn(gb, dgl)
        dg = dyv * sg + _dot(dgl, wgt_ref[...])
        dypre = gelu_vjp(dg)[0]
        dd_ref[...] += jnp.sum(dypre * uf, axis=0, keepdims=True)
        dyb = dypre.astype(BF16)
        for j in range(SSM_SUB):
            dyj = dyb[:, j * SUB_IN:(j + 1) * SUB_IN]
            _store_sub(lr_scr, j, _dot(dyj, cr_ref[j]))
            _store_sub(li_scr, j, -_dot(dyj, ci_ref[j]))
            dctr_ref[j] += _dot_tn(_load_sub(hr_ref, j).astype(BF16), dyj)
            dcti_ref[j] -= _dot_tn(_load_sub(hi_ref, j).astype(BF16), dyj)

        ar = ar_ref[...]
        ai = ai_ref[...]

        def adjoint(lr, li, rows):
            nlr = ar * lr + ai * li + lr_scr[rows, :]
            nli = ar * li - ai * lr + li_scr[rows, :]
            lr_scr[rows, :] = nlr
            li_scr[rows, :] = nli
            return nlr, nli

        def back(k, carry):
            lr, li, acc_r, acc_i = carry
            t = SCAN_T - 1 - k
            lr, li = adjoint(lr, li, _step_rows(t))
            hpr = hr_ref[_step_rows(t - 1), :]
            hpi = hi_ref[_step_rows(t - 1), :]
            return lr, li, acc_r + lr * hpr + li * hpi, acc_i + li * hpr - lr * hpi

        zero = jnp.zeros(_TM_SHAPE, F32)
        lr, li, acc_r, acc_i = lax.fori_loop(0, SCAN_T - 1, back, (car_r[...], car_i[...], zero, zero), unroll=8)
        lr, li = adjoint(lr, li, pl.ds(0, LANE_GROUPS))
        car_r[...] = lr
        car_i[...] = li
        first_tile = step == nt - 1
        hpr = jnp.where(first_tile, 0.0, hpr_ref[...])
        hpi = jnp.where(first_tile, 0.0, hpi_ref[...])
        dar_ref[...] += acc_r + lr * hpr + li * hpi
        dai_ref[...] += acc_i + li * hpr - lr * hpi

        dus = []
        for j in range(SSM_SUB):
            lrb = _load_sub(lr_scr, j).astype(BF16)
            lib = _load_sub(li_scr, j).astype(BF16)
            uj = u[:, j * SUB_IN:(j + 1) * SUB_IN]
            dus.append(_dot(lrb, bbr_ref[j]) + _dot(lib, bbi_ref[j]))
            dbbr_ref[j] += _dot_tn(uj, lrb)
            dbbi_ref[j] += _dot_tn(uj, lib)
        du_ref[...] = (jnp.concatenate(dus, axis=1) + dypre * d_ref[...]).astype(BF16)

    def rev(i):
        return nt - 1 - i

    row = pl.BlockSpec((SCAN_T, BRANCH), lambda i: (rev(i), 0))
    st = pl.BlockSpec((SCAN_T * LANE_GROUPS, 128), lambda i: (rev(i), 0))
    before = pl.BlockSpec(_TM_SHAPE, lambda i: (jnp.maximum(rev(i) * SCAN_T - 1, 0), 0))
    tm = _full(_TM_SHAPE)
    return _call(
        body, name="ssm_bwd", steps=nt,
        in_specs=[row, pl.BlockSpec((SCAN_T, BRANCH), lambda i: (rev(i), cu)), row, st, st, before, before,
                  _full(_SUB_SHAPE_OUT), _full(_SUB_SHAPE_OUT), _full(_SUB_SHAPE_IN), _full(_SUB_SHAPE_IN),
                  tm, tm, _full((1, BRANCH)), _full((BRANCH, BRANCH)), _full((BRANCH, BRANCH))],
        out_specs=[row, _full(_SUB_SHAPE_IN), _full(_SUB_SHAPE_IN), _full(_SUB_SHAPE_OUT), _full(_SUB_SHAPE_OUT),
                   tm, tm, _full((1, BRANCH)), _full((BRANCH, BRANCH))],
        out_shape=[jax.ShapeDtypeStruct((S, BRANCH), BF16), jax.ShapeDtypeStruct(_SUB_SHAPE_IN, F32),
                   jax.ShapeDtypeStruct(_SUB_SHAPE_IN, F32), jax.ShapeDtypeStruct(_SUB_SHAPE_OUT, F32),
                   jax.ShapeDtypeStruct(_SUB_SHAPE_OUT, F32), jax.ShapeDtypeStruct(_TM_SHAPE, F32),
                   jax.ShapeDtypeStruct(_TM_SHAPE, F32), jax.ShapeDtypeStruct((1, BRANCH), F32),
                   jax.ShapeDtypeStruct((BRANCH, BRANCH), F32)],
        scratch=[pltpu.VMEM((SCAN_T * LANE_GROUPS, 128), F32), pltpu.VMEM((SCAN_T * LANE_GROUPS, 128), F32),
                 pltpu.VMEM(_TM_SHAPE, F32), pltpu.VMEM(_TM_SHAPE, F32)],
        args=(dy, z, ypre, h_re, h_im, h_re, h_im, bbt_re, bbt_im, c_re, c_im, a_re, a_im, d_skip, wglu, wglu_t),
        side=side)


def _blockdiag(x):
    gs = SSM_GROUPS // SSM_SUB
    x = x.reshape(SSM_SUB, gs, SSM_GROUP, SSM_STATE)
    eye = jnp.eye(gs, dtype=x.dtype)
    return (x[:, :, :, None, :] * eye[None, :, None, :, None]).reshape(SSM_SUB, SUB_IN, SUB_ST)


def _blockdiag_extract(x):
    gs = SSM_GROUPS // SSM_SUB
    x = x.reshape(SSM_SUB, gs, SSM_GROUP, gs, SSM_STATE)
    eye = jnp.eye(gs, dtype=x.dtype)
    return jnp.sum(x * eye[None, :, None, :, None], axis=3).reshape(SSM_GROUPS, SSM_GROUP, SSM_STATE)


def loss_head(x, g, target):
    S = x.shape[0]

    def body(x_ref, g_ref, t_ref, dx_ref, loss_ref, dg_ref):
        @pl.when(pl.program_id(0) == 0)
        def _():
            loss_ref[...] = jnp.zeros_like(loss_ref)
            dg_ref[...] = jnp.zeros_like(dg_ref)

        xv = x_ref[...]
        gv = g_ref[...]
        r = lax.rsqrt(jnp.mean(xv * xv, axis=-1, keepdims=True) + RMS_EPS)
        xhat = xv * r
        err = xhat * gv - t_ref[...]
        loss_ref[...] += jnp.sum((err * err).reshape(TM // 8, 8, D_MODEL), axis=0) * (0.5 / D_MODEL)
        dy = err * (1.0 / D_MODEL)
        dxhat = dy * gv
        dx_ref[...] = r * (dxhat - xhat * jnp.mean(dxhat * xhat, axis=-1, keepdims=True))
        dg_ref[...] += jnp.sum((dy * xhat).reshape(TM // 8, 8, D_MODEL), axis=0)

    row = pl.BlockSpec((TM, D_MODEL), lambda i: (i, 0))
    acc = _full((8, D_MODEL))
    return pl.pallas_call(
        body, name="loss_head", grid=(S // TM,), in_specs=[row, _full((1, D_MODEL)), row],
        out_specs=[row, acc, acc],
        out_shape=[jax.ShapeDtypeStruct((S, D_MODEL), F32), jax.ShapeDtypeStruct((8, D_MODEL), F32),
                   jax.ShapeDtypeStruct((8, D_MODEL), F32)],
        compiler_params=_cparams("arbitrary"),
    )(x, g, target)


def _x_spec():
    return pl.BlockSpec((TM, D_MODEL), lambda i: (i, 0))


def _g_spec():
    return pl.BlockSpec((1, D_MODEL), lambda i: (0, 0))


def _norm_prologue(x, g):
    h = _rms(x, g).astype(BF16)
    return h, h


def _cast_prologue(x):
    return (x.astype(BF16),)


def _swiglu_prologue(h1, h2):
    a = h1.astype(F32)
    act = (a * jax.nn.sigmoid(a) * h2.astype(F32)).astype(BF16)
    return act, act


def _concat_prologue(*pieces):
    return (jnp.concatenate(pieces, axis=1),)


def _ssm_consts(lw):
    a_re, a_im, bbt_re, bbt_im = ssm_disc_fwd(
        lw["ssm_lambda_re"], lw["ssm_lambda_im"], lw["ssm_log_dt"].reshape(SSM_GROUPS, 1), lw["bt_re"], lw["bt_im"])
    bb_re = _blockdiag(bbt_re).astype(BF16)
    bb_im = _blockdiag(bbt_im).astype(BF16)
    c_re = _blockdiag(lw["ssm_c_re"]).astype(BF16)
    c_im = _blockdiag(lw["ssm_c_im"]).astype(BF16)
    return dict(
        a_re=a_re.reshape(_TM_SHAPE), a_im=a_im.reshape(_TM_SHAPE),
        bb_re=bb_re, bb_im=bb_im, bbt_re=jnp.swapaxes(bb_re, 1, 2), bbt_im=jnp.swapaxes(bb_im, 1, 2),
        c_re=c_re, c_im=c_im, ct_re=jnp.swapaxes(c_re, 1, 2), ct_im=jnp.swapaxes(c_im, 1, 2))


def layer_fwd(x, lw, bias, next_shards=None, place=None):
    z, h = fused_mm("in_proj", [x, lw["norm_mix"]], [_x_spec(), _g_spec()], _norm_prologue, lw["w_in"], tn=2944,
                    out_dtype=BF16, extras=((D_MODEL, BF16),))
    sc = _ssm_consts(lw)
    (y_ssm, ypre, h_re, h_im), g4 = ssm_fwd(
        z, sc["bb_re"], sc["bb_im"], sc["ct_re"], sc["ct_im"], sc["a_re"], sc["a_im"], lw["ssm_d"], lw["ssm_w_glu"],
        side=None if next_shards is None else ("gather_chips", next_shards))
    y_conv = conv_fwd(z, lw["conv_w"])
    if next_shards is not None:
        g4 = [_put_slot(g, b, place[0]) for g, b in zip(g4, next_shards)]
    y_attn, g8 = attn_fwd(z, bias, lw["attn_sinks"], side=None if next_shards is None else ("gather_cores", g4))
    next_gathered = None if next_shards is None else [_put_slot(g, b, place[1]) for g, b in zip(g8, g4)]
    merged = merge_fwd(z, y_ssm, y_conv, y_attn, lw["w_branch"])
    x1 = fused_mm("out_proj", [merged], [_x_spec()], _cast_prologue, lw["w_out"], tn=1024, out_dtype=F32, res=x)
    hf, hn1 = fused_mm("ffn_in", [x1, lw["norm_ffn"]], [_x_spec(), _g_spec()], _norm_prologue, lw["w_ffn_in"], tn=2816,
                       out_dtype=BF16, extras=((D_MODEL, BF16),))
    x2, act = fused_mm("ffn_out", [hf, hf], [_row_spec(FFN_HIDDEN, 0), _row_spec(FFN_HIDDEN, 1)], _swiglu_prologue,
                       lw["w_ffn_out"], tn=1024, out_dtype=F32, res=x1, extras=((FFN_HIDDEN, BF16),))
    a_pre, hn2 = fused_mm("ple_gate", [x2, lw["norm_ple"]], [_x_spec(), _g_spec()], _norm_prologue, lw["w_ple_gate"],
                          tn=1024, out_dtype=BF16, extras=((D_MODEL, BF16),))
    pp = fused_mm("ple_proj", [lw["p"]], [_row_spec(PLE_DIM)], _cast_prologue, lw["w_ple_proj"], tn=1024, out_dtype=BF16)
    x3 = ple_combine(x2, a_pre, pp)
    res = dict(x=x, z=z, h=h, y_ssm=y_ssm, ypre=ypre, h_re=h_re, h_im=h_im, y_conv=y_conv, y_attn=y_attn, merged=merged,
               x1=x1, hf=hf, hn1=hn1, act=act, x2=x2, a_pre=a_pre, hn2=hn2, pp=pp)
    return x3, res, next_gathered


def _pair_sums(split, from_sibling):
    return [pair_sum("pair_sum_" + n, a.reshape(2, -1, a.shape[-1]), b.reshape(-1, b.shape[-1])).reshape(b.shape)
            for n, a, b in zip(SHARDED, split, from_sibling)]


def layer_bwd(dx3, lw, res, bias, pending=None):
    g = {}
    da, dpp = ple_bwd(dx3, res["a_pre"], res["pp"])
    g["w_ple_proj"] = mm_tn("d_w_ple_proj", lw["p"], dpp)
    g["w_ple_gate"] = mm_tn("d_w_ple_gate", res["hn2"], da)
    dhn2 = fused_mm("d_ple_gate", [da], [_x_spec()], _cast_prologue, lw["w_ple_gate_t"], tn=1024, out_dtype=BF16)
    dx2, g["norm_ple"] = norm_bwd("ple_norm_bwd", dhn2, res["x2"], lw["norm_ple"], dx3)
    sums = None
    if pending is None:
        dact = fused_mm("d_ffn_out", [dx2], [_x_spec()], _cast_prologue, lw["w_ffn_out_t"], tn=1408, out_dtype=BF16)
    else:
        dact, from_sibling = fused_mm("d_ffn_out", [dx2], [_x_spec()], _cast_prologue, lw["w_ffn_out_t"], tn=1408,
                                      out_dtype=BF16, side=("scatter_cores", pending))
        sums = _pair_sums(pending, from_sibling)
    g["w_ffn_out"] = mm_tn("d_w_ffn_out", res["act"], dx2)
    dh1, dh2 = swiglu_bwd(dact, res["hf"])
    g["w_ffn_in"] = jnp.concatenate([mm_tn("d_w_ffn_in_a", res["hn1"], dh1), mm_tn("d_w_ffn_in_b", res["hn1"], dh2)],
                                    axis=1)
    dhn1 = fused_mm("d_ffn_in", [dh1, dh2], [_row_spec(FFN_HIDDEN), _row_spec(FFN_HIDDEN)], _concat_prologue,
                    lw["w_ffn_in_t"], tn=1024, out_dtype=BF16)
    dx1, g["norm_ffn"] = norm_bwd("ffn_norm_bwd", dhn1, res["x1"], lw["norm_ffn"], dx2)
    dmerged = fused_mm("d_out_proj", [dx1], [_x_spec()], _cast_prologue, lw["w_out_t"], tn=1024, out_dtype=BF16)
    g["w_out"] = mm_tn("d_w_out", res["merged"], dx1)
    z = res["z"]
    ys = (res["y_ssm"], res["y_conv"], res["y_attn"])
    dgates, dbs = merge_bwd(dmerged, z, *ys, lw["w_branch"])
    dys, dwb = [], []
    for r in range(3):
        dys.append(fused_mm(f"d_branch_{r}", [dbs[r]], [_x_spec()], _cast_prologue, lw["w_branch_t"][r], tn=1024,
                            out_dtype=BF16))
        dwb.append(mm_tn(f"d_w_branch_{r}", ys[r], dbs[r]))
    g["w_branch"] = jnp.stack(dwb)
    sc = _ssm_consts(lw)
    (du, dbb_re, dbb_im, dct_re, dct_im, da_re, da_im, g["ssm_d"], g["ssm_w_glu"]), received = ssm_bwd(
        dys[0], z, res["ypre"], res["h_re"], res["h_im"], sc["bbt_re"], sc["bbt_im"], sc["c_re"], sc["c_im"],
        sc["a_re"], sc["a_im"], lw["ssm_d"], lw["ssm_w_glu"], lw["ssm_w_glu_t"],
        side=None if pending is None else ("scatter_chips", sums))
    g["ssm_c_re"] = _blockdiag_extract(jnp.swapaxes(dct_re, 1, 2))
    g["ssm_c_im"] = _blockdiag_extract(jnp.swapaxes(dct_im, 1, 2))
    (g["ssm_lambda_re"], g["ssm_lambda_im"], dlog_dt, g["bt_re"], g["bt_im"]) = ssm_disc_bwd(
        lw["ssm_lambda_re"], lw["ssm_lambda_im"], lw["ssm_log_dt"].reshape(SSM_GROUPS, 1), lw["bt_re"], lw["bt_im"],
        da_re.reshape(_GN), da_im.reshape(_GN),
        _blockdiag_extract(dbb_re), _blockdiag_extract(dbb_im))
    g["ssm_log_dt"] = dlog_dt.reshape(SSM_GROUPS)
    dconv, g["conv_w"] = conv_bwd(dys[1], z, lw["conv_w"])
    dq, dkv_cur, dkv_prev, g["dbias"], g["attn_sinks"] = attn_bwd(dys[2], z, bias, lw["attn_sinks"])
    dkv = kv_shift_add(dkv_cur, dkv_prev)
    pieces = [dgates[0], dgates[1], dgates[2], du, dconv, dq, dkv]
    widths = [D_MODEL, D_MODEL, D_MODEL, BRANCH, 3 * BRANCH, BRANCH, KV_WIDTH]
    g["w_in"] = jnp.concatenate([mm_tn(f"d_w_in_{k}", res["h"], pc) for k, pc in enumerate(pieces)], axis=1)
    dhn0 = fused_mm("d_in_proj", pieces, [_row_spec(w) for w in widths], _concat_prologue, lw["w_in_t"], tn=1024,
                    out_dtype=BF16)
    dx0, g["norm_mix"] = norm_bwd("mix_norm_bwd", dhn0, res["x"], lw["norm_mix"], dx1)
    return dx0, g, (sums, received)


def adamw(name, parts, w, m, v):
    n, R, C = parts.shape
    tr = _pick(R, (512, 256, 128, 64, 32, 16, 8))

    def body(p_ref, w_ref, m_ref, v_ref, g_ref, d_ref, nm_ref, nv_ref):
        gsum = p_ref[0].astype(F32)
        for k in range(1, n):
            gsum = gsum + p_ref[k].astype(F32)
        mn = ADAM_B1 * m_ref[...] + (1.0 - ADAM_B1) * gsum
        vn = ADAM_B2 * v_ref[...] + (1.0 - ADAM_B2) * jnp.square(gsum)
        m_hat = mn / (1.0 - ADAM_B1 ** ADAM_STEP)
        v_hat = vn / (1.0 - ADAM_B2 ** ADAM_STEP)
        g_ref[...] = gsum
        d_ref[...] = -ADAM_LR * (m_hat / (jnp.sqrt(v_hat) + ADAM_EPS) + ADAM_WD * w_ref[...])
        nm_ref[...] = mn
        nv_ref[...] = vn

    blk = pl.BlockSpec((tr, C), lambda i: (i, 0))
    return pl.pallas_call(
        body, name=name, grid=(R // tr,), in_specs=[pl.BlockSpec((n, tr, C), lambda i: (0, i, 0)), blk, blk, blk],
        out_specs=[blk] * 4, out_shape=[jax.ShapeDtypeStruct((R, C), F32)] * 4, compiler_params=_cparams("parallel"),
    )(parts, w, m, v)


_ANY = pl.BlockSpec(memory_space=pl.ANY)


def _coords():
    return lax.axis_index("x"), lax.axis_index("y"), lax.axis_index("c")


def _chip_peers(x, y):
    return [(1 - x, y), (x, 1 - y), (1 - x, 1 - y)]


def _gather_chips_copies(x_refs, out_refs, send_sems, recv_sems):
    x, y, c = _coords()
    me = 2 * x + y
    peers = _chip_peers(x, y)

    def copy(i, k, slot):
        return pltpu.make_async_remote_copy(
            src_ref=x_refs[i], dst_ref=out_refs[i].at[slot], send_sem=send_sems.at[3 * i + k],
            recv_sem=recv_sems.at[3 * i + k], device_id=(*peers[k], c), device_id_type=MESH)

    n = len(x_refs)
    sends = [copy(i, k, me) for i in range(n) for k in range(3)]
    recvs = [copy(i, k, 2 * px + py) for i in range(n) for k, (px, py) in enumerate(peers)]
    return sends, recvs


def _gather_cores_copies(x_refs, out_refs, send_sems, recv_sems):
    x, y, c = _coords()

    def copy(i, slot):
        return pltpu.make_async_remote_copy(
            src_ref=x_refs[i], dst_ref=out_refs[i].at[slot], send_sem=send_sems.at[i], recv_sem=recv_sems.at[i],
            device_id=(x, y, 1 - c), device_id_type=MESH)

    n = len(x_refs)
    return [copy(i, c) for i in range(n)], [copy(i, 1 - c) for i in range(n)]


def _scatter_cores_copies(x_refs, out_refs, send_sems, recv_sems):
    x, y, c = _coords()
    copies = [pltpu.make_async_remote_copy(
        src_ref=x_refs[i].at[1 - c], dst_ref=out_refs[i], send_sem=send_sems.at[i], recv_sem=recv_sems.at[i],
        device_id=(x, y, 1 - c), device_id_type=MESH) for i in range(len(x_refs))]
    return copies, copies


def _scatter_chips_copies(x_refs, out_refs, send_sems, recv_sems):
    x, y, c = _coords()
    me = 2 * x + y
    peers = _chip_peers(x, y)

    def copy(i, k, src_slot, dst_slot):
        return pltpu.make_async_remote_copy(
            src_ref=x_refs[i].at[src_slot], dst_ref=out_refs[i].at[dst_slot], send_sem=send_sems.at[3 * i + k],
            recv_sem=recv_sems.at[3 * i + k], device_id=(*peers[k], c), device_id_type=MESH)

    n = len(x_refs)
    sends = [copy(i, k, 2 * px + py, me) for i in range(n) for k, (px, py) in enumerate(peers)]
    recvs = [copy(i, k, me, 2 * px + py) for i in range(n) for k, (px, py) in enumerate(peers)]
    return sends, recvs


_EXCHANGES = {
    "gather_chips": (lambda b: (4,) + b.shape, 3, _gather_chips_copies),
    "gather_cores": (lambda b: (2,) + b.shape, 1, _gather_cores_copies),
    "scatter_cores": (lambda b: b.shape[1:], 1, _scatter_cores_copies),
    "scatter_chips": (lambda b: b.shape, 3, _scatter_chips_copies),
}


def exchange(name, kind, blks):
    out_shape_of, per_block, make_copies = _EXCHANGES[kind]
    n = len(blks)

    def body(*refs):
        sends, recvs = make_copies(refs[:n], refs[n:2 * n], refs[2 * n], refs[2 * n + 1])
        for cp in sends:
            cp.start()
        for cp in recvs:
            cp.wait_recv()
        for cp in sends:
            cp.wait_send()

    return pl.pallas_call(
        body, name=name, in_specs=[_ANY] * n, out_specs=[_ANY] * n,
        out_shape=[jax.ShapeDtypeStruct(out_shape_of(b), b.dtype) for b in blks],
        scratch_shapes=[pltpu.SemaphoreType.DMA((per_block * n,)), pltpu.SemaphoreType.DMA((per_block * n,))],
    )(*blks)


def _put_slot(buf, block, idx):
    return lax.dynamic_update_slice(buf, block[None].astype(buf.dtype), (idx,) + (0,) * block.ndim)


def pair_sum(name, mine, theirs):
    _, R, C = mine.shape
    tr = _pick(R, (1024, 512, 256, 128, 64, 32, 16))
    c_idx = lax.axis_index("c").astype(jnp.int32).reshape(1)

    def body(c_ref, a_ref, b_ref, o_ref):
        o_ref[...] = (a_ref[0].astype(F32) + b_ref[...].astype(F32)).astype(BF16)

    return pl.pallas_call(
        body, name=name,
        grid_spec=pltpu.PrefetchScalarGridSpec(
            num_scalar_prefetch=1, grid=(R // tr,),
            in_specs=[pl.BlockSpec((1, tr, C), lambda i, c: (c[0], i, 0)), pl.BlockSpec((tr, C), lambda i, c: (i, 0))],
            out_specs=pl.BlockSpec((tr, C), lambda i, c: (i, 0))),
        out_shape=jax.ShapeDtypeStruct(theirs.shape, BF16), compiler_params=_cparams("parallel"),
    )(c_idx, mine, theirs)


SHARDED = {
    "w_in": ((D_MODEL, IN_WIDTH), 2), "ssm_w_glu": ((BRANCH, BRANCH), 1), "conv_w": ((3, BRANCH), 2),
    "w_branch": ((3, BRANCH, D_MODEL), 3), "w_out": ((D_MODEL, D_MODEL), 1), "w_ffn_in": ((D_MODEL, 2 * FFN_HIDDEN), 2),
    "w_ffn_out": ((FFN_HIDDEN, D_MODEL), 1), "w_ple_gate": ((D_MODEL, D_MODEL), 1), "w_ple_proj": ((PLE_DIM, D_MODEL), 2),
}
SMALL = ["rel_bias", "norm_mix", "ssm_lambda_re", "ssm_lambda_im", "ssm_b_re", "ssm_b_im", "ssm_c_re", "ssm_c_im", "ssm_d",
         "ssm_log_dt", "attn_sinks", "norm_ffn", "norm_ple", "norm_final"]
WEIGHTS = ["rel_bias", "norm_mix", "w_in", "ssm_lambda_re", "ssm_lambda_im", "ssm_b_re", "ssm_b_im", "ssm_c_re", "ssm_c_im",
           "ssm_d", "ssm_log_dt", "ssm_w_glu", "conv_w", "attn_sinks", "w_branch", "w_out", "norm_ffn", "w_ffn_in",
           "w_ffn_out", "norm_ple", "w_ple_gate", "w_ple_proj", "norm_final"]


def _pad_to(flat, n):
    return jnp.pad(flat, [(0, 0)] * (flat.ndim - 1) + [(0, n - flat.shape[-1])])


def _unshard(g8, name):
    axis = SHARDED[name][1] - 1
    shard = g8.shape[2:]
    b = g8.reshape((2, 2, 2) + shard)
    b = jnp.moveaxis(b, (1, 2, 0), (axis, axis + 1, axis + 2))
    full = list(shard)
    full[axis] *= N_DEV
    return b.reshape(full)


def _shard_split(full, name):
    axis = SHARDED[name][1] - 1
    dims = list(full.shape)
    dims[axis:axis + 1] = [2, 2, 2, dims[axis] // N_DEV]
    b = jnp.moveaxis(full.reshape(dims), (axis, axis + 1, axis + 2), (1, 2, 0))
    return b.reshape((2, 4) + b.shape[3:])


def _small_sizes(shapes):
    return [-(-int(np.prod(shapes[n])) // 128) * 128 for n in SMALL]


def pack_small(vals, shapes, extra):
    segs = [_pad_to(vals[n].reshape(-1).astype(F32), s) for n, s in zip(SMALL, _small_sizes(shapes))]
    segs.append(_pad_to(extra.reshape(-1), 128))
    flat = jnp.concatenate(segs)
    rows = -(-flat.shape[0] // (128 * 8)) * 8
    return _pad_to(flat, rows * 128).reshape(rows, 128)


def unpack_small(packed, shapes):
    flat = packed.reshape(-1)
    out, off = {}, 0
    for n, s in zip(SMALL, _small_sizes(shapes)):
        out[n] = flat[off:off + int(np.prod(shapes[n]))].reshape(shapes[n])
        off += s
    return out, flat[off]


def _layer_weights(gathered, small, p, i):
    full = {n: _unshard(g, n) for n, g in zip(SHARDED, gathered)}
    w_in = full["w_in"]
    w_in_p = jnp.concatenate([w_in[:, 2816:], w_in[:, :2560], w_in[:, 2560:2816]], axis=-1)
    return dict(
        w_in=w_in_p, w_in_t=w_in_p.T,
        ssm_w_glu=full["ssm_w_glu"], ssm_w_glu_t=full["ssm_w_glu"].T,
        conv_w=full["conv_w"],
        w_branch=full["w_branch"], w_branch_t=jnp.swapaxes(full["w_branch"], 1, 2),
        w_out=full["w_out"], w_out_t=full["w_out"].T,
        w_ffn_in=full["w_ffn_in"], w_ffn_in_t=full["w_ffn_in"].T,
        w_ffn_out=full["w_ffn_out"], w_ffn_out_t=full["w_ffn_out"].T,
        w_ple_gate=full["w_ple_gate"], w_ple_gate_t=full["w_ple_gate"].T,
        w_ple_proj=full["w_ple_proj"],
        norm_mix=small["norm_mix"][i][None, :], norm_ffn=small["norm_ffn"][i][None, :],
        norm_ple=small["norm_ple"][i][None, :],
        ssm_lambda_re=small["ssm_lambda_re"][i], ssm_lambda_im=small["ssm_lambda_im"][i],
        ssm_log_dt=small["ssm_log_dt"][i],
        bt_re=jnp.swapaxes(small["ssm_b_re"][i], 1, 2), bt_im=jnp.swapaxes(small["ssm_b_im"][i], 1, 2),
        ssm_c_re=small["ssm_c_re"][i], ssm_c_im=small["ssm_c_im"][i], ssm_d=small["ssm_d"][i][None, :],
        attn_sinks=small["attn_sinks"][i], p=p[i],
    )


def matrix_grads(g):
    w_in_g = g["w_in"]
    return dict(
        w_in=jnp.concatenate([w_in_g[:, 3072:5632], w_in_g[:, 5632:], w_in_g[:, :3072]], axis=-1),
        ssm_w_glu=g["ssm_w_glu"], conv_w=jnp.sum(g["conv_w"].reshape(3, 8, BRANCH), axis=1),
        w_branch=g["w_branch"], w_out=g["w_out"], w_ffn_in=g["w_ffn_in"], w_ffn_out=g["w_ffn_out"],
        w_ple_gate=g["w_ple_gate"], w_ple_proj=g["w_ple_proj"])


def small_grads(per_layer, dg_final):
    keys = ("dbias", "norm_mix", "ssm_lambda_re", "ssm_lambda_im", "bt_re", "bt_im", "ssm_c_re", "ssm_c_im", "ssm_d",
            "ssm_log_dt", "attn_sinks", "norm_ffn", "norm_ple")
    g = {k: jnp.stack([gl[k] for gl in per_layer]) for k in keys}
    drel = rel_bias_bwd(g["dbias"].reshape(DEPTH, N_Q_HEADS, BLOCK * 2 * BLOCK)).T
    return dict(
        rel_bias=drel, norm_mix=jnp.sum(g["norm_mix"], axis=1), ssm_lambda_re=g["ssm_lambda_re"],
        ssm_lambda_im=g["ssm_lambda_im"], ssm_b_re=jnp.swapaxes(g["bt_re"], 2, 3), ssm_b_im=jnp.swapaxes(g["bt_im"], 2, 3),
        ssm_c_re=g["ssm_c_re"], ssm_c_im=g["ssm_c_im"], ssm_d=g["ssm_d"][:, 0, :], ssm_log_dt=g["ssm_log_dt"],
        attn_sinks=g["attn_sinks"][:, :, 0], norm_ffn=jnp.sum(g["norm_ffn"], axis=1), norm_ple=jnp.sum(g["norm_ple"], axis=1),
        norm_final=jnp.sum(dg_final, axis=0))


def kernel(x, p, rel_bias, norm_mix, w_in, ssm_lambda_re, ssm_lambda_im, ssm_b_re, ssm_b_im, ssm_c_re, ssm_c_im, ssm_d, ssm_log_dt, ssm_w_glu, conv_w, attn_sinks, w_branch, w_out, norm_ffn, w_ffn_in, w_ffn_out, norm_ple, w_ple_gate, w_ple_proj, norm_final, loss_target, m_rel_bias, m_norm_mix, m_w_in, m_ssm_lambda_re, m_ssm_lambda_im, m_ssm_b_re, m_ssm_b_im, m_ssm_c_re, m_ssm_c_im, m_ssm_d, m_ssm_log_dt, m_ssm_w_glu, m_conv_w, m_attn_sinks, m_w_branch, m_w_out, m_norm_ffn, m_w_ffn_in, m_w_ffn_out, m_norm_ple, m_w_ple_gate, m_w_ple_proj, m_norm_final, v_rel_bias, v_norm_mix, v_w_in, v_ssm_lambda_re, v_ssm_lambda_im, v_ssm_b_re, v_ssm_b_im, v_ssm_c_re, v_ssm_c_im, v_ssm_d, v_ssm_log_dt, v_ssm_w_glu, v_conv_w, v_attn_sinks, v_w_branch, v_w_out, v_norm_ffn, v_w_ffn_in, v_w_ffn_out, v_norm_ple, v_w_ple_gate, v_w_ple_proj, v_norm_final):
    args = dict(locals())
    w = {n: args[n] for n in WEIGHTS}
    m = {n: args["m_" + n] for n in WEIGHTS}
    v = {n: args["v_" + n] for n in WEIGHTS}
    shapes = {n: w[n].shape for n in SMALL}

    x_i, y_i, c_i = _coords()
    chip = 2 * x_i + y_i
    place = (chip, c_i)
    small = {n: w[n] for n in SMALL}

    def all_gather(tag, blks):
        g4 = exchange(f"gather_{tag}_chips", "gather_chips", blks)
        g4 = [_put_slot(g, b, chip) for g, b in zip(g4, blks)]
        g8 = exchange(f"gather_{tag}_cores", "gather_cores", g4)
        return [_put_slot(g, b, c_i) for g, b in zip(g8, g4)]

    def shards_of(layer):
        return [w[n][layer] if n == "conv_w" else w[n][layer].astype(BF16) for n in SHARDED]

    bias = rel_bias_fwd(small["rel_bias"].T).reshape(N_Q_HEADS, BLOCK, 2 * BLOCK)
    xs, layers, res = x[0], [], []
    gathered = all_gather("w", shards_of(0))
    for layer in range(DEPTH):
        layers.append(_layer_weights(gathered, small, p[:, 0], layer))
        nxt = shards_of(layer + 1) if layer + 1 < DEPTH else None
        xs, res_l, gathered = layer_fwd(xs, layers[layer], bias, nxt, place)
        res.append(res_l)
    grad_x, loss_parts, dg_final = loss_head(xs, small["norm_final"][None, :], loss_target[0])

    per_layer, reduced, pending = [None] * DEPTH, [None] * DEPTH, None
    for layer in reversed(range(DEPTH)):
        grad_x, per_layer[layer], done = layer_bwd(grad_x, layers[layer], res[layer], bias, pending)
        if pending is not None:
            reduced[layer + 1] = done
        mg = matrix_grads(per_layer[layer])
        pending = [_shard_split(mg[n], n).astype(BF16) for n in SHARDED]
    sums = _pair_sums(pending, exchange("scatter_g_cores", "scatter_cores", pending))
    reduced[0] = (sums, exchange("scatter_g_chips", "scatter_chips", sums))

    outs = ({}, {}, {}, {})
    for k, name in enumerate(SHARDED):
        parts = jnp.stack([_put_slot(rcv[k], lax.dynamic_index_in_dim(sm[k], chip, 0, keepdims=False), chip)
                           for sm, rcv in reduced], axis=1)
        cols = parts.shape[-1]
        res4 = adamw("adamw_" + name, parts.reshape(4, -1, cols), w[name].reshape(-1, cols), m[name].reshape(-1, cols),
                     v[name].reshape(-1, cols))
        for d, o in zip(outs, res4):
            d[name] = o.reshape(w[name].shape)

    small_local = pack_small(small_grads(per_layer, dg_final), shapes, jnp.sum(loss_parts))
    small_all = all_gather("s", [small_local])[0]
    zero = jnp.zeros((1,), F32)
    res4 = adamw("adamw_small", small_all.reshape(N_DEV, small_local.shape[0], 128),
                 pack_small({n: w[n] for n in SMALL}, shapes, zero), pack_small({n: m[n] for n in SMALL}, shapes, zero),
                 pack_small({n: v[n] for n in SMALL}, shapes, zero))
    loss = None
    for d, r in zip(outs, res4):
        vals, extra = unpack_small(r, shapes)
        d.update(vals)
        if loss is None:
            loss = extra

    return (loss, grad_x[None], *[d[n] for d in outs for n in WEIGHTS])
```

```python
import functools
import math

import numpy as np
import jax
import jax.numpy as jnp
from jax import lax
from jax.experimental import pallas as pl
from jax.experimental.pallas import tpu as pltpu

F32 = jnp.float32
BF16 = jnp.bfloat16
MESH = pl.DeviceIdType.MESH

D_MODEL = 1024
DEPTH = 4
PLE_DIM = 256
BRANCH = 512
SSM_GROUPS = 32
SSM_GROUP = 16
SSM_STATE = 64
SSM_LANES = SSM_GROUPS * SSM_STATE
SSM_SUB = 4
SUB_IN = BRANCH // SSM_SUB
SUB_ST = SSM_LANES // SSM_SUB
HEAD_DIM = 64
N_Q_HEADS = 8
N_KV_HEADS = 2
GQA_GROUP = 4
KV_WIDTH = 2 * N_KV_HEADS * HEAD_DIM
WINDOW = 128
BLOCK = 128
ATTN_SCALE = 1.0 / math.sqrt(HEAD_DIM)
REL_BUCKETS = 32
REL_MAX_DIST = 128
FFN_HIDDEN = 2816
RMS_EPS = 1e-6
IN_WIDTH = 5888
N_DEV = 8

ADAM_LR = 0.001
ADAM_B1 = 0.9
ADAM_B2 = 0.999
ADAM_EPS = 1e-08
ADAM_WD = 0.01
ADAM_STEP = 10

COL_U = 3072
COL_KV = 5632
NEG = -1e30

SCAN_T = 256
TM = 512
TM_W = 1024
VMEM_LIMIT = 52 * 1024 * 1024


def _cparams(*sem):
    return pltpu.CompilerParams(dimension_semantics=sem, vmem_limit_bytes=VMEM_LIMIT)


def _full(shape):
    n = len(shape)
    return pl.BlockSpec(shape, lambda *_: (0,) * n)


def _pick(n, cands):
    for c in cands:
        if n % c == 0:
            return c
    return n


def _call(body, *, name, steps, in_specs, out_specs, out_shape, scratch, args, side=None):
    in_specs, out_specs, out_shape = list(in_specs), list(out_specs), list(out_shape)
    scratch, args = list(scratch), list(args)
    n_in, n_out, n_scr = len(in_specs), len(out_specs), len(scratch)
    n = 0
    if side is not None:
        kind, blks = side
        out_shape_of, per_block, make_copies = _EXCHANGES[kind]
        n = len(blks)
        inner = body

        def body(*refs):
            ins, sx = refs[:n_in], refs[n_in:n_in + n]
            outs, so = refs[n_in + n:n_in + n + n_out], refs[n_in + n + n_out:n_in + 2 * n + n_out]
            scr = refs[n_in + 2 * n + n_out:n_in + 2 * n + n_out + n_scr]
            send_sems, recv_sems = refs[-2:]

            @pl.when(pl.program_id(0) == 0)
            def _():
                sends, _ = make_copies(sx, so, send_sems, recv_sems)
                for cp in sends:
                    cp.start()

            inner(*ins, *outs, *scr)

            @pl.when(pl.program_id(0) == steps - 1)
            def _():
                sends, recvs = make_copies(sx, so, send_sems, recv_sems)
                for cp in recvs:
                    cp.wait_recv()
                for cp in sends:
                    cp.wait_send()

        in_specs += [_ANY] * n
        args += list(blks)
        out_specs += [_ANY] * n
        out_shape += [jax.ShapeDtypeStruct(out_shape_of(b), b.dtype) for b in blks]
        scratch += [pltpu.SemaphoreType.DMA((per_block * n,)), pltpu.SemaphoreType.DMA((per_block * n,))]
    outs = pl.pallas_call(
        body, name=name, grid=(steps,), in_specs=in_specs, out_specs=out_specs, out_shape=out_shape,
        scratch_shapes=scratch, compiler_params=_cparams("arbitrary"),
    )(*args)
    return outs[:n_out], outs[n_out:]


def _dot(a, b):
    return jnp.dot(a, b, preferred_element_type=F32)


def _dot_tn(a, b):
    return lax.dot_general(a, b, (((0,), (0,)), ((), ())), preferred_element_type=F32)


def _dot_nt(a, b):
    return lax.dot_general(a, b, (((1,), (1,)), ((), ())), preferred_element_type=F32)


def _rms(x, g):
    r = lax.rsqrt(jnp.mean(x * x, axis=-1, keepdims=True) + RMS_EPS)
    return x * r * g


def fused_mm(name, ins, in_specs, prologue, w, *, tn, out_dtype, res=None, extras=(), side=None, nt=False):
    S = ins[0].shape[0]
    N, K = w.shape if nt else w.shape[::-1]
    tn = min(tn, N)
    n_in, n_ex = len(ins), len(extras)

    def body(*refs):
        in_refs = refs[:n_in]
        w_ref = refs[n_in]
        pos = n_in + 1
        res_ref = None
        if res is not None:
            res_ref = refs[pos]
            pos += 1
        o_ref = refs[pos]
        ex_refs = refs[pos + 1:pos + 1 + n_ex]
        a_scr = refs[-1]
        out = prologue(*[r[...] for r in in_refs])
        a_scr[...] = out[0]
        for r, e in zip(ex_refs, out[1:]):
            r[...] = e.astype(r.dtype)
        for j in range(N // tn):
            cs = slice(j * tn, (j + 1) * tn)
            acc = _dot_nt(a_scr[...], w_ref[cs, :]) if nt else _dot(a_scr[...], w_ref[:, cs])
            if res_ref is not None:
                acc = acc + res_ref[:, cs]
            o_ref[:, cs] = acc.astype(o_ref.dtype)

    specs = list(in_specs) + [pl.BlockSpec(w.shape, lambda i: (0, 0), pipeline_mode=pl.Buffered(1))]
    args = list(ins) + [w]
    if res is not None:
        specs.append(pl.BlockSpec((TM, N), lambda i: (i, 0)))
        args.append(res)
    out_shape = [jax.ShapeDtypeStruct((S, N), out_dtype)]
    out_specs = [pl.BlockSpec((TM, N), lambda i: (i, 0))]
    for cols, dt in extras:
        out_shape.append(jax.ShapeDtypeStruct((S, cols), dt))
        out_specs.append(pl.BlockSpec((TM, cols), lambda i: (i, 0)))
    outs, side_outs = _call(body, name=name, steps=S // TM, in_specs=specs, out_specs=out_specs, out_shape=out_shape,
                            scratch=[pltpu.VMEM((TM, K), BF16)], args=args, side=side)
    result = outs if n_ex else outs[0]
    return result if side is None else (result, side_outs)


def _row_spec(cols, blk=0, tm=TM):
    return pl.BlockSpec((tm, cols), lambda i: (i, blk))


def mm_norm_bwd(name, pieces, w, x, g, dres, *, tm):
    S = x.shape[0]
    n = len(pieces)

    def body(*refs):
        x_ref, g_ref, dres_ref, w_ref, dx_ref, dg_ref = refs[n:]

        @pl.when(pl.program_id(0) == 0)
        def _():
            dg_ref[...] = jnp.zeros_like(dg_ref)

        tiles = [r[...].astype(BF16) for r in refs[:n]]
        a = tiles[0] if n == 1 else jnp.concatenate(tiles, axis=1)
        dh = _dot_nt(a, w_ref[...])
        xv = x_ref[...]
        r = lax.rsqrt(jnp.mean(xv * xv, axis=-1, keepdims=True) + RMS_EPS)
        xhat = xv * r
        dxhat = dh * g_ref[...]
        dx_ref[...] = dres_ref[...] + r * (dxhat - xhat * jnp.mean(dxhat * xhat, axis=-1, keepdims=True))
        dg_ref[...] += jnp.sum((dh * xhat).reshape(tm // 8, 8, D_MODEL), axis=0)

    row = _row_spec(D_MODEL, tm=tm)
    return pl.pallas_call(
        body, name=name, grid=(S // tm,),
        in_specs=[_row_spec(p.shape[1], tm=tm) for p in pieces]
        + [row, _full((1, D_MODEL)), row, pl.BlockSpec(w.shape, lambda i: (0, 0), pipeline_mode=pl.Buffered(1))],
        out_specs=[row, _full((8, D_MODEL))],
        out_shape=[jax.ShapeDtypeStruct((S, D_MODEL), F32), jax.ShapeDtypeStruct((8, D_MODEL), F32)],
        compiler_params=_cparams("arbitrary"),
    )(*pieces, x, g, dres, w)


def mm_tn(name, a, b):
    S, K = a.shape
    N = b.shape[1]
    tk = _pick(K, (1024, 1408, 512, 256))
    tn = _pick(N, (1024, 1408, 1536, 512, 256))

    def body(a_ref, b_ref, o_ref):
        @pl.when(pl.program_id(2) == 0)
        def _():
            o_ref[...] = jnp.zeros_like(o_ref)

        o_ref[...] += _dot_tn(a_ref[...].astype(BF16), b_ref[...].astype(BF16))

    return pl.pallas_call(
        body, name=name, grid=(K // tk, N // tn, S // TM_W),
        in_specs=[pl.BlockSpec((TM_W, tk), lambda k, n, s: (s, k)), pl.BlockSpec((TM_W, tn), lambda k, n, s: (s, n))],
        out_specs=pl.BlockSpec((tk, tn), lambda k, n, s: (k, n)),
        out_shape=jax.ShapeDtypeStruct((K, N), F32),
        compiler_params=_cparams("parallel", "parallel", "arbitrary"),
    )(a, b)


def swiglu_bwd(dact, hf):
    S = hf.shape[0]

    def body(da_ref, h1_ref, h2_ref, o1_ref, o2_ref):
        h1 = h1_ref[...].astype(F32)
        h2 = h2_ref[...].astype(F32)
        da = da_ref[...].astype(F32)
        sg = jax.nn.sigmoid(h1)
        o1_ref[...] = (da * h2 * sg * (1.0 + h1 * (1.0 - sg))).astype(BF16)
        o2_ref[...] = (da * h1 * sg).astype(BF16)

    tn = 1408
    nn = FFN_HIDDEN // tn
    return pl.pallas_call(
        body, name="swiglu_bwd", grid=(S // TM, nn),
        in_specs=[pl.BlockSpec((TM, tn), lambda i, j: (i, j)), pl.BlockSpec((TM, tn), lambda i, j: (i, j)),
                  pl.BlockSpec((TM, tn), lambda i, j: (i, nn + j))],
        out_specs=[pl.BlockSpec((TM, tn), lambda i, j: (i, j)), pl.BlockSpec((TM, tn), lambda i, j: (i, j))],
        out_shape=[jax.ShapeDtypeStruct((S, FFN_HIDDEN), BF16)] * 2,
        compiler_params=_cparams("parallel", "parallel"),
    )(dact, hf, hf)


def ple_combine(x2, a_pre, pp):
    S = x2.shape[0]

    def body(x_ref, a_ref, p_ref, o_ref):
        o_ref[...] = x_ref[...] + jax.nn.sigmoid(a_ref[...].astype(F32)) * p_ref[...].astype(F32)

    row = pl.BlockSpec((TM, D_MODEL), lambda i: (i, 0))
    return pl.pallas_call(
        body, name="ple_combine", grid=(S // TM,), in_specs=[row, row, row], out_specs=row,
        out_shape=jax.ShapeDtypeStruct((S, D_MODEL), F32), compiler_params=_cparams("parallel"),
    )(x2, a_pre, pp)


def ple_bwd(dx3, a_pre, pp):
    S = dx3.shape[0]

    def body(dx_ref, a_ref, p_ref, da_ref, dpp_ref):
        dx = dx_ref[...]
        pg = jax.nn.sigmoid(a_ref[...].astype(F32))
        dpp_ref[...] = (dx * pg).astype(BF16)
        da_ref[...] = (dx * p_ref[...].astype(F32) * pg * (1.0 - pg)).astype(BF16)

    row = pl.BlockSpec((TM, D_MODEL), lambda i: (i, 0))
    return pl.pallas_call(
        body, name="ple_bwd", grid=(S // TM,), in_specs=[row, row, row], out_specs=[row, row],
        out_shape=[jax.ShapeDtypeStruct((S, D_MODEL), BF16)] * 2, compiler_params=_cparams("parallel"),
    )(dx3, a_pre, pp)


def _gate_specs(tn, nn):
    return [pl.BlockSpec((TM, tn), functools.partial(lambda i, j, r: (i, r * nn + j), r=r)) for r in range(3)]


def merge_fwd(z, y_ssm, y_conv, y_attn, wb):
    S = z.shape[0]
    tn = 512
    nn = D_MODEL // tn

    def body(g0, g1, g2, y0, y1, y2, w_ref, o_ref):
        acc = jnp.zeros((TM, tn), F32)
        for r, (g_ref, y_ref) in enumerate(((g0, y0), (g1, y1), (g2, y2))):
            acc += jax.nn.sigmoid(g_ref[...].astype(F32)) * _dot(y_ref[...], w_ref[r])
        o_ref[...] = acc.astype(BF16)

    y_spec = pl.BlockSpec((TM, BRANCH), lambda i, j: (i, 0))
    return pl.pallas_call(
        body, name="merge_fwd", grid=(S // TM, nn),
        in_specs=_gate_specs(tn, nn) + [y_spec] * 3 + [pl.BlockSpec((3, BRANCH, tn), lambda i, j: (0, 0, j))],
        out_specs=pl.BlockSpec((TM, tn), lambda i, j: (i, j)),
        out_shape=jax.ShapeDtypeStruct((S, D_MODEL), BF16), compiler_params=_cparams("parallel", "parallel"),
    )(z, z, z, y_ssm, y_conv, y_attn, wb)


def merge_bwd(dmerged, z, y_ssm, y_conv, y_attn, wb):
    S = z.shape[0]
    tn = 512
    nn = D_MODEL // tn

    def body(dm_ref, g0, g1, g2, y0, y1, y2, w_ref, dg0, dg1, dg2, db0, db1, db2):
        dm = dm_ref[...].astype(F32)
        for r, (g_ref, y_ref, dg_ref, db_ref) in enumerate(((g0, y0, dg0, db0), (g1, y1, dg1, db1), (g2, y2, dg2, db2))):
            sg = jax.nn.sigmoid(g_ref[...].astype(F32))
            b = _dot(y_ref[...], w_ref[r])
            dg_ref[...] = (dm * b * sg * (1.0 - sg)).astype(BF16)
            db_ref[...] = (dm * sg).astype(BF16)

    y_spec = pl.BlockSpec((TM, BRANCH), lambda i, j: (i, 0))
    outs = pl.pallas_call(
        body, name="merge_bwd", grid=(S // TM, nn),
        in_specs=[pl.BlockSpec((TM, tn), lambda i, j: (i, j))] + _gate_specs(tn, nn) + [y_spec] * 3
        + [pl.BlockSpec((3, BRANCH, tn), lambda i, j: (0, 0, j))],
        out_specs=[pl.BlockSpec((TM, tn), lambda i, j: (i, j))] * 6,
        out_shape=[jax.ShapeDtypeStruct((S, D_MODEL), BF16)] * 6, compiler_params=_cparams("parallel", "parallel"),
    )(dmerged, z, z, z, y_ssm, y_conv, y_attn, wb)
    return outs[:3], outs[3:]


def _shift_down(v, halo, k):
    rolled = pltpu.roll(v, k, 0)
    h = pltpu.roll(halo, k, 0)
    row = lax.broadcasted_iota(jnp.int32, v.shape, 0)
    head = jnp.concatenate([h, jnp.zeros((v.shape[0] - 8, v.shape[1]), v.dtype)], axis=0)
    return jnp.where(row < k, head, rolled)


def _shift_up(v, halo, k):
    n = v.shape[0]
    rolled = pltpu.roll(v, n - k, 0)
    h = pltpu.roll(halo, 8 - k, 0)
    row = lax.broadcasted_iota(jnp.int32, v.shape, 0)
    tail = jnp.concatenate([jnp.zeros((n - 8, v.shape[1]), v.dtype), h], axis=0)
    return jnp.where(row >= n - k, tail, rolled)


def _conv_specs():
    rb = TM // 8
    c0 = COL_U // BRANCH

    def cur(k):
        return pl.BlockSpec((TM, BRANCH), lambda i: (i, c0 + k))

    def prev(k):
        return pl.BlockSpec((8, BRANCH), lambda i: (jnp.maximum(i * rb - 1, 0), c0 + k))

    return [cur(1), cur(2), cur(3), prev(2), prev(3)]


def conv_fwd(z, conv_w):
    S = z.shape[0]

    def body(cb_ref, cc_ref, cx_ref, pc_ref, px_ref, w_ref, o_ref):
        first = pl.program_id(0) == 0
        v = cc_ref[...].astype(F32) * cx_ref[...].astype(F32)
        pv = jnp.where(first, 0.0, pc_ref[...].astype(F32) * px_ref[...].astype(F32))
        w = w_ref[...]
        y = w[2:3] * v + w[1:2] * _shift_down(v, pv, 1) + w[0:1] * _shift_down(v, pv, 2)
        o_ref[...] = (cb_ref[...].astype(F32) * y).astype(BF16)

    return pl.pallas_call(
        body, name="conv_fwd", grid=(S // TM,), in_specs=_conv_specs() + [_full((3, BRANCH))],
        out_specs=pl.BlockSpec((TM, BRANCH), lambda i: (i, 0)),
        out_shape=jax.ShapeDtypeStruct((S, BRANCH), BF16), compiler_params=_cparams("parallel"),
    )(z, z, z, z, z, conv_w)


def conv_bwd(dy, z, conv_w):
    S = z.shape[0]
    rb = TM // 8
    nt = S // TM
    c0 = COL_U // BRANCH

    def body(dy_ref, cb_ref, cc_ref, cx_ref, pc_ref, px_ref, ndy_ref, ncb_ref, w_ref, o_ref, dw_ref):
        i = pl.program_id(0)

        @pl.when(i == 0)
        def _():
            dw_ref[...] = jnp.zeros_like(dw_ref)

        cb = cb_ref[...].astype(F32)
        cc = cc_ref[...].astype(F32)
        cx = cx_ref[...].astype(F32)
        dyv = dy_ref[...].astype(F32)
        v = cc * cx
        pv = jnp.where(i == 0, 0.0, pc_ref[...].astype(F32) * px_ref[...].astype(F32))
        v1 = _shift_down(v, pv, 1)
        v2 = _shift_down(v, pv, 2)
        w = w_ref[...]
        conv = w[2:3] * v + w[1:2] * v1 + w[0:1] * v2
        dc = dyv * cb
        ndc = jnp.where(i == nt - 1, 0.0, ndy_ref[...].astype(F32) * ncb_ref[...].astype(F32))
        dv = w[2:3] * dc + w[1:2] * _shift_up(dc, ndc, 1) + w[0:1] * _shift_up(dc, ndc, 2)
        o_ref[:, 0:BRANCH] = (dyv * conv).astype(BF16)
        o_ref[:, BRANCH:2 * BRANCH] = (dv * cx).astype(BF16)
        o_ref[:, 2 * BRANCH:3 * BRANCH] = (dv * cc).astype(BF16)
        for k, vk in enumerate((v2, v1, v)):
            dw_ref[8 * k:8 * k + 8, :] += jnp.sum((dc * vk).reshape(rb, 8, BRANCH), axis=0)

    nxt = jnp.minimum

    return pl.pallas_call(
        body, name="conv_bwd", grid=(nt,),
        in_specs=[pl.BlockSpec((TM, BRANCH), lambda i: (i, 0))] + _conv_specs()
        + [pl.BlockSpec((8, BRANCH), lambda i: (nxt((i + 1) * rb, S // 8 - 1), 0)),
           pl.BlockSpec((8, BRANCH), lambda i: (nxt((i + 1) * rb, S // 8 - 1), c0 + 1)),
           _full((3, BRANCH))],
        out_specs=[pl.BlockSpec((TM, 3 * BRANCH), lambda i: (i, 0)), _full((24, BRANCH))],
        out_shape=[jax.ShapeDtypeStruct((S, 3 * BRANCH), BF16), jax.ShapeDtypeStruct((24, BRANCH), F32)],
        compiler_params=_cparams("arbitrary"),
    )(dy, z, z, z, z, z, dy, z, conv_w)


def _bucket_onehot_t():
    qi = np.arange(BLOCK)[:, None]
    kj = np.arange(2 * BLOCK)[None, :]
    dist = np.clip(qi + BLOCK - kj, 0, REL_MAX_DIST - 1)
    exact = REL_BUCKETS // 2
    df = np.maximum(dist, 1).astype(np.float32)
    large = exact + (np.log(df / np.float32(exact)) / np.float32(math.log(REL_MAX_DIST / exact))
                     * np.float32(REL_BUCKETS - exact)).astype(np.int32)
    large = np.minimum(large, REL_BUCKETS - 1)
    bucket = np.where(dist < exact, dist, large).reshape(-1)
    return (np.arange(REL_BUCKETS)[:, None] == bucket[None, :]).astype(np.float32)


def rel_bias_fwd(rel_bias_t):
    n = BLOCK * 2 * BLOCK

    def body(r_ref, oh_ref, o_ref):
        o_ref[...] = jnp.dot(r_ref[...], oh_ref[...], precision=lax.Precision.HIGHEST, preferred_element_type=F32)

    return pl.pallas_call(
        body, name="rel_bias_fwd", grid=(1,), in_specs=[_full((N_Q_HEADS, REL_BUCKETS)), _full((REL_BUCKETS, n))],
        out_specs=_full((N_Q_HEADS, n)), out_shape=jax.ShapeDtypeStruct((N_Q_HEADS, n), F32),
        compiler_params=_cparams("arbitrary"),
    )(rel_bias_t, jnp.asarray(_bucket_onehot_t()))


def rel_bias_bwd(dbias):
    n_l = dbias.shape[0]
    n = BLOCK * 2 * BLOCK

    def body(d_ref, oh_ref, o_ref):
        tot = d_ref[0]
        for l in range(1, n_l):
            tot = tot + d_ref[l]
        o_ref[...] = lax.dot_general(tot, oh_ref[...], (((1,), (1,)), ((), ())), precision=lax.Precision.HIGHEST,
                                     preferred_element_type=F32)

    return pl.pallas_call(
        body, name="rel_bias_bwd", grid=(1,), in_specs=[_full((n_l, N_Q_HEADS, n)), _full((REL_BUCKETS, n))],
        out_specs=_full((N_Q_HEADS, REL_BUCKETS)), out_shape=jax.ShapeDtypeStruct((N_Q_HEADS, REL_BUCKETS), F32),
        compiler_params=_cparams("arbitrary"),
    )(dbias, jnp.asarray(_bucket_onehot_t()))


def _attn_valid(first):
    qi = lax.broadcasted_iota(jnp.int32, (BLOCK, 2 * BLOCK), 0)
    kj = lax.broadcasted_iota(jnp.int32, (BLOCK, 2 * BLOCK), 1)
    dist = qi + BLOCK - kj
    return (dist >= 0) & (dist < WINDOW) & (jnp.logical_not(first) | (kj >= BLOCK))


def _attn_weights(qh, kcat, bias_h, valid, sink):
    s = _dot_nt(qh, kcat) * ATTN_SCALE + bias_h
    s = jnp.where(valid, s, NEG)
    m = jnp.maximum(jnp.max(s, axis=-1, keepdims=True), sink)
    p = jnp.exp(s - m)
    esink = jnp.exp(sink - m)
    inv = 1.0 / (jnp.sum(p, axis=-1, keepdims=True) + esink)
    return p * inv, esink * inv


def _kv_heads(kvp, kvc, hk):
    ks = slice(hk * HEAD_DIM, (hk + 1) * HEAD_DIM)
    vs = slice(KV_WIDTH // 2 + hk * HEAD_DIM, KV_WIDTH // 2 + (hk + 1) * HEAD_DIM)
    return jnp.concatenate([kvp[:, ks], kvc[:, ks]], axis=0), jnp.concatenate([kvp[:, vs], kvc[:, vs]], axis=0)


def _attn_specs():
    cq = (COL_U + 4 * BRANCH) // BRANCH
    ckv = COL_KV // KV_WIDTH
    return [pl.BlockSpec((BLOCK, BRANCH), lambda n: (n, cq)),
            pl.BlockSpec((BLOCK, KV_WIDTH), lambda n: (n, ckv)),
            pl.BlockSpec((BLOCK, KV_WIDTH), lambda n: (jnp.maximum(n - 1, 0), ckv)),
            _full((N_Q_HEADS, BLOCK, 2 * BLOCK)),
            pl.BlockSpec(memory_space=pltpu.SMEM)]


def attn_fwd(z, bias, sinks, side=None):
    S = z.shape[0]

    def body(q_ref, kvc_ref, kvp_ref, b_ref, sink_ref, o_ref):
        valid = _attn_valid(pl.program_id(0) == 0)
        q = q_ref[...]
        kvc = kvc_ref[...]
        kvp = kvp_ref[...]
        outs = []
        for hk in range(N_KV_HEADS):
            kcat, vcat = _kv_heads(kvp, kvc, hk)
            for g in range(GQA_GROUP):
                h = hk * GQA_GROUP + g
                w, _ = _attn_weights(q[:, h * HEAD_DIM:(h + 1) * HEAD_DIM], kcat, b_ref[h], valid, sink_ref[h])
                outs.append(_dot(w.astype(BF16), vcat))
        o_ref[...] = jnp.concatenate(outs, axis=1).astype(BF16)

    outs, side_outs = _call(
        body, name="attn_fwd", steps=S // BLOCK, in_specs=_attn_specs(),
        out_specs=[pl.BlockSpec((BLOCK, BRANCH), lambda n: (n, 0))],
        out_shape=[jax.ShapeDtypeStruct((S, BRANCH), BF16)], scratch=[], args=(z, z, z, bias, sinks), side=side)
    return outs[0], side_outs


def attn_bwd(do, z, bias, sinks):
    S = z.shape[0]

    def body(do_ref, q_ref, kvc_ref, kvp_ref, b_ref, sink_ref, dq_ref, dc_ref, dp_ref, db_ref, ds_ref):
        first = pl.program_id(0) == 0

        @pl.when(first)
        def _():
            db_ref[...] = jnp.zeros_like(db_ref)
            ds_ref[...] = jnp.zeros_like(ds_ref)

        valid = _attn_valid(first)
        q = q_ref[...]
        kvc = kvc_ref[...]
        kvp = kvp_ref[...]
        dov = do_ref[...]
        dqs, dks, dvs = [], [], []
        for hk in range(N_KV_HEADS):
            kcat, vcat = _kv_heads(kvp, kvc, hk)
            dk = jnp.zeros((2 * BLOCK, HEAD_DIM), F32)
            dv = jnp.zeros((2 * BLOCK, HEAD_DIM), F32)
            for g in range(GQA_GROUP):
                h = hk * GQA_GROUP + g
                qh = q[:, h * HEAD_DIM:(h + 1) * HEAD_DIM]
                doh = dov[:, h * HEAD_DIM:(h + 1) * HEAD_DIM]
                w, wsink = _attn_weights(qh, kcat, b_ref[h], valid, sink_ref[h])
                dv += _dot_tn(w.astype(BF16), doh)
                dw = _dot_nt(doh, vcat)
                delta = jnp.sum(w * dw, axis=-1, keepdims=True)
                ds = w * (dw - delta)
                db_ref[h] += ds
                tot = jnp.sum(-wsink * delta, axis=0, keepdims=True)
                ds_ref[h:h + 1, :] += jnp.broadcast_to(tot, (1, BLOCK))
                dsb = (ds * ATTN_SCALE).astype(BF16)
                dqs.append(_dot(dsb, kcat))
                dk += _dot_tn(dsb, qh)
            dks.append(dk)
            dvs.append(dv)
        dq_ref[...] = jnp.concatenate(dqs, axis=1).astype(BF16)
        both = jnp.concatenate(dks + dvs, axis=1)
        dp_ref[...] = both[:BLOCK]
        dc_ref[...] = both[BLOCK:]

    blk = pl.BlockSpec((BLOCK, BRANCH), lambda n: (n, 0))
    kvb = pl.BlockSpec((BLOCK, KV_WIDTH), lambda n: (n, 0))
    return pl.pallas_call(
        body, name="attn_bwd", grid=(S // BLOCK,), in_specs=[blk] + _attn_specs(),
        out_specs=[blk, kvb, kvb, _full((N_Q_HEADS, BLOCK, 2 * BLOCK)), _full((N_Q_HEADS, BLOCK))],
        out_shape=[jax.ShapeDtypeStruct((S, BRANCH), BF16), jax.ShapeDtypeStruct((S, KV_WIDTH), F32),
                   jax.ShapeDtypeStruct((S, KV_WIDTH), F32), jax.ShapeDtypeStruct((N_Q_HEADS, BLOCK, 2 * BLOCK), F32),
                   jax.ShapeDtypeStruct((N_Q_HEADS, BLOCK), F32)],
        compiler_params=_cparams("arbitrary"),
    )(do, z, z, z, bias, sinks)


def kv_shift_add(dcur, dprev):
    S = dcur.shape[0]
    nt = S // TM
    per_tile = TM // BLOCK

    def body(c_ref, p_ref, n_ref, o_ref):
        nxt = jnp.where(pl.program_id(0) == nt - 1, 0.0, n_ref[...])
        o_ref[...] = (c_ref[...] + jnp.concatenate([p_ref[BLOCK:, :], nxt], axis=0)).astype(BF16)

    tile = pl.BlockSpec((TM, KV_WIDTH), lambda i: (i, 0))
    return pl.pallas_call(
        body, name="kv_shift_add", grid=(nt,),
        in_specs=[tile, tile,
                  pl.BlockSpec((BLOCK, KV_WIDTH), lambda i: (jnp.minimum((i + 1) * per_tile, S // BLOCK - 1), 0))],
        out_specs=tile, out_shape=jax.ShapeDtypeStruct((S, KV_WIDTH), BF16), compiler_params=_cparams("parallel"),
    )(dcur, dprev, dprev)


def _ssm_disc(lam_re, lam_im, log_dt, bt_re, bt_im):
    dt = jnp.exp(log_dt)
    mag = jnp.exp(lam_re * dt)
    ang = lam_im * dt
    a_re = mag * jnp.cos(ang)
    a_im = mag * jnp.sin(ang)
    den = lam_re * lam_re + lam_im * lam_im
    nr = a_re - 1.0
    coef_re = (nr * lam_re + a_im * lam_im) / den
    coef_im = (a_im * lam_re - nr * lam_im) / den
    bb_re = coef_re[:, None, :] * bt_re - coef_im[:, None, :] * bt_im
    bb_im = coef_re[:, None, :] * bt_im + coef_im[:, None, :] * bt_re
    return a_re, a_im, bb_re, bb_im


_GN = (SSM_GROUPS, SSM_STATE)
_GPN = (SSM_GROUPS, SSM_GROUP, SSM_STATE)


def ssm_disc_fwd(lam_re, lam_im, log_dt, bt_re, bt_im):
    def body(lr_ref, li_ref, dt_ref, br_ref, bi_ref, ar_ref, ai_ref, bbr_ref, bbi_ref):
        a_re, a_im, bb_re, bb_im = _ssm_disc(lr_ref[...], li_ref[...], dt_ref[...], br_ref[...], bi_ref[...])
        ar_ref[...] = a_re
        ai_ref[...] = a_im
        bbr_ref[...] = bb_re
        bbi_ref[...] = bb_im

    return pl.pallas_call(
        body, name="ssm_disc_fwd", grid=(1,),
        in_specs=[_full(_GN), _full(_GN), _full((SSM_GROUPS, 1)), _full(_GPN), _full(_GPN)],
        out_specs=[_full(_GN), _full(_GN), _full(_GPN), _full(_GPN)],
        out_shape=[jax.ShapeDtypeStruct(s, F32) for s in (_GN, _GN, _GPN, _GPN)],
        compiler_params=_cparams("arbitrary"),
    )(lam_re, lam_im, log_dt, bt_re, bt_im)


def ssm_disc_bwd(lam_re, lam_im, log_dt, bt_re, bt_im, da_re, da_im, dbb_re, dbb_im):
    def body(lr_ref, li_ref, dt_ref, br_ref, bi_ref, dar_ref, dai_ref, dbr_ref, dbi_ref, o_lr, o_li, o_dt, o_br, o_bi):
        prim = (lr_ref[...], li_ref[...], dt_ref[...], br_ref[...], bi_ref[...])
        _, vjp = jax.vjp(_ssm_disc, *prim)
        grads = vjp((dar_ref[...], dai_ref[...], dbr_ref[...], dbi_ref[...]))
        for r, v in zip((o_lr, o_li, o_dt, o_br, o_bi), grads):
            r[...] = v

    shapes = (_GN, _GN, (SSM_GROUPS, 1), _GPN, _GPN)
    return pl.pallas_call(
        body, name="ssm_disc_bwd", grid=(1,),
        in_specs=[_full(s) for s in shapes + (_GN, _GN, _GPN, _GPN)],
        out_specs=[_full(s) for s in shapes], out_shape=[jax.ShapeDtypeStruct(s, F32) for s in shapes],
        compiler_params=_cparams("arbitrary"),
    )(lam_re, lam_im, log_dt, bt_re, bt_im, da_re, da_im, dbb_re, dbb_im)


LANE_GROUPS = SSM_LANES // 128
SUB_GROUPS = SUB_ST // 128
_TM_SHAPE = (LANE_GROUPS, 128)


def _step_rows(t):
    return pl.ds(pl.multiple_of(t * LANE_GROUPS, LANE_GROUPS), LANE_GROUPS)


def _group_rows(j):
    return pl.ds(j, SCAN_T, stride=LANE_GROUPS)


def _store_sub(ref, j, val):
    for k in range(SUB_GROUPS):
        ref[_group_rows(j * SUB_GROUPS + k), :] = val[:, k * 128:(k + 1) * 128]


def _load_sub(ref, j):
    return jnp.concatenate([ref[_group_rows(j * SUB_GROUPS + k), :] for k in range(SUB_GROUPS)], axis=1)


_SUB_SHAPE_IN = (SSM_SUB, SUB_IN, SUB_ST)
_SUB_SHAPE_OUT = (SSM_SUB, SUB_ST, SUB_IN)


def ssm_fwd(z, bb_re, bb_im, ct_re, ct_im, a_re, a_im, d_skip, wglu, side=None):
    S = z.shape[0]
    cu = COL_U // BRANCH

    def body(u_ref, bbr_ref, bbi_ref, ctr_ref, cti_ref, ar_ref, ai_ref, d_ref, wg_ref,
             y_ref, ypre_ref, hr_ref, hi_ref, bur, bui, car_r, car_i):
        @pl.when(pl.program_id(0) == 0)
        def _():
            car_r[...] = jnp.zeros_like(car_r)
            car_i[...] = jnp.zeros_like(car_i)

        u = u_ref[...]
        for j in range(SSM_SUB):
            uj = u[:, j * SUB_IN:(j + 1) * SUB_IN]
            _store_sub(bur, j, _dot(uj, bbr_ref[j]))
            _store_sub(bui, j, _dot(uj, bbi_ref[j]))
        ar = ar_ref[...]
        ai = ai_ref[...]

        def step(t, carry):
            hr, hi = carry
            rows = _step_rows(t)
            nhr = ar * hr - ai * hi + bur[rows, :]
            nhi = ar * hi + ai * hr + bui[rows, :]
            hr_ref[rows, :] = nhr
            hi_ref[rows, :] = nhi
            return nhr, nhi

        hr, hi = lax.fori_loop(0, SCAN_T, step, (car_r[...], car_i[...]), unroll=8)
        car_r[...] = hr
        car_i[...] = hi
        ys = []
        for j in range(SSM_SUB):
            ys.append(_dot(_load_sub(hr_ref, j).astype(BF16), ctr_ref[j])
                      - _dot(_load_sub(hi_ref, j).astype(BF16), cti_ref[j]))
        ypre = jnp.concatenate(ys, axis=1) + d_ref[...] * u.astype(F32)
        ypre_ref[...] = ypre
        g = jax.nn.gelu(ypre)
        y_ref[...] = (g * jax.nn.sigmoid(_dot(g.astype(BF16), wg_ref[...]))).astype(BF16)

    row = pl.BlockSpec((SCAN_T, BRANCH), lambda i: (i, 0))
    st = pl.BlockSpec((SCAN_T * LANE_GROUPS, 128), lambda i: (i, 0))
    return _call(
        body, name="ssm_fwd", steps=S // SCAN_T,
        in_specs=[pl.BlockSpec((SCAN_T, BRANCH), lambda i: (i, cu)), _full(_SUB_SHAPE_IN), _full(_SUB_SHAPE_IN),
                  _full(_SUB_SHAPE_OUT), _full(_SUB_SHAPE_OUT), _full(_TM_SHAPE), _full(_TM_SHAPE), _full((1, BRANCH)),
                  _full((BRANCH, BRANCH))],
        out_specs=[row, row, st, st],
        out_shape=[jax.ShapeDtypeStruct((S, BRANCH), BF16), jax.ShapeDtypeStruct((S, BRANCH), F32),
                   jax.ShapeDtypeStruct((S * LANE_GROUPS, 128), F32), jax.ShapeDtypeStruct((S * LANE_GROUPS, 128), F32)],
        scratch=[pltpu.VMEM((SCAN_T * LANE_GROUPS, 128), F32), pltpu.VMEM((SCAN_T * LANE_GROUPS, 128), F32),
                 pltpu.VMEM(_TM_SHAPE, F32), pltpu.VMEM(_TM_SHAPE, F32)],
        args=(z, bb_re, bb_im, ct_re, ct_im, a_re, a_im, d_skip, wglu), side=side)


def ssm_bwd(dy, z, ypre, h_re, h_im, bbt_re, bbt_im, c_re, c_im, a_re, a_im, d_skip, wglu, wglu_t, side=None):
    S = z.shape[0]
    nt = S // SCAN_T
    cu = COL_U // BRANCH

    def body(dy_ref, u_ref, ypre_ref, hr_ref, hi_ref, hpr_ref, hpi_ref, bbr_ref, bbi_ref, cr_ref, ci_ref, ar_ref, ai_ref,
             d_ref, wg_ref, wgt_ref,
             du_ref, dbbr_ref, dbbi_ref, dctr_ref, dcti_ref, dar_ref, dai_ref, dd_ref, dwg_ref,
             lr_scr, li_scr, car_r, car_i):
        step = pl.program_id(0)

        @pl.when(step == 0)
        def _():
            for r in (dbbr_ref, dbbi_ref, dctr_ref, dcti_ref, dar_ref, dai_ref, dd_ref, dwg_ref, car_r, car_i):
                r[...] = jnp.zeros_like(r)

        u = u_ref[...]
        uf = u.astype(F32)
        dyv = dy_ref[...].astype(F32)
        g, gelu_vjp = jax.vjp(jax.nn.gelu, ypre_ref[...])
        gb = g.astype(BF16)
        sg = jax.nn.sigmoid(_dot(gb, wg_ref[...]))
        dgl = (dyv * g * sg * (1.0 - sg)).astype(BF16)
        dwg_ref[...] += _dot_tn(gb, dgl)
        dg = dyv * sg + _dot(dgl, wgt_ref[...])
        dypre = gelu_vjp(dg)[0]
        dd_ref[...] += jnp.sum(dypre * uf, axis=0, keepdims=True)
        dyb = dypre.astype(BF16)
        for j in range(SSM_SUB):
            dyj = dyb[:, j * SUB_IN:(j + 1) * SUB_IN]
            _store_sub(lr_scr, j, _dot(dyj, cr_ref[j]))
            _store_sub(li_scr, j, -_dot(dyj, ci_ref[j]))
            dctr_ref[j] += _dot_tn(_load_sub(hr_ref, j).astype(BF16), dyj)
            dcti_ref[j] -= _dot_tn(_load_sub(hi_ref, j).astype(BF16), dyj)

        ar = ar_ref[...]
        ai = ai_ref[...]

        def adjoint(lr, li, rows):
            nlr = ar * lr + ai * li + lr_scr[rows, :]
            nli = ar * li - ai * lr + li_scr[rows, :]
            lr_scr[rows, :] = nlr
            li_scr[rows, :] = nli
            return nlr, nli

        def back(k, carry):
            lr, li, acc_r, acc_i = carry
            t = SCAN_T - 1 - k
            lr, li = adjoint(lr, li, _step_rows(t))
            hpr = hr_ref[_step_rows(t - 1), :]
            hpi = hi_ref[_step_rows(t - 1), :]
            return lr, li, acc_r + lr * hpr + li * hpi, acc_i + li * hpr - lr * hpi

        zero = jnp.zeros(_TM_SHAPE, F32)
        lr, li, acc_r, acc_i = lax.fori_loop(0, SCAN_T - 1, back, (car_r[...], car_i[...], zero, zero), unroll=8)
        lr, li = adjoint(lr, li, pl.ds(0, LANE_GROUPS))
        car_r[...] = lr
        car_i[...] = li
        first_tile = step == nt - 1
        hpr = jnp.where(first_tile, 0.0, hpr_ref[...])
        hpi = jnp.where(first_tile, 0.0, hpi_ref[...])
        dar_ref[...] += acc_r + lr * hpr + li * hpi
        dai_ref[...] += acc_i + li * hpr - lr * hpi

        dus = []
        for j in range(SSM_SUB):
            lrb = _load_sub(lr_scr, j).astype(BF16)
            lib = _load_sub(li_scr, j).astype(BF16)
            uj = u[:, j * SUB_IN:(j + 1) * SUB_IN]
            dus.append(_dot(lrb, bbr_ref[j]) + _dot(lib, bbi_ref[j]))
            dbbr_ref[j] += _dot_tn(uj, lrb)
            dbbi_ref[j] += _dot_tn(uj, lib)
        du_ref[...] = (jnp.concatenate(dus, axis=1) + dypre * d_ref[...]).astype(BF16)

    def rev(i):
        return nt - 1 - i

    row = pl.BlockSpec((SCAN_T, BRANCH), lambda i: (rev(i), 0))
    st = pl.BlockSpec((SCAN_T * LANE_GROUPS, 128), lambda i: (rev(i), 0))
    before = pl.BlockSpec(_TM_SHAPE, lambda i: (jnp.maximum(rev(i) * SCAN_T - 1, 0), 0))
    tm = _full(_TM_SHAPE)
    return _call(
        body, name="ssm_bwd", steps=nt,
        in_specs=[row, pl.BlockSpec((SCAN_T, BRANCH), lambda i: (rev(i), cu)), row, st, st, before, before,
                  _full(_SUB_SHAPE_OUT), _full(_SUB_SHAPE_OUT), _full(_SUB_SHAPE_IN), _full(_SUB_SHAPE_IN),
                  tm, tm, _full((1, BRANCH)), _full((BRANCH, BRANCH)), _full((BRANCH, BRANCH))],
        out_specs=[row, _full(_SUB_SHAPE_IN), _full(_SUB_SHAPE_IN), _full(_SUB_SHAPE_OUT), _full(_SUB_SHAPE_OUT),
                   tm, tm, _full((1, BRANCH)), _full((BRANCH, BRANCH))],
        out_shape=[jax.ShapeDtypeStruct((S, BRANCH), BF16), jax.ShapeDtypeStruct(_SUB_SHAPE_IN, F32),
                   jax.ShapeDtypeStruct(_SUB_SHAPE_IN, F32), jax.ShapeDtypeStruct(_SUB_SHAPE_OUT, F32),
                   jax.ShapeDtypeStruct(_SUB_SHAPE_OUT, F32), jax.ShapeDtypeStruct(_TM_SHAPE, F32),
                   jax.ShapeDtypeStruct(_TM_SHAPE, F32), jax.ShapeDtypeStruct((1, BRANCH), F32),
                   jax.ShapeDtypeStruct((BRANCH, BRANCH), F32)],
        scratch=[pltpu.VMEM((SCAN_T * LANE_GROUPS, 128), F32), pltpu.VMEM((SCAN_T * LANE_GROUPS, 128), F32),
                 pltpu.VMEM(_TM_SHAPE, F32), pltpu.VMEM(_TM_SHAPE, F32)],
        args=(dy, z, ypre, h_re, h_im, h_re, h_im, bbt_re, bbt_im, c_re, c_im, a_re, a_im, d_skip, wglu, wglu_t),
        side=side)


def _blockdiag(x):
    gs = SSM_GROUPS // SSM_SUB
    x = x.reshape(SSM_SUB, gs, SSM_GROUP, SSM_STATE)
    eye = jnp.eye(gs, dtype=x.dtype)
    return (x[:, :, :, None, :] * eye[None, :, None, :, None]).reshape(SSM_SUB, SUB_IN, SUB_ST)


def _blockdiag_extract(x):
    gs = SSM_GROUPS // SSM_SUB
    x = x.reshape(SSM_SUB, gs, SSM_GROUP, gs, SSM_STATE)
    eye = jnp.eye(gs, dtype=x.dtype)
    return jnp.sum(x * eye[None, :, None, :, None], axis=3).reshape(SSM_GROUPS, SSM_GROUP, SSM_STATE)


def loss_head(x, g, target):
    S = x.shape[0]

    def body(x_ref, g_ref, t_ref, dx_ref, loss_ref, dg_ref):
        @pl.when(pl.program_id(0) == 0)
        def _():
            loss_ref[...] = jnp.zeros_like(loss_ref)
            dg_ref[...] = jnp.zeros_like(dg_ref)

        xv = x_ref[...]
        gv = g_ref[...]
        r = lax.rsqrt(jnp.mean(xv * xv, axis=-1, keepdims=True) + RMS_EPS)
        xhat = xv * r
        err = xhat * gv - t_ref[...]
        loss_ref[...] += jnp.sum((err * err).reshape(TM // 8, 8, D_MODEL), axis=0) * (0.5 / D_MODEL)
        dy = err * (1.0 / D_MODEL)
        dxhat = dy * gv
        dx_ref[...] = r * (dxhat - xhat * jnp.mean(dxhat * xhat, axis=-1, keepdims=True))
        dg_ref[...] += jnp.sum((dy * xhat).reshape(TM // 8, 8, D_MODEL), axis=0)

    row = pl.BlockSpec((TM, D_MODEL), lambda i: (i, 0))
    acc = _full((8, D_MODEL))
    return pl.pallas_call(
        body, name="loss_head", grid=(S // TM,), in_specs=[row, _full((1, D_MODEL)), row],
        out_specs=[row, acc, acc],
        out_shape=[jax.ShapeDtypeStruct((S, D_MODEL), F32), jax.ShapeDtypeStruct((8, D_MODEL), F32),
                   jax.ShapeDtypeStruct((8, D_MODEL), F32)],
        compiler_params=_cparams("arbitrary"),
    )(x, g, target)


def _x_spec():
    return pl.BlockSpec((TM, D_MODEL), lambda i: (i, 0))


def _g_spec():
    return pl.BlockSpec((1, D_MODEL), lambda i: (0, 0))


def _norm_prologue(x, g):
    h = _rms(x, g).astype(BF16)
    return h, h


def _cast_prologue(x):
    return (x.astype(BF16),)


def _swiglu_prologue(h1, h2):
    a = h1.astype(F32)
    act = (a * jax.nn.sigmoid(a) * h2.astype(F32)).astype(BF16)
    return act, act


def _ssm_consts(lw):
    a_re, a_im, bbt_re, bbt_im = ssm_disc_fwd(
        lw["ssm_lambda_re"], lw["ssm_lambda_im"], lw["ssm_log_dt"].reshape(SSM_GROUPS, 1), lw["bt_re"], lw["bt_im"])
    bb_re = _blockdiag(bbt_re).astype(BF16)
    bb_im = _blockdiag(bbt_im).astype(BF16)
    c_re = _blockdiag(lw["ssm_c_re"]).astype(BF16)
    c_im = _blockdiag(lw["ssm_c_im"]).astype(BF16)
    return dict(
        a_re=a_re.reshape(_TM_SHAPE), a_im=a_im.reshape(_TM_SHAPE),
        bb_re=bb_re, bb_im=bb_im, bbt_re=jnp.swapaxes(bb_re, 1, 2), bbt_im=jnp.swapaxes(bb_im, 1, 2),
        c_re=c_re, c_im=c_im, ct_re=jnp.swapaxes(c_re, 1, 2), ct_im=jnp.swapaxes(c_im, 1, 2))


def layer_fwd(x, lw, bias, next_shards=None, place=None):
    sides = (None, None) if next_shards is None else (("gather_chips", next_shards[:GATHER_SPLIT]),
                                                      ("gather_chips", next_shards[GATHER_SPLIT:]))
    out = fused_mm("in_proj", [x, lw["norm_mix"]], [_x_spec(), _g_spec()], _norm_prologue, lw["w_in"], tn=2944,
                   out_dtype=BF16, extras=((D_MODEL, BF16),), side=sides[0])
    (z, h), g4a = (out, []) if next_shards is None else out
    sc = _ssm_consts(lw)
    (y_ssm, ypre, h_re, h_im), g4b = ssm_fwd(
        z, sc["bb_re"], sc["bb_im"], sc["ct_re"], sc["ct_im"], sc["a_re"], sc["a_im"], lw["ssm_d"], lw["ssm_w_glu"],
        side=sides[1])
    y_conv = conv_fwd(z, lw["conv_w"])
    if next_shards is not None:
        g4 = [_put_slot(g, b, place[0]) for g, b in zip(list(g4a) + list(g4b), next_shards)]
    y_attn, g8 = attn_fwd(z, bias, lw["attn_sinks"], side=None if next_shards is None else ("gather_cores", g4))
    next_gathered = None if next_shards is None else [_put_slot(g, b, place[1]) for g, b in zip(g8, g4)]
    merged = merge_fwd(z, y_ssm, y_conv, y_attn, lw["w_branch"])
    x1 = fused_mm("out_proj", [merged], [_x_spec()], _cast_prologue, lw["w_out"], tn=1024, out_dtype=F32, res=x)
    hf, hn1 = fused_mm("ffn_in", [x1, lw["norm_ffn"]], [_x_spec(), _g_spec()], _norm_prologue, lw["w_ffn_in"], tn=2816,
                       out_dtype=BF16, extras=((D_MODEL, BF16),))
    x2, act = fused_mm("ffn_out", [hf, hf], [_row_spec(FFN_HIDDEN, 0), _row_spec(FFN_HIDDEN, 1)], _swiglu_prologue,
                       lw["w_ffn_out"], tn=1024, out_dtype=F32, res=x1, extras=((FFN_HIDDEN, BF16),))
    a_pre, hn2 = fused_mm("ple_gate", [x2, lw["norm_ple"]], [_x_spec(), _g_spec()], _norm_prologue, lw["w_ple_gate"],
                          tn=1024, out_dtype=BF16, extras=((D_MODEL, BF16),))
    pp = fused_mm("ple_proj", [lw["p"]], [_row_spec(PLE_DIM)], _cast_prologue, lw["w_ple_proj"], tn=1024, out_dtype=BF16)
    x3 = ple_combine(x2, a_pre, pp)
    res = dict(x=x, z=z, h=h, y_ssm=y_ssm, ypre=ypre, h_re=h_re, h_im=h_im, y_conv=y_conv, y_attn=y_attn, merged=merged,
               x1=x1, hf=hf, hn1=hn1, act=act, x2=x2, a_pre=a_pre, hn2=hn2, pp=pp)
    return x3, res, next_gathered


def _pair_sums(split, from_sibling):
    return [pair_sum("pair_sum_" + n, a.reshape(2, -1, a.shape[-1]), b.reshape(-1, b.shape[-1])).reshape(b.shape)
            for n, a, b in zip(SHARDED, split, from_sibling)]


def layer_bwd(dx3, lw, res, bias, pending=None):
    g = {}
    da, dpp = ple_bwd(dx3, res["a_pre"], res["pp"])
    g["w_ple_proj"] = mm_tn("d_w_ple_proj", lw["p"], dpp)
    g["w_ple_gate"] = mm_tn("d_w_ple_gate", res["hn2"], da)
    dx2, g["norm_ple"] = mm_norm_bwd("d_ple_gate", [da], lw["w_ple_gate"], res["x2"], lw["norm_ple"], dx3, tm=TM)
    sums = None
    if pending is None:
        dact = fused_mm("d_ffn_out", [dx2], [_x_spec()], _cast_prologue, lw["w_ffn_out"], tn=1408, out_dtype=BF16,
                        nt=True)
    else:
        dact, from_sibling = fused_mm("d_ffn_out", [dx2], [_x_spec()], _cast_prologue, lw["w_ffn_out"], tn=1408,
                                      out_dtype=BF16, side=("scatter_cores", pending), nt=True)
        sums = _pair_sums(pending, from_sibling)
    g["w_ffn_out"] = mm_tn("d_w_ffn_out", res["act"], dx2)
    dh1, dh2 = swiglu_bwd(dact, res["hf"])
    g["w_ffn_in"] = jnp.concatenate([mm_tn("d_w_ffn_in_a", res["hn1"], dh1), mm_tn("d_w_ffn_in_b", res["hn1"], dh2)],
                                    axis=1)
    dx1, g["norm_ffn"] = mm_norm_bwd("d_ffn_in", [dh1, dh2], lw["w_ffn_in"], res["x1"], lw["norm_ffn"], dx2, tm=TM // 2)
    dmerged = fused_mm("d_out_proj", [dx1], [_x_spec()], _cast_prologue, lw["w_out"], tn=1024, out_dtype=BF16, nt=True)
    g["w_out"] = mm_tn("d_w_out", res["merged"], dx1)
    z = res["z"]
    ys = (res["y_ssm"], res["y_conv"], res["y_attn"])
    dgates, dbs = merge_bwd(dmerged, z, *ys, lw["w_branch"])
    dys, dwb = [], []
    for r in range(3):
        dys.append(fused_mm(f"d_branch_{r}", [dbs[r]], [_x_spec()], _cast_prologue, lw["w_branch"][r], tn=1024,
                            out_dtype=BF16, nt=True))
        dwb.append(mm_tn(f"d_w_branch_{r}", ys[r], dbs[r]))
    g["w_branch"] = jnp.stack(dwb)
    sc = _ssm_consts(lw)
    (du, dbb_re, dbb_im, dct_re, dct_im, da_re, da_im, g["ssm_d"], g["ssm_w_glu"]), received = ssm_bwd(
        dys[0], z, res["ypre"], res["h_re"], res["h_im"], sc["bbt_re"], sc["bbt_im"], sc["c_re"], sc["c_im"],
        sc["a_re"], sc["a_im"], lw["ssm_d"], lw["ssm_w_glu"], lw["ssm_w_glu_t"],
        side=None if pending is None else ("scatter_chips", sums))
    g["ssm_c_re"] = _blockdiag_extract(jnp.swapaxes(dct_re, 1, 2))
    g["ssm_c_im"] = _blockdiag_extract(jnp.swapaxes(dct_im, 1, 2))
    (g["ssm_lambda_re"], g["ssm_lambda_im"], dlog_dt, g["bt_re"], g["bt_im"]) = ssm_disc_bwd(
        lw["ssm_lambda_re"], lw["ssm_lambda_im"], lw["ssm_log_dt"].reshape(SSM_GROUPS, 1), lw["bt_re"], lw["bt_im"],
        da_re.reshape(_GN), da_im.reshape(_GN),
        _blockdiag_extract(dbb_re), _blockdiag_extract(dbb_im))
    g["ssm_log_dt"] = dlog_dt.reshape(SSM_GROUPS)
    dconv, g["conv_w"] = conv_bwd(dys[1], z, lw["conv_w"])
    dq, dkv_cur, dkv_prev, g["dbias"], g["attn_sinks"] = attn_bwd(dys[2], z, bias, lw["attn_sinks"])
    dkv = kv_shift_add(dkv_cur, dkv_prev)
    pieces = [dgates[0], dgates[1], dgates[2], du, dconv, dq, dkv]
    g["w_in"] = jnp.concatenate([mm_tn(f"d_w_in_{k}", res["h"], pc) for k, pc in enumerate(pieces)], axis=1)
    dx0, g["norm_mix"] = mm_norm_bwd("d_in_proj", pieces, lw["w_in"], res["x"], lw["norm_mix"], dx1, tm=TM // 2)
    return dx0, g, (sums, received)


def adamw(name, parts, w, m, v):
    n, R, C = parts.shape
    tr = _pick(R, (512, 256, 128, 64, 32, 16, 8))

    def body(p_ref, w_ref, m_ref, v_ref, g_ref, d_ref, nm_ref, nv_ref):
        gsum = p_ref[0].astype(F32)
        for k in range(1, n):
            gsum = gsum + p_ref[k].astype(F32)
        mn = ADAM_B1 * m_ref[...] + (1.0 - ADAM_B1) * gsum
        vn = ADAM_B2 * v_ref[...] + (1.0 - ADAM_B2) * jnp.square(gsum)
        m_hat = mn / (1.0 - ADAM_B1 ** ADAM_STEP)
        v_hat = vn / (1.0 - ADAM_B2 ** ADAM_STEP)
        g_ref[...] = gsum
        d_ref[...] = -ADAM_LR * (m_hat / (jnp.sqrt(v_hat) + ADAM_EPS) + ADAM_WD * w_ref[...])
        nm_ref[...] = mn
        nv_ref[...] = vn

    blk = pl.BlockSpec((tr, C), lambda i: (i, 0))
    return pl.pallas_call(
        body, name=name, grid=(R // tr,), in_specs=[pl.BlockSpec((n, tr, C), lambda i: (0, i, 0)), blk, blk, blk],
        out_specs=[blk] * 4, out_shape=[jax.ShapeDtypeStruct((R, C), F32)] * 4, compiler_params=_cparams("parallel"),
    )(parts, w, m, v)


_ANY = pl.BlockSpec(memory_space=pl.ANY)


def _coords():
    return lax.axis_index("x"), lax.axis_index("y"), lax.axis_index("c")


def _chip_peers(x, y):
    return [(1 - x, y), (x, 1 - y), (1 - x, 1 - y)]


def _gather_chips_copies(x_refs, out_refs, send_sems, recv_sems):
    x, y, c = _coords()
    me = 2 * x + y
    peers = _chip_peers(x, y)

    def copy(i, k, slot):
        return pltpu.make_async_remote_copy(
            src_ref=x_refs[i], dst_ref=out_refs[i].at[slot], send_sem=send_sems.at[3 * i + k],
            recv_sem=recv_sems.at[3 * i + k], device_id=(*peers[k], c), device_id_type=MESH)

    n = len(x_refs)
    sends = [copy(i, k, me) for i in range(n) for k in range(3)]
    recvs = [copy(i, k, 2 * px + py) for i in range(n) for k, (px, py) in enumerate(peers)]
    return sends, recvs


def _gather_cores_copies(x_refs, out_refs, send_sems, recv_sems):
    x, y, c = _coords()

    def copy(i, slot):
        return pltpu.make_async_remote_copy(
            src_ref=x_refs[i], dst_ref=out_refs[i].at[slot], send_sem=send_sems.at[i], recv_sem=recv_sems.at[i],
            device_id=(x, y, 1 - c), device_id_type=MESH)

    n = len(x_refs)
    return [copy(i, c) for i in range(n)], [copy(i, 1 - c) for i in range(n)]


def _scatter_cores_copies(x_refs, out_refs, send_sems, recv_sems):
    x, y, c = _coords()
    copies = [pltpu.make_async_remote_copy(
        src_ref=x_refs[i].at[1 - c], dst_ref=out_refs[i], send_sem=send_sems.at[i], recv_sem=recv_sems.at[i],
        device_id=(x, y, 1 - c), device_id_type=MESH) for i in range(len(x_refs))]
    return copies, copies


def _scatter_chips_copies(x_refs, out_refs, send_sems, recv_sems):
    x, y, c = _coords()
    me = 2 * x + y
    peers = _chip_peers(x, y)

    def copy(i, k, src_slot, dst_slot):
        return pltpu.make_async_remote_copy(
            src_ref=x_refs[i].at[src_slot], dst_ref=out_refs[i].at[dst_slot], send_sem=send_sems.at[3 * i + k],
            recv_sem=recv_sems.at[3 * i + k], device_id=(*peers[k], c), device_id_type=MESH)

    n = len(x_refs)
    sends = [copy(i, k, 2 * px + py, me) for i in range(n) for k, (px, py) in enumerate(peers)]
    recvs = [copy(i, k, me, 2 * px + py) for i in range(n) for k, (px, py) in enumerate(peers)]
    return sends, recvs


_EXCHANGES = {
    "gather_chips": (lambda b: (4,) + b.shape, 3, _gather_chips_copies),
    "gather_cores": (lambda b: (2,) + b.shape, 1, _gather_cores_copies),
    "scatter_cores": (lambda b: b.shape[1:], 1, _scatter_cores_copies),
    "scatter_chips": (lambda b: b.shape, 3, _scatter_chips_copies),
}


def exchange(name, kind, blks):
    out_shape_of, per_block, make_copies = _EXCHANGES[kind]
    n = len(blks)

    def body(*refs):
        sends, recvs = make_copies(refs[:n], refs[n:2 * n], refs[2 * n], refs[2 * n + 1])
        for cp in sends:
            cp.start()
        for cp in recvs:
            cp.wait_recv()
        for cp in sends:
            cp.wait_send()

    return pl.pallas_call(
        body, name=name, in_specs=[_ANY] * n, out_specs=[_ANY] * n,
        out_shape=[jax.ShapeDtypeStruct(out_shape_of(b), b.dtype) for b in blks],
        scratch_shapes=[pltpu.SemaphoreType.DMA((per_block * n,)), pltpu.SemaphoreType.DMA((per_block * n,))],
    )(*blks)


def _put_slot(buf, block, idx):
    return lax.dynamic_update_slice(buf, block[None].astype(buf.dtype), (idx,) + (0,) * block.ndim)


def pair_sum(name, mine, theirs):
    _, R, C = mine.shape
    tr = _pick(R, (1024, 512, 256, 128, 64, 32, 16))
    c_idx = lax.axis_index("c").astype(jnp.int32).reshape(1)

    def body(c_ref, a_ref, b_ref, o_ref):
        o_ref[...] = (a_ref[0].astype(F32) + b_ref[...].astype(F32)).astype(BF16)

    return pl.pallas_call(
        body, name=name,
        grid_spec=pltpu.PrefetchScalarGridSpec(
            num_scalar_prefetch=1, grid=(R // tr,),
            in_specs=[pl.BlockSpec((1, tr, C), lambda i, c: (c[0], i, 0)), pl.BlockSpec((tr, C), lambda i, c: (i, 0))],
            out_specs=pl.BlockSpec((tr, C), lambda i, c: (i, 0))),
        out_shape=jax.ShapeDtypeStruct(theirs.shape, BF16), compiler_params=_cparams("parallel"),
    )(c_idx, mine, theirs)


SHARDED = {
    "w_in": ((D_MODEL, IN_WIDTH), 2), "ssm_w_glu": ((BRANCH, BRANCH), 1), "conv_w": ((3, BRANCH), 2),
    "w_branch": ((3, BRANCH, D_MODEL), 3), "w_out": ((D_MODEL, D_MODEL), 1), "w_ffn_in": ((D_MODEL, 2 * FFN_HIDDEN), 2),
    "w_ffn_out": ((FFN_HIDDEN, D_MODEL), 1), "w_ple_gate": ((D_MODEL, D_MODEL), 1), "w_ple_proj": ((PLE_DIM, D_MODEL), 2),
}
SMALL = ["rel_bias", "norm_mix", "ssm_lambda_re", "ssm_lambda_im", "ssm_b_re", "ssm_b_im", "ssm_c_re", "ssm_c_im", "ssm_d",
         "ssm_log_dt", "attn_sinks", "norm_ffn", "norm_ple", "norm_final"]
GATHER_SPLIT = 4
WEIGHTS = ["rel_bias", "norm_mix", "w_in", "ssm_lambda_re", "ssm_lambda_im", "ssm_b_re", "ssm_b_im", "ssm_c_re", "ssm_c_im",
           "ssm_d", "ssm_log_dt", "ssm_w_glu", "conv_w", "attn_sinks", "w_branch", "w_out", "norm_ffn", "w_ffn_in",
           "w_ffn_out", "norm_ple", "w_ple_gate", "w_ple_proj", "norm_final"]


def _pad_to(flat, n):
    return jnp.pad(flat, [(0, 0)] * (flat.ndim - 1) + [(0, n - flat.shape[-1])])


def _unshard(g8, name):
    axis = SHARDED[name][1] - 1
    shard = g8.shape[2:]
    b = g8.reshape((2, 2, 2) + shard)
    b = jnp.moveaxis(b, (1, 2, 0), (axis, axis + 1, axis + 2))
    full = list(shard)
    full[axis] *= N_DEV
    return b.reshape(full)


def _shard_split(full, name):
    axis = SHARDED[name][1] - 1
    dims = list(full.shape)
    dims[axis:axis + 1] = [2, 2, 2, dims[axis] // N_DEV]
    b = jnp.moveaxis(full.reshape(dims), (axis, axis + 1, axis + 2), (1, 2, 0))
    return b.reshape((2, 4) + b.shape[3:])


def _small_sizes(shapes):
    return [-(-int(np.prod(shapes[n])) // 128) * 128 for n in SMALL]


def pack_small(vals, shapes, extra):
    segs = [_pad_to(vals[n].reshape(-1).astype(F32), s) for n, s in zip(SMALL, _small_sizes(shapes))]
    segs.append(_pad_to(extra.reshape(-1), 128))
    flat = jnp.concatenate(segs)
    rows = -(-flat.shape[0] // (128 * 8)) * 8
    return _pad_to(flat, rows * 128).reshape(rows, 128)


def unpack_small(packed, shapes):
    flat = packed.reshape(-1)
    out, off = {}, 0
    for n, s in zip(SMALL, _small_sizes(shapes)):
        out[n] = flat[off:off + int(np.prod(shapes[n]))].reshape(shapes[n])
        off += s
    return out, flat[off]


def _layer_weights(gathered, small, p, i):
    full = {n: _unshard(g, n) for n, g in zip(SHARDED, gathered)}
    w_in = full["w_in"]
    w_in_p = jnp.concatenate([w_in[:, 2816:], w_in[:, :2560], w_in[:, 2560:2816]], axis=-1)
    return dict(
        w_in=w_in_p,
        ssm_w_glu=full["ssm_w_glu"], ssm_w_glu_t=full["ssm_w_glu"].T,
        conv_w=full["conv_w"],
        w_branch=full["w_branch"], w_out=full["w_out"], w_ffn_in=full["w_ffn_in"], w_ffn_out=full["w_ffn_out"],
        w_ple_gate=full["w_ple_gate"],
        w_ple_proj=full["w_ple_proj"],
        norm_mix=small["norm_mix"][i][None, :], norm_ffn=small["norm_ffn"][i][None, :],
        norm_ple=small["norm_ple"][i][None, :],
        ssm_lambda_re=small["ssm_lambda_re"][i], ssm_lambda_im=small["ssm_lambda_im"][i],
        ssm_log_dt=small["ssm_log_dt"][i],
        bt_re=jnp.swapaxes(small["ssm_b_re"][i], 1, 2), bt_im=jnp.swapaxes(small["ssm_b_im"][i], 1, 2),
        ssm_c_re=small["ssm_c_re"][i], ssm_c_im=small["ssm_c_im"][i], ssm_d=small["ssm_d"][i][None, :],
        attn_sinks=small["attn_sinks"][i], p=p[i],
    )


def matrix_grads(g):
    w_in_g = g["w_in"]
    return dict(
        w_in=jnp.concatenate([w_in_g[:, 3072:5632], w_in_g[:, 5632:], w_in_g[:, :3072]], axis=-1),
        ssm_w_glu=g["ssm_w_glu"], conv_w=jnp.sum(g["conv_w"].reshape(3, 8, BRANCH), axis=1),
        w_branch=g["w_branch"], w_out=g["w_out"], w_ffn_in=g["w_ffn_in"], w_ffn_out=g["w_ffn_out"],
        w_ple_gate=g["w_ple_gate"], w_ple_proj=g["w_ple_proj"])


def small_grads(per_layer, dg_final):
    keys = ("dbias", "norm_mix", "ssm_lambda_re", "ssm_lambda_im", "bt_re", "bt_im", "ssm_c_re", "ssm_c_im", "ssm_d",
            "ssm_log_dt", "attn_sinks", "norm_ffn", "norm_ple")
    g = {k: jnp.stack([gl[k] for gl in per_layer]) for k in keys}
    drel = rel_bias_bwd(g["dbias"].reshape(DEPTH, N_Q_HEADS, BLOCK * 2 * BLOCK)).T
    return dict(
        rel_bias=drel, norm_mix=jnp.sum(g["norm_mix"], axis=1), ssm_lambda_re=g["ssm_lambda_re"],
        ssm_lambda_im=g["ssm_lambda_im"], ssm_b_re=jnp.swapaxes(g["bt_re"], 2, 3), ssm_b_im=jnp.swapaxes(g["bt_im"], 2, 3),
        ssm_c_re=g["ssm_c_re"], ssm_c_im=g["ssm_c_im"], ssm_d=g["ssm_d"][:, 0, :], ssm_log_dt=g["ssm_log_dt"],
        attn_sinks=g["attn_sinks"][:, :, 0], norm_ffn=jnp.sum(g["norm_ffn"], axis=1), norm_ple=jnp.sum(g["norm_ple"], axis=1),
        norm_final=jnp.sum(dg_final, axis=0))


def kernel(x, p, rel_bias, norm_mix, w_in, ssm_lambda_re, ssm_lambda_im, ssm_b_re, ssm_b_im, ssm_c_re, ssm_c_im, ssm_d, ssm_log_dt, ssm_w_glu, conv_w, attn_sinks, w_branch, w_out, norm_ffn, w_ffn_in, w_ffn_out, norm_ple, w_ple_gate, w_ple_proj, norm_final, loss_target, m_rel_bias, m_norm_mix, m_w_in, m_ssm_lambda_re, m_ssm_lambda_im, m_ssm_b_re, m_ssm_b_im, m_ssm_c_re, m_ssm_c_im, m_ssm_d, m_ssm_log_dt, m_ssm_w_glu, m_conv_w, m_attn_sinks, m_w_branch, m_w_out, m_norm_ffn, m_w_ffn_in, m_w_ffn_out, m_norm_ple, m_w_ple_gate, m_w_ple_proj, m_norm_final, v_rel_bias, v_norm_mix, v_w_in, v_ssm_lambda_re, v_ssm_lambda_im, v_ssm_b_re, v_ssm_b_im, v_ssm_c_re, v_ssm_c_im, v_ssm_d, v_ssm_log_dt, v_ssm_w_glu, v_conv_w, v_attn_sinks, v_w_branch, v_w_out, v_norm_ffn, v_w_ffn_in, v_w_ffn_out, v_norm_ple, v_w_ple_gate, v_w_ple_proj, v_norm_final):
    args = dict(locals())
    w = {n: args[n] for n in WEIGHTS}
    m = {n: args["m_" + n] for n in WEIGHTS}
    v = {n: args["v_" + n] for n in WEIGHTS}
    shapes = {n: w[n].shape for n in SMALL}

    x_i, y_i, c_i = _coords()
    chip = 2 * x_i + y_i
    place = (chip, c_i)
    small = {n: w[n] for n in SMALL}

    def all_gather(tag, blks):
        g4 = exchange(f"gather_{tag}_chips", "gather_chips", blks)
        g4 = [_put_slot(g, b, chip) for g, b in zip(g4, blks)]
        g8 = exchange(f"gather_{tag}_cores", "gather_cores", g4)
        return [_put_slot(g, b, c_i) for g, b in zip(g8, g4)]

    def shards_of(layer):
        return [w[n][layer] if n == "conv_w" else w[n][layer].astype(BF16) for n in SHARDED]

    bias = rel_bias_fwd(small["rel_bias"].T).reshape(N_Q_HEADS, BLOCK, 2 * BLOCK)
    xs, layers, res = x[0], [], []
    gathered = all_gather("w", shards_of(0))
    for layer in range(DEPTH):
        layers.append(_layer_weights(gathered, small, p[:, 0], layer))
        nxt = shards_of(layer + 1) if layer + 1 < DEPTH else None
        xs, res_l, gathered = layer_fwd(xs, layers[layer], bias, nxt, place)
        res.append(res_l)
    grad_x, loss_parts, dg_final = loss_head(xs, small["norm_final"][None, :], loss_target[0])

    per_layer, reduced, pending = [None] * DEPTH, [None] * DEPTH, None
    for layer in reversed(range(DEPTH)):
        grad_x, per_layer[layer], done = layer_bwd(grad_x, layers[layer], res[layer], bias, pending)
        if pending is not None:
            reduced[layer + 1] = done
        mg = matrix_grads(per_layer[layer])
        pending = [_shard_split(mg[n], n).astype(BF16) for n in SHARDED]
    sums = _pair_sums(pending, exchange("scatter_g_cores", "scatter_cores", pending))
    reduced[0] = (sums, exchange("scatter_g_chips", "scatter_chips", sums))

    outs = ({}, {}, {}, {})
    for k, name in enumerate(SHARDED):
        parts = jnp.stack([_put_slot(rcv[k], lax.dynamic_index_in_dim(sm[k], chip, 0, keepdims=False), chip)
                           for sm, rcv in reduced], axis=1)
        cols = parts.shape[-1]
        res4 = adamw("adamw_" + name, parts.reshape(4, -1, cols), w[name].reshape(-1, cols), m[name].reshape(-1, cols),
                     v[name].reshape(-1, cols))
        for d, o in zip(outs, res4):
            d[name] = o.reshape(w[name].shape)

    small_local = pack_small(small_grads(per_layer, dg_final), shapes, jnp.sum(loss_parts))
    small_all = all_gather("s", [small_local])[0]
    zero = jnp.zeros((1,), F32)
    res4 = adamw("adamw_small", small_all.reshape(N_DEV, small_local.shape[0], 128),
                 pack_small({n: w[n] for n in SMALL}, shapes, zero), pack_small({n: m[n] for n in SMALL}, shapes, zero),
                 pack_small({n: v[n] for n in SMALL}, shapes, zero))
    loss = None
    for d, r in zip(outs, res4):
        vals, extra = unpack_small(r, shapes)
        d.update(vals)
        if loss is None:
            loss = extra

    return (loss, grad_x[None], *[d[n] for d in outs for n in WEIGHTS])
```

```python
import functools
import math

import numpy as np
import jax
import jax.numpy as jnp
from jax import lax
from jax.experimental import pallas as pl
from jax.experimental.pallas import tpu as pltpu

F32 = jnp.float32
BF16 = jnp.bfloat16
MESH = pl.DeviceIdType.MESH

D_MODEL = 1024
DEPTH = 4
PLE_DIM = 256
BRANCH = 512
SSM_GROUPS = 32
SSM_GROUP = 16
SSM_STATE = 64
SSM_LANES = SSM_GROUPS * SSM_STATE
SSM_SUB = 4
SUB_IN = BRANCH // SSM_SUB
SUB_ST = SSM_LANES // SSM_SUB
HEAD_DIM = 64
N_Q_HEADS = 8
N_KV_HEADS = 2
GQA_GROUP = 4
KV_WIDTH = 2 * N_KV_HEADS * HEAD_DIM
WINDOW = 128
BLOCK = 128
ATTN_SCALE = 1.0 / math.sqrt(HEAD_DIM)
REL_BUCKETS = 32
REL_MAX_DIST = 128
FFN_HIDDEN = 2816
RMS_EPS = 1e-6
IN_WIDTH = 5888
N_DEV = 8

ADAM_LR = 0.001
ADAM_B1 = 0.9
ADAM_B2 = 0.999
ADAM_EPS = 1e-08
ADAM_WD = 0.01
ADAM_STEP = 10

COL_U = 3072
COL_KV = 5632
NEG = -1e30

SCAN_T = 256
TM = 512
TM_W = 1024
VMEM_LIMIT = 52 * 1024 * 1024


def _cparams(*sem):
    return pltpu.CompilerParams(dimension_semantics=sem, vmem_limit_bytes=VMEM_LIMIT)


def _full(shape):
    n = len(shape)
    return pl.BlockSpec(shape, lambda *_: (0,) * n)


def _pick(n, cands):
    for c in cands:
        if n % c == 0:
            return c
    return n


def _call(body, *, name, steps, in_specs, out_specs, out_shape, scratch, args, side=None):
    in_specs, out_specs, out_shape = list(in_specs), list(out_specs), list(out_shape)
    scratch, args = list(scratch), list(args)
    n_in, n_out, n_scr = len(in_specs), len(out_specs), len(scratch)
    n = 0
    if side is not None:
        kind, blks = side
        out_shape_of, per_block, make_copies = _EXCHANGES[kind]
        n = len(blks)
        inner = body

        def body(*refs):
            ins, sx = refs[:n_in], refs[n_in:n_in + n]
            outs, so = refs[n_in + n:n_in + n + n_out], refs[n_in + n + n_out:n_in + 2 * n + n_out]
            scr = refs[n_in + 2 * n + n_out:n_in + 2 * n + n_out + n_scr]
            send_sems, recv_sems = refs[-2:]

            @pl.when(pl.program_id(0) == 0)
            def _():
                sends, _ = make_copies(sx, so, send_sems, recv_sems)
                for cp in sends:
                    cp.start()

            inner(*ins, *outs, *scr)

            @pl.when(pl.program_id(0) == steps - 1)
            def _():
                sends, recvs = make_copies(sx, so, send_sems, recv_sems)
                for cp in recvs:
                    cp.wait_recv()
                for cp in sends:
                    cp.wait_send()

        in_specs += [_ANY] * n
        args += list(blks)
        out_specs += [_ANY] * n
        out_shape += [jax.ShapeDtypeStruct(out_shape_of(b), b.dtype) for b in blks]
        scratch += [pltpu.SemaphoreType.DMA((per_block * n,)), pltpu.SemaphoreType.DMA((per_block * n,))]
    outs = pl.pallas_call(
        body, name=name, grid=(steps,), in_specs=in_specs, out_specs=out_specs, out_shape=out_shape,
        scratch_shapes=scratch, compiler_params=_cparams("arbitrary"),
    )(*args)
    return outs[:n_out], outs[n_out:]


def _dot(a, b):
    return jnp.dot(a, b, preferred_element_type=F32)


def _dot_tn(a, b):
    return lax.dot_general(a, b, (((0,), (0,)), ((), ())), preferred_element_type=F32)


def _dot_nt(a, b):
    return lax.dot_general(a, b, (((1,), (1,)), ((), ())), preferred_element_type=F32)


def _rms(x, g):
    r = lax.rsqrt(jnp.mean(x * x, axis=-1, keepdims=True) + RMS_EPS)
    return x * r * g


def fused_mm(name, ins, in_specs, prologue, w, *, tn, out_dtype, res=None, extras=(), side=None, nt=False):
    S = ins[0].shape[0]
    N, K = w.shape if nt else w.shape[::-1]
    tn = min(tn, N)
    n_in, n_ex = len(ins), len(extras)

    def body(*refs):
        in_refs = refs[:n_in]
        w_ref = refs[n_in]
        pos = n_in + 1
        res_ref = None
        if res is not None:
            res_ref = refs[pos]
            pos += 1
        o_ref = refs[pos]
        ex_refs = refs[pos + 1:pos + 1 + n_ex]
        a_scr = refs[-1]
        out = prologue(*[r[...] for r in in_refs])
        a_scr[...] = out[0]
        for r, e in zip(ex_refs, out[1:]):
            r[...] = e.astype(r.dtype)
        for j in range(N // tn):
            cs = slice(j * tn, (j + 1) * tn)
            acc = _dot_nt(a_scr[...], w_ref[cs, :]) if nt else _dot(a_scr[...], w_ref[:, cs])
            if res_ref is not None:
                acc = acc + res_ref[:, cs]
            o_ref[:, cs] = acc.astype(o_ref.dtype)

    specs = list(in_specs) + [pl.BlockSpec(w.shape, lambda i: (0, 0), pipeline_mode=pl.Buffered(1))]
    args = list(ins) + [w]
    if res is not None:
        specs.append(pl.BlockSpec((TM, N), lambda i: (i, 0)))
        args.append(res)
    out_shape = [jax.ShapeDtypeStruct((S, N), out_dtype)]
    out_specs = [pl.BlockSpec((TM, N), lambda i: (i, 0))]
    for cols, dt in extras:
        out_shape.append(jax.ShapeDtypeStruct((S, cols), dt))
        out_specs.append(pl.BlockSpec((TM, cols), lambda i: (i, 0)))
    outs, side_outs = _call(body, name=name, steps=S // TM, in_specs=specs, out_specs=out_specs, out_shape=out_shape,
                            scratch=[pltpu.VMEM((TM, K), BF16)], args=args, side=side)
    result = outs if n_ex else outs[0]
    return result if side is None else (result, side_outs)


def _row_spec(cols, blk=0, tm=TM):
    return pl.BlockSpec((tm, cols), lambda i: (i, blk))


def mm_norm_bwd(name, ins, in_specs, w, x, g, dres, *, tm, pre=None, extras=()):
    S = x.shape[0]
    n = len(ins)

    def body(*refs):
        x_ref, g_ref, dres_ref, w_ref, dx_ref, dg_ref = refs[n:n + 6]

        @pl.when(pl.program_id(0) == 0)
        def _():
            dg_ref[...] = jnp.zeros_like(dg_ref)

        tiles = [r[...] for r in refs[:n]]
        if pre is not None:
            tiles, extra_tiles = pre(*tiles)
            for r, e in zip(refs[n + 6:], extra_tiles):
                r[...] = e.astype(r.dtype)
        tiles = [t.astype(BF16) for t in tiles]
        a = tiles[0] if len(tiles) == 1 else jnp.concatenate(tiles, axis=1)
        dh = _dot_nt(a, w_ref[...])
        xv = x_ref[...]
        r = lax.rsqrt(jnp.mean(xv * xv, axis=-1, keepdims=True) + RMS_EPS)
        xhat = xv * r
        dxhat = dh * g_ref[...]
        dx_ref[...] = dres_ref[...] + r * (dxhat - xhat * jnp.mean(dxhat * xhat, axis=-1, keepdims=True))
        dg_ref[...] += jnp.sum((dh * xhat).reshape(tm // 8, 8, D_MODEL), axis=0)

    row = _row_spec(D_MODEL, tm=tm)
    return pl.pallas_call(
        body, name=name, grid=(S // tm,),
        in_specs=list(in_specs)
        + [row, _full((1, D_MODEL)), row, pl.BlockSpec(w.shape, lambda i: (0, 0), pipeline_mode=pl.Buffered(1))],
        out_specs=[row, _full((8, D_MODEL))] + [_row_spec(cols, tm=tm) for cols, _ in extras],
        out_shape=[jax.ShapeDtypeStruct((S, D_MODEL), F32), jax.ShapeDtypeStruct((8, D_MODEL), F32)]
        + [jax.ShapeDtypeStruct((S, cols), dt) for cols, dt in extras],
        compiler_params=_cparams("arbitrary"),
    )(*ins, x, g, dres, w)


def mm_tn(name, a, b):
    S, K = a.shape
    N = b.shape[1]
    tk = _pick(K, (1024, 1408, 512, 256))
    tn = _pick(N, (1024, 1408, 1536, 512, 256))

    def body(a_ref, b_ref, o_ref):
        @pl.when(pl.program_id(2) == 0)
        def _():
            o_ref[...] = jnp.zeros_like(o_ref)

        o_ref[...] += _dot_tn(a_ref[...].astype(BF16), b_ref[...].astype(BF16))

    return pl.pallas_call(
        body, name=name, grid=(K // tk, N // tn, S // TM_W),
        in_specs=[pl.BlockSpec((TM_W, tk), lambda k, n, s: (s, k)), pl.BlockSpec((TM_W, tn), lambda k, n, s: (s, n))],
        out_specs=pl.BlockSpec((tk, tn), lambda k, n, s: (k, n)),
        out_shape=jax.ShapeDtypeStruct((K, N), F32),
        compiler_params=_cparams("parallel", "parallel", "arbitrary"),
    )(a, b)


def _swiglu_bwd_pre(dact, h1, h2):
    h1 = h1.astype(F32)
    h2 = h2.astype(F32)
    da = dact.astype(F32)
    sg = jax.nn.sigmoid(h1)
    halves = [(da * h2 * sg * (1.0 + h1 * (1.0 - sg))).astype(BF16), (da * h1 * sg).astype(BF16)]
    return halves, halves


def _ple_bwd_pre(dx, a_pre, pp):
    pg = jax.nn.sigmoid(a_pre.astype(F32))
    da = (dx * pp.astype(F32) * pg * (1.0 - pg)).astype(BF16)
    return [da], [da, (dx * pg).astype(BF16)]


def ple_combine(x2, a_pre, pp):
    S = x2.shape[0]

    def body(x_ref, a_ref, p_ref, o_ref):
        o_ref[...] = x_ref[...] + jax.nn.sigmoid(a_ref[...].astype(F32)) * p_ref[...].astype(F32)

    row = pl.BlockSpec((TM, D_MODEL), lambda i: (i, 0))
    return pl.pallas_call(
        body, name="ple_combine", grid=(S // TM,), in_specs=[row, row, row], out_specs=row,
        out_shape=jax.ShapeDtypeStruct((S, D_MODEL), F32), compiler_params=_cparams("parallel"),
    )(x2, a_pre, pp)


def _gate_specs(tn, nn):
    return [pl.BlockSpec((TM, tn), functools.partial(lambda i, j, r: (i, r * nn + j), r=r)) for r in range(3)]


def merge_fwd(z, y_ssm, y_conv, y_attn, wb):
    S = z.shape[0]
    tn = 512
    nn = D_MODEL // tn

    def body(g0, g1, g2, y0, y1, y2, w_ref, o_ref):
        acc = jnp.zeros((TM, tn), F32)
        for r, (g_ref, y_ref) in enumerate(((g0, y0), (g1, y1), (g2, y2))):
            acc += jax.nn.sigmoid(g_ref[...].astype(F32)) * _dot(y_ref[...], w_ref[r])
        o_ref[...] = acc.astype(BF16)

    y_spec = pl.BlockSpec((TM, BRANCH), lambda i, j: (i, 0))
    return pl.pallas_call(
        body, name="merge_fwd", grid=(S // TM, nn),
        in_specs=_gate_specs(tn, nn) + [y_spec] * 3 + [pl.BlockSpec((3, BRANCH, tn), lambda i, j: (0, 0, j))],
        out_specs=pl.BlockSpec((TM, tn), lambda i, j: (i, j)),
        out_shape=jax.ShapeDtypeStruct((S, D_MODEL), BF16), compiler_params=_cparams("parallel", "parallel"),
    )(z, z, z, y_ssm, y_conv, y_attn, wb)


def merge_bwd(dmerged, z, y_ssm, y_conv, y_attn, wb):
    S = z.shape[0]
    tn = 512
    nn = D_MODEL // tn

    def body(dm_ref, g0, g1, g2, y0, y1, y2, w_ref, dg0, dg1, dg2, db0, db1, db2):
        dm = dm_ref[...].astype(F32)
        for r, (g_ref, y_ref, dg_ref, db_ref) in enumerate(((g0, y0, dg0, db0), (g1, y1, dg1, db1), (g2, y2, dg2, db2))):
            sg = jax.nn.sigmoid(g_ref[...].astype(F32))
            b = _dot(y_ref[...], w_ref[r])
            dg_ref[...] = (dm * b * sg * (1.0 - sg)).astype(BF16)
            db_ref[...] = (dm * sg).astype(BF16)

    y_spec = pl.BlockSpec((TM, BRANCH), lambda i, j: (i, 0))
    outs = pl.pallas_call(
        body, name="merge_bwd", grid=(S // TM, nn),
        in_specs=[pl.BlockSpec((TM, tn), lambda i, j: (i, j))] + _gate_specs(tn, nn) + [y_spec] * 3
        + [pl.BlockSpec((3, BRANCH, tn), lambda i, j: (0, 0, j))],
        out_specs=[pl.BlockSpec((TM, tn), lambda i, j: (i, j))] * 6,
        out_shape=[jax.ShapeDtypeStruct((S, D_MODEL), BF16)] * 6, compiler_params=_cparams("parallel", "parallel"),
    )(dmerged, z, z, z, y_ssm, y_conv, y_attn, wb)
    return outs[:3], outs[3:]


def _shift_down(v, halo, k):
    rolled = pltpu.roll(v, k, 0)
    h = pltpu.roll(halo, k, 0)
    row = lax.broadcasted_iota(jnp.int32, v.shape, 0)
    head = jnp.concatenate([h, jnp.zeros((v.shape[0] - 8, v.shape[1]), v.dtype)], axis=0)
    return jnp.where(row < k, head, rolled)


def _shift_up(v, halo, k):
    n = v.shape[0]
    rolled = pltpu.roll(v, n - k, 0)
    h = pltpu.roll(halo, 8 - k, 0)
    row = lax.broadcasted_iota(jnp.int32, v.shape, 0)
    tail = jnp.concatenate([jnp.zeros((n - 8, v.shape[1]), v.dtype), h], axis=0)
    return jnp.where(row >= n - k, tail, rolled)


def _conv_specs():
    rb = TM // 8
    c0 = COL_U // BRANCH

    def cur(k):
        return pl.BlockSpec((TM, BRANCH), lambda i: (i, c0 + k))

    def prev(k):
        return pl.BlockSpec((8, BRANCH), lambda i: (jnp.maximum(i * rb - 1, 0), c0 + k))

    return [cur(1), cur(2), cur(3), prev(2), prev(3)]


def conv_fwd(z, conv_w):
    S = z.shape[0]

    def body(cb_ref, cc_ref, cx_ref, pc_ref, px_ref, w_ref, o_ref):
        first = pl.program_id(0) == 0
        v = cc_ref[...].astype(F32) * cx_ref[...].astype(F32)
        pv = jnp.where(first, 0.0, pc_ref[...].astype(F32) * px_ref[...].astype(F32))
        w = w_ref[...]
        y = w[2:3] * v + w[1:2] * _shift_down(v, pv, 1) + w[0:1] * _shift_down(v, pv, 2)
        o_ref[...] = (cb_ref[...].astype(F32) * y).astype(BF16)

    return pl.pallas_call(
        body, name="conv_fwd", grid=(S // TM,), in_specs=_conv_specs() + [_full((3, BRANCH))],
        out_specs=pl.BlockSpec((TM, BRANCH), lambda i: (i, 0)),
        out_shape=jax.ShapeDtypeStruct((S, BRANCH), BF16), compiler_params=_cparams("parallel"),
    )(z, z, z, z, z, conv_w)


def conv_bwd(dy, z, conv_w):
    S = z.shape[0]
    rb = TM // 8
    nt = S // TM
    c0 = COL_U // BRANCH

    def body(dy_ref, cb_ref, cc_ref, cx_ref, pc_ref, px_ref, ndy_ref, ncb_ref, w_ref, o_ref, dw_ref):
        i = pl.program_id(0)

        @pl.when(i == 0)
        def _():
            dw_ref[...] = jnp.zeros_like(dw_ref)

        cb = cb_ref[...].astype(F32)
        cc = cc_ref[...].astype(F32)
        cx = cx_ref[...].astype(F32)
        dyv = dy_ref[...].astype(F32)
        v = cc * cx
        pv = jnp.where(i == 0, 0.0, pc_ref[...].astype(F32) * px_ref[...].astype(F32))
        v1 = _shift_down(v, pv, 1)
        v2 = _shift_down(v, pv, 2)
        w = w_ref[...]
        conv = w[2:3] * v + w[1:2] * v1 + w[0:1] * v2
        dc = dyv * cb
        ndc = jnp.where(i == nt - 1, 0.0, ndy_ref[...].astype(F32) * ncb_ref[...].astype(F32))
        dv = w[2:3] * dc + w[1:2] * _shift_up(dc, ndc, 1) + w[0:1] * _shift_up(dc, ndc, 2)
        o_ref[:, 0:BRANCH] = (dyv * conv).astype(BF16)
        o_ref[:, BRANCH:2 * BRANCH] = (dv * cx).astype(BF16)
        o_ref[:, 2 * BRANCH:3 * BRANCH] = (dv * cc).astype(BF16)
        for k, vk in enumerate((v2, v1, v)):
            dw_ref[8 * k:8 * k + 8, :] += jnp.sum((dc * vk).reshape(rb, 8, BRANCH), axis=0)

    nxt = jnp.minimum

    return pl.pallas_call(
        body, name="conv_bwd", grid=(nt,),
        in_specs=[pl.BlockSpec((TM, BRANCH), lambda i: (i, 0))] + _conv_specs()
        + [pl.BlockSpec((8, BRANCH), lambda i: (nxt((i + 1) * rb, S // 8 - 1), 0)),
           pl.BlockSpec((8, BRANCH), lambda i: (nxt((i + 1) * rb, S // 8 - 1), c0 + 1)),
           _full((3, BRANCH))],
        out_specs=[pl.BlockSpec((TM, 3 * BRANCH), lambda i: (i, 0)), _full((24, BRANCH))],
        out_shape=[jax.ShapeDtypeStruct((S, 3 * BRANCH), BF16), jax.ShapeDtypeStruct((24, BRANCH), F32)],
        compiler_params=_cparams("arbitrary"),
    )(dy, z, z, z, z, z, dy, z, conv_w)


def _bucket_onehot_t():
    qi = np.arange(BLOCK)[:, None]
    kj = np.arange(2 * BLOCK)[None, :]
    dist = np.clip(qi + BLOCK - kj, 0, REL_MAX_DIST - 1)
    exact = REL_BUCKETS // 2
    df = np.maximum(dist, 1).astype(np.float32)
    large = exact + (np.log(df / np.float32(exact)) / np.float32(math.log(REL_MAX_DIST / exact))
                     * np.float32(REL_BUCKETS - exact)).astype(np.int32)
    large = np.minimum(large, REL_BUCKETS - 1)
    bucket = np.where(dist < exact, dist, large).reshape(-1)
    return (np.arange(REL_BUCKETS)[:, None] == bucket[None, :]).astype(np.float32)


def rel_bias_fwd(rel_bias_t):
    n = BLOCK * 2 * BLOCK

    def body(r_ref, oh_ref, o_ref):
        o_ref[...] = jnp.dot(r_ref[...], oh_ref[...], precision=lax.Precision.HIGHEST, preferred_element_type=F32)

    return pl.pallas_call(
        body, name="rel_bias_fwd", grid=(1,), in_specs=[_full((N_Q_HEADS, REL_BUCKETS)), _full((REL_BUCKETS, n))],
        out_specs=_full((N_Q_HEADS, n)), out_shape=jax.ShapeDtypeStruct((N_Q_HEADS, n), F32),
        compiler_params=_cparams("arbitrary"),
    )(rel_bias_t, jnp.asarray(_bucket_onehot_t()))


def rel_bias_bwd(dbias):
    n_l = dbias.shape[0]
    n = BLOCK * 2 * BLOCK

    def body(d_ref, oh_ref, o_ref):
        tot = d_ref[0]
        for l in range(1, n_l):
            tot = tot + d_ref[l]
        o_ref[...] = lax.dot_general(tot, oh_ref[...], (((1,), (1,)), ((), ())), precision=lax.Precision.HIGHEST,
                                     preferred_element_type=F32)

    return pl.pallas_call(
        body, name="rel_bias_bwd", grid=(1,), in_specs=[_full((n_l, N_Q_HEADS, n)), _full((REL_BUCKETS, n))],
        out_specs=_full((N_Q_HEADS, REL_BUCKETS)), out_shape=jax.ShapeDtypeStruct((N_Q_HEADS, REL_BUCKETS), F32),
        compiler_params=_cparams("arbitrary"),
    )(dbias, jnp.asarray(_bucket_onehot_t()))


def _attn_valid(first):
    qi = lax.broadcasted_iota(jnp.int32, (BLOCK, 2 * BLOCK), 0)
    kj = lax.broadcasted_iota(jnp.int32, (BLOCK, 2 * BLOCK), 1)
    dist = qi + BLOCK - kj
    return (dist >= 0) & (dist < WINDOW) & (jnp.logical_not(first) | (kj >= BLOCK))


def _attn_weights(qh, kcat, bias_h, valid, sink):
    s = _dot_nt(qh, kcat) * ATTN_SCALE + bias_h
    s = jnp.where(valid, s, NEG)
    m = jnp.maximum(jnp.max(s, axis=-1, keepdims=True), sink)
    p = jnp.exp(s - m)
    esink = jnp.exp(sink - m)
    inv = 1.0 / (jnp.sum(p, axis=-1, keepdims=True) + esink)
    return p * inv, esink * inv


def _kv_heads(kvp, kvc, hk):
    ks = slice(hk * HEAD_DIM, (hk + 1) * HEAD_DIM)
    vs = slice(KV_WIDTH // 2 + hk * HEAD_DIM, KV_WIDTH // 2 + (hk + 1) * HEAD_DIM)
    return jnp.concatenate([kvp[:, ks], kvc[:, ks]], axis=0), jnp.concatenate([kvp[:, vs], kvc[:, vs]], axis=0)


def _attn_specs():
    cq = (COL_U + 4 * BRANCH) // BRANCH
    ckv = COL_KV // KV_WIDTH
    return [pl.BlockSpec((BLOCK, BRANCH), lambda n: (n, cq)),
            pl.BlockSpec((BLOCK, KV_WIDTH), lambda n: (n, ckv)),
            pl.BlockSpec((BLOCK, KV_WIDTH), lambda n: (jnp.maximum(n - 1, 0), ckv)),
            _full((N_Q_HEADS, BLOCK, 2 * BLOCK)),
            pl.BlockSpec(memory_space=pltpu.SMEM)]


def attn_fwd(z, bias, sinks, side=None):
    S = z.shape[0]

    def body(q_ref, kvc_ref, kvp_ref, b_ref, sink_ref, o_ref):
        valid = _attn_valid(pl.program_id(0) == 0)
        q = q_ref[...]
        kvc = kvc_ref[...]
        kvp = kvp_ref[...]
        outs = []
        for hk in range(N_KV_HEADS):
            kcat, vcat = _kv_heads(kvp, kvc, hk)
            for g in range(GQA_GROUP):
                h = hk * GQA_GROUP + g
                w, _ = _attn_weights(q[:, h * HEAD_DIM:(h + 1) * HEAD_DIM], kcat, b_ref[h], valid, sink_ref[h])
                outs.append(_dot(w.astype(BF16), vcat))
        o_ref[...] = jnp.concatenate(outs, axis=1).astype(BF16)

    outs, side_outs = _call(
        body, name="attn_fwd", steps=S // BLOCK, in_specs=_attn_specs(),
        out_specs=[pl.BlockSpec((BLOCK, BRANCH), lambda n: (n, 0))],
        out_shape=[jax.ShapeDtypeStruct((S, BRANCH), BF16)], scratch=[], args=(z, z, z, bias, sinks), side=side)
    return outs[0], side_outs


def attn_bwd(do, z, bias, sinks):
    S = z.shape[0]

    def body(do_ref, q_ref, kvc_ref, kvp_ref, b_ref, sink_ref, dq_ref, dc_ref, dp_ref, db_ref, ds_ref):
        first = pl.program_id(0) == 0

        @pl.when(first)
        def _():
            db_ref[...] = jnp.zeros_like(db_ref)
            ds_ref[...] = jnp.zeros_like(ds_ref)

        valid = _attn_valid(first)
        q = q_ref[...]
        kvc = kvc_ref[...]
        kvp = kvp_ref[...]
        dov = do_ref[...]
        dqs, dks, dvs = [], [], []
        for hk in range(N_KV_HEADS):
            kcat, vcat = _kv_heads(kvp, kvc, hk)
            dk = jnp.zeros((2 * BLOCK, HEAD_DIM), F32)
            dv = jnp.zeros((2 * BLOCK, HEAD_DIM), F32)
            for g in range(GQA_GROUP):
                h = hk * GQA_GROUP + g
                qh = q[:, h * HEAD_DIM:(h + 1) * HEAD_DIM]
                doh = dov[:, h * HEAD_DIM:(h + 1) * HEAD_DIM]
                w, wsink = _attn_weights(qh, kcat, b_ref[h], valid, sink_ref[h])
                dv += _dot_tn(w.astype(BF16), doh)
                dw = _dot_nt(doh, vcat)
                delta = jnp.sum(w * dw, axis=-1, keepdims=True)
                ds = w * (dw - delta)
                db_ref[h] += ds
                tot = jnp.sum(-wsink * delta, axis=0, keepdims=True)
                ds_ref[h:h + 1, :] += jnp.broadcast_to(tot, (1, BLOCK))
                dsb = (ds * ATTN_SCALE).astype(BF16)
                dqs.append(_dot(dsb, kcat))
                dk += _dot_tn(dsb, qh)
            dks.append(dk)
            dvs.append(dv)
        dq_ref[...] = jnp.concatenate(dqs, axis=1).astype(BF16)
        both = jnp.concatenate(dks + dvs, axis=1)
        dp_ref[...] = both[:BLOCK]
        dc_ref[...] = both[BLOCK:]

    blk = pl.BlockSpec((BLOCK, BRANCH), lambda n: (n, 0))
    kvb = pl.BlockSpec((BLOCK, KV_WIDTH), lambda n: (n, 0))
    return pl.pallas_call(
        body, name="attn_bwd", grid=(S // BLOCK,), in_specs=[blk] + _attn_specs(),
        out_specs=[blk, kvb, kvb, _full((N_Q_HEADS, BLOCK, 2 * BLOCK)), _full((N_Q_HEADS, BLOCK))],
        out_shape=[jax.ShapeDtypeStruct((S, BRANCH), BF16), jax.ShapeDtypeStruct((S, KV_WIDTH), F32),
                   jax.ShapeDtypeStruct((S, KV_WIDTH), F32), jax.ShapeDtypeStruct((N_Q_HEADS, BLOCK, 2 * BLOCK), F32),
                   jax.ShapeDtypeStruct((N_Q_HEADS, BLOCK), F32)],
        compiler_params=_cparams("arbitrary"),
    )(do, z, z, z, bias, sinks)


def kv_shift_add(dcur, dprev):
    S = dcur.shape[0]
    nt = S // TM
    per_tile = TM // BLOCK

    def body(c_ref, p_ref, n_ref, o_ref):
        nxt = jnp.where(pl.program_id(0) == nt - 1, 0.0, n_ref[...])
        o_ref[...] = (c_ref[...] + jnp.concatenate([p_ref[BLOCK:, :], nxt], axis=0)).astype(BF16)

    tile = pl.BlockSpec((TM, KV_WIDTH), lambda i: (i, 0))
    return pl.pallas_call(
        body, name="kv_shift_add", grid=(nt,),
        in_specs=[tile, tile,
                  pl.BlockSpec((BLOCK, KV_WIDTH), lambda i: (jnp.minimum((i + 1) * per_tile, S // BLOCK - 1), 0))],
        out_specs=tile, out_shape=jax.ShapeDtypeStruct((S, KV_WIDTH), BF16), compiler_params=_cparams("parallel"),
    )(dcur, dprev, dprev)


def _ssm_disc(lam_re, lam_im, log_dt, bt_re, bt_im):
    dt = jnp.exp(log_dt)
    mag = jnp.exp(lam_re * dt)
    ang = lam_im * dt
    a_re = mag * jnp.cos(ang)
    a_im = mag * jnp.sin(ang)
    den = lam_re * lam_re + lam_im * lam_im
    nr = a_re - 1.0
    coef_re = (nr * lam_re + a_im * lam_im) / den
    coef_im = (a_im * lam_re - nr * lam_im) / den
    bb_re = coef_re[:, None, :] * bt_re - coef_im[:, None, :] * bt_im
    bb_im = coef_re[:, None, :] * bt_im + coef_im[:, None, :] * bt_re
    return a_re, a_im, bb_re, bb_im


_GN = (SSM_GROUPS, SSM_STATE)
_GPN = (SSM_GROUPS, SSM_GROUP, SSM_STATE)


def ssm_disc_fwd(lam_re, lam_im, log_dt, bt_re, bt_im):
    def body(lr_ref, li_ref, dt_ref, br_ref, bi_ref, ar_ref, ai_ref, bbr_ref, bbi_ref):
        a_re, a_im, bb_re, bb_im = _ssm_disc(lr_ref[...], li_ref[...], dt_ref[...], br_ref[...], bi_ref[...])
        ar_ref[...] = a_re
        ai_ref[...] = a_im
        bbr_ref[...] = bb_re
        bbi_ref[...] = bb_im

    return pl.pallas_call(
        body, name="ssm_disc_fwd", grid=(1,),
        in_specs=[_full(_GN), _full(_GN), _full((SSM_GROUPS, 1)), _full(_GPN), _full(_GPN)],
        out_specs=[_full(_GN), _full(_GN), _full(_GPN), _full(_GPN)],
        out_shape=[jax.ShapeDtypeStruct(s, F32) for s in (_GN, _GN, _GPN, _GPN)],
        compiler_params=_cparams("arbitrary"),
    )(lam_re, lam_im, log_dt, bt_re, bt_im)


def ssm_disc_bwd(lam_re, lam_im, log_dt, bt_re, bt_im, da_re, da_im, dbb_re, dbb_im):
    def body(lr_ref, li_ref, dt_ref, br_ref, bi_ref, dar_ref, dai_ref, dbr_ref, dbi_ref, o_lr, o_li, o_dt, o_br, o_bi):
        prim = (lr_ref[...], li_ref[...], dt_ref[...], br_ref[...], bi_ref[...])
        _, vjp = jax.vjp(_ssm_disc, *prim)
        grads = vjp((dar_ref[...], dai_ref[...], dbr_ref[...], dbi_ref[...]))
        for r, v in zip((o_lr, o_li, o_dt, o_br, o_bi), grads):
            r[...] = v

    shapes = (_GN, _GN, (SSM_GROUPS, 1), _GPN, _GPN)
    return pl.pallas_call(
        body, name="ssm_disc_bwd", grid=(1,),
        in_specs=[_full(s) for s in shapes + (_GN, _GN, _GPN, _GPN)],
        out_specs=[_full(s) for s in shapes], out_shape=[jax.ShapeDtypeStruct(s, F32) for s in shapes],
        compiler_params=_cparams("arbitrary"),
    )(lam_re, lam_im, log_dt, bt_re, bt_im, da_re, da_im, dbb_re, dbb_im)


LANE_GROUPS = SSM_LANES // 128
SUB_GROUPS = SUB_ST // 128
_TM_SHAPE = (LANE_GROUPS, 128)


def _step_rows(t):
    return pl.ds(pl.multiple_of(t * LANE_GROUPS, LANE_GROUPS), LANE_GROUPS)


def _group_rows(j):
    return pl.ds(j, SCAN_T, stride=LANE_GROUPS)


def _store_sub(ref, j, val):
    for k in range(SUB_GROUPS):
        ref[_group_rows(j * SUB_GROUPS + k), :] = val[:, k * 128:(k + 1) * 128]


def _load_sub(ref, j):
    return jnp.concatenate([ref[_group_rows(j * SUB_GROUPS + k), :] for k in range(SUB_GROUPS)], axis=1)


_SUB_SHAPE_IN = (SSM_SUB, SUB_IN, SUB_ST)
_SUB_SHAPE_OUT = (SSM_SUB, SUB_ST, SUB_IN)


def ssm_fwd(z, bb_re, bb_im, ct_re, ct_im, a_re, a_im, d_skip, wglu, side=None):
    S = z.shape[0]
    cu = COL_U // BRANCH

    def body(u_ref, bbr_ref, bbi_ref, ctr_ref, cti_ref, ar_ref, ai_ref, d_ref, wg_ref,
             y_ref, ypre_ref, hr_ref, hi_ref, hrow_r, hrow_i, bur, bui, car_r, car_i):
        @pl.when(pl.program_id(0) == 0)
        def _():
            car_r[...] = jnp.zeros_like(car_r)
            car_i[...] = jnp.zeros_like(car_i)

        u = u_ref[...]
        for j in range(SSM_SUB):
            uj = u[:, j * SUB_IN:(j + 1) * SUB_IN]
            _store_sub(bur, j, _dot(uj, bbr_ref[j]))
            _store_sub(bui, j, _dot(uj, bbi_ref[j]))
        ar = ar_ref[...]
        ai = ai_ref[...]

        def step(t, carry):
            hr, hi = carry
            rows = _step_rows(t)
            nhr = ar * hr - ai * hi + bur[rows, :]
            nhi = ar * hi + ai * hr + bui[rows, :]
            hr_ref[rows, :] = nhr
            hi_ref[rows, :] = nhi
            return nhr, nhi

        hr, hi = lax.fori_loop(0, SCAN_T, step, (car_r[...], car_i[...]), unroll=8)
        car_r[...] = hr
        car_i[...] = hi
        ys = []
        for j in range(SSM_SUB):
            cs = slice(j * SUB_ST, (j + 1) * SUB_ST)
            hrow_r[:, cs] = _load_sub(hr_ref, j).astype(BF16)
            hrow_i[:, cs] = _load_sub(hi_ref, j).astype(BF16)
            ys.append(_dot(hrow_r[:, cs], ctr_ref[j]) - _dot(hrow_i[:, cs], cti_ref[j]))
        ypre = jnp.concatenate(ys, axis=1) + d_ref[...] * u.astype(F32)
        ypre_ref[...] = ypre
        g = jax.nn.gelu(ypre)
        y_ref[...] = (g * jax.nn.sigmoid(_dot(g.astype(BF16), wg_ref[...]))).astype(BF16)

    row = pl.BlockSpec((SCAN_T, BRANCH), lambda i: (i, 0))
    st = pl.BlockSpec((SCAN_T * LANE_GROUPS, 128), lambda i: (i, 0))
    wide = pl.BlockSpec((SCAN_T, SSM_LANES), lambda i: (i, 0))
    return _call(
        body, name="ssm_fwd", steps=S // SCAN_T,
        in_specs=[pl.BlockSpec((SCAN_T, BRANCH), lambda i: (i, cu)), _full(_SUB_SHAPE_IN), _full(_SUB_SHAPE_IN),
                  _full(_SUB_SHAPE_OUT), _full(_SUB_SHAPE_OUT), _full(_TM_SHAPE), _full(_TM_SHAPE), _full((1, BRANCH)),
                  _full((BRANCH, BRANCH))],
        out_specs=[row, row, st, st, wide, wide],
        out_shape=[jax.ShapeDtypeStruct((S, BRANCH), BF16), jax.ShapeDtypeStruct((S, BRANCH), F32),
                   jax.ShapeDtypeStruct((S * LANE_GROUPS, 128), F32), jax.ShapeDtypeStruct((S * LANE_GROUPS, 128), F32),
                   jax.ShapeDtypeStruct((S, SSM_LANES), BF16), jax.ShapeDtypeStruct((S, SSM_LANES), BF16)],
        scratch=[pltpu.VMEM((SCAN_T * LANE_GROUPS, 128), F32), pltpu.VMEM((SCAN_T * LANE_GROUPS, 128), F32),
                 pltpu.VMEM(_TM_SHAPE, F32), pltpu.VMEM(_TM_SHAPE, F32)],
        args=(z, bb_re, bb_im, ct_re, ct_im, a_re, a_im, d_skip, wglu), side=side)


def ssm_bwd(dy, z, ypre, h_re, h_im, hrow_re, hrow_im, bbt_re, bbt_im, c_re, c_im, a_re, a_im, d_skip, wglu, wglu_t,
            side=None):
    S = z.shape[0]
    nt = S // SCAN_T
    cu = COL_U // BRANCH

    def body(dy_ref, u_ref, ypre_ref, hr_ref, hi_ref, hpr_ref, hpi_ref, hrow_r, hrow_i, bbr_ref, bbi_ref, cr_ref, ci_ref,
             ar_ref, ai_ref, d_ref, wg_ref, wgt_ref,
             du_ref, dbbr_ref, dbbi_ref, dctr_ref, dcti_ref, dar_ref, dai_ref, dd_ref, dwg_ref,
             lr_scr, li_scr, car_r, car_i):
        step = pl.program_id(0)

        @pl.when(step == 0)
        def _():
            for r in (dbbr_ref, dbbi_ref, dctr_ref, dcti_ref, dar_ref, dai_ref, dd_ref, dwg_ref, car_r, car_i):
                r[...] = jnp.zeros_like(r)

        u = u_ref[...]
        uf = u.astype(F32)
        dyv = dy_ref[...].astype(F32)
        g, gelu_vjp = jax.vjp(jax.nn.gelu, ypre_ref[...])
        gb = g.astype(BF16)
        sg = jax.nn.sigmoid(_dot(gb, wg_ref[...]))
        dgl = (dyv * g * sg * (1.0 - sg)).astype(BF16)
        dwg_ref[...] += _dot_tn(gb, dgl)
        dg = dyv * sg + _dot(dgl, wgt_ref[...])
        dypre = gelu_vjp(dg)[0]
        dd_ref[...] += jnp.sum(dypre * uf, axis=0, keepdims=True)
        dyb = dypre.astype(BF16)
        for j in range(SSM_SUB):
            dyj = dyb[:, j * SUB_IN:(j + 1) * SUB_IN]
            _store_sub(lr_scr, j, _dot(dyj, cr_ref[j]))
            _store_sub(li_scr, j, -_dot(dyj, ci_ref[j]))
            cs = slice(j * SUB_ST, (j + 1) * SUB_ST)
            dctr_ref[j] += _dot_tn(hrow_r[:, cs], dyj)
            dcti_ref[j] -= _dot_tn(hrow_i[:, cs], dyj)

        ar = ar_ref[...]
        ai = ai_ref[...]

        def adjoint(lr, li, rows):
            nlr = ar * lr + ai * li + lr_scr[rows, :]
            nli = ar * li - ai * lr + li_scr[rows, :]
            lr_scr[rows, :] = nlr
            li_scr[rows, :] = nli
            return nlr, nli

        def back(k, carry):
            lr, li, acc_r, acc_i = carry
            t = SCAN_T - 1 - k
            lr, li = adjoint(lr, li, _step_rows(t))
            hpr = hr_ref[_step_rows(t - 1), :]
            hpi = hi_ref[_step_rows(t - 1), :]
            return lr, li, acc_r + lr * hpr + li * hpi, acc_i + li * hpr - lr * hpi

        zero = jnp.zeros(_TM_SHAPE, F32)
        lr, li, acc_r, acc_i = lax.fori_loop(0, SCAN_T - 1, back, (car_r[...], car_i[...], zero, zero), unroll=8)
        lr, li = adjoint(lr, li, pl.ds(0, LANE_GROUPS))
        car_r[...] = lr
        car_i[...] = li
        first_tile = step == nt - 1
        hpr = jnp.where(first_tile, 0.0, hpr_ref[...])
        hpi = jnp.where(first_tile, 0.0, hpi_ref[...])
        dar_ref[...] += acc_r + lr * hpr + li * hpi
        dai_ref[...] += acc_i + li * hpr - lr * hpi

        dus = []
        for j in range(SSM_SUB):
            lrb = _load_sub(lr_scr, j).astype(BF16)
            lib = _load_sub(li_scr, j).astype(BF16)
            uj = u[:, j * SUB_IN:(j + 1) * SUB_IN]
            dus.append(_dot(lrb, bbr_ref[j]) + _dot(lib, bbi_ref[j]))
            dbbr_ref[j] += _dot_tn(uj, lrb)
            dbbi_ref[j] += _dot_tn(uj, lib)
        du_ref[...] = (jnp.concatenate(dus, axis=1) + dypre * d_ref[...]).astype(BF16)

    def rev(i):
        return nt - 1 - i

    row = pl.BlockSpec((SCAN_T, BRANCH), lambda i: (rev(i), 0))
    st = pl.BlockSpec((SCAN_T * LANE_GROUPS, 128), lambda i: (rev(i), 0))
    before = pl.BlockSpec(_TM_SHAPE, lambda i: (jnp.maximum(rev(i) * SCAN_T - 1, 0), 0))
    wide = pl.BlockSpec((SCAN_T, SSM_LANES), lambda i: (rev(i), 0))
    tm = _full(_TM_SHAPE)
    return _call(
        body, name="ssm_bwd", steps=nt,
        in_specs=[row, pl.BlockSpec((SCAN_T, BRANCH), lambda i: (rev(i), cu)), row, st, st, before, before, wide, wide,
                  _full(_SUB_SHAPE_OUT), _full(_SUB_SHAPE_OUT), _full(_SUB_SHAPE_IN), _full(_SUB_SHAPE_IN),
                  tm, tm, _full((1, BRANCH)), _full((BRANCH, BRANCH)), _full((BRANCH, BRANCH))],
        out_specs=[row, _full(_SUB_SHAPE_IN), _full(_SUB_SHAPE_IN), _full(_SUB_SHAPE_OUT), _full(_SUB_SHAPE_OUT),
                   tm, tm, _full((1, BRANCH)), _full((BRANCH, BRANCH))],
        out_shape=[jax.ShapeDtypeStruct((S, BRANCH), BF16), jax.ShapeDtypeStruct(_SUB_SHAPE_IN, F32),
                   jax.ShapeDtypeStruct(_SUB_SHAPE_IN, F32), jax.ShapeDtypeStruct(_SUB_SHAPE_OUT, F32),
                   jax.ShapeDtypeStruct(_SUB_SHAPE_OUT, F32), jax.ShapeDtypeStruct(_TM_SHAPE, F32),
                   jax.ShapeDtypeStruct(_TM_SHAPE, F32), jax.ShapeDtypeStruct((1, BRANCH), F32),
                   jax.ShapeDtypeStruct((BRANCH, BRANCH), F32)],
        scratch=[pltpu.VMEM((SCAN_T * LANE_GROUPS, 128), F32), pltpu.VMEM((SCAN_T * LANE_GROUPS, 128), F32),
                 pltpu.VMEM(_TM_SHAPE, F32), pltpu.VMEM(_TM_SHAPE, F32)],
        args=(dy, z, ypre, h_re, h_im, h_re, h_im, hrow_re, hrow_im, bbt_re, bbt_im, c_re, c_im, a_re, a_im, d_skip, wglu,
              wglu_t),
        side=side)


def _blockdiag(x):
    gs = SSM_GROUPS // SSM_SUB
    x = x.reshape(SSM_SUB, gs, SSM_GROUP, SSM_STATE)
    eye = jnp.eye(gs, dtype=x.dtype)
    return (x[:, :, :, None, :] * eye[None, :, None, :, None]).reshape(SSM_SUB, SUB_IN, SUB_ST)


def _blockdiag_extract(x):
    gs = SSM_GROUPS // SSM_SUB
    x = x.reshape(SSM_SUB, gs, SSM_GROUP, gs, SSM_STATE)
    eye = jnp.eye(gs, dtype=x.dtype)
    return jnp.sum(x * eye[None, :, None, :, None], axis=3).reshape(SSM_GROUPS, SSM_GROUP, SSM_STATE)


def loss_head(x, g, target):
    S = x.shape[0]

    def body(x_ref, g_ref, t_ref, dx_ref, loss_ref, dg_ref):
        @pl.when(pl.program_id(0) == 0)
        def _():
            loss_ref[...] = jnp.zeros_like(loss_ref)
            dg_ref[...] = jnp.zeros_like(dg_ref)

        xv = x_ref[...]
        gv = g_ref[...]
        r = lax.rsqrt(jnp.mean(xv * xv, axis=-1, keepdims=True) + RMS_EPS)
        xhat = xv * r
        err = xhat * gv - t_ref[...]
        loss_ref[...] += jnp.sum((err * err).reshape(TM // 8, 8, D_MODEL), axis=0) * (0.5 / D_MODEL)
        dy = err * (1.0 / D_MODEL)
        dxhat = dy * gv
        dx_ref[...] = r * (dxhat - xhat * jnp.mean(dxhat * xhat, axis=-1, keepdims=True))
        dg_ref[...] += jnp.sum((dy * xhat).reshape(TM // 8, 8, D_MODEL), axis=0)

    row = pl.BlockSpec((TM, D_MODEL), lambda i: (i, 0))
    acc = _full((8, D_MODEL))
    return pl.pallas_call(
        body, name="loss_head", grid=(S // TM,), in_specs=[row, _full((1, D_MODEL)), row],
        out_specs=[row, acc, acc],
        out_shape=[jax.ShapeDtypeStruct((S, D_MODEL), F32), jax.ShapeDtypeStruct((8, D_MODEL), F32),
                   jax.ShapeDtypeStruct((8, D_MODEL), F32)],
        compiler_params=_cparams("arbitrary"),
    )(x, g, target)


def _x_spec():
    return pl.BlockSpec((TM, D_MODEL), lambda i: (i, 0))


def _g_spec():
    return pl.BlockSpec((1, D_MODEL), lambda i: (0, 0))


def _norm_prologue(x, g):
    h = _rms(x, g).astype(BF16)
    return h, h


def _cast_prologue(x):
    return (x.astype(BF16),)


def _swiglu_prologue(h1, h2):
    a = h1.astype(F32)
    act = (a * jax.nn.sigmoid(a) * h2.astype(F32)).astype(BF16)
    return act, act


def _ssm_consts(lw):
    a_re, a_im, bbt_re, bbt_im = ssm_disc_fwd(
        lw["ssm_lambda_re"], lw["ssm_lambda_im"], lw["ssm_log_dt"].reshape(SSM_GROUPS, 1), lw["bt_re"], lw["bt_im"])
    bb_re = _blockdiag(bbt_re).astype(BF16)
    bb_im = _blockdiag(bbt_im).astype(BF16)
    c_re = _blockdiag(lw["ssm_c_re"]).astype(BF16)
    c_im = _blockdiag(lw["ssm_c_im"]).astype(BF16)
    return dict(
        a_re=a_re.reshape(_TM_SHAPE), a_im=a_im.reshape(_TM_SHAPE),
        bb_re=bb_re, bb_im=bb_im, bbt_re=jnp.swapaxes(bb_re, 1, 2), bbt_im=jnp.swapaxes(bb_im, 1, 2),
        c_re=c_re, c_im=c_im, ct_re=jnp.swapaxes(c_re, 1, 2), ct_im=jnp.swapaxes(c_im, 1, 2))


def layer_fwd(x, lw, bias, next_shards=None, place=None):
    sides = (None, None) if next_shards is None else (("gather_chips", next_shards[:GATHER_SPLIT]),
                                                      ("gather_chips", next_shards[GATHER_SPLIT:]))
    out = fused_mm("in_proj", [x, lw["norm_mix"]], [_x_spec(), _g_spec()], _norm_prologue, lw["w_in"], tn=2944,
                   out_dtype=BF16, extras=((D_MODEL, BF16),), side=sides[0])
    (z, h), g4a = (out, []) if next_shards is None else out
    sc = _ssm_consts(lw)
    (y_ssm, ypre, h_re, h_im, hrow_re, hrow_im), g4b = ssm_fwd(
        z, sc["bb_re"], sc["bb_im"], sc["ct_re"], sc["ct_im"], sc["a_re"], sc["a_im"], lw["ssm_d"], lw["ssm_w_glu"],
        side=sides[1])
    y_conv = conv_fwd(z, lw["conv_w"])
    if next_shards is not None:
        g4 = [_put_slot(g, b, place[0]) for g, b in zip(list(g4a) + list(g4b), next_shards)]
    y_attn, g8 = attn_fwd(z, bias, lw["attn_sinks"], side=None if next_shards is None else ("gather_cores", g4))
    next_gathered = None if next_shards is None else [_put_slot(g, b, place[1]) for g, b in zip(g8, g4)]
    merged = merge_fwd(z, y_ssm, y_conv, y_attn, lw["w_branch"])
    x1 = fused_mm("out_proj", [merged], [_x_spec()], _cast_prologue, lw["w_out"], tn=1024, out_dtype=F32, res=x)
    hf, hn1 = fused_mm("ffn_in", [x1, lw["norm_ffn"]], [_x_spec(), _g_spec()], _norm_prologue, lw["w_ffn_in"], tn=2816,
                       out_dtype=BF16, extras=((D_MODEL, BF16),))
    x2, act = fused_mm("ffn_out", [hf, hf], [_row_spec(FFN_HIDDEN, 0), _row_spec(FFN_HIDDEN, 1)], _swiglu_prologue,
                       lw["w_ffn_out"], tn=1024, out_dtype=F32, res=x1, extras=((FFN_HIDDEN, BF16),))
    a_pre, hn2 = fused_mm("ple_gate", [x2, lw["norm_ple"]], [_x_spec(), _g_spec()], _norm_prologue, lw["w_ple_gate"],
                          tn=1024, out_dtype=BF16, extras=((D_MODEL, BF16),))
    pp = fused_mm("ple_proj", [lw["p"]], [_row_spec(PLE_DIM)], _cast_prologue, lw["w_ple_proj"], tn=1024, out_dtype=BF16)
    x3 = ple_combine(x2, a_pre, pp)
    res = dict(x=x, z=z, h=h, y_ssm=y_ssm, ypre=ypre, h_re=h_re, h_im=h_im, hrow_re=hrow_re, hrow_im=hrow_im, y_conv=y_conv, y_attn=y_attn, merged=merged,
               x1=x1, hf=hf, hn1=hn1, act=act, x2=x2, a_pre=a_pre, hn2=hn2, pp=pp)
    return x3, res, next_gathered


def _pair_sums(split, from_sibling):
    return [pair_sum("pair_sum_" + n, a.reshape(2, -1, a.shape[-1]), b.reshape(-1, b.shape[-1])).reshape(b.shape)
            for n, a, b in zip(SHARDED, split, from_sibling)]


def layer_bwd(dx3, lw, res, bias, pending=None):
    g = {}
    wide = ((D_MODEL, BF16), (D_MODEL, BF16))
    dx2, g["norm_ple"], da, dpp = mm_norm_bwd(
        "d_ple_gate", [dx3, res["a_pre"], res["pp"]], [_row_spec(D_MODEL)] * 3, lw["w_ple_gate"], res["x2"],
        lw["norm_ple"], dx3, tm=TM, pre=_ple_bwd_pre, extras=wide)
    g["w_ple_proj"] = mm_tn("d_w_ple_proj", lw["p"], dpp)
    g["w_ple_gate"] = mm_tn("d_w_ple_gate", res["hn2"], da)
    sums = None
    if pending is None:
        dact = fused_mm("d_ffn_out", [dx2], [_x_spec()], _cast_prologue, lw["w_ffn_out"], tn=1408, out_dtype=BF16,
                        nt=True)
    else:
        dact, from_sibling = fused_mm("d_ffn_out", [dx2], [_x_spec()], _cast_prologue, lw["w_ffn_out"], tn=1408,
                                      out_dtype=BF16, side=("scatter_cores", pending), nt=True)
        sums = _pair_sums(pending, from_sibling)
    g["w_ffn_out"] = mm_tn("d_w_ffn_out", res["act"], dx2)
    half = TM // 2
    dx1, g["norm_ffn"], dh1, dh2 = mm_norm_bwd(
        "d_ffn_in", [dact, res["hf"], res["hf"]],
        [_row_spec(FFN_HIDDEN, 0, half), _row_spec(FFN_HIDDEN, 0, half), _row_spec(FFN_HIDDEN, 1, half)],
        lw["w_ffn_in"], res["x1"], lw["norm_ffn"], dx2, tm=half, pre=_swiglu_bwd_pre,
        extras=((FFN_HIDDEN, BF16), (FFN_HIDDEN, BF16)))
    g["w_ffn_in"] = jnp.concatenate([mm_tn("d_w_ffn_in_a", res["hn1"], dh1), mm_tn("d_w_ffn_in_b", res["hn1"], dh2)],
                                    axis=1)
    dmerged = fused_mm("d_out_proj", [dx1], [_x_spec()], _cast_prologue, lw["w_out"], tn=1024, out_dtype=BF16, nt=True)
    g["w_out"] = mm_tn("d_w_out", res["merged"], dx1)
    z = res["z"]
    ys = (res["y_ssm"], res["y_conv"], res["y_attn"])
    dgates, dbs = merge_bwd(dmerged, z, *ys, lw["w_branch"])
    dys, dwb = [], []
    for r in range(3):
        dys.append(fused_mm(f"d_branch_{r}", [dbs[r]], [_x_spec()], _cast_prologue, lw["w_branch"][r], tn=1024,
                            out_dtype=BF16, nt=True))
        dwb.append(mm_tn(f"d_w_branch_{r}", ys[r], dbs[r]))
    g["w_branch"] = jnp.stack(dwb)
    sc = _ssm_consts(lw)
    (du, dbb_re, dbb_im, dct_re, dct_im, da_re, da_im, g["ssm_d"], g["ssm_w_glu"]), received = ssm_bwd(
        dys[0], z, res["ypre"], res["h_re"], res["h_im"], res["hrow_re"], res["hrow_im"], sc["bbt_re"], sc["bbt_im"], sc["c_re"], sc["c_im"],
        sc["a_re"], sc["a_im"], lw["ssm_d"], lw["ssm_w_glu"], lw["ssm_w_glu_t"],
        side=None if pending is None else ("scatter_chips", sums))
    g["ssm_c_re"] = _blockdiag_extract(jnp.swapaxes(dct_re, 1, 2))
    g["ssm_c_im"] = _blockdiag_extract(jnp.swapaxes(dct_im, 1, 2))
    (g["ssm_lambda_re"], g["ssm_lambda_im"], dlog_dt, g["bt_re"], g["bt_im"]) = ssm_disc_bwd(
        lw["ssm_lambda_re"], lw["ssm_lambda_im"], lw["ssm_log_dt"].reshape(SSM_GROUPS, 1), lw["bt_re"], lw["bt_im"],
        da_re.reshape(_GN), da_im.reshape(_GN),
        _blockdiag_extract(dbb_re), _blockdiag_extract(dbb_im))
    g["ssm_log_dt"] = dlog_dt.reshape(SSM_GROUPS)
    dconv, g["conv_w"] = conv_bwd(dys[1], z, lw["conv_w"])
    dq, dkv_cur, dkv_prev, g["dbias"], g["attn_sinks"] = attn_bwd(dys[2], z, bias, lw["attn_sinks"])
    dkv = kv_shift_add(dkv_cur, dkv_prev)
    pieces = [dgates[0], dgates[1], dgates[2], du, dconv, dq, dkv]
    g["w_in"] = jnp.concatenate([mm_tn(f"d_w_in_{k}", res["h"], pc) for k, pc in enumerate(pieces)], axis=1)
    dx0, g["norm_mix"] = mm_norm_bwd("d_in_proj", pieces, [_row_spec(pc.shape[1], tm=half) for pc in pieces], lw["w_in"],
                                     res["x"], lw["norm_mix"], dx1, tm=half)
    return dx0, g, (sums, received)


def adamw(name, parts, w, m, v):
    n, R, C = parts.shape
    tr = _pick(R, (512, 256, 128, 64, 32, 16, 8))

    def body(p_ref, w_ref, m_ref, v_ref, g_ref, d_ref, nm_ref, nv_ref):
        gsum = p_ref[0].astype(F32)
        for k in range(1, n):
            gsum = gsum + p_ref[k].astype(F32)
        mn = ADAM_B1 * m_ref[...] + (1.0 - ADAM_B1) * gsum
        vn = ADAM_B2 * v_ref[...] + (1.0 - ADAM_B2) * jnp.square(gsum)
        m_hat = mn / (1.0 - ADAM_B1 ** ADAM_STEP)
        v_hat = vn / (1.0 - ADAM_B2 ** ADAM_STEP)
        g_ref[...] = gsum
        d_ref[...] = -ADAM_LR * (m_hat / (jnp.sqrt(v_hat) + ADAM_EPS) + ADAM_WD * w_ref[...])
        nm_ref[...] = mn
        nv_ref[...] = vn

    blk = pl.BlockSpec((tr, C), lambda i: (i, 0))
    return pl.pallas_call(
        body, name=name, grid=(R // tr,), in_specs=[pl.BlockSpec((n, tr, C), lambda i: (0, i, 0)), blk, blk, blk],
        out_specs=[blk] * 4, out_shape=[jax.ShapeDtypeStruct((R, C), F32)] * 4, compiler_params=_cparams("parallel"),
    )(parts, w, m, v)


_ANY = pl.BlockSpec(memory_space=pl.ANY)


def _coords():
    return lax.axis_index("x"), lax.axis_index("y"), lax.axis_index("c")


def _chip_peers(x, y):
    return [(1 - x, y), (x, 1 - y), (1 - x, 1 - y)]


def _gather_chips_copies(x_refs, out_refs, send_sems, recv_sems):
    x, y, c = _coords()
    me = 2 * x + y
    peers = _chip_peers(x, y)

    def copy(i, k, slot):
        return pltpu.make_async_remote_copy(
            src_ref=x_refs[i], dst_ref=out_refs[i].at[slot], send_sem=send_sems.at[3 * i + k],
            recv_sem=recv_sems.at[3 * i + k], device_id=(*peers[k], c), device_id_type=MESH)

    n = len(x_refs)
    sends = [copy(i, k, me) for i in range(n) for k in range(3)]
    recvs = [copy(i, k, 2 * px + py) for i in range(n) for k, (px, py) in enumerate(peers)]
    return sends, recvs


def _gather_cores_copies(x_refs, out_refs, send_sems, recv_sems):
    x, y, c = _coords()

    def copy(i, slot):
        return pltpu.make_async_remote_copy(
            src_ref=x_refs[i], dst_ref=out_refs[i].at[slot], send_sem=send_sems.at[i], recv_sem=recv_sems.at[i],
            device_id=(x, y, 1 - c), device_id_type=MESH)

    n = len(x_refs)
    return [copy(i, c) for i in range(n)], [copy(i, 1 - c) for i in range(n)]


def _scatter_cores_copies(x_refs, out_refs, send_sems, recv_sems):
    x, y, c = _coords()
    copies = [pltpu.make_async_remote_copy(
        src_ref=x_refs[i].at[1 - c], dst_ref=out_refs[i], send_sem=send_sems.at[i], recv_sem=recv_sems.at[i],
        device_id=(x, y, 1 - c), device_id_type=MESH) for i in range(len(x_refs))]
    return copies, copies


def _scatter_chips_copies(x_refs, out_refs, send_sems, recv_sems):
    x, y, c = _coords()
    me = 2 * x + y
    peers = _chip_peers(x, y)

    def copy(i, k, src_slot, dst_slot):
        return pltpu.make_async_remote_copy(
            src_ref=x_refs[i].at[src_slot], dst_ref=out_refs[i].at[dst_slot], send_sem=send_sems.at[3 * i + k],
            recv_sem=recv_sems.at[3 * i + k], device_id=(*peers[k], c), device_id_type=MESH)

    n = len(x_refs)
    sends = [copy(i, k, 2 * px + py, me) for i in range(n) for k, (px, py) in enumerate(peers)]
    recvs = [copy(i, k, me, 2 * px + py) for i in range(n) for k, (px, py) in enumerate(peers)]
    return sends, recvs


_EXCHANGES = {
    "gather_chips": (lambda b: (4,) + b.shape, 3, _gather_chips_copies),
    "gather_cores": (lambda b: (2,) + b.shape, 1, _gather_cores_copies),
    "scatter_cores": (lambda b: b.shape[1:], 1, _scatter_cores_copies),
    "scatter_chips": (lambda b: b.shape, 3, _scatter_chips_copies),
}


def exchange(name, kind, blks):
    out_shape_of, per_block, make_copies = _EXCHANGES[kind]
    n = len(blks)

    def body(*refs):
        sends, recvs = make_copies(refs[:n], refs[n:2 * n], refs[2 * n], refs[2 * n + 1])
        for cp in sends:
            cp.start()
        for cp in recvs:
            cp.wait_recv()
        for cp in sends:
            cp.wait_send()

    return pl.pallas_call(
        body, name=name, in_specs=[_ANY] * n, out_specs=[_ANY] * n,
        out_shape=[jax.ShapeDtypeStruct(out_shape_of(b), b.dtype) for b in blks],
        scratch_shapes=[pltpu.SemaphoreType.DMA((per_block * n,)), pltpu.SemaphoreType.DMA((per_block * n,))],
    )(*blks)


def _put_slot(buf, block, idx):
    return lax.dynamic_update_slice(buf, block[None].astype(buf.dtype), (idx,) + (0,) * block.ndim)


def pair_sum(name, mine, theirs):
    _, R, C = mine.shape
    tr = _pick(R, (1024, 512, 256, 128, 64, 32, 16))
    c_idx = lax.axis_index("c").astype(jnp.int32).reshape(1)

    def body(c_ref, a_ref, b_ref, o_ref):
        o_ref[...] = (a_ref[0].astype(F32) + b_ref[...].astype(F32)).astype(BF16)

    return pl.pallas_call(
        body, name=name,
        grid_spec=pltpu.PrefetchScalarGridSpec(
            num_scalar_prefetch=1, grid=(R // tr,),
            in_specs=[pl.BlockSpec((1, tr, C), lambda i, c: (c[0], i, 0)), pl.BlockSpec((tr, C), lambda i, c: (i, 0))],
            out_specs=pl.BlockSpec((tr, C), lambda i, c: (i, 0))),
        out_shape=jax.ShapeDtypeStruct(theirs.shape, BF16), compiler_params=_cparams("parallel"),
    )(c_idx, mine, theirs)


SHARDED = {
    "w_in": ((D_MODEL, IN_WIDTH), 2), "ssm_w_glu": ((BRANCH, BRANCH), 1), "conv_w": ((3, BRANCH), 2),
    "w_branch": ((3, BRANCH, D_MODEL), 3), "w_out": ((D_MODEL, D_MODEL), 1), "w_ffn_in": ((D_MODEL, 2 * FFN_HIDDEN), 2),
    "w_ffn_out": ((FFN_HIDDEN, D_MODEL), 1), "w_ple_gate": ((D_MODEL, D_MODEL), 1), "w_ple_proj": ((PLE_DIM, D_MODEL), 2),
}
SMALL = ["rel_bias", "norm_mix", "ssm_lambda_re", "ssm_lambda_im", "ssm_b_re", "ssm_b_im", "ssm_c_re", "ssm_c_im", "ssm_d",
         "ssm_log_dt", "attn_sinks", "norm_ffn", "norm_ple", "norm_final"]
GATHER_SPLIT = 4
WEIGHTS = ["rel_bias", "norm_mix", "w_in", "ssm_lambda_re", "ssm_lambda_im", "ssm_b_re", "ssm_b_im", "ssm_c_re", "ssm_c_im",
           "ssm_d", "ssm_log_dt", "ssm_w_glu", "conv_w", "attn_sinks", "w_branch", "w_out", "norm_ffn", "w_ffn_in",
           "w_ffn_out", "norm_ple", "w_ple_gate", "w_ple_proj", "norm_final"]


def _pad_to(flat, n):
    return jnp.pad(flat, [(0, 0)] * (flat.ndim - 1) + [(0, n - flat.shape[-1])])


def _unshard(g8, name):
    axis = SHARDED[name][1] - 1
    shard = g8.shape[2:]
    b = g8.reshape((2, 2, 2) + shard)
    b = jnp.moveaxis(b, (1, 2, 0), (axis, axis + 1, axis + 2))
    full = list(shard)
    full[axis] *= N_DEV
    return b.reshape(full)


def _shard_split(full, name):
    axis = SHARDED[name][1] - 1
    dims = list(full.shape)
    dims[axis:axis + 1] = [2, 2, 2, dims[axis] // N_DEV]
    b = jnp.moveaxis(full.reshape(dims), (axis, axis + 1, axis + 2), (1, 2, 0))
    return b.reshape((2, 4) + b.shape[3:])


def _small_sizes(shapes):
    return [-(-int(np.prod(shapes[n])) // 128) * 128 for n in SMALL]


def pack_small(vals, shapes, extra):
    segs = [_pad_to(vals[n].reshape(-1).astype(F32), s) for n, s in zip(SMALL, _small_sizes(shapes))]
    segs.append(_pad_to(extra.reshape(-1), 128))
    flat = jnp.concatenate(segs)
    rows = -(-flat.shape[0] // (128 * 8)) * 8
    return _pad_to(flat, rows * 128).reshape(rows, 128)


def unpack_small(packed, shapes):
    flat = packed.reshape(-1)
    out, off = {}, 0
    for n, s in zip(SMALL, _small_sizes(shapes)):
        out[n] = flat[off:off + int(np.prod(shapes[n]))].reshape(shapes[n])
        off += s
    return out, flat[off]


def _layer_weights(gathered, small, p, i):
    full = {n: _unshard(g, n) for n, g in zip(SHARDED, gathered)}
    w_in = full["w_in"]
    w_in_p = jnp.concatenate([w_in[:, 2816:], w_in[:, :2560], w_in[:, 2560:2816]], axis=-1)
    return dict(
        w_in=w_in_p,
        ssm_w_glu=full["ssm_w_glu"], ssm_w_glu_t=full["ssm_w_glu"].T,
        conv_w=full["conv_w"],
        w_branch=full["w_branch"], w_out=full["w_out"], w_ffn_in=full["w_ffn_in"], w_ffn_out=full["w_ffn_out"],
        w_ple_gate=full["w_ple_gate"],
        w_ple_proj=full["w_ple_proj"],
        norm_mix=small["norm_mix"][i][None, :], norm_ffn=small["norm_ffn"][i][None, :],
        norm_ple=small["norm_ple"][i][None, :],
        ssm_lambda_re=small["ssm_lambda_re"][i], ssm_lambda_im=small["ssm_lambda_im"][i],
        ssm_log_dt=small["ssm_log_dt"][i],
        bt_re=jnp.swapaxes(small["ssm_b_re"][i], 1, 2), bt_im=jnp.swapaxes(small["ssm_b_im"][i], 1, 2),
        ssm_c_re=small["ssm_c_re"][i], ssm_c_im=small["ssm_c_im"][i], ssm_d=small["ssm_d"][i][None, :],
        attn_sinks=small["attn_sinks"][i], p=p[i],
    )


def matrix_grads(g):
    w_in_g = g["w_in"]
    return dict(
        w_in=jnp.concatenate([w_in_g[:, 3072:5632], w_in_g[:, 5632:], w_in_g[:, :3072]], axis=-1),
        ssm_w_glu=g["ssm_w_glu"], conv_w=jnp.sum(g["conv_w"].reshape(3, 8, BRANCH), axis=1),
        w_branch=g["w_branch"], w_out=g["w_out"], w_ffn_in=g["w_ffn_in"], w_ffn_out=g["w_ffn_out"],
        w_ple_gate=g["w_ple_gate"], w_ple_proj=g["w_ple_proj"])


def small_grads(per_layer, dg_final):
    keys = ("dbias", "norm_mix", "ssm_lambda_re", "ssm_lambda_im", "bt_re", "bt_im", "ssm_c_re", "ssm_c_im", "ssm_d",
            "ssm_log_dt", "attn_sinks", "norm_ffn", "norm_ple")
    g = {k: jnp.stack([gl[k] for gl in per_layer]) for k in keys}
    drel = rel_bias_bwd(g["dbias"].reshape(DEPTH, N_Q_HEADS, BLOCK * 2 * BLOCK)).T
    return dict(
        rel_bias=drel, norm_mix=jnp.sum(g["norm_mix"], axis=1), ssm_lambda_re=g["ssm_lambda_re"],
        ssm_lambda_im=g["ssm_lambda_im"], ssm_b_re=jnp.swapaxes(g["bt_re"], 2, 3), ssm_b_im=jnp.swapaxes(g["bt_im"], 2, 3),
        ssm_c_re=g["ssm_c_re"], ssm_c_im=g["ssm_c_im"], ssm_d=g["ssm_d"][:, 0, :], ssm_log_dt=g["ssm_log_dt"],
        attn_sinks=g["attn_sinks"][:, :, 0], norm_ffn=jnp.sum(g["norm_ffn"], axis=1), norm_ple=jnp.sum(g["norm_ple"], axis=1),
        norm_final=jnp.sum(dg_final, axis=0))


def kernel(x, p, rel_bias, norm_mix, w_in, ssm_lambda_re, ssm_lambda_im, ssm_b_re, ssm_b_im, ssm_c_re, ssm_c_im, ssm_d, ssm_log_dt, ssm_w_glu, conv_w, attn_sinks, w_branch, w_out, norm_ffn, w_ffn_in, w_ffn_out, norm_ple, w_ple_gate, w_ple_proj, norm_final, loss_target, m_rel_bias, m_norm_mix, m_w_in, m_ssm_lambda_re, m_ssm_lambda_im, m_ssm_b_re, m_ssm_b_im, m_ssm_c_re, m_ssm_c_im, m_ssm_d, m_ssm_log_dt, m_ssm_w_glu, m_conv_w, m_attn_sinks, m_w_branch, m_w_out, m_norm_ffn, m_w_ffn_in, m_w_ffn_out, m_norm_ple, m_w_ple_gate, m_w_ple_proj, m_norm_final, v_rel_bias, v_norm_mix, v_w_in, v_ssm_lambda_re, v_ssm_lambda_im, v_ssm_b_re, v_ssm_b_im, v_ssm_c_re, v_ssm_c_im, v_ssm_d, v_ssm_log_dt, v_ssm_w_glu, v_conv_w, v_attn_sinks, v_w_branch, v_w_out, v_norm_ffn, v_w_ffn_in, v_w_ffn_out, v_norm_ple, v_w_ple_gate, v_w_ple_proj, v_norm_final):
    args = dict(locals())
    w = {n: args[n] for n in WEIGHTS}
    m = {n: args["m_" + n] for n in WEIGHTS}
    v = {n: args["v_" + n] for n in WEIGHTS}
    shapes = {n: w[n].shape for n in SMALL}

    x_i, y_i, c_i = _coords()
    chip = 2 * x_i + y_i
    place = (chip, c_i)
    small = {n: w[n] for n in SMALL}

    def all_gather(tag, blks):
        g4 = exchange(f"gather_{tag}_chips", "gather_chips", blks)
        g4 = [_put_slot(g, b, chip) for g, b in zip(g4, blks)]
        g8 = exchange(f"gather_{tag}_cores", "gather_cores", g4)
        return [_put_slot(g, b, c_i) for g, b in zip(g8, g4)]

    def shards_of(layer):
        return [w[n][layer] if n == "conv_w" else w[n][layer].astype(BF16) for n in SHARDED]

    bias = rel_bias_fwd(small["rel_bias"].T).reshape(N_Q_HEADS, BLOCK, 2 * BLOCK)
    xs, layers, res = x[0], [], []
    gathered = all_gather("w", shards_of(0))
    for layer in range(DEPTH):
        layers.append(_layer_weights(gathered, small, p[:, 0], layer))
        nxt = shards_of(layer + 1) if layer + 1 < DEPTH else None
        xs, res_l, gathered = layer_fwd(xs, layers[layer], bias, nxt, place)
        res.append(res_l)
    grad_x, loss_parts, dg_final = loss_head(xs, small["norm_final"][None, :], loss_target[0])

    per_layer, reduced, pending = [None] * DEPTH, [None] * DEPTH, None
    for layer in reversed(range(DEPTH)):
        grad_x, per_layer[layer], done = layer_bwd(grad_x, layers[layer], res[layer], bias, pending)
        if pending is not None:
            reduced[layer + 1] = done
        mg = matrix_grads(per_layer[layer])
        pending = [_shard_split(mg[n], n).astype(BF16) for n in SHARDED]
    sums = _pair_sums(pending, exchange("scatter_g_cores", "scatter_cores", pending))
    reduced[0] = (sums, exchange("scatter_g_chips", "scatter_chips", sums))

    outs = ({}, {}, {}, {})
    for k, name in enumerate(SHARDED):
        parts = jnp.stack([_put_slot(rcv[k], lax.dynamic_index_in_dim(sm[k], chip, 0, keepdims=False), chip)
                           for sm, rcv in reduced], axis=1)
        cols = parts.shape[-1]
        res4 = adamw("adamw_" + name, parts.reshape(4, -1, cols), w[name].reshape(-1, cols), m[name].reshape(-1, cols),
                     v[name].reshape(-1, cols))
        for d, o in zip(outs, res4):
            d[name] = o.reshape(w[name].shape)

    small_local = pack_small(small_grads(per_layer, dg_final), shapes, jnp.sum(loss_parts))
    small_all = all_gather("s", [small_local])[0]
    zero = jnp.zeros((1,), F32)
    res4 = adamw("adamw_small", small_all.reshape(N_DEV, small_local.shape[0], 128),
                 pack_small({n: w[n] for n in SMALL}, shapes, zero), pack_small({n: m[n] for n in SMALL}, shapes, zero),
                 pack_small({n: v[n] for n in SMALL}, shapes, zero))
    loss = None
    for d, r in zip(outs, res4):
        vals, extra = unpack_small(r, shapes)
        d.update(vals)
        if loss is None:
            loss = extra

    return (loss, grad_x[None], *[d[n] for d in outs for n in WEIGHTS])
```

```python
import functools
import math

import numpy as np
import jax
import jax.numpy as jnp
from jax import lax
from jax.experimental import pallas as pl
from jax.experimental.pallas import tpu as pltpu

F32 = jnp.float32
BF16 = jnp.bfloat16
MESH = pl.DeviceIdType.MESH

D_MODEL = 1024
DEPTH = 4
PLE_DIM = 256
BRANCH = 512
SSM_GROUPS = 32
SSM_GROUP = 16
SSM_STATE = 64
SSM_LANES = SSM_GROUPS * SSM_STATE
SSM_SUB = 4
SUB_IN = BRANCH // SSM_SUB
SUB_ST = SSM_LANES // SSM_SUB
HEAD_DIM = 64
N_Q_HEADS = 8
N_KV_HEADS = 2
GQA_GROUP = 4
KV_WIDTH = 2 * N_KV_HEADS * HEAD_DIM
WINDOW = 128
BLOCK = 128
ATTN_SCALE = 1.0 / math.sqrt(HEAD_DIM)
REL_BUCKETS = 32
REL_MAX_DIST = 128
FFN_HIDDEN = 2816
RMS_EPS = 1e-6
IN_WIDTH = 5888
N_DEV = 8

ADAM_LR = 0.001
ADAM_B1 = 0.9
ADAM_B2 = 0.999
ADAM_EPS = 1e-08
ADAM_WD = 0.01
ADAM_STEP = 10

COL_U = 3072
COL_KV = 5632
NEG = -1e30

SCAN_T = 256
TM = 512
TM_W = 1024
VMEM_LIMIT = 52 * 1024 * 1024


def _cparams(*sem):
    return pltpu.CompilerParams(dimension_semantics=sem, vmem_limit_bytes=VMEM_LIMIT)


def _full(shape):
    n = len(shape)
    return pl.BlockSpec(shape, lambda *_: (0,) * n)


def _pick(n, cands):
    for c in cands:
        if n % c == 0:
            return c
    return n


def _call(body, *, name, steps, in_specs, out_specs, out_shape, scratch, args, side=None):
    in_specs, out_specs, out_shape = list(in_specs), list(out_specs), list(out_shape)
    scratch, args = list(scratch), list(args)
    n_in, n_out, n_scr = len(in_specs), len(out_specs), len(scratch)
    n = 0
    if side is not None:
        kind, blks = side
        out_shape_of, per_block, make_copies = _EXCHANGES[kind]
        n = len(blks)
        inner = body

        def body(*refs):
            ins, sx = refs[:n_in], refs[n_in:n_in + n]
            outs, so = refs[n_in + n:n_in + n + n_out], refs[n_in + n + n_out:n_in + 2 * n + n_out]
            scr = refs[n_in + 2 * n + n_out:n_in + 2 * n + n_out + n_scr]
            send_sems, recv_sems = refs[-2:]

            @pl.when(pl.program_id(0) == 0)
            def _():
                sends, _ = make_copies(sx, so, send_sems, recv_sems)
                for cp in sends:
                    cp.start()

            inner(*ins, *outs, *scr)

            @pl.when(pl.program_id(0) == steps - 1)
            def _():
                sends, recvs = make_copies(sx, so, send_sems, recv_sems)
                for cp in recvs:
                    cp.wait_recv()
                for cp in sends:
                    cp.wait_send()

        in_specs += [_ANY] * n
        args += list(blks)
        out_specs += [_ANY] * n
        out_shape += [jax.ShapeDtypeStruct(out_shape_of(b), b.dtype) for b in blks]
        scratch += [pltpu.SemaphoreType.DMA((per_block * n,)), pltpu.SemaphoreType.DMA((per_block * n,))]
    outs = pl.pallas_call(
        body, name=name, grid=(steps,), in_specs=in_specs, out_specs=out_specs, out_shape=out_shape,
        scratch_shapes=scratch, compiler_params=_cparams("arbitrary"),
    )(*args)
    return outs[:n_out], outs[n_out:]


def _dot(a, b):
    return jnp.dot(a, b, preferred_element_type=F32)


def _dot_tn(a, b):
    return lax.dot_general(a, b, (((0,), (0,)), ((), ())), preferred_element_type=F32)


def _dot_nt(a, b):
    return lax.dot_general(a, b, (((1,), (1,)), ((), ())), preferred_element_type=F32)


def _rms(x, g):
    r = lax.rsqrt(jnp.mean(x * x, axis=-1, keepdims=True) + RMS_EPS)
    return x * r * g


def fused_mm(name, ins, in_specs, prologue, w, *, tn, out_dtype, res=None, extras=(), side=None, nt=False):
    S = ins[0].shape[0]
    N, K = w.shape if nt else w.shape[::-1]
    tn = min(tn, N)
    n_in, n_ex = len(ins), len(extras)

    def body(*refs):
        in_refs = refs[:n_in]
        w_ref = refs[n_in]
        pos = n_in + 1
        res_ref = None
        if res is not None:
            res_ref = refs[pos]
            pos += 1
        o_ref = refs[pos]
        ex_refs = refs[pos + 1:pos + 1 + n_ex]
        a_scr = refs[-1]
        out = prologue(*[r[...] for r in in_refs])
        a_scr[...] = out[0]
        for r, e in zip(ex_refs, out[1:]):
            r[...] = e.astype(r.dtype)
        for j in range(N // tn):
            cs = slice(j * tn, (j + 1) * tn)
            acc = _dot_nt(a_scr[...], w_ref[cs, :]) if nt else _dot(a_scr[...], w_ref[:, cs])
            if res_ref is not None:
                acc = acc + res_ref[:, cs]
            o_ref[:, cs] = acc.astype(o_ref.dtype)

    specs = list(in_specs) + [pl.BlockSpec(w.shape, lambda i: (0, 0), pipeline_mode=pl.Buffered(1))]
    args = list(ins) + [w]
    if res is not None:
        specs.append(pl.BlockSpec((TM, N), lambda i: (i, 0)))
        args.append(res)
    out_shape = [jax.ShapeDtypeStruct((S, N), out_dtype)]
    out_specs = [pl.BlockSpec((TM, N), lambda i: (i, 0))]
    for cols, dt in extras:
        out_shape.append(jax.ShapeDtypeStruct((S, cols), dt))
        out_specs.append(pl.BlockSpec((TM, cols), lambda i: (i, 0)))
    outs, side_outs = _call(body, name=name, steps=S // TM, in_specs=specs, out_specs=out_specs, out_shape=out_shape,
                            scratch=[pltpu.VMEM((TM, K), BF16)], args=args, side=side)
    result = outs if n_ex else outs[0]
    return result if side is None else (result, side_outs)


def _row_spec(cols, blk=0, tm=TM):
    return pl.BlockSpec((tm, cols), lambda i: (i, blk))


def mm_norm_bwd(name, ins, in_specs, w, x, g, dres, *, tm, pre=None, extras=()):
    S = x.shape[0]
    n = len(ins)

    def body(*refs):
        x_ref, g_ref, dres_ref, w_ref, dx_ref, dg_ref = refs[n:n + 6]

        @pl.when(pl.program_id(0) == 0)
        def _():
            dg_ref[...] = jnp.zeros_like(dg_ref)

        tiles = [r[...] for r in refs[:n]]
        if pre is not None:
            tiles, extra_tiles = pre(*tiles)
            for r, e in zip(refs[n + 6:], extra_tiles):
                r[...] = e.astype(r.dtype)
        tiles = [t.astype(BF16) for t in tiles]
        a = tiles[0] if len(tiles) == 1 else jnp.concatenate(tiles, axis=1)
        dh = _dot_nt(a, w_ref[...])
        xv = x_ref[...]
        r = lax.rsqrt(jnp.mean(xv * xv, axis=-1, keepdims=True) + RMS_EPS)
        xhat = xv * r
        dxhat = dh * g_ref[...]
        dx_ref[...] = dres_ref[...] + r * (dxhat - xhat * jnp.mean(dxhat * xhat, axis=-1, keepdims=True))
        dg_ref[...] += jnp.sum((dh * xhat).reshape(tm // 8, 8, D_MODEL), axis=0)

    row = _row_spec(D_MODEL, tm=tm)
    return pl.pallas_call(
        body, name=name, grid=(S // tm,),
        in_specs=list(in_specs)
        + [row, _full((1, D_MODEL)), row, pl.BlockSpec(w.shape, lambda i: (0, 0), pipeline_mode=pl.Buffered(1))],
        out_specs=[row, _full((8, D_MODEL))] + [_row_spec(cols, tm=tm) for cols, _ in extras],
        out_shape=[jax.ShapeDtypeStruct((S, D_MODEL), F32), jax.ShapeDtypeStruct((8, D_MODEL), F32)]
        + [jax.ShapeDtypeStruct((S, cols), dt) for cols, dt in extras],
        compiler_params=_cparams("arbitrary"),
    )(*ins, x, g, dres, w)


def mm_tn(name, a, b):
    S, K = a.shape
    N = b.shape[1]
    tk = _pick(K, (1024, 1408, 512, 256))
    tn = _pick(N, (1024, 1408, 1536, 512, 256))

    def body(a_ref, b_ref, o_ref):
        @pl.when(pl.program_id(2) == 0)
        def _():
            o_ref[...] = jnp.zeros_like(o_ref)

        o_ref[...] += _dot_tn(a_ref[...].astype(BF16), b_ref[...].astype(BF16))

    return pl.pallas_call(
        body, name=name, grid=(K // tk, N // tn, S // TM_W),
        in_specs=[pl.BlockSpec((TM_W, tk), lambda k, n, s: (s, k)), pl.BlockSpec((TM_W, tn), lambda k, n, s: (s, n))],
        out_specs=pl.BlockSpec((tk, tn), lambda k, n, s: (k, n)),
        out_shape=jax.ShapeDtypeStruct((K, N), F32),
        compiler_params=_cparams("parallel", "parallel", "arbitrary"),
    )(a, b)


def _swiglu_bwd_pre(dact, h1, h2):
    h1 = h1.astype(F32)
    h2 = h2.astype(F32)
    da = dact.astype(F32)
    sg = jax.nn.sigmoid(h1)
    halves = [(da * h2 * sg * (1.0 + h1 * (1.0 - sg))).astype(BF16), (da * h1 * sg).astype(BF16)]
    return halves, halves


def _ple_bwd_pre(dx, a_pre, pp):
    pg = jax.nn.sigmoid(a_pre.astype(F32))
    da = (dx * pp.astype(F32) * pg * (1.0 - pg)).astype(BF16)
    return [da], [da, (dx * pg).astype(BF16)]


def ple_combine(x2, a_pre, pp):
    S = x2.shape[0]

    def body(x_ref, a_ref, p_ref, o_ref):
        o_ref[...] = x_ref[...] + jax.nn.sigmoid(a_ref[...].astype(F32)) * p_ref[...].astype(F32)

    row = pl.BlockSpec((TM, D_MODEL), lambda i: (i, 0))
    return pl.pallas_call(
        body, name="ple_combine", grid=(S // TM,), in_specs=[row, row, row], out_specs=row,
        out_shape=jax.ShapeDtypeStruct((S, D_MODEL), F32), compiler_params=_cparams("parallel"),
    )(x2, a_pre, pp)


def _gate_specs(tn, nn):
    return [pl.BlockSpec((TM, tn), functools.partial(lambda i, j, r: (i, r * nn + j), r=r)) for r in range(3)]


def merge_fwd(z, y_ssm, y_conv, y_attn, wb):
    S = z.shape[0]
    tn = 512
    nn = D_MODEL // tn

    def body(g0, g1, g2, y0, y1, y2, w_ref, o_ref):
        acc = jnp.zeros((TM, tn), F32)
        for r, (g_ref, y_ref) in enumerate(((g0, y0), (g1, y1), (g2, y2))):
            acc += jax.nn.sigmoid(g_ref[...].astype(F32)) * _dot(y_ref[...], w_ref[r])
        o_ref[...] = acc.astype(BF16)

    y_spec = pl.BlockSpec((TM, BRANCH), lambda i, j: (i, 0))
    return pl.pallas_call(
        body, name="merge_fwd", grid=(S // TM, nn),
        in_specs=_gate_specs(tn, nn) + [y_spec] * 3 + [pl.BlockSpec((3, BRANCH, tn), lambda i, j: (0, 0, j))],
        out_specs=pl.BlockSpec((TM, tn), lambda i, j: (i, j)),
        out_shape=jax.ShapeDtypeStruct((S, D_MODEL), BF16), compiler_params=_cparams("parallel", "parallel"),
    )(z, z, z, y_ssm, y_conv, y_attn, wb)


def merge_bwd(dmerged, z, y_ssm, y_conv, y_attn, wb):
    S = z.shape[0]
    tn = 512
    nn = D_MODEL // tn

    def body(dm_ref, g0, g1, g2, y0, y1, y2, w_ref, dg0, dg1, dg2, db0, db1, db2):
        dm = dm_ref[...].astype(F32)
        for r, (g_ref, y_ref, dg_ref, db_ref) in enumerate(((g0, y0, dg0, db0), (g1, y1, dg1, db1), (g2, y2, dg2, db2))):
            sg = jax.nn.sigmoid(g_ref[...].astype(F32))
            b = _dot(y_ref[...], w_ref[r])
            dg_ref[...] = (dm * b * sg * (1.0 - sg)).astype(BF16)
            db_ref[...] = (dm * sg).astype(BF16)

    y_spec = pl.BlockSpec((TM, BRANCH), lambda i, j: (i, 0))
    outs = pl.pallas_call(
        body, name="merge_bwd", grid=(S // TM, nn),
        in_specs=[pl.BlockSpec((TM, tn), lambda i, j: (i, j))] + _gate_specs(tn, nn) + [y_spec] * 3
        + [pl.BlockSpec((3, BRANCH, tn), lambda i, j: (0, 0, j))],
        out_specs=[pl.BlockSpec((TM, tn), lambda i, j: (i, j))] * 6,
        out_shape=[jax.ShapeDtypeStruct((S, D_MODEL), BF16)] * 6, compiler_params=_cparams("parallel", "parallel"),
    )(dmerged, z, z, z, y_ssm, y_conv, y_attn, wb)
    return outs[:3], outs[3:]


def _shift_down(v, halo, k):
    rolled = pltpu.roll(v, k, 0)
    h = pltpu.roll(halo, k, 0)
    row = lax.broadcasted_iota(jnp.int32, v.shape, 0)
    head = jnp.concatenate([h, jnp.zeros((v.shape[0] - 8, v.shape[1]), v.dtype)], axis=0)
    return jnp.where(row < k, head, rolled)


def _shift_up(v, halo, k):
    n = v.shape[0]
    rolled = pltpu.roll(v, n - k, 0)
    h = pltpu.roll(halo, 8 - k, 0)
    row = lax.broadcasted_iota(jnp.int32, v.shape, 0)
    tail = jnp.concatenate([jnp.zeros((n - 8, v.shape[1]), v.dtype), h], axis=0)
    return jnp.where(row >= n - k, tail, rolled)


def _conv_specs():
    rb = TM // 8
    c0 = COL_U // BRANCH

    def cur(k):
        return pl.BlockSpec((TM, BRANCH), lambda i: (i, c0 + k))

    def prev(k):
        return pl.BlockSpec((8, BRANCH), lambda i: (jnp.maximum(i * rb - 1, 0), c0 + k))

    return [cur(1), cur(2), cur(3), prev(2), prev(3)]


def conv_fwd(z, conv_w):
    S = z.shape[0]

    def body(cb_ref, cc_ref, cx_ref, pc_ref, px_ref, w_ref, o_ref):
        first = pl.program_id(0) == 0
        v = cc_ref[...].astype(F32) * cx_ref[...].astype(F32)
        pv = jnp.where(first, 0.0, pc_ref[...].astype(F32) * px_ref[...].astype(F32))
        w = w_ref[...]
        y = w[2:3] * v + w[1:2] * _shift_down(v, pv, 1) + w[0:1] * _shift_down(v, pv, 2)
        o_ref[...] = (cb_ref[...].astype(F32) * y).astype(BF16)

    return pl.pallas_call(
        body, name="conv_fwd", grid=(S // TM,), in_specs=_conv_specs() + [_full((3, BRANCH))],
        out_specs=pl.BlockSpec((TM, BRANCH), lambda i: (i, 0)),
        out_shape=jax.ShapeDtypeStruct((S, BRANCH), BF16), compiler_params=_cparams("parallel"),
    )(z, z, z, z, z, conv_w)


def conv_bwd(dy, z, conv_w):
    S = z.shape[0]
    rb = TM // 8
    nt = S // TM
    c0 = COL_U // BRANCH

    def body(dy_ref, cb_ref, cc_ref, cx_ref, pc_ref, px_ref, ndy_ref, ncb_ref, w_ref, o_ref, dw_ref):
        i = pl.program_id(0)

        @pl.when(i == 0)
        def _():
            dw_ref[...] = jnp.zeros_like(dw_ref)

        cb = cb_ref[...].astype(F32)
        cc = cc_ref[...].astype(F32)
        cx = cx_ref[...].astype(F32)
        dyv = dy_ref[...].astype(F32)
        v = cc * cx
        pv = jnp.where(i == 0, 0.0, pc_ref[...].astype(F32) * px_ref[...].astype(F32))
        v1 = _shift_down(v, pv, 1)
        v2 = _shift_down(v, pv, 2)
        w = w_ref[...]
        conv = w[2:3] * v + w[1:2] * v1 + w[0:1] * v2
        dc = dyv * cb
        ndc = jnp.where(i == nt - 1, 0.0, ndy_ref[...].astype(F32) * ncb_ref[...].astype(F32))
        dv = w[2:3] * dc + w[1:2] * _shift_up(dc, ndc, 1) + w[0:1] * _shift_up(dc, ndc, 2)
        o_ref[:, 0:BRANCH] = (dyv * conv).astype(BF16)
        o_ref[:, BRANCH:2 * BRANCH] = (dv * cx).astype(BF16)
        o_ref[:, 2 * BRANCH:3 * BRANCH] = (dv * cc).astype(BF16)
        for k, vk in enumerate((v2, v1, v)):
            dw_ref[8 * k:8 * k + 8, :] += jnp.sum((dc * vk).reshape(rb, 8, BRANCH), axis=0)

    nxt = jnp.minimum

    return pl.pallas_call(
        body, name="conv_bwd", grid=(nt,),
        in_specs=[pl.BlockSpec((TM, BRANCH), lambda i: (i, 0))] + _conv_specs()
        + [pl.BlockSpec((8, BRANCH), lambda i: (nxt((i + 1) * rb, S // 8 - 1), 0)),
           pl.BlockSpec((8, BRANCH), lambda i: (nxt((i + 1) * rb, S // 8 - 1), c0 + 1)),
           _full((3, BRANCH))],
        out_specs=[pl.BlockSpec((TM, 3 * BRANCH), lambda i: (i, 0)), _full((24, BRANCH))],
        out_shape=[jax.ShapeDtypeStruct((S, 3 * BRANCH), BF16), jax.ShapeDtypeStruct((24, BRANCH), F32)],
        compiler_params=_cparams("arbitrary"),
    )(dy, z, z, z, z, z, dy, z, conv_w)


def _bucket_onehot_t():
    qi = np.arange(BLOCK)[:, None]
    kj = np.arange(2 * BLOCK)[None, :]
    dist = np.clip(qi + BLOCK - kj, 0, REL_MAX_DIST - 1)
    exact = REL_BUCKETS // 2
    df = np.maximum(dist, 1).astype(np.float32)
    large = exact + (np.log(df / np.float32(exact)) / np.float32(math.log(REL_MAX_DIST / exact))
                     * np.float32(REL_BUCKETS - exact)).astype(np.int32)
    large = np.minimum(large, REL_BUCKETS - 1)
    bucket = np.where(dist < exact, dist, large).reshape(-1)
    return (np.arange(REL_BUCKETS)[:, None] == bucket[None, :]).astype(np.float32)


def rel_bias_fwd(rel_bias_t):
    n = BLOCK * 2 * BLOCK

    def body(r_ref, oh_ref, o_ref):
        o_ref[...] = jnp.dot(r_ref[...], oh_ref[...], precision=lax.Precision.HIGHEST, preferred_element_type=F32)

    return pl.pallas_call(
        body, name="rel_bias_fwd", grid=(1,), in_specs=[_full((N_Q_HEADS, REL_BUCKETS)), _full((REL_BUCKETS, n))],
        out_specs=_full((N_Q_HEADS, n)), out_shape=jax.ShapeDtypeStruct((N_Q_HEADS, n), F32),
        compiler_params=_cparams("arbitrary"),
    )(rel_bias_t, jnp.asarray(_bucket_onehot_t()))


def rel_bias_bwd(dbias):
    n_l = dbias.shape[0]
    n = BLOCK * 2 * BLOCK

    def body(d_ref, oh_ref, o_ref):
        tot = d_ref[0]
        for l in range(1, n_l):
            tot = tot + d_ref[l]
        o_ref[...] = lax.dot_general(tot, oh_ref[...], (((1,), (1,)), ((), ())), precision=lax.Precision.HIGHEST,
                                     preferred_element_type=F32)

    return pl.pallas_call(
        body, name="rel_bias_bwd", grid=(1,), in_specs=[_full((n_l, N_Q_HEADS, n)), _full((REL_BUCKETS, n))],
        out_specs=_full((N_Q_HEADS, REL_BUCKETS)), out_shape=jax.ShapeDtypeStruct((N_Q_HEADS, REL_BUCKETS), F32),
        compiler_params=_cparams("arbitrary"),
    )(dbias, jnp.asarray(_bucket_onehot_t()))


def _attn_valid(first):
    qi = lax.broadcasted_iota(jnp.int32, (BLOCK, 2 * BLOCK), 0)
    kj = lax.broadcasted_iota(jnp.int32, (BLOCK, 2 * BLOCK), 1)
    dist = qi + BLOCK - kj
    return (dist >= 0) & (dist < WINDOW) & (jnp.logical_not(first) | (kj >= BLOCK))


def _attn_weights(qh, kcat, bias_h, valid, sink):
    s = _dot_nt(qh, kcat) * ATTN_SCALE + bias_h
    s = jnp.where(valid, s, NEG)
    m = jnp.maximum(jnp.max(s, axis=-1, keepdims=True), sink)
    p = jnp.exp(s - m)
    esink = jnp.exp(sink - m)
    inv = 1.0 / (jnp.sum(p, axis=-1, keepdims=True) + esink)
    return p * inv, esink * inv


def _kv_heads(kvp, kvc, hk):
    ks = slice(hk * HEAD_DIM, (hk + 1) * HEAD_DIM)
    vs = slice(KV_WIDTH // 2 + hk * HEAD_DIM, KV_WIDTH // 2 + (hk + 1) * HEAD_DIM)
    return jnp.concatenate([kvp[:, ks], kvc[:, ks]], axis=0), jnp.concatenate([kvp[:, vs], kvc[:, vs]], axis=0)


def _attn_specs():
    cq = (COL_U + 4 * BRANCH) // BRANCH
    ckv = COL_KV // KV_WIDTH
    return [pl.BlockSpec((BLOCK, BRANCH), lambda n: (n, cq)),
            pl.BlockSpec((BLOCK, KV_WIDTH), lambda n: (n, ckv)),
            pl.BlockSpec((BLOCK, KV_WIDTH), lambda n: (jnp.maximum(n - 1, 0), ckv)),
            _full((N_Q_HEADS, BLOCK, 2 * BLOCK)),
            pl.BlockSpec(memory_space=pltpu.SMEM)]


def attn_fwd(z, bias, sinks, side=None):
    S = z.shape[0]

    def body(q_ref, kvc_ref, kvp_ref, b_ref, sink_ref, o_ref):
        valid = _attn_valid(pl.program_id(0) == 0)
        q = q_ref[...]
        kvc = kvc_ref[...]
        kvp = kvp_ref[...]
        outs = []
        for hk in range(N_KV_HEADS):
            kcat, vcat = _kv_heads(kvp, kvc, hk)
            for g in range(GQA_GROUP):
                h = hk * GQA_GROUP + g
                w, _ = _attn_weights(q[:, h * HEAD_DIM:(h + 1) * HEAD_DIM], kcat, b_ref[h], valid, sink_ref[h])
                outs.append(_dot(w.astype(BF16), vcat))
        o_ref[...] = jnp.concatenate(outs, axis=1).astype(BF16)

    outs, side_outs = _call(
        body, name="attn_fwd", steps=S // BLOCK, in_specs=_attn_specs(),
        out_specs=[pl.BlockSpec((BLOCK, BRANCH), lambda n: (n, 0))],
        out_shape=[jax.ShapeDtypeStruct((S, BRANCH), BF16)], scratch=[], args=(z, z, z, bias, sinks), side=side)
    return outs[0], side_outs


def attn_bwd(do, z, bias, sinks):
    S = z.shape[0]

    def body(do_ref, q_ref, kvc_ref, kvp_ref, bt_ref, sink_ref, dq_ref, dc_ref, dp_ref, db_ref, ds_ref):
        first = pl.program_id(0) == 0

        @pl.when(first)
        def _():
            db_ref[...] = jnp.zeros_like(db_ref)
            ds_ref[...] = jnp.zeros_like(ds_ref)

        kj = lax.broadcasted_iota(jnp.int32, (2 * BLOCK, BLOCK), 0)
        dist = lax.broadcasted_iota(jnp.int32, (2 * BLOCK, BLOCK), 1) + BLOCK - kj
        valid = (dist >= 0) & (dist < WINDOW) & (jnp.logical_not(first) | (kj >= BLOCK))
        valid4 = jnp.concatenate([valid] * GQA_GROUP, axis=1)
        q = q_ref[...]
        kvc = kvc_ref[...]
        kvp = kvp_ref[...]
        dov = do_ref[...]
        dqs, dks, dvs = [], [], []
        for hk in range(N_KV_HEADS):
            kcat, vcat = _kv_heads(kvp, kvc, hk)
            heads = range(hk * GQA_GROUP, (hk + 1) * GQA_GROUP)
            q4 = jnp.concatenate([q[:, h * HEAD_DIM:(h + 1) * HEAD_DIM] for h in heads], axis=0)
            do4 = jnp.concatenate([dov[:, h * HEAD_DIM:(h + 1) * HEAD_DIM] for h in heads], axis=0)
            bias4 = jnp.concatenate([bt_ref[h] for h in heads], axis=1)
            sink4 = jnp.concatenate([jnp.full((1, BLOCK), sink_ref[h], F32) for h in heads], axis=1)
            s = jnp.where(valid4, _dot_nt(kcat, q4) * ATTN_SCALE + bias4, NEG)
            m = jnp.maximum(jnp.max(s, axis=0, keepdims=True), sink4)
            p = jnp.exp(s - m)
            esink = jnp.exp(sink4 - m)
            inv = 1.0 / (jnp.sum(p, axis=0, keepdims=True) + esink)
            w = p * inv
            dvs.append(_dot(w.astype(BF16), do4))
            dw = _dot_nt(vcat, do4)
            delta = jnp.sum(w * dw, axis=0, keepdims=True)
            ds = w * (dw - delta)
            dsink = -(esink * inv) * delta
            for g, h in enumerate(heads):
                lanes = slice(g * BLOCK, (g + 1) * BLOCK)
                db_ref[h] += ds[:, lanes]
                ds_ref[h:h + 1, :] += jnp.broadcast_to(jnp.sum(dsink[:, lanes], axis=1, keepdims=True), (1, BLOCK))
            dsb = (ds * ATTN_SCALE).astype(BF16)
            dks.append(_dot(dsb, q4))
            dq4 = _dot_tn(dsb, kcat)
            dqs += [dq4[g * BLOCK:(g + 1) * BLOCK] for g in range(GQA_GROUP)]
        dq_ref[...] = jnp.concatenate(dqs, axis=1).astype(BF16)
        both = jnp.concatenate(dks + dvs, axis=1)
        dp_ref[...] = both[:BLOCK]
        dc_ref[...] = both[BLOCK:]

    blk = pl.BlockSpec((BLOCK, BRANCH), lambda n: (n, 0))
    kvb = pl.BlockSpec((BLOCK, KV_WIDTH), lambda n: (n, 0))
    keys_first = (N_Q_HEADS, 2 * BLOCK, BLOCK)
    specs = _attn_specs()
    specs[3] = _full(keys_first)
    dq, dkv_cur, dkv_prev, dbias_t, dsinks = pl.pallas_call(
        body, name="attn_bwd", grid=(S // BLOCK,), in_specs=[blk] + specs,
        out_specs=[blk, kvb, kvb, _full(keys_first), _full((N_Q_HEADS, BLOCK))],
        out_shape=[jax.ShapeDtypeStruct((S, BRANCH), BF16), jax.ShapeDtypeStruct((S, KV_WIDTH), F32),
                   jax.ShapeDtypeStruct((S, KV_WIDTH), F32), jax.ShapeDtypeStruct(keys_first, F32),
                   jax.ShapeDtypeStruct((N_Q_HEADS, BLOCK), F32)],
        compiler_params=_cparams("arbitrary"),
    )(do, z, z, z, jnp.swapaxes(bias, 1, 2), sinks)
    return dq, dkv_cur, dkv_prev, jnp.swapaxes(dbias_t, 1, 2), dsinks


def kv_shift_add(dcur, dprev):
    S = dcur.shape[0]
    nt = S // TM
    per_tile = TM // BLOCK

    def body(c_ref, p_ref, n_ref, o_ref):
        nxt = jnp.where(pl.program_id(0) == nt - 1, 0.0, n_ref[...])
        o_ref[...] = (c_ref[...] + jnp.concatenate([p_ref[BLOCK:, :], nxt], axis=0)).astype(BF16)

    tile = pl.BlockSpec((TM, KV_WIDTH), lambda i: (i, 0))
    return pl.pallas_call(
        body, name="kv_shift_add", grid=(nt,),
        in_specs=[tile, tile,
                  pl.BlockSpec((BLOCK, KV_WIDTH), lambda i: (jnp.minimum((i + 1) * per_tile, S // BLOCK - 1), 0))],
        out_specs=tile, out_shape=jax.ShapeDtypeStruct((S, KV_WIDTH), BF16), compiler_params=_cparams("parallel"),
    )(dcur, dprev, dprev)


def _ssm_disc(lam_re, lam_im, log_dt, bt_re, bt_im):
    dt = jnp.exp(log_dt)
    mag = jnp.exp(lam_re * dt)
    ang = lam_im * dt
    a_re = mag * jnp.cos(ang)
    a_im = mag * jnp.sin(ang)
    den = lam_re * lam_re + lam_im * lam_im
    nr = a_re - 1.0
    coef_re = (nr * lam_re + a_im * lam_im) / den
    coef_im = (a_im * lam_re - nr * lam_im) / den
    bb_re = coef_re[:, None, :] * bt_re - coef_im[:, None, :] * bt_im
    bb_im = coef_re[:, None, :] * bt_im + coef_im[:, None, :] * bt_re
    return a_re, a_im, bb_re, bb_im


_GN = (SSM_GROUPS, SSM_STATE)
_GPN = (SSM_GROUPS, SSM_GROUP, SSM_STATE)


def ssm_disc_fwd(lam_re, lam_im, log_dt, bt_re, bt_im):
    def body(lr_ref, li_ref, dt_ref, br_ref, bi_ref, ar_ref, ai_ref, bbr_ref, bbi_ref):
        a_re, a_im, bb_re, bb_im = _ssm_disc(lr_ref[...], li_ref[...], dt_ref[...], br_ref[...], bi_ref[...])
        ar_ref[...] = a_re
        ai_ref[...] = a_im
        bbr_ref[...] = bb_re
        bbi_ref[...] = bb_im

    return pl.pallas_call(
        body, name="ssm_disc_fwd", grid=(1,),
        in_specs=[_full(_GN), _full(_GN), _full((SSM_GROUPS, 1)), _full(_GPN), _full(_GPN)],
        out_specs=[_full(_GN), _full(_GN), _full(_GPN), _full(_GPN)],
        out_shape=[jax.ShapeDtypeStruct(s, F32) for s in (_GN, _GN, _GPN, _GPN)],
        compiler_params=_cparams("arbitrary"),
    )(lam_re, lam_im, log_dt, bt_re, bt_im)


def ssm_disc_bwd(lam_re, lam_im, log_dt, bt_re, bt_im, da_re, da_im, dbb_re, dbb_im):
    def body(lr_ref, li_ref, dt_ref, br_ref, bi_ref, dar_ref, dai_ref, dbr_ref, dbi_ref, o_lr, o_li, o_dt, o_br, o_bi):
        prim = (lr_ref[...], li_ref[...], dt_ref[...], br_ref[...], bi_ref[...])
        _, vjp = jax.vjp(_ssm_disc, *prim)
        grads = vjp((dar_ref[...], dai_ref[...], dbr_ref[...], dbi_ref[...]))
        for r, v in zip((o_lr, o_li, o_dt, o_br, o_bi), grads):
            r[...] = v

    shapes = (_GN, _GN, (SSM_GROUPS, 1), _GPN, _GPN)
    return pl.pallas_call(
        body, name="ssm_disc_bwd", grid=(1,),
        in_specs=[_full(s) for s in shapes + (_GN, _GN, _GPN, _GPN)],
        out_specs=[_full(s) for s in shapes], out_shape=[jax.ShapeDtypeStruct(s, F32) for s in shapes],
        compiler_params=_cparams("arbitrary"),
    )(lam_re, lam_im, log_dt, bt_re, bt_im, da_re, da_im, dbb_re, dbb_im)


LANE_GROUPS = SSM_LANES // 128
SUB_GROUPS = SUB_ST // 128
_TM_SHAPE = (LANE_GROUPS, 128)


def _step_rows(t):
    return pl.ds(pl.multiple_of(t * LANE_GROUPS, LANE_GROUPS), LANE_GROUPS)


def _group_rows(j):
    return pl.ds(j, SCAN_T, stride=LANE_GROUPS)


def _store_sub(ref, j, val):
    for k in range(SUB_GROUPS):
        ref[_group_rows(j * SUB_GROUPS + k), :] = val[:, k * 128:(k + 1) * 128]


def _load_sub(ref, j):
    return jnp.concatenate([ref[_group_rows(j * SUB_GROUPS + k), :] for k in range(SUB_GROUPS)], axis=1)


_SUB_SHAPE_IN = (SSM_SUB, SUB_IN, SUB_ST)
_SUB_SHAPE_OUT = (SSM_SUB, SUB_ST, SUB_IN)


def ssm_fwd(z, bb_re, bb_im, ct_re, ct_im, a_re, a_im, d_skip, wglu, side=None):
    S = z.shape[0]
    cu = COL_U // BRANCH

    def body(u_ref, bbr_ref, bbi_ref, ctr_ref, cti_ref, ar_ref, ai_ref, d_ref, wg_ref,
             y_ref, ypre_ref, hr_ref, hi_ref, hrow_r, hrow_i, bur, bui, car_r, car_i):
        @pl.when(pl.program_id(0) == 0)
        def _():
            car_r[...] = jnp.zeros_like(car_r)
            car_i[...] = jnp.zeros_like(car_i)

        u = u_ref[...]
        for j in range(SSM_SUB):
            uj = u[:, j * SUB_IN:(j + 1) * SUB_IN]
            _store_sub(bur, j, _dot(uj, bbr_ref[j]))
            _store_sub(bui, j, _dot(uj, bbi_ref[j]))
        ar = ar_ref[...]
        ai = ai_ref[...]

        def step(t, carry):
            hr, hi = carry
            rows = _step_rows(t)
            nhr = ar * hr - ai * hi + bur[rows, :]
            nhi = ar * hi + ai * hr + bui[rows, :]
            hr_ref[rows, :] = nhr
            hi_ref[rows, :] = nhi
            return nhr, nhi

        hr, hi = lax.fori_loop(0, SCAN_T, step, (car_r[...], car_i[...]), unroll=8)
        car_r[...] = hr
        car_i[...] = hi
        ys = []
        for j in range(SSM_SUB):
            cs = slice(j * SUB_ST, (j + 1) * SUB_ST)
            hrow_r[:, cs] = _load_sub(hr_ref, j).astype(BF16)
            hrow_i[:, cs] = _load_sub(hi_ref, j).astype(BF16)
            ys.append(_dot(hrow_r[:, cs], ctr_ref[j]) - _dot(hrow_i[:, cs], cti_ref[j]))
        ypre = jnp.concatenate(ys, axis=1) + d_ref[...] * u.astype(F32)
        ypre_ref[...] = ypre
        g = jax.nn.gelu(ypre)
        y_ref[...] = (g * jax.nn.sigmoid(_dot(g.astype(BF16), wg_ref[...]))).astype(BF16)

    row = pl.BlockSpec((SCAN_T, BRANCH), lambda i: (i, 0))
    st = pl.BlockSpec((SCAN_T * LANE_GROUPS, 128), lambda i: (i, 0))
    wide = pl.BlockSpec((SCAN_T, SSM_LANES), lambda i: (i, 0))
    return _call(
        body, name="ssm_fwd", steps=S // SCAN_T,
        in_specs=[pl.BlockSpec((SCAN_T, BRANCH), lambda i: (i, cu)), _full(_SUB_SHAPE_IN), _full(_SUB_SHAPE_IN),
                  _full(_SUB_SHAPE_OUT), _full(_SUB_SHAPE_OUT), _full(_TM_SHAPE), _full(_TM_SHAPE), _full((1, BRANCH)),
                  _full((BRANCH, BRANCH))],
        out_specs=[row, row, st, st, wide, wide],
        out_shape=[jax.ShapeDtypeStruct((S, BRANCH), BF16), jax.ShapeDtypeStruct((S, BRANCH), F32),
                   jax.ShapeDtypeStruct((S * LANE_GROUPS, 128), F32), jax.ShapeDtypeStruct((S * LANE_GROUPS, 128), F32),
                   jax.ShapeDtypeStruct((S, SSM_LANES), BF16), jax.ShapeDtypeStruct((S, SSM_LANES), BF16)],
        scratch=[pltpu.VMEM((SCAN_T * LANE_GROUPS, 128), F32), pltpu.VMEM((SCAN_T * LANE_GROUPS, 128), F32),
                 pltpu.VMEM(_TM_SHAPE, F32), pltpu.VMEM(_TM_SHAPE, F32)],
        args=(z, bb_re, bb_im, ct_re, ct_im, a_re, a_im, d_skip, wglu), side=side)


def ssm_bwd(dy, z, ypre, h_re, h_im, hrow_re, hrow_im, bbt_re, bbt_im, c_re, c_im, a_re, a_im, d_skip, wglu, wglu_t,
            side=None):
    S = z.shape[0]
    nt = S // SCAN_T
    cu = COL_U // BRANCH

    def body(dy_ref, u_ref, ypre_ref, hr_ref, hi_ref, hpr_ref, hpi_ref, hrow_r, hrow_i, bbr_ref, bbi_ref, cr_ref, ci_ref,
             ar_ref, ai_ref, d_ref, wg_ref, wgt_ref,
             du_ref, dbbr_ref, dbbi_ref, dctr_ref, dcti_ref, dar_ref, dai_ref, dd_ref, dwg_ref,
             lr_scr, li_scr, car_r, car_i):
        step = pl.program_id(0)

        @pl.when(step == 0)
        def _():
            for r in (dbbr_ref, dbbi_ref, dctr_ref, dcti_ref, dar_ref, dai_ref, dd_ref, dwg_ref, car_r, car_i):
                r[...] = jnp.zeros_like(r)

        u = u_ref[...]
        uf = u.astype(F32)
        dyv = dy_ref[...].astype(F32)
        g, gelu_vjp = jax.vjp(jax.nn.gelu, ypre_ref[...])
        gb = g.astype(BF16)
        sg = jax.nn.sigmoid(_dot(gb, wg_ref[...]))
        dgl = (dyv * g * sg * (1.0 - sg)).astype(BF16)
        dwg_ref[...] += _dot_tn(gb, dgl)
        dg = dyv * sg + _dot(dgl, wgt_ref[...])
        dypre = gelu_vjp(dg)[0]
        dd_ref[...] += jnp.sum(dypre * uf, axis=0, keepdims=True)
        dyb = dypre.astype(BF16)
        for j in range(SSM_SUB):
            dyj = dyb[:, j * SUB_IN:(j + 1) * SUB_IN]
            _store_sub(lr_scr, j, _dot(dyj, cr_ref[j]))
            _store_sub(li_scr, j, -_dot(dyj, ci_ref[j]))
            cs = slice(j * SUB_ST, (j + 1) * SUB_ST)
            dctr_ref[j] += _dot_tn(hrow_r[:, cs], dyj)
            dcti_ref[j] -= _dot_tn(hrow_i[:, cs], dyj)

        ar = ar_ref[...]
        ai = ai_ref[...]

        def adjoint(lr, li, rows):
            nlr = ar * lr + ai * li + lr_scr[rows, :]
            nli = ar * li - ai * lr + li_scr[rows, :]
            lr_scr[rows, :] = nlr
            li_scr[rows, :] = nli
            return nlr, nli

        def back(k, carry):
            lr, li, acc_r, acc_i = carry
            t = SCAN_T - 1 - k
            lr, li = adjoint(lr, li, _step_rows(t))
            hpr = hr_ref[_step_rows(t - 1), :]
            hpi = hi_ref[_step_rows(t - 1), :]
            return lr, li, acc_r + lr * hpr + li * hpi, acc_i + li * hpr - lr * hpi

        zero = jnp.zeros(_TM_SHAPE, F32)
        lr, li, acc_r, acc_i = lax.fori_loop(0, SCAN_T - 1, back, (car_r[...], car_i[...], zero, zero), unroll=8)
        lr, li = adjoint(lr, li, pl.ds(0, LANE_GROUPS))
        car_r[...] = lr
        car_i[...] = li
        first_tile = step == nt - 1
        hpr = jnp.where(first_tile, 0.0, hpr_ref[...])
        hpi = jnp.where(first_tile, 0.0, hpi_ref[...])
        dar_ref[...] += acc_r + lr * hpr + li * hpi
        dai_ref[...] += acc_i + li * hpr - lr * hpi

        dus = []
        for j in range(SSM_SUB):
            lrb = _load_sub(lr_scr, j).astype(BF16)
            lib = _load_sub(li_scr, j).astype(BF16)
            uj = u[:, j * SUB_IN:(j + 1) * SUB_IN]
            dus.append(_dot(lrb, bbr_ref[j]) + _dot(lib, bbi_ref[j]))
            dbbr_ref[j] += _dot_tn(uj, lrb)
            dbbi_ref[j] += _dot_tn(uj, lib)
        du_ref[...] = (jnp.concatenate(dus, axis=1) + dypre * d_ref[...]).astype(BF16)

    def rev(i):
        return nt - 1 - i

    row = pl.BlockSpec((SCAN_T, BRANCH), lambda i: (rev(i), 0))
    st = pl.BlockSpec((SCAN_T * LANE_GROUPS, 128), lambda i: (rev(i), 0))
    before = pl.BlockSpec(_TM_SHAPE, lambda i: (jnp.maximum(rev(i) * SCAN_T - 1, 0), 0))
    wide = pl.BlockSpec((SCAN_T, SSM_LANES), lambda i: (rev(i), 0))
    tm = _full(_TM_SHAPE)
    return _call(
        body, name="ssm_bwd", steps=nt,
        in_specs=[row, pl.BlockSpec((SCAN_T, BRANCH), lambda i: (rev(i), cu)), row, st, st, before, before, wide, wide,
                  _full(_SUB_SHAPE_OUT), _full(_SUB_SHAPE_OUT), _full(_SUB_SHAPE_IN), _full(_SUB_SHAPE_IN),
                  tm, tm, _full((1, BRANCH)), _full((BRANCH, BRANCH)), _full((BRANCH, BRANCH))],
        out_specs=[row, _full(_SUB_SHAPE_IN), _full(_SUB_SHAPE_IN), _full(_SUB_SHAPE_OUT), _full(_SUB_SHAPE_OUT),
                   tm, tm, _full((1, BRANCH)), _full((BRANCH, BRANCH))],
        out_shape=[jax.ShapeDtypeStruct((S, BRANCH), BF16), jax.ShapeDtypeStruct(_SUB_SHAPE_IN, F32),
                   jax.ShapeDtypeStruct(_SUB_SHAPE_IN, F32), jax.ShapeDtypeStruct(_SUB_SHAPE_OUT, F32),
                   jax.ShapeDtypeStruct(_SUB_SHAPE_OUT, F32), jax.ShapeDtypeStruct(_TM_SHAPE, F32),
                   jax.ShapeDtypeStruct(_TM_SHAPE, F32), jax.ShapeDtypeStruct((1, BRANCH), F32),
                   jax.ShapeDtypeStruct((BRANCH, BRANCH), F32)],
        scratch=[pltpu.VMEM((SCAN_T * LANE_GROUPS, 128), F32), pltpu.VMEM((SCAN_T * LANE_GROUPS, 128), F32),
                 pltpu.VMEM(_TM_SHAPE, F32), pltpu.VMEM(_TM_SHAPE, F32)],
        args=(dy, z, ypre, h_re, h_im, h_re, h_im, hrow_re, hrow_im, bbt_re, bbt_im, c_re, c_im, a_re, a_im, d_skip, wglu,
              wglu_t),
        side=side)


def _blockdiag(x):
    gs = SSM_GROUPS // SSM_SUB
    x = x.reshape(SSM_SUB, gs, SSM_GROUP, SSM_STATE)
    eye = jnp.eye(gs, dtype=x.dtype)
    return (x[:, :, :, None, :] * eye[None, :, None, :, None]).reshape(SSM_SUB, SUB_IN, SUB_ST)


def _blockdiag_extract(x):
    gs = SSM_GROUPS // SSM_SUB
    x = x.reshape(SSM_SUB, gs, SSM_GROUP, gs, SSM_STATE)
    eye = jnp.eye(gs, dtype=x.dtype)
    return jnp.sum(x * eye[None, :, None, :, None], axis=3).reshape(SSM_GROUPS, SSM_GROUP, SSM_STATE)


def loss_head(x, g, target):
    S = x.shape[0]

    def body(x_ref, g_ref, t_ref, dx_ref, loss_ref, dg_ref):
        @pl.when(pl.program_id(0) == 0)
        def _():
            loss_ref[...] = jnp.zeros_like(loss_ref)
            dg_ref[...] = jnp.zeros_like(dg_ref)

        xv = x_ref[...]
        gv = g_ref[...]
        r = lax.rsqrt(jnp.mean(xv * xv, axis=-1, keepdims=True) + RMS_EPS)
        xhat = xv * r
        err = xhat * gv - t_ref[...]
        loss_ref[...] += jnp.sum((err * err).reshape(TM // 8, 8, D_MODEL), axis=0) * (0.5 / D_MODEL)
        dy = err * (1.0 / D_MODEL)
        dxhat = dy * gv
        dx_ref[...] = r * (dxhat - xhat * jnp.mean(dxhat * xhat, axis=-1, keepdims=True))
        dg_ref[...] += jnp.sum((dy * xhat).reshape(TM // 8, 8, D_MODEL), axis=0)

    row = pl.BlockSpec((TM, D_MODEL), lambda i: (i, 0))
    acc = _full((8, D_MODEL))
    return pl.pallas_call(
        body, name="loss_head", grid=(S // TM,), in_specs=[row, _full((1, D_MODEL)), row],
        out_specs=[row, acc, acc],
        out_shape=[jax.ShapeDtypeStruct((S, D_MODEL), F32), jax.ShapeDtypeStruct((8, D_MODEL), F32),
                   jax.ShapeDtypeStruct((8, D_MODEL), F32)],
        compiler_params=_cparams("arbitrary"),
    )(x, g, target)


def _x_spec():
    return pl.BlockSpec((TM, D_MODEL), lambda i: (i, 0))


def _g_spec():
    return pl.BlockSpec((1, D_MODEL), lambda i: (0, 0))


def _norm_prologue(x, g):
    h = _rms(x, g).astype(BF16)
    return h, h


def _cast_prologue(x):
    return (x.astype(BF16),)


def _swiglu_prologue(h1, h2):
    a = h1.astype(F32)
    act = (a * jax.nn.sigmoid(a) * h2.astype(F32)).astype(BF16)
    return act, act


def _ssm_consts(lw):
    a_re, a_im, bbt_re, bbt_im = ssm_disc_fwd(
        lw["ssm_lambda_re"], lw["ssm_lambda_im"], lw["ssm_log_dt"].reshape(SSM_GROUPS, 1), lw["bt_re"], lw["bt_im"])
    bb_re = _blockdiag(bbt_re).astype(BF16)
    bb_im = _blockdiag(bbt_im).astype(BF16)
    c_re = _blockdiag(lw["ssm_c_re"]).astype(BF16)
    c_im = _blockdiag(lw["ssm_c_im"]).astype(BF16)
    return dict(
        a_re=a_re.reshape(_TM_SHAPE), a_im=a_im.reshape(_TM_SHAPE),
        bb_re=bb_re, bb_im=bb_im, bbt_re=jnp.swapaxes(bb_re, 1, 2), bbt_im=jnp.swapaxes(bb_im, 1, 2),
        c_re=c_re, c_im=c_im, ct_re=jnp.swapaxes(c_re, 1, 2), ct_im=jnp.swapaxes(c_im, 1, 2))


def layer_fwd(x, lw, bias, next_shards=None, place=None):
    sides = (None, None) if next_shards is None else (("gather_chips", next_shards[:GATHER_SPLIT]),
                                                      ("gather_chips", next_shards[GATHER_SPLIT:]))
    out = fused_mm("in_proj", [x, lw["norm_mix"]], [_x_spec(), _g_spec()], _norm_prologue, lw["w_in"], tn=2944,
                   out_dtype=BF16, extras=((D_MODEL, BF16),), side=sides[0])
    (z, h), g4a = (out, []) if next_shards is None else out
    sc = _ssm_consts(lw)
    (y_ssm, ypre, h_re, h_im, hrow_re, hrow_im), g4b = ssm_fwd(
        z, sc["bb_re"], sc["bb_im"], sc["ct_re"], sc["ct_im"], sc["a_re"], sc["a_im"], lw["ssm_d"], lw["ssm_w_glu"],
        side=sides[1])
    y_conv = conv_fwd(z, lw["conv_w"])
    if next_shards is not None:
        g4 = [_put_slot(g, b, place[0]) for g, b in zip(list(g4a) + list(g4b), next_shards)]
    y_attn, g8 = attn_fwd(z, bias, lw["attn_sinks"], side=None if next_shards is None else ("gather_cores", g4))
    next_gathered = None if next_shards is None else [_put_slot(g, b, place[1]) for g, b in zip(g8, g4)]
    merged = merge_fwd(z, y_ssm, y_conv, y_attn, lw["w_branch"])
    x1 = fused_mm("out_proj", [merged], [_x_spec()], _cast_prologue, lw["w_out"], tn=1024, out_dtype=F32, res=x)
    hf, hn1 = fused_mm("ffn_in", [x1, lw["norm_ffn"]], [_x_spec(), _g_spec()], _norm_prologue, lw["w_ffn_in"], tn=2816,
                       out_dtype=BF16, extras=((D_MODEL, BF16),))
    x2, act = fused_mm("ffn_out", [hf, hf], [_row_spec(FFN_HIDDEN, 0), _row_spec(FFN_HIDDEN, 1)], _swiglu_prologue,
                       lw["w_ffn_out"], tn=1024, out_dtype=F32, res=x1, extras=((FFN_HIDDEN, BF16),))
    a_pre, hn2 = fused_mm("ple_gate", [x2, lw["norm_ple"]], [_x_spec(), _g_spec()], _norm_prologue, lw["w_ple_gate"],
                          tn=1024, out_dtype=BF16, extras=((D_MODEL, BF16),))
    pp = fused_mm("ple_proj", [lw["p"]], [_row_spec(PLE_DIM)], _cast_prologue, lw["w_ple_proj"], tn=1024, out_dtype=BF16)
    x3 = ple_combine(x2, a_pre, pp)
    res = dict(x=x, z=z, h=h, y_ssm=y_ssm, ypre=ypre, h_re=h_re, h_im=h_im, hrow_re=hrow_re, hrow_im=hrow_im, y_conv=y_conv, y_attn=y_attn, merged=merged,
               x1=x1, hf=hf, hn1=hn1, act=act, x2=x2, a_pre=a_pre, hn2=hn2, pp=pp)
    return x3, res, next_gathered


def _pair_sums(split, from_sibling):
    return [pair_sum("pair_sum_" + n, a.reshape(2, -1, a.shape[-1]), b.reshape(-1, b.shape[-1])).reshape(b.shape)
            for n, a, b in zip(SHARDED, split, from_sibling)]


def layer_bwd(dx3, lw, res, bias, pending=None):
    g = {}
    wide = ((D_MODEL, BF16), (D_MODEL, BF16))
    dx2, g["norm_ple"], da, dpp = mm_norm_bwd(
        "d_ple_gate", [dx3, res["a_pre"], res["pp"]], [_row_spec(D_MODEL)] * 3, lw["w_ple_gate"], res["x2"],
        lw["norm_ple"], dx3, tm=TM, pre=_ple_bwd_pre, extras=wide)
    g["w_ple_proj"] = mm_tn("d_w_ple_proj", lw["p"], dpp)
    g["w_ple_gate"] = mm_tn("d_w_ple_gate", res["hn2"], da)
    sums = None
    if pending is None:
        dact = fused_mm("d_ffn_out", [dx2], [_x_spec()], _cast_prologue, lw["w_ffn_out"], tn=1408, out_dtype=BF16,
                        nt=True)
    else:
        dact, from_sibling = fused_mm("d_ffn_out", [dx2], [_x_spec()], _cast_prologue, lw["w_ffn_out"], tn=1408,
                                      out_dtype=BF16, side=("scatter_cores", pending), nt=True)
        sums = _pair_sums(pending, from_sibling)
    g["w_ffn_out"] = mm_tn("d_w_ffn_out", res["act"], dx2)
    half = TM // 2
    dx1, g["norm_ffn"], dh1, dh2 = mm_norm_bwd(
        "d_ffn_in", [dact, res["hf"], res["hf"]],
        [_row_spec(FFN_HIDDEN, 0, half), _row_spec(FFN_HIDDEN, 0, half), _row_spec(FFN_HIDDEN, 1, half)],
        lw["w_ffn_in"], res["x1"], lw["norm_ffn"], dx2, tm=half, pre=_swiglu_bwd_pre,
        extras=((FFN_HIDDEN, BF16), (FFN_HIDDEN, BF16)))
    g["w_ffn_in"] = jnp.concatenate([mm_tn("d_w_ffn_in_a", res["hn1"], dh1), mm_tn("d_w_ffn_in_b", res["hn1"], dh2)],
                                    axis=1)
    dmerged = fused_mm("d_out_proj", [dx1], [_x_spec()], _cast_prologue, lw["w_out"], tn=1024, out_dtype=BF16, nt=True)
    g["w_out"] = mm_tn("d_w_out", res["merged"], dx1)
    z = res["z"]
    ys = (res["y_ssm"], res["y_conv"], res["y_attn"])
    dgates, dbs = merge_bwd(dmerged, z, *ys, lw["w_branch"])
    dys, dwb = [], []
    for r in range(3):
        dys.append(fused_mm(f"d_branch_{r}", [dbs[r]], [_x_spec()], _cast_prologue, lw["w_branch"][r], tn=1024,
                            out_dtype=BF16, nt=True))
        dwb.append(mm_tn(f"d_w_branch_{r}", ys[r], dbs[r]))
    g["w_branch"] = jnp.stack(dwb)
    sc = _ssm_consts(lw)
    (du, dbb_re, dbb_im, dct_re, dct_im, da_re, da_im, g["ssm_d"], g["ssm_w_glu"]), received = ssm_bwd(
        dys[0], z, res["ypre"], res["h_re"], res["h_im"], res["hrow_re"], res["hrow_im"], sc["bbt_re"], sc["bbt_im"], sc["c_re"], sc["c_im"],
        sc["a_re"], sc["a_im"], lw["ssm_d"], lw["ssm_w_glu"], lw["ssm_w_glu_t"],
        side=None if pending is None else ("scatter_chips", sums))
    g["ssm_c_re"] = _blockdiag_extract(jnp.swapaxes(dct_re, 1, 2))
    g["ssm_c_im"] = _blockdiag_extract(jnp.swapaxes(dct_im, 1, 2))
    (g["ssm_lambda_re"], g["ssm_lambda_im"], dlog_dt, g["bt_re"], g["bt_im"]) = ssm_disc_bwd(
        lw["ssm_lambda_re"], lw["ssm_lambda_im"], lw["ssm_log_dt"].reshape(SSM_GROUPS, 1), lw["bt_re"], lw["bt_im"],
        da_re.reshape(_GN), da_im.reshape(_GN),
        _blockdiag_extract(dbb_re), _blockdiag_extract(dbb_im))
    g["ssm_log_dt"] = dlog_dt.reshape(SSM_GROUPS)
    dconv, g["conv_w"] = conv_bwd(dys[1], z, lw["conv_w"])
    dq, dkv_cur, dkv_prev, g["dbias"], g["attn_sinks"] = attn_bwd(dys[2], z, bias, lw["attn_sinks"])
    dkv = kv_shift_add(dkv_cur, dkv_prev)
    pieces = [dgates[0], dgates[1], dgates[2], du, dconv, dq, dkv]
    g["w_in"] = jnp.concatenate([mm_tn(f"d_w_in_{k}", res["h"], pc) for k, pc in enumerate(pieces)], axis=1)
    dx0, g["norm_mix"] = mm_norm_bwd("d_in_proj", pieces, [_row_spec(pc.shape[1], tm=half) for pc in pieces], lw["w_in"],
                                     res["x"], lw["norm_mix"], dx1, tm=half)
    return dx0, g, (sums, received)


def adamw(name, parts, w, m, v):
    n, R, C = parts.shape
    tr = _pick(R, (512, 256, 128, 64, 32, 16, 8))

    def body(p_ref, w_ref, m_ref, v_ref, g_ref, d_ref, nm_ref, nv_ref):
        gsum = p_ref[0].astype(F32)
        for k in range(1, n):
            gsum = gsum + p_ref[k].astype(F32)
        mn = ADAM_B1 * m_ref[...] + (1.0 - ADAM_B1) * gsum
        vn = ADAM_B2 * v_ref[...] + (1.0 - ADAM_B2) * jnp.square(gsum)
        m_hat = mn / (1.0 - ADAM_B1 ** ADAM_STEP)
        v_hat = vn / (1.0 - ADAM_B2 ** ADAM_STEP)
        g_ref[...] = gsum
        d_ref[...] = -ADAM_LR * (m_hat / (jnp.sqrt(v_hat) + ADAM_EPS) + ADAM_WD * w_ref[...])
        nm_ref[...] = mn
        nv_ref[...] = vn

    blk = pl.BlockSpec((tr, C), lambda i: (i, 0))
    return pl.pallas_call(
        body, name=name, grid=(R // tr,), in_specs=[pl.BlockSpec((n, tr, C), lambda i: (0, i, 0)), blk, blk, blk],
        out_specs=[blk] * 4, out_shape=[jax.ShapeDtypeStruct((R, C), F32)] * 4, compiler_params=_cparams("parallel"),
    )(parts, w, m, v)


_ANY = pl.BlockSpec(memory_space=pl.ANY)


def _coords():
    return lax.axis_index("x"), lax.axis_index("y"), lax.axis_index("c")


def _chip_peers(x, y):
    return [(1 - x, y), (x, 1 - y), (1 - x, 1 - y)]


def _gather_chips_copies(x_refs, out_refs, send_sems, recv_sems):
    x, y, c = _coords()
    me = 2 * x + y
    peers = _chip_peers(x, y)

    def copy(i, k, slot):
        return pltpu.make_async_remote_copy(
            src_ref=x_refs[i], dst_ref=out_refs[i].at[slot], send_sem=send_sems.at[3 * i + k],
            recv_sem=recv_sems.at[3 * i + k], device_id=(*peers[k], c), device_id_type=MESH)

    n = len(x_refs)
    sends = [copy(i, k, me) for i in range(n) for k in range(3)]
    recvs = [copy(i, k, 2 * px + py) for i in range(n) for k, (px, py) in enumerate(peers)]
    return sends, recvs


def _gather_cores_copies(x_refs, out_refs, send_sems, recv_sems):
    x, y, c = _coords()

    def copy(i, slot):
        return pltpu.make_async_remote_copy(
            src_ref=x_refs[i], dst_ref=out_refs[i].at[slot], send_sem=send_sems.at[i], recv_sem=recv_sems.at[i],
            device_id=(x, y, 1 - c), device_id_type=MESH)

    n = len(x_refs)
    return [copy(i, c) for i in range(n)], [copy(i, 1 - c) for i in range(n)]


def _scatter_cores_copies(x_refs, out_refs, send_sems, recv_sems):
    x, y, c = _coords()
    copies = [pltpu.make_async_remote_copy(
        src_ref=x_refs[i].at[1 - c], dst_ref=out_refs[i], send_sem=send_sems.at[i], recv_sem=recv_sems.at[i],
        device_id=(x, y, 1 - c), device_id_type=MESH) for i in range(len(x_refs))]
    return copies, copies


def _scatter_chips_copies(x_refs, out_refs, send_sems, recv_sems):
    x, y, c = _coords()
    me = 2 * x + y
    peers = _chip_peers(x, y)

    def copy(i, k, src_slot, dst_slot):
        return pltpu.make_async_remote_copy(
            src_ref=x_refs[i].at[src_slot], dst_ref=out_refs[i].at[dst_slot], send_sem=send_sems.at[3 * i + k],
            recv_sem=recv_sems.at[3 * i + k], device_id=(*peers[k], c), device_id_type=MESH)

    n = len(x_refs)
    sends = [copy(i, k, 2 * px + py, me) for i in range(n) for k, (px, py) in enumerate(peers)]
    recvs = [copy(i, k, me, 2 * px + py) for i in range(n) for k, (px, py) in enumerate(peers)]
    return sends, recvs


_EXCHANGES = {
    "gather_chips": (lambda b: (4,) + b.shape, 3, _gather_chips_copies),
    "gather_cores": (lambda b: (2,) + b.shape, 1, _gather_cores_copies),
    "scatter_cores": (lambda b: b.shape[1:], 1, _scatter_cores_copies),
    "scatter_chips": (lambda b: b.shape, 3, _scatter_chips_copies),
}


def exchange(name, kind, blks):
    out_shape_of, per_block, make_copies = _EXCHANGES[kind]
    n = len(blks)

    def body(*refs):
        sends, recvs = make_copies(refs[:n], refs[n:2 * n], refs[2 * n], refs[2 * n + 1])
        for cp in sends:
            cp.start()
        for cp in recvs:
            cp.wait_recv()
        for cp in sends:
            cp.wait_send()

    return pl.pallas_call(
        body, name=name, in_specs=[_ANY] * n, out_specs=[_ANY] * n,
        out_shape=[jax.ShapeDtypeStruct(out_shape_of(b), b.dtype) for b in blks],
        scratch_shapes=[pltpu.SemaphoreType.DMA((per_block * n,)), pltpu.SemaphoreType.DMA((per_block * n,))],
    )(*blks)


def _put_slot(buf, block, idx):
    return lax.dynamic_update_slice(buf, block[None].astype(buf.dtype), (idx,) + (0,) * block.ndim)


def pair_sum(name, mine, theirs):
    _, R, C = mine.shape
    tr = _pick(R, (1024, 512, 256, 128, 64, 32, 16))
    c_idx = lax.axis_index("c").astype(jnp.int32).reshape(1)

    def body(c_ref, a_ref, b_ref, o_ref):
        o_ref[...] = (a_ref[0].astype(F32) + b_ref[...].astype(F32)).astype(BF16)

    return pl.pallas_call(
        body, name=name,
        grid_spec=pltpu.PrefetchScalarGridSpec(
            num_scalar_prefetch=1, grid=(R // tr,),
            in_specs=[pl.BlockSpec((1, tr, C), lambda i, c: (c[0], i, 0)), pl.BlockSpec((tr, C), lambda i, c: (i, 0))],
            out_specs=pl.BlockSpec((tr, C), lambda i, c: (i, 0))),
        out_shape=jax.ShapeDtypeStruct(theirs.shape, BF16), compiler_params=_cparams("parallel"),
    )(c_idx, mine, theirs)


SHARDED = {
    "w_in": ((D_MODEL, IN_WIDTH), 2), "ssm_w_glu": ((BRANCH, BRANCH), 1), "conv_w": ((3, BRANCH), 2),
    "w_branch": ((3, BRANCH, D_MODEL), 3), "w_out": ((D_MODEL, D_MODEL), 1), "w_ffn_in": ((D_MODEL, 2 * FFN_HIDDEN), 2),
    "w_ffn_out": ((FFN_HIDDEN, D_MODEL), 1), "w_ple_gate": ((D_MODEL, D_MODEL), 1), "w_ple_proj": ((PLE_DIM, D_MODEL), 2),
}
SMALL = ["rel_bias", "norm_mix", "ssm_lambda_re", "ssm_lambda_im", "ssm_b_re", "ssm_b_im", "ssm_c_re", "ssm_c_im", "ssm_d",
         "ssm_log_dt", "attn_sinks", "norm_ffn", "norm_ple", "norm_final"]
GATHER_SPLIT = 4
WEIGHTS = ["rel_bias", "norm_mix", "w_in", "ssm_lambda_re", "ssm_lambda_im", "ssm_b_re", "ssm_b_im", "ssm_c_re", "ssm_c_im",
           "ssm_d", "ssm_log_dt", "ssm_w_glu", "conv_w", "attn_sinks", "w_branch", "w_out", "norm_ffn", "w_ffn_in",
           "w_ffn_out", "norm_ple", "w_ple_gate", "w_ple_proj", "norm_final"]


def _pad_to(flat, n):
    return jnp.pad(flat, [(0, 0)] * (flat.ndim - 1) + [(0, n - flat.shape[-1])])


def _unshard(g8, name):
    axis = SHARDED[name][1] - 1
    shard = g8.shape[2:]
    b = g8.reshape((2, 2, 2) + shard)
    b = jnp.moveaxis(b, (1, 2, 0), (axis, axis + 1, axis + 2))
    full = list(shard)
    full[axis] *= N_DEV
    return b.reshape(full)


def _shard_split(full, name):
    axis = SHARDED[name][1] - 1
    dims = list(full.shape)
    dims[axis:axis + 1] = [2, 2, 2, dims[axis] // N_DEV]
    b = jnp.moveaxis(full.reshape(dims), (axis, axis + 1, axis + 2), (1, 2, 0))
    return b.reshape((2, 4) + b.shape[3:])


def _small_sizes(shapes):
    return [-(-int(np.prod(shapes[n])) // 128) * 128 for n in SMALL]


def pack_small(vals, shapes, extra):
    segs = [_pad_to(vals[n].reshape(-1).astype(F32), s) for n, s in zip(SMALL, _small_sizes(shapes))]
    segs.append(_pad_to(extra.reshape(-1), 128))
    flat = jnp.concatenate(segs)
    rows = -(-flat.shape[0] // (128 * 8)) * 8
    return _pad_to(flat, rows * 128).reshape(rows, 128)


def unpack_small(packed, shapes):
    flat = packed.reshape(-1)
    out, off = {}, 0
    for n, s in zip(SMALL, _small_sizes(shapes)):
        out[n] = flat[off:off + int(np.prod(shapes[n]))].reshape(shapes[n])
        off += s
    return out, flat[off]


def _layer_weights(gathered, small, p, i):
    full = {n: _unshard(g, n) for n, g in zip(SHARDED, gathered)}
    w_in = full["w_in"]
    w_in_p = jnp.concatenate([w_in[:, 2816:], w_in[:, :2560], w_in[:, 2560:2816]], axis=-1)
    return dict(
        w_in=w_in_p,
        ssm_w_glu=full["ssm_w_glu"], ssm_w_glu_t=full["ssm_w_glu"].T,
        conv_w=full["conv_w"],
        w_branch=full["w_branch"], w_out=full["w_out"], w_ffn_in=full["w_ffn_in"], w_ffn_out=full["w_ffn_out"],
        w_ple_gate=full["w_ple_gate"],
        w_ple_proj=full["w_ple_proj"],
        norm_mix=small["norm_mix"][i][None, :], norm_ffn=small["norm_ffn"][i][None, :],
        norm_ple=small["norm_ple"][i][None, :],
        ssm_lambda_re=small["ssm_lambda_re"][i], ssm_lambda_im=small["ssm_lambda_im"][i],
        ssm_log_dt=small["ssm_log_dt"][i],
        bt_re=jnp.swapaxes(small["ssm_b_re"][i], 1, 2), bt_im=jnp.swapaxes(small["ssm_b_im"][i], 1, 2),
        ssm_c_re=small["ssm_c_re"][i], ssm_c_im=small["ssm_c_im"][i], ssm_d=small["ssm_d"][i][None, :],
        attn_sinks=small["attn_sinks"][i], p=p[i],
    )


def matrix_grads(g):
    w_in_g = g["w_in"]
    return dict(
        w_in=jnp.concatenate([w_in_g[:, 3072:5632], w_in_g[:, 5632:], w_in_g[:, :3072]], axis=-1),
        ssm_w_glu=g["ssm_w_glu"], conv_w=jnp.sum(g["conv_w"].reshape(3, 8, BRANCH), axis=1),
        w_branch=g["w_branch"], w_out=g["w_out"], w_ffn_in=g["w_ffn_in"], w_ffn_out=g["w_ffn_out"],
        w_ple_gate=g["w_ple_gate"], w_ple_proj=g["w_ple_proj"])


def small_grads(per_layer, dg_final):
    keys = ("dbias", "norm_mix", "ssm_lambda_re", "ssm_lambda_im", "bt_re", "bt_im", "ssm_c_re", "ssm_c_im", "ssm_d",
            "ssm_log_dt", "attn_sinks", "norm_ffn", "norm_ple")
    g = {k: jnp.stack([gl[k] for gl in per_layer]) for k in keys}
    drel = rel_bias_bwd(g["dbias"].reshape(DEPTH, N_Q_HEADS, BLOCK * 2 * BLOCK)).T
    return dict(
        rel_bias=drel, norm_mix=jnp.sum(g["norm_mix"], axis=1), ssm_lambda_re=g["ssm_lambda_re"],
        ssm_lambda_im=g["ssm_lambda_im"], ssm_b_re=jnp.swapaxes(g["bt_re"], 2, 3), ssm_b_im=jnp.swapaxes(g["bt_im"], 2, 3),
        ssm_c_re=g["ssm_c_re"], ssm_c_im=g["ssm_c_im"], ssm_d=g["ssm_d"][:, 0, :], ssm_log_dt=g["ssm_log_dt"],
        attn_sinks=g["attn_sinks"][:, :, 0], norm_ffn=jnp.sum(g["norm_ffn"], axis=1), norm_ple=jnp.sum(g["norm_ple"], axis=1),
        norm_final=jnp.sum(dg_final, axis=0))


def kernel(x, p, rel_bias, norm_mix, w_in, ssm_lambda_re, ssm_lambda_im, ssm_b_re, ssm_b_im, ssm_c_re, ssm_c_im, ssm_d, ssm_log_dt, ssm_w_glu, conv_w, attn_sinks, w_branch, w_out, norm_ffn, w_ffn_in, w_ffn_out, norm_ple, w_ple_gate, w_ple_proj, norm_final, loss_target, m_rel_bias, m_norm_mix, m_w_in, m_ssm_lambda_re, m_ssm_lambda_im, m_ssm_b_re, m_ssm_b_im, m_ssm_c_re, m_ssm_c_im, m_ssm_d, m_ssm_log_dt, m_ssm_w_glu, m_conv_w, m_attn_sinks, m_w_branch, m_w_out, m_norm_ffn, m_w_ffn_in, m_w_ffn_out, m_norm_ple, m_w_ple_gate, m_w_ple_proj, m_norm_final, v_rel_bias, v_norm_mix, v_w_in, v_ssm_lambda_re, v_ssm_lambda_im, v_ssm_b_re, v_ssm_b_im, v_ssm_c_re, v_ssm_c_im, v_ssm_d, v_ssm_log_dt, v_ssm_w_glu, v_conv_w, v_attn_sinks, v_w_branch, v_w_out, v_norm_ffn, v_w_ffn_in, v_w_ffn_out, v_norm_ple, v_w_ple_gate, v_w_ple_proj, v_norm_final):
    args = dict(locals())
    w = {n: args[n] for n in WEIGHTS}
    m = {n: args["m_" + n] for n in WEIGHTS}
    v = {n: args["v_" + n] for n in WEIGHTS}
    shapes = {n: w[n].shape for n in SMALL}

    x_i, y_i, c_i = _coords()
    chip = 2 * x_i + y_i
    place = (chip, c_i)
    small = {n: w[n] for n in SMALL}

    def all_gather(tag, blks):
        g4 = exchange(f"gather_{tag}_chips", "gather_chips", blks)
        g4 = [_put_slot(g, b, chip) for g, b in zip(g4, blks)]
        g8 = exchange(f"gather_{tag}_cores", "gather_cores", g4)
        return [_put_slot(g, b, c_i) for g, b in zip(g8, g4)]

    def shards_of(layer):
        return [w[n][layer] if n == "conv_w" else w[n][layer].astype(BF16) for n in SHARDED]

    bias = rel_bias_fwd(small["rel_bias"].T).reshape(N_Q_HEADS, BLOCK, 2 * BLOCK)
    xs, layers, res = x[0], [], []
    gathered = all_gather("w", shards_of(0))
    for layer in range(DEPTH):
        layers.append(_layer_weights(gathered, small, p[:, 0], layer))
        nxt = shards_of(layer + 1) if layer + 1 < DEPTH else None
        xs, res_l, gathered = layer_fwd(xs, layers[layer], bias, nxt, place)
        res.append(res_l)
    grad_x, loss_parts, dg_final = loss_head(xs, small["norm_final"][None, :], loss_target[0])

    per_layer, reduced, pending = [None] * DEPTH, [None] * DEPTH, None
    for layer in reversed(range(DEPTH)):
        grad_x, per_layer[layer], done = layer_bwd(grad_x, layers[layer], res[layer], bias, pending)
        if pending is not None:
            reduced[layer + 1] = done
        mg = matrix_grads(per_layer[layer])
        pending = [_shard_split(mg[n], n).astype(BF16) for n in SHARDED]
    sums = _pair_sums(pending, exchange("scatter_g_cores", "scatter_cores", pending))
    reduced[0] = (sums, exchange("scatter_g_chips", "scatter_chips", sums))

    outs = ({}, {}, {}, {})
    for k, name in enumerate(SHARDED):
        parts = jnp.stack([_put_slot(rcv[k], lax.dynamic_index_in_dim(sm[k], chip, 0, keepdims=False), chip)
                           for sm, rcv in reduced], axis=1)
        cols = parts.shape[-1]
        res4 = adamw("adamw_" + name, parts.reshape(4, -1, cols), w[name].reshape(-1, cols), m[name].reshape(-1, cols),
                     v[name].reshape(-1, cols))
        for d, o in zip(outs, res4):
            d[name] = o.reshape(w[name].shape)

    small_local = pack_small(small_grads(per_layer, dg_final), shapes, jnp.sum(loss_parts))
    small_all = all_gather("s", [small_local])[0]
    zero = jnp.zeros((1,), F32)
    res4 = adamw("adamw_small", small_all.reshape(N_DEV, small_local.shape[0], 128),
                 pack_small({n: w[n] for n in SMALL}, shapes, zero), pack_small({n: m[n] for n in SMALL}, shapes, zero),
                 pack_small({n: v[n] for n in SMALL}, shapes, zero))
    loss = None
    for d, r in zip(outs, res4):
        vals, extra = unpack_small(r, shapes)
        d.update(vals)
        if loss is None:
            loss = extra

    return (loss, grad_x[None], *[d[n] for d in outs for n in WEIGHTS])
```

```python
import functools
import math

import numpy as np
import jax
import jax.numpy as jnp
from jax import lax
from jax.experimental import pallas as pl
from jax.experimental.pallas import tpu as pltpu

F32 = jnp.float32
BF16 = jnp.bfloat16
MESH = pl.DeviceIdType.MESH

D_MODEL = 1024
DEPTH = 4
PLE_DIM = 256
BRANCH = 512
SSM_GROUPS = 32
SSM_GROUP = 16
SSM_STATE = 64
SSM_LANES = SSM_GROUPS * SSM_STATE
SSM_SUB = 4
SUB_IN = BRANCH // SSM_SUB
SUB_ST = SSM_LANES // SSM_SUB
HEAD_DIM = 64
N_Q_HEADS = 8
N_KV_HEADS = 2
GQA_GROUP = 4
KV_WIDTH = 2 * N_KV_HEADS * HEAD_DIM
WINDOW = 128
BLOCK = 128
ATTN_SCALE = 1.0 / math.sqrt(HEAD_DIM)
REL_BUCKETS = 32
REL_MAX_DIST = 128
FFN_HIDDEN = 2816
RMS_EPS = 1e-6
IN_WIDTH = 5888
N_DEV = 8

ADAM_LR = 0.001
ADAM_B1 = 0.9
ADAM_B2 = 0.999
ADAM_EPS = 1e-08
ADAM_WD = 0.01
ADAM_STEP = 10

COL_U = 3072
COL_KV = 5632
NEG = -1e30

SCAN_T = 256
TM = 512
TM_W = 1024
VMEM_LIMIT = 52 * 1024 * 1024


def _cparams(*sem):
    return pltpu.CompilerParams(dimension_semantics=sem, vmem_limit_bytes=VMEM_LIMIT)


def _full(shape):
    n = len(shape)
    return pl.BlockSpec(shape, lambda *_: (0,) * n)


def _pick(n, cands):
    for c in cands:
        if n % c == 0:
            return c
    return n


def _call(body, *, name, steps, in_specs, out_specs, out_shape, scratch, args, side=None):
    in_specs, out_specs, out_shape = list(in_specs), list(out_specs), list(out_shape)
    scratch, args = list(scratch), list(args)
    n_in, n_out, n_scr = len(in_specs), len(out_specs), len(scratch)
    n = 0
    if side is not None:
        kind, blks = side
        out_shape_of, per_block, make_copies = _EXCHANGES[kind]
        n = len(blks)
        inner = body

        def body(*refs):
            ins, sx = refs[:n_in], refs[n_in:n_in + n]
            outs, so = refs[n_in + n:n_in + n + n_out], refs[n_in + n + n_out:n_in + 2 * n + n_out]
            scr = refs[n_in + 2 * n + n_out:n_in + 2 * n + n_out + n_scr]
            send_sems, recv_sems = refs[-2:]

            @pl.when(pl.program_id(0) == 0)
            def _():
                sends, _ = make_copies(sx, so, send_sems, recv_sems)
                for cp in sends:
                    cp.start()

            inner(*ins, *outs, *scr)

            @pl.when(pl.program_id(0) == steps - 1)
            def _():
                sends, recvs = make_copies(sx, so, send_sems, recv_sems)
                for cp in recvs:
                    cp.wait_recv()
                for cp in sends:
                    cp.wait_send()

        in_specs += [_ANY] * n
        args += list(blks)
        out_specs += [_ANY] * n
        out_shape += [jax.ShapeDtypeStruct(out_shape_of(b), b.dtype) for b in blks]
        scratch += [pltpu.SemaphoreType.DMA((per_block * n,)), pltpu.SemaphoreType.DMA((per_block * n,))]
    outs = pl.pallas_call(
        body, name=name, grid=(steps,), in_specs=in_specs, out_specs=out_specs, out_shape=out_shape,
        scratch_shapes=scratch, compiler_params=_cparams("arbitrary"),
    )(*args)
    return outs[:n_out], outs[n_out:]


def _dot(a, b):
    return jnp.dot(a, b, preferred_element_type=F32)


def _dot_tn(a, b):
    return lax.dot_general(a, b, (((0,), (0,)), ((), ())), preferred_element_type=F32)


def _dot_nt(a, b):
    return lax.dot_general(a, b, (((1,), (1,)), ((), ())), preferred_element_type=F32)


def _rms(x, g):
    r = lax.rsqrt(jnp.mean(x * x, axis=-1, keepdims=True) + RMS_EPS)
    return x * r * g


def fused_mm(name, ins, in_specs, prologue, w, *, tn, out_dtype, res=None, extras=(), side=None, nt=False):
    S = ins[0].shape[0]
    N, K = w.shape if nt else w.shape[::-1]
    tn = min(tn, N)
    n_in, n_ex = len(ins), len(extras)

    def body(*refs):
        in_refs = refs[:n_in]
        w_ref = refs[n_in]
        pos = n_in + 1
        res_ref = None
        if res is not None:
            res_ref = refs[pos]
            pos += 1
        o_ref = refs[pos]
        ex_refs = refs[pos + 1:pos + 1 + n_ex]
        a_scr = refs[-1]
        out = prologue(*[r[...] for r in in_refs])
        a_scr[...] = out[0]
        for r, e in zip(ex_refs, out[1:]):
            r[...] = e.astype(r.dtype)
        for j in range(N // tn):
            cs = slice(j * tn, (j + 1) * tn)
            acc = _dot_nt(a_scr[...], w_ref[cs, :]) if nt else _dot(a_scr[...], w_ref[:, cs])
            if res_ref is not None:
                acc = acc + res_ref[:, cs]
            o_ref[:, cs] = acc.astype(o_ref.dtype)

    specs = list(in_specs) + [pl.BlockSpec(w.shape, lambda i: (0, 0), pipeline_mode=pl.Buffered(1))]
    args = list(ins) + [w]
    if res is not None:
        specs.append(pl.BlockSpec((TM, N), lambda i: (i, 0)))
        args.append(res)
    out_shape = [jax.ShapeDtypeStruct((S, N), out_dtype)]
    out_specs = [pl.BlockSpec((TM, N), lambda i: (i, 0))]
    for cols, dt in extras:
        out_shape.append(jax.ShapeDtypeStruct((S, cols), dt))
        out_specs.append(pl.BlockSpec((TM, cols), lambda i: (i, 0)))
    outs, side_outs = _call(body, name=name, steps=S // TM, in_specs=specs, out_specs=out_specs, out_shape=out_shape,
                            scratch=[pltpu.VMEM((TM, K), BF16)], args=args, side=side)
    result = outs if n_ex else outs[0]
    return result if side is None else (result, side_outs)


def _row_spec(cols, blk=0, tm=TM):
    return pl.BlockSpec((tm, cols), lambda i: (i, blk))


def mm_norm_bwd(name, ins, in_specs, w, x, g, dres, *, tm, pre=None, extras=()):
    S = x.shape[0]
    n = len(ins)

    def body(*refs):
        x_ref, g_ref, dres_ref, w_ref, dx_ref, dg_ref = refs[n:n + 6]

        @pl.when(pl.program_id(0) == 0)
        def _():
            dg_ref[...] = jnp.zeros_like(dg_ref)

        tiles = [r[...] for r in refs[:n]]
        if pre is not None:
            tiles, extra_tiles = pre(*tiles)
            for r, e in zip(refs[n + 6:], extra_tiles):
                r[...] = e.astype(r.dtype)
        tiles = [t.astype(BF16) for t in tiles]
        a = tiles[0] if len(tiles) == 1 else jnp.concatenate(tiles, axis=1)
        dh = _dot_nt(a, w_ref[...])
        xv = x_ref[...]
        r = lax.rsqrt(jnp.mean(xv * xv, axis=-1, keepdims=True) + RMS_EPS)
        xhat = xv * r
        dxhat = dh * g_ref[...]
        dx_ref[...] = dres_ref[...] + r * (dxhat - xhat * jnp.mean(dxhat * xhat, axis=-1, keepdims=True))
        dg_ref[...] += jnp.sum((dh * xhat).reshape(tm // 8, 8, D_MODEL), axis=0)

    row = _row_spec(D_MODEL, tm=tm)
    return pl.pallas_call(
        body, name=name, grid=(S // tm,),
        in_specs=list(in_specs)
        + [row, _full((1, D_MODEL)), row, pl.BlockSpec(w.shape, lambda i: (0, 0), pipeline_mode=pl.Buffered(1))],
        out_specs=[row, _full((8, D_MODEL))] + [_row_spec(cols, tm=tm) for cols, _ in extras],
        out_shape=[jax.ShapeDtypeStruct((S, D_MODEL), F32), jax.ShapeDtypeStruct((8, D_MODEL), F32)]
        + [jax.ShapeDtypeStruct((S, cols), dt) for cols, dt in extras],
        compiler_params=_cparams("arbitrary"),
    )(*ins, x, g, dres, w)


def mm_tn_multi(name, a, pieces):
    S, K = a.shape
    n = len(pieces)

    def body(a_ref, *refs):
        @pl.when(pl.program_id(0) == 0)
        def _():
            for o_ref in refs[n:]:
                o_ref[...] = jnp.zeros_like(o_ref)

        at = a_ref[...].astype(BF16)
        for p_ref, o_ref in zip(refs[:n], refs[n:]):
            o_ref[...] += _dot_tn(at, p_ref[...].astype(BF16))

    return pl.pallas_call(
        body, name=name, grid=(S // TM_W,),
        in_specs=[_row_spec(K, tm=TM_W)] + [_row_spec(p.shape[1], tm=TM_W) for p in pieces],
        out_specs=[_full((K, p.shape[1])) for p in pieces],
        out_shape=[jax.ShapeDtypeStruct((K, p.shape[1]), F32) for p in pieces],
        compiler_params=_cparams("arbitrary"),
    )(a, *pieces)


def mm_tn(name, a, b):
    S, K = a.shape
    N = b.shape[1]
    tk = _pick(K, (1024, 1408, 512, 256))
    tn = _pick(N, (1024, 1408, 1536, 512, 256))

    def body(a_ref, b_ref, o_ref):
        @pl.when(pl.program_id(2) == 0)
        def _():
            o_ref[...] = jnp.zeros_like(o_ref)

        o_ref[...] += _dot_tn(a_ref[...].astype(BF16), b_ref[...].astype(BF16))

    return pl.pallas_call(
        body, name=name, grid=(K // tk, N // tn, S // TM_W),
        in_specs=[pl.BlockSpec((TM_W, tk), lambda k, n, s: (s, k)), pl.BlockSpec((TM_W, tn), lambda k, n, s: (s, n))],
        out_specs=pl.BlockSpec((tk, tn), lambda k, n, s: (k, n)),
        out_shape=jax.ShapeDtypeStruct((K, N), F32),
        compiler_params=_cparams("parallel", "parallel", "arbitrary"),
    )(a, b)


def _swiglu_bwd_pre(dact, h1, h2):
    h1 = h1.astype(F32)
    h2 = h2.astype(F32)
    da = dact.astype(F32)
    sg = jax.nn.sigmoid(h1)
    halves = [(da * h2 * sg * (1.0 + h1 * (1.0 - sg))).astype(BF16), (da * h1 * sg).astype(BF16)]
    return halves, halves


def _ple_bwd_pre(dx, a_pre, pp):
    pg = jax.nn.sigmoid(a_pre.astype(F32))
    da = (dx * pp.astype(F32) * pg * (1.0 - pg)).astype(BF16)
    return [da], [da, (dx * pg).astype(BF16)]


def ple_combine(x2, a_pre, pp):
    S = x2.shape[0]

    def body(x_ref, a_ref, p_ref, o_ref):
        o_ref[...] = x_ref[...] + jax.nn.sigmoid(a_ref[...].astype(F32)) * p_ref[...].astype(F32)

    row = pl.BlockSpec((TM, D_MODEL), lambda i: (i, 0))
    return pl.pallas_call(
        body, name="ple_combine", grid=(S // TM,), in_specs=[row, row, row], out_specs=row,
        out_shape=jax.ShapeDtypeStruct((S, D_MODEL), F32), compiler_params=_cparams("parallel"),
    )(x2, a_pre, pp)


def _gate_specs(tn, nn):
    return [pl.BlockSpec((TM, tn), functools.partial(lambda i, j, r: (i, r * nn + j), r=r)) for r in range(3)]


def merge_fwd(z, y_ssm, y_conv, y_attn, wb):
    S = z.shape[0]
    tn = 512
    nn = D_MODEL // tn

    def body(g0, g1, g2, y0, y1, y2, w_ref, o_ref):
        acc = jnp.zeros((TM, tn), F32)
        for r, (g_ref, y_ref) in enumerate(((g0, y0), (g1, y1), (g2, y2))):
            acc += jax.nn.sigmoid(g_ref[...].astype(F32)) * _dot(y_ref[...], w_ref[r])
        o_ref[...] = acc.astype(BF16)

    y_spec = pl.BlockSpec((TM, BRANCH), lambda i, j: (i, 0))
    return pl.pallas_call(
        body, name="merge_fwd", grid=(S // TM, nn),
        in_specs=_gate_specs(tn, nn) + [y_spec] * 3 + [pl.BlockSpec((3, BRANCH, tn), lambda i, j: (0, 0, j))],
        out_specs=pl.BlockSpec((TM, tn), lambda i, j: (i, j)),
        out_shape=jax.ShapeDtypeStruct((S, D_MODEL), BF16), compiler_params=_cparams("parallel", "parallel"),
    )(z, z, z, y_ssm, y_conv, y_attn, wb)


def merge_bwd(dmerged, z, y_ssm, y_conv, y_attn, wb):
    S = z.shape[0]
    tn = 512
    nn = D_MODEL // tn

    def body(dm_ref, g0, g1, g2, y0, y1, y2, w_ref, dg0, dg1, dg2, db0, db1, db2):
        dm = dm_ref[...].astype(F32)
        for r, (g_ref, y_ref, dg_ref, db_ref) in enumerate(((g0, y0, dg0, db0), (g1, y1, dg1, db1), (g2, y2, dg2, db2))):
            sg = jax.nn.sigmoid(g_ref[...].astype(F32))
            b = _dot(y_ref[...], w_ref[r])
            dg_ref[...] = (dm * b * sg * (1.0 - sg)).astype(BF16)
            db_ref[...] = (dm * sg).astype(BF16)

    y_spec = pl.BlockSpec((TM, BRANCH), lambda i, j: (i, 0))
    outs = pl.pallas_call(
        body, name="merge_bwd", grid=(S // TM, nn),
        in_specs=[pl.BlockSpec((TM, tn), lambda i, j: (i, j))] + _gate_specs(tn, nn) + [y_spec] * 3
        + [pl.BlockSpec((3, BRANCH, tn), lambda i, j: (0, 0, j))],
        out_specs=[pl.BlockSpec((TM, tn), lambda i, j: (i, j))] * 6,
        out_shape=[jax.ShapeDtypeStruct((S, D_MODEL), BF16)] * 6, compiler_params=_cparams("parallel", "parallel"),
    )(dmerged, z, z, z, y_ssm, y_conv, y_attn, wb)
    return outs[:3], outs[3:]


def _shift_down(v, halo, k):
    rolled = pltpu.roll(v, k, 0)
    h = pltpu.roll(halo, k, 0)
    row = lax.broadcasted_iota(jnp.int32, v.shape, 0)
    head = jnp.concatenate([h, jnp.zeros((v.shape[0] - 8, v.shape[1]), v.dtype)], axis=0)
    return jnp.where(row < k, head, rolled)


def _shift_up(v, halo, k):
    n = v.shape[0]
    rolled = pltpu.roll(v, n - k, 0)
    h = pltpu.roll(halo, 8 - k, 0)
    row = lax.broadcasted_iota(jnp.int32, v.shape, 0)
    tail = jnp.concatenate([jnp.zeros((n - 8, v.shape[1]), v.dtype), h], axis=0)
    return jnp.where(row >= n - k, tail, rolled)


def _conv_specs():
    rb = TM // 8
    c0 = COL_U // BRANCH

    def cur(k):
        return pl.BlockSpec((TM, BRANCH), lambda i: (i, c0 + k))

    def prev(k):
        return pl.BlockSpec((8, BRANCH), lambda i: (jnp.maximum(i * rb - 1, 0), c0 + k))

    return [cur(1), cur(2), cur(3), prev(2), prev(3)]


def conv_fwd(z, conv_w):
    S = z.shape[0]

    def body(cb_ref, cc_ref, cx_ref, pc_ref, px_ref, w_ref, o_ref):
        first = pl.program_id(0) == 0
        v = cc_ref[...].astype(F32) * cx_ref[...].astype(F32)
        pv = jnp.where(first, 0.0, pc_ref[...].astype(F32) * px_ref[...].astype(F32))
        w = w_ref[...]
        y = w[2:3] * v + w[1:2] * _shift_down(v, pv, 1) + w[0:1] * _shift_down(v, pv, 2)
        o_ref[...] = (cb_ref[...].astype(F32) * y).astype(BF16)

    return pl.pallas_call(
        body, name="conv_fwd", grid=(S // TM,), in_specs=_conv_specs() + [_full((3, BRANCH))],
        out_specs=pl.BlockSpec((TM, BRANCH), lambda i: (i, 0)),
        out_shape=jax.ShapeDtypeStruct((S, BRANCH), BF16), compiler_params=_cparams("parallel"),
    )(z, z, z, z, z, conv_w)


def conv_bwd(dy, z, conv_w):
    S = z.shape[0]
    rb = TM // 8
    nt = S // TM
    c0 = COL_U // BRANCH

    def body(dy_ref, cb_ref, cc_ref, cx_ref, pc_ref, px_ref, ndy_ref, ncb_ref, w_ref, o_ref, dw_ref):
        i = pl.program_id(0)

        @pl.when(i == 0)
        def _():
            dw_ref[...] = jnp.zeros_like(dw_ref)

        cb = cb_ref[...].astype(F32)
        cc = cc_ref[...].astype(F32)
        cx = cx_ref[...].astype(F32)
        dyv = dy_ref[...].astype(F32)
        v = cc * cx
        pv = jnp.where(i == 0, 0.0, pc_ref[...].astype(F32) * px_ref[...].astype(F32))
        v1 = _shift_down(v, pv, 1)
        v2 = _shift_down(v, pv, 2)
        w = w_ref[...]
        conv = w[2:3] * v + w[1:2] * v1 + w[0:1] * v2
        dc = dyv * cb
        ndc = jnp.where(i == nt - 1, 0.0, ndy_ref[...].astype(F32) * ncb_ref[...].astype(F32))
        dv = w[2:3] * dc + w[1:2] * _shift_up(dc, ndc, 1) + w[0:1] * _shift_up(dc, ndc, 2)
        o_ref[:, 0:BRANCH] = (dyv * conv).astype(BF16)
        o_ref[:, BRANCH:2 * BRANCH] = (dv * cx).astype(BF16)
        o_ref[:, 2 * BRANCH:3 * BRANCH] = (dv * cc).astype(BF16)
        for k, vk in enumerate((v2, v1, v)):
            dw_ref[8 * k:8 * k + 8, :] += jnp.sum((dc * vk).reshape(rb, 8, BRANCH), axis=0)

    nxt = jnp.minimum

    return pl.pallas_call(
        body, name="conv_bwd", grid=(nt,),
        in_specs=[pl.BlockSpec((TM, BRANCH), lambda i: (i, 0))] + _conv_specs()
        + [pl.BlockSpec((8, BRANCH), lambda i: (nxt((i + 1) * rb, S // 8 - 1), 0)),
           pl.BlockSpec((8, BRANCH), lambda i: (nxt((i + 1) * rb, S // 8 - 1), c0 + 1)),
           _full((3, BRANCH))],
        out_specs=[pl.BlockSpec((TM, 3 * BRANCH), lambda i: (i, 0)), _full((24, BRANCH))],
        out_shape=[jax.ShapeDtypeStruct((S, 3 * BRANCH), BF16), jax.ShapeDtypeStruct((24, BRANCH), F32)],
        compiler_params=_cparams("arbitrary"),
    )(dy, z, z, z, z, z, dy, z, conv_w)


def _bucket_onehot_t():
    qi = np.arange(BLOCK)[:, None]
    kj = np.arange(2 * BLOCK)[None, :]
    dist = np.clip(qi + BLOCK - kj, 0, REL_MAX_DIST - 1)
    exact = REL_BUCKETS // 2
    df = np.maximum(dist, 1).astype(np.float32)
    large = exact + (np.log(df / np.float32(exact)) / np.float32(math.log(REL_MAX_DIST / exact))
                     * np.float32(REL_BUCKETS - exact)).astype(np.int32)
    large = np.minimum(large, REL_BUCKETS - 1)
    bucket = np.where(dist < exact, dist, large).reshape(-1)
    return (np.arange(REL_BUCKETS)[:, None] == bucket[None, :]).astype(np.float32)


def rel_bias_fwd(rel_bias_t):
    n = BLOCK * 2 * BLOCK

    def body(r_ref, oh_ref, o_ref):
        o_ref[...] = jnp.dot(r_ref[...], oh_ref[...], precision=lax.Precision.HIGHEST, preferred_element_type=F32)

    return pl.pallas_call(
        body, name="rel_bias_fwd", grid=(1,), in_specs=[_full((N_Q_HEADS, REL_BUCKETS)), _full((REL_BUCKETS, n))],
        out_specs=_full((N_Q_HEADS, n)), out_shape=jax.ShapeDtypeStruct((N_Q_HEADS, n), F32),
        compiler_params=_cparams("arbitrary"),
    )(rel_bias_t, jnp.asarray(_bucket_onehot_t()))


def rel_bias_bwd(dbias):
    n_l = dbias.shape[0]
    n = BLOCK * 2 * BLOCK

    def body(d_ref, oh_ref, o_ref):
        tot = d_ref[0]
        for l in range(1, n_l):
            tot = tot + d_ref[l]
        o_ref[...] = lax.dot_general(tot, oh_ref[...], (((1,), (1,)), ((), ())), precision=lax.Precision.HIGHEST,
                                     preferred_element_type=F32)

    return pl.pallas_call(
        body, name="rel_bias_bwd", grid=(1,), in_specs=[_full((n_l, N_Q_HEADS, n)), _full((REL_BUCKETS, n))],
        out_specs=_full((N_Q_HEADS, REL_BUCKETS)), out_shape=jax.ShapeDtypeStruct((N_Q_HEADS, REL_BUCKETS), F32),
        compiler_params=_cparams("arbitrary"),
    )(dbias, jnp.asarray(_bucket_onehot_t()))


def _attn_valid(first):
    qi = lax.broadcasted_iota(jnp.int32, (BLOCK, 2 * BLOCK), 0)
    kj = lax.broadcasted_iota(jnp.int32, (BLOCK, 2 * BLOCK), 1)
    dist = qi + BLOCK - kj
    return (dist >= 0) & (dist < WINDOW) & (jnp.logical_not(first) | (kj >= BLOCK))


def _attn_weights(qh, kcat, bias_h, valid, sink):
    s = _dot_nt(qh, kcat) * ATTN_SCALE + bias_h
    s = jnp.where(valid, s, NEG)
    m = jnp.maximum(jnp.max(s, axis=-1, keepdims=True), sink)
    p = jnp.exp(s - m)
    esink = jnp.exp(sink - m)
    inv = 1.0 / (jnp.sum(p, axis=-1, keepdims=True) + esink)
    return p * inv, esink * inv


def _kv_heads(kvp, kvc, hk):
    ks = slice(hk * HEAD_DIM, (hk + 1) * HEAD_DIM)
    vs = slice(KV_WIDTH // 2 + hk * HEAD_DIM, KV_WIDTH // 2 + (hk + 1) * HEAD_DIM)
    return jnp.concatenate([kvp[:, ks], kvc[:, ks]], axis=0), jnp.concatenate([kvp[:, vs], kvc[:, vs]], axis=0)


def _attn_specs():
    cq = (COL_U + 4 * BRANCH) // BRANCH
    ckv = COL_KV // KV_WIDTH
    return [pl.BlockSpec((BLOCK, BRANCH), lambda n: (n, cq)),
            pl.BlockSpec((BLOCK, KV_WIDTH), lambda n: (n, ckv)),
            pl.BlockSpec((BLOCK, KV_WIDTH), lambda n: (jnp.maximum(n - 1, 0), ckv)),
            _full((N_Q_HEADS, BLOCK, 2 * BLOCK)),
            pl.BlockSpec(memory_space=pltpu.SMEM)]


def attn_fwd(z, bias, sinks, side=None):
    S = z.shape[0]

    def body(q_ref, kvc_ref, kvp_ref, b_ref, sink_ref, o_ref):
        valid = _attn_valid(pl.program_id(0) == 0)
        q = q_ref[...]
        kvc = kvc_ref[...]
        kvp = kvp_ref[...]
        outs = []
        for hk in range(N_KV_HEADS):
            kcat, vcat = _kv_heads(kvp, kvc, hk)
            for g in range(GQA_GROUP):
                h = hk * GQA_GROUP + g
                w, _ = _attn_weights(q[:, h * HEAD_DIM:(h + 1) * HEAD_DIM], kcat, b_ref[h], valid, sink_ref[h])
                outs.append(_dot(w.astype(BF16), vcat))
        o_ref[...] = jnp.concatenate(outs, axis=1).astype(BF16)

    outs, side_outs = _call(
        body, name="attn_fwd", steps=S // BLOCK, in_specs=_attn_specs(),
        out_specs=[pl.BlockSpec((BLOCK, BRANCH), lambda n: (n, 0))],
        out_shape=[jax.ShapeDtypeStruct((S, BRANCH), BF16)], scratch=[], args=(z, z, z, bias, sinks), side=side)
    return outs[0], side_outs


def attn_bwd(do, z, bias, sinks, side=None):
    S = z.shape[0]

    def body(do_ref, q_ref, kvc_ref, kvp_ref, bt_ref, sink_ref, dq_ref, dc_ref, dp_ref, db_ref, ds_ref):
        first = pl.program_id(0) == 0

        @pl.when(first)
        def _():
            db_ref[...] = jnp.zeros_like(db_ref)
            ds_ref[...] = jnp.zeros_like(ds_ref)

        kj = lax.broadcasted_iota(jnp.int32, (2 * BLOCK, BLOCK), 0)
        dist = lax.broadcasted_iota(jnp.int32, (2 * BLOCK, BLOCK), 1) + BLOCK - kj
        valid = (dist >= 0) & (dist < WINDOW) & (jnp.logical_not(first) | (kj >= BLOCK))
        valid4 = jnp.concatenate([valid] * GQA_GROUP, axis=1)
        q = q_ref[...]
        kvc = kvc_ref[...]
        kvp = kvp_ref[...]
        dov = do_ref[...]
        dqs, dks, dvs = [], [], []
        for hk in range(N_KV_HEADS):
            kcat, vcat = _kv_heads(kvp, kvc, hk)
            heads = range(hk * GQA_GROUP, (hk + 1) * GQA_GROUP)
            q4 = jnp.concatenate([q[:, h * HEAD_DIM:(h + 1) * HEAD_DIM] for h in heads], axis=0)
            do4 = jnp.concatenate([dov[:, h * HEAD_DIM:(h + 1) * HEAD_DIM] for h in heads], axis=0)
            bias4 = jnp.concatenate([bt_ref[h] for h in heads], axis=1)
            sink4 = jnp.concatenate([jnp.full((1, BLOCK), sink_ref[h], F32) for h in heads], axis=1)
            s = jnp.where(valid4, _dot_nt(kcat, q4) * ATTN_SCALE + bias4, NEG)
            m = jnp.maximum(jnp.max(s, axis=0, keepdims=True), sink4)
            p = jnp.exp(s - m)
            esink = jnp.exp(sink4 - m)
            inv = 1.0 / (jnp.sum(p, axis=0, keepdims=True) + esink)
            w = p * inv
            dvs.append(_dot(w.astype(BF16), do4))
            dw = _dot_nt(vcat, do4)
            delta = jnp.sum(w * dw, axis=0, keepdims=True)
            ds = w * (dw - delta)
            dsink = -(esink * inv) * delta
            for g, h in enumerate(heads):
                lanes = slice(g * BLOCK, (g + 1) * BLOCK)
                db_ref[h] += ds[:, lanes]
                ds_ref[h:h + 1, :] += jnp.broadcast_to(jnp.sum(dsink[:, lanes], axis=1, keepdims=True), (1, BLOCK))
            dsb = (ds * ATTN_SCALE).astype(BF16)
            dks.append(_dot(dsb, q4))
            dq4 = _dot_tn(dsb, kcat)
            dqs += [dq4[g * BLOCK:(g + 1) * BLOCK] for g in range(GQA_GROUP)]
        dq_ref[...] = jnp.concatenate(dqs, axis=1).astype(BF16)
        both = jnp.concatenate(dks + dvs, axis=1)
        dp_ref[...] = both[:BLOCK]
        dc_ref[...] = both[BLOCK:]

    blk = pl.BlockSpec((BLOCK, BRANCH), lambda n: (n, 0))
    kvb = pl.BlockSpec((BLOCK, KV_WIDTH), lambda n: (n, 0))
    keys_first = (N_Q_HEADS, 2 * BLOCK, BLOCK)
    specs = _attn_specs()
    specs[3] = _full(keys_first)
    (dq, dkv_cur, dkv_prev, dbias_t, dsinks), side_outs = _call(
        body, name="attn_bwd", steps=S // BLOCK, in_specs=[blk] + specs,
        out_specs=[blk, kvb, kvb, _full(keys_first), _full((N_Q_HEADS, BLOCK))],
        out_shape=[jax.ShapeDtypeStruct((S, BRANCH), BF16), jax.ShapeDtypeStruct((S, KV_WIDTH), F32),
                   jax.ShapeDtypeStruct((S, KV_WIDTH), F32), jax.ShapeDtypeStruct(keys_first, F32),
                   jax.ShapeDtypeStruct((N_Q_HEADS, BLOCK), F32)],
        scratch=[], args=(do, z, z, z, jnp.swapaxes(bias, 1, 2), sinks), side=side)
    return (dq, dkv_cur, dkv_prev, jnp.swapaxes(dbias_t, 1, 2), dsinks), side_outs


def kv_shift_add(dcur, dprev):
    S = dcur.shape[0]
    nt = S // TM
    per_tile = TM // BLOCK

    def body(c_ref, p_ref, n_ref, o_ref):
        nxt = jnp.where(pl.program_id(0) == nt - 1, 0.0, n_ref[...])
        o_ref[...] = (c_ref[...] + jnp.concatenate([p_ref[BLOCK:, :], nxt], axis=0)).astype(BF16)

    tile = pl.BlockSpec((TM, KV_WIDTH), lambda i: (i, 0))
    return pl.pallas_call(
        body, name="kv_shift_add", grid=(nt,),
        in_specs=[tile, tile,
                  pl.BlockSpec((BLOCK, KV_WIDTH), lambda i: (jnp.minimum((i + 1) * per_tile, S // BLOCK - 1), 0))],
        out_specs=tile, out_shape=jax.ShapeDtypeStruct((S, KV_WIDTH), BF16), compiler_params=_cparams("parallel"),
    )(dcur, dprev, dprev)


def _ssm_disc(lam_re, lam_im, log_dt, bt_re, bt_im):
    dt = jnp.exp(log_dt)
    mag = jnp.exp(lam_re * dt)
    ang = lam_im * dt
    a_re = mag * jnp.cos(ang)
    a_im = mag * jnp.sin(ang)
    den = lam_re * lam_re + lam_im * lam_im
    nr = a_re - 1.0
    coef_re = (nr * lam_re + a_im * lam_im) / den
    coef_im = (a_im * lam_re - nr * lam_im) / den
    bb_re = coef_re[:, None, :] * bt_re - coef_im[:, None, :] * bt_im
    bb_im = coef_re[:, None, :] * bt_im + coef_im[:, None, :] * bt_re
    return a_re, a_im, bb_re, bb_im


_GN = (SSM_GROUPS, SSM_STATE)
_GPN = (SSM_GROUPS, SSM_GROUP, SSM_STATE)


def ssm_disc_fwd(lam_re, lam_im, log_dt, bt_re, bt_im):
    def body(lr_ref, li_ref, dt_ref, br_ref, bi_ref, ar_ref, ai_ref, bbr_ref, bbi_ref):
        a_re, a_im, bb_re, bb_im = _ssm_disc(lr_ref[...], li_ref[...], dt_ref[...], br_ref[...], bi_ref[...])
        ar_ref[...] = a_re
        ai_ref[...] = a_im
        bbr_ref[...] = bb_re
        bbi_ref[...] = bb_im

    return pl.pallas_call(
        body, name="ssm_disc_fwd", grid=(1,),
        in_specs=[_full(_GN), _full(_GN), _full((SSM_GROUPS, 1)), _full(_GPN), _full(_GPN)],
        out_specs=[_full(_GN), _full(_GN), _full(_GPN), _full(_GPN)],
        out_shape=[jax.ShapeDtypeStruct(s, F32) for s in (_GN, _GN, _GPN, _GPN)],
        compiler_params=_cparams("arbitrary"),
    )(lam_re, lam_im, log_dt, bt_re, bt_im)


def ssm_disc_bwd(lam_re, lam_im, log_dt, bt_re, bt_im, da_re, da_im, dbb_re, dbb_im):
    def body(lr_ref, li_ref, dt_ref, br_ref, bi_ref, dar_ref, dai_ref, dbr_ref, dbi_ref, o_lr, o_li, o_dt, o_br, o_bi):
        prim = (lr_ref[...], li_ref[...], dt_ref[...], br_ref[...], bi_ref[...])
        _, vjp = jax.vjp(_ssm_disc, *prim)
        grads = vjp((dar_ref[...], dai_ref[...], dbr_ref[...], dbi_ref[...]))
        for r, v in zip((o_lr, o_li, o_dt, o_br, o_bi), grads):
            r[...] = v

    shapes = (_GN, _GN, (SSM_GROUPS, 1), _GPN, _GPN)
    return pl.pallas_call(
        body, name="ssm_disc_bwd", grid=(1,),
        in_specs=[_full(s) for s in shapes + (_GN, _GN, _GPN, _GPN)],
        out_specs=[_full(s) for s in shapes], out_shape=[jax.ShapeDtypeStruct(s, F32) for s in shapes],
        compiler_params=_cparams("arbitrary"),
    )(lam_re, lam_im, log_dt, bt_re, bt_im, da_re, da_im, dbb_re, dbb_im)


LANE_GROUPS = SSM_LANES // 128
SUB_GROUPS = SUB_ST // 128
_TM_SHAPE = (LANE_GROUPS, 128)


def _step_rows(t):
    return pl.ds(pl.multiple_of(t * LANE_GROUPS, LANE_GROUPS), LANE_GROUPS)


def _group_rows(j):
    return pl.ds(j, SCAN_T, stride=LANE_GROUPS)


def _store_sub(ref, j, val):
    for k in range(SUB_GROUPS):
        ref[_group_rows(j * SUB_GROUPS + k), :] = val[:, k * 128:(k + 1) * 128]


def _load_sub(ref, j):
    return jnp.concatenate([ref[_group_rows(j * SUB_GROUPS + k), :] for k in range(SUB_GROUPS)], axis=1)


_SUB_SHAPE_IN = (SSM_SUB, SUB_IN, SUB_ST)
_SUB_SHAPE_OUT = (SSM_SUB, SUB_ST, SUB_IN)


def ssm_fwd(z, bb_re, bb_im, ct_re, ct_im, a_re, a_im, d_skip, wglu, side=None):
    S = z.shape[0]
    cu = COL_U // BRANCH

    def body(u_ref, bbr_ref, bbi_ref, ctr_ref, cti_ref, ar_ref, ai_ref, d_ref, wg_ref,
             y_ref, ypre_ref, hr_ref, hi_ref, hrow_r, hrow_i, bur, bui, car_r, car_i):
        @pl.when(pl.program_id(0) == 0)
        def _():
            car_r[...] = jnp.zeros_like(car_r)
            car_i[...] = jnp.zeros_like(car_i)

        u = u_ref[...]
        for j in range(SSM_SUB):
            uj = u[:, j * SUB_IN:(j + 1) * SUB_IN]
            _store_sub(bur, j, _dot(uj, bbr_ref[j]))
            _store_sub(bui, j, _dot(uj, bbi_ref[j]))
        ar = ar_ref[...]
        ai = ai_ref[...]

        def step(t, carry):
            hr, hi = carry
            rows = _step_rows(t)
            nhr = ar * hr - ai * hi + bur[rows, :]
            nhi = ar * hi + ai * hr + bui[rows, :]
            hr_ref[rows, :] = nhr
            hi_ref[rows, :] = nhi
            return nhr, nhi

        hr, hi = lax.fori_loop(0, SCAN_T, step, (car_r[...], car_i[...]), unroll=8)
        car_r[...] = hr
        car_i[...] = hi
        ys = []
        for j in range(SSM_SUB):
            cs = slice(j * SUB_ST, (j + 1) * SUB_ST)
            hrow_r[:, cs] = _load_sub(hr_ref, j).astype(BF16)
            hrow_i[:, cs] = _load_sub(hi_ref, j).astype(BF16)
            ys.append(_dot(hrow_r[:, cs], ctr_ref[j]) - _dot(hrow_i[:, cs], cti_ref[j]))
        ypre = jnp.concatenate(ys, axis=1) + d_ref[...] * u.astype(F32)
        ypre_ref[...] = ypre
        g = jax.nn.gelu(ypre)
        y_ref[...] = (g * jax.nn.sigmoid(_dot(g.astype(BF16), wg_ref[...]))).astype(BF16)

    row = pl.BlockSpec((SCAN_T, BRANCH), lambda i: (i, 0))
    st = pl.BlockSpec((SCAN_T * LANE_GROUPS, 128), lambda i: (i, 0))
    wide = pl.BlockSpec((SCAN_T, SSM_LANES), lambda i: (i, 0))
    return _call(
        body, name="ssm_fwd", steps=S // SCAN_T,
        in_specs=[pl.BlockSpec((SCAN_T, BRANCH), lambda i: (i, cu)), _full(_SUB_SHAPE_IN), _full(_SUB_SHAPE_IN),
                  _full(_SUB_SHAPE_OUT), _full(_SUB_SHAPE_OUT), _full(_TM_SHAPE), _full(_TM_SHAPE), _full((1, BRANCH)),
                  _full((BRANCH, BRANCH))],
        out_specs=[row, row, st, st, wide, wide],
        out_shape=[jax.ShapeDtypeStruct((S, BRANCH), BF16), jax.ShapeDtypeStruct((S, BRANCH), F32),
                   jax.ShapeDtypeStruct((S * LANE_GROUPS, 128), F32), jax.ShapeDtypeStruct((S * LANE_GROUPS, 128), F32),
                   jax.ShapeDtypeStruct((S, SSM_LANES), BF16), jax.ShapeDtypeStruct((S, SSM_LANES), BF16)],
        scratch=[pltpu.VMEM((SCAN_T * LANE_GROUPS, 128), F32), pltpu.VMEM((SCAN_T * LANE_GROUPS, 128), F32),
                 pltpu.VMEM(_TM_SHAPE, F32), pltpu.VMEM(_TM_SHAPE, F32)],
        args=(z, bb_re, bb_im, ct_re, ct_im, a_re, a_im, d_skip, wglu), side=side)


def ssm_bwd(dy, z, ypre, h_re, h_im, hrow_re, hrow_im, bbt_re, bbt_im, c_re, c_im, a_re, a_im, d_skip, wglu, wglu_t,
            side=None):
    S = z.shape[0]
    nt = S // SCAN_T
    cu = COL_U // BRANCH

    def body(dy_ref, u_ref, ypre_ref, hr_ref, hi_ref, hpr_ref, hpi_ref, hrow_r, hrow_i, bbr_ref, bbi_ref, cr_ref, ci_ref,
             ar_ref, ai_ref, d_ref, wg_ref, wgt_ref,
             du_ref, dbbr_ref, dbbi_ref, dctr_ref, dcti_ref, dar_ref, dai_ref, dd_ref, dwg_ref,
             lr_scr, li_scr, car_r, car_i):
        step = pl.program_id(0)

        @pl.when(step == 0)
        def _():
            for r in (dbbr_ref, dbbi_ref, dctr_ref, dcti_ref, dar_ref, dai_ref, dd_ref, dwg_ref, car_r, car_i):
                r[...] = jnp.zeros_like(r)

        u = u_ref[...]
        uf = u.astype(F32)
        dyv = dy_ref[...].astype(F32)
        g, gelu_vjp = jax.vjp(jax.nn.gelu, ypre_ref[...])
        gb = g.astype(BF16)
        sg = jax.nn.sigmoid(_dot(gb, wg_ref[...]))
        dgl = (dyv * g * sg * (1.0 - sg)).astype(BF16)
        dwg_ref[...] += _dot_tn(gb, dgl)
        dg = dyv * sg + _dot(dgl, wgt_ref[...])
        dypre = gelu_vjp(dg)[0]
        dd_ref[...] += jnp.sum(dypre * uf, axis=0, keepdims=True)
        dyb = dypre.astype(BF16)
        for j in range(SSM_SUB):
            dyj = dyb[:, j * SUB_IN:(j + 1) * SUB_IN]
            _store_sub(lr_scr, j, _dot(dyj, cr_ref[j]))
            _store_sub(li_scr, j, -_dot(dyj, ci_ref[j]))
            cs = slice(j * SUB_ST, (j + 1) * SUB_ST)
            dctr_ref[j] += _dot_tn(hrow_r[:, cs], dyj)
            dcti_ref[j] -= _dot_tn(hrow_i[:, cs], dyj)

        ar = ar_ref[...]
        ai = ai_ref[...]

        def adjoint(lr, li, rows):
            nlr = ar * lr + ai * li + lr_scr[rows, :]
            nli = ar * li - ai * lr + li_scr[rows, :]
            lr_scr[rows, :] = nlr
            li_scr[rows, :] = nli
            return nlr, nli

        def back(k, carry):
            lr, li, acc_r, acc_i = carry
            t = SCAN_T - 1 - k
            lr, li = adjoint(lr, li, _step_rows(t))
            hpr = hr_ref[_step_rows(t - 1), :]
            hpi = hi_ref[_step_rows(t - 1), :]
            return lr, li, acc_r + lr * hpr + li * hpi, acc_i + li * hpr - lr * hpi

        zero = jnp.zeros(_TM_SHAPE, F32)
        lr, li, acc_r, acc_i = lax.fori_loop(0, SCAN_T - 1, back, (car_r[...], car_i[...], zero, zero), unroll=8)
        lr, li = adjoint(lr, li, pl.ds(0, LANE_GROUPS))
        car_r[...] = lr
        car_i[...] = li
        first_tile = step == nt - 1
        hpr = jnp.where(first_tile, 0.0, hpr_ref[...])
        hpi = jnp.where(first_tile, 0.0, hpi_ref[...])
        dar_ref[...] += acc_r + lr * hpr + li * hpi
        dai_ref[...] += acc_i + li * hpr - lr * hpi

        dus = []
        for j in range(SSM_SUB):
            lrb = _load_sub(lr_scr, j).astype(BF16)
            lib = _load_sub(li_scr, j).astype(BF16)
            uj = u[:, j * SUB_IN:(j + 1) * SUB_IN]
            dus.append(_dot(lrb, bbr_ref[j]) + _dot(lib, bbi_ref[j]))
            dbbr_ref[j] += _dot_tn(uj, lrb)
            dbbi_ref[j] += _dot_tn(uj, lib)
        du_ref[...] = (jnp.concatenate(dus, axis=1) + dypre * d_ref[...]).astype(BF16)

    def rev(i):
        return nt - 1 - i

    row = pl.BlockSpec((SCAN_T, BRANCH), lambda i: (rev(i), 0))
    st = pl.BlockSpec((SCAN_T * LANE_GROUPS, 128), lambda i: (rev(i), 0))
    before = pl.BlockSpec(_TM_SHAPE, lambda i: (jnp.maximum(rev(i) * SCAN_T - 1, 0), 0))
    wide = pl.BlockSpec((SCAN_T, SSM_LANES), lambda i: (rev(i), 0))
    tm = _full(_TM_SHAPE)
    return _call(
        body, name="ssm_bwd", steps=nt,
        in_specs=[row, pl.BlockSpec((SCAN_T, BRANCH), lambda i: (rev(i), cu)), row, st, st, before, before, wide, wide,
                  _full(_SUB_SHAPE_OUT), _full(_SUB_SHAPE_OUT), _full(_SUB_SHAPE_IN), _full(_SUB_SHAPE_IN),
                  tm, tm, _full((1, BRANCH)), _full((BRANCH, BRANCH)), _full((BRANCH, BRANCH))],
        out_specs=[row, _full(_SUB_SHAPE_IN), _full(_SUB_SHAPE_IN), _full(_SUB_SHAPE_OUT), _full(_SUB_SHAPE_OUT),
                   tm, tm, _full((1, BRANCH)), _full((BRANCH, BRANCH))],
        out_shape=[jax.ShapeDtypeStruct((S, BRANCH), BF16), jax.ShapeDtypeStruct(_SUB_SHAPE_IN, F32),
                   jax.ShapeDtypeStruct(_SUB_SHAPE_IN, F32), jax.ShapeDtypeStruct(_SUB_SHAPE_OUT, F32),
                   jax.ShapeDtypeStruct(_SUB_SHAPE_OUT, F32), jax.ShapeDtypeStruct(_TM_SHAPE, F32),
                   jax.ShapeDtypeStruct(_TM_SHAPE, F32), jax.ShapeDtypeStruct((1, BRANCH), F32),
                   jax.ShapeDtypeStruct((BRANCH, BRANCH), F32)],
        scratch=[pltpu.VMEM((SCAN_T * LANE_GROUPS, 128), F32), pltpu.VMEM((SCAN_T * LANE_GROUPS, 128), F32),
                 pltpu.VMEM(_TM_SHAPE, F32), pltpu.VMEM(_TM_SHAPE, F32)],
        args=(dy, z, ypre, h_re, h_im, h_re, h_im, hrow_re, hrow_im, bbt_re, bbt_im, c_re, c_im, a_re, a_im, d_skip, wglu,
              wglu_t),
        side=side)


def _blockdiag(x):
    gs = SSM_GROUPS // SSM_SUB
    x = x.reshape(SSM_SUB, gs, SSM_GROUP, SSM_STATE)
    eye = jnp.eye(gs, dtype=x.dtype)
    return (x[:, :, :, None, :] * eye[None, :, None, :, None]).reshape(SSM_SUB, SUB_IN, SUB_ST)


def _blockdiag_extract(x):
    gs = SSM_GROUPS // SSM_SUB
    x = x.reshape(SSM_SUB, gs, SSM_GROUP, gs, SSM_STATE)
    eye = jnp.eye(gs, dtype=x.dtype)
    return jnp.sum(x * eye[None, :, None, :, None], axis=3).reshape(SSM_GROUPS, SSM_GROUP, SSM_STATE)


def loss_head(x, g, target):
    S = x.shape[0]

    def body(x_ref, g_ref, t_ref, dx_ref, loss_ref, dg_ref):
        @pl.when(pl.program_id(0) == 0)
        def _():
            loss_ref[...] = jnp.zeros_like(loss_ref)
            dg_ref[...] = jnp.zeros_like(dg_ref)

        xv = x_ref[...]
        gv = g_ref[...]
        r = lax.rsqrt(jnp.mean(xv * xv, axis=-1, keepdims=True) + RMS_EPS)
        xhat = xv * r
        err = xhat * gv - t_ref[...]
        loss_ref[...] += jnp.sum((err * err).reshape(TM // 8, 8, D_MODEL), axis=0) * (0.5 / D_MODEL)
        dy = err * (1.0 / D_MODEL)
        dxhat = dy * gv
        dx_ref[...] = r * (dxhat - xhat * jnp.mean(dxhat * xhat, axis=-1, keepdims=True))
        dg_ref[...] += jnp.sum((dy * xhat).reshape(TM // 8, 8, D_MODEL), axis=0)

    row = pl.BlockSpec((TM, D_MODEL), lambda i: (i, 0))
    acc = _full((8, D_MODEL))
    return pl.pallas_call(
        body, name="loss_head", grid=(S // TM,), in_specs=[row, _full((1, D_MODEL)), row],
        out_specs=[row, acc, acc],
        out_shape=[jax.ShapeDtypeStruct((S, D_MODEL), F32), jax.ShapeDtypeStruct((8, D_MODEL), F32),
                   jax.ShapeDtypeStruct((8, D_MODEL), F32)],
        compiler_params=_cparams("arbitrary"),
    )(x, g, target)


def _x_spec():
    return pl.BlockSpec((TM, D_MODEL), lambda i: (i, 0))


def _g_spec():
    return pl.BlockSpec((1, D_MODEL), lambda i: (0, 0))


def _norm_prologue(x, g):
    h = _rms(x, g).astype(BF16)
    return h, h


def _cast_prologue(x):
    return (x.astype(BF16),)


def _swiglu_prologue(h1, h2):
    a = h1.astype(F32)
    act = (a * jax.nn.sigmoid(a) * h2.astype(F32)).astype(BF16)
    return act, act


def _ssm_consts(lw):
    a_re, a_im, bbt_re, bbt_im = ssm_disc_fwd(
        lw["ssm_lambda_re"], lw["ssm_lambda_im"], lw["ssm_log_dt"].reshape(SSM_GROUPS, 1), lw["bt_re"], lw["bt_im"])
    bb_re = _blockdiag(bbt_re).astype(BF16)
    bb_im = _blockdiag(bbt_im).astype(BF16)
    c_re = _blockdiag(lw["ssm_c_re"]).astype(BF16)
    c_im = _blockdiag(lw["ssm_c_im"]).astype(BF16)
    return dict(
        a_re=a_re.reshape(_TM_SHAPE), a_im=a_im.reshape(_TM_SHAPE),
        bb_re=bb_re, bb_im=bb_im, bbt_re=jnp.swapaxes(bb_re, 1, 2), bbt_im=jnp.swapaxes(bb_im, 1, 2),
        c_re=c_re, c_im=c_im, ct_re=jnp.swapaxes(c_re, 1, 2), ct_im=jnp.swapaxes(c_im, 1, 2))


def layer_fwd(x, lw, bias, next_shards=None, place=None):
    sides = (None, None) if next_shards is None else (("gather_chips", next_shards[:GATHER_SPLIT]),
                                                      ("gather_chips", next_shards[GATHER_SPLIT:]))
    out = fused_mm("in_proj", [x, lw["norm_mix"]], [_x_spec(), _g_spec()], _norm_prologue, lw["w_in"], tn=2944,
                   out_dtype=BF16, extras=((D_MODEL, BF16),), side=sides[0])
    (z, h), g4a = (out, []) if next_shards is None else out
    sc = _ssm_consts(lw)
    (y_ssm, ypre, h_re, h_im, hrow_re, hrow_im), g4b = ssm_fwd(
        z, sc["bb_re"], sc["bb_im"], sc["ct_re"], sc["ct_im"], sc["a_re"], sc["a_im"], lw["ssm_d"], lw["ssm_w_glu"],
        side=sides[1])
    y_conv = conv_fwd(z, lw["conv_w"])
    if next_shards is not None:
        g4 = [_put_slot(g, b, place[0]) for g, b in zip(list(g4a) + list(g4b), next_shards)]
    y_attn, g8 = attn_fwd(z, bias, lw["attn_sinks"], side=None if next_shards is None else ("gather_cores", g4))
    next_gathered = None if next_shards is None else [_put_slot(g, b, place[1]) for g, b in zip(g8, g4)]
    merged = merge_fwd(z, y_ssm, y_conv, y_attn, lw["w_branch"])
    x1 = fused_mm("out_proj", [merged], [_x_spec()], _cast_prologue, lw["w_out"], tn=1024, out_dtype=F32, res=x)
    hf, hn1 = fused_mm("ffn_in", [x1, lw["norm_ffn"]], [_x_spec(), _g_spec()], _norm_prologue, lw["w_ffn_in"], tn=2816,
                       out_dtype=BF16, extras=((D_MODEL, BF16),))
    x2, act = fused_mm("ffn_out", [hf, hf], [_row_spec(FFN_HIDDEN, 0), _row_spec(FFN_HIDDEN, 1)], _swiglu_prologue,
                       lw["w_ffn_out"], tn=1024, out_dtype=F32, res=x1, extras=((FFN_HIDDEN, BF16),))
    a_pre, hn2 = fused_mm("ple_gate", [x2, lw["norm_ple"]], [_x_spec(), _g_spec()], _norm_prologue, lw["w_ple_gate"],
                          tn=1024, out_dtype=BF16, extras=((D_MODEL, BF16),))
    pp = fused_mm("ple_proj", [lw["p"]], [_row_spec(PLE_DIM)], _cast_prologue, lw["w_ple_proj"], tn=1024, out_dtype=BF16)
    x3 = ple_combine(x2, a_pre, pp)
    res = dict(x=x, z=z, h=h, y_ssm=y_ssm, ypre=ypre, h_re=h_re, h_im=h_im, hrow_re=hrow_re, hrow_im=hrow_im, y_conv=y_conv, y_attn=y_attn, merged=merged,
               x1=x1, hf=hf, hn1=hn1, act=act, x2=x2, a_pre=a_pre, hn2=hn2, pp=pp)
    return x3, res, next_gathered


def _pair_sums(split, from_sibling, names):
    return [pair_sum("pair_sum_" + n, a.reshape(2, -1, a.shape[-1]), b.reshape(-1, b.shape[-1])).reshape(b.shape)
            for n, a, b in zip(names, split, from_sibling)]


def layer_bwd(dx3, lw, res, bias, pending=None, scatter_own=False):
    g = {}
    wide = ((D_MODEL, BF16), (D_MODEL, BF16))
    dx2, g["norm_ple"], da, dpp = mm_norm_bwd(
        "d_ple_gate", [dx3, res["a_pre"], res["pp"]], [_row_spec(D_MODEL)] * 3, lw["w_ple_gate"], res["x2"],
        lw["norm_ple"], dx3, tm=TM, pre=_ple_bwd_pre, extras=wide)
    g["w_ple_proj"] = mm_tn("d_w_ple_proj", lw["p"], dpp)
    g["w_ple_gate"] = mm_tn("d_w_ple_gate", res["hn2"], da)
    sums = None
    if pending is None:
        dact = fused_mm("d_ffn_out", [dx2], [_x_spec()], _cast_prologue, lw["w_ffn_out"], tn=1408, out_dtype=BF16,
                        nt=True)
    else:
        dact, from_sibling = fused_mm("d_ffn_out", [dx2], [_x_spec()], _cast_prologue, lw["w_ffn_out"], tn=1408,
                                      out_dtype=BF16, side=("scatter_cores", pending), nt=True)
        sums = _pair_sums(pending, from_sibling, SHARDED_NAMES)
    g["w_ffn_out"] = mm_tn("d_w_ffn_out", res["act"], dx2)
    half = TM // 2
    dx1, g["norm_ffn"], dh1, dh2 = mm_norm_bwd(
        "d_ffn_in", [dact, res["hf"], res["hf"]],
        [_row_spec(FFN_HIDDEN, 0, half), _row_spec(FFN_HIDDEN, 0, half), _row_spec(FFN_HIDDEN, 1, half)],
        lw["w_ffn_in"], res["x1"], lw["norm_ffn"], dx2, tm=half, pre=_swiglu_bwd_pre,
        extras=((FFN_HIDDEN, BF16), (FFN_HIDDEN, BF16)))
    g["w_ffn_in"] = jnp.concatenate([mm_tn("d_w_ffn_in_a", res["hn1"], dh1), mm_tn("d_w_ffn_in_b", res["hn1"], dh2)],
                                    axis=1)
    dmerged = fused_mm("d_out_proj", [dx1], [_x_spec()], _cast_prologue, lw["w_out"], tn=1024, out_dtype=BF16, nt=True)
    g["w_out"] = mm_tn("d_w_out", res["merged"], dx1)
    z = res["z"]
    ys = (res["y_ssm"], res["y_conv"], res["y_attn"])
    dgates, dbs = merge_bwd(dmerged, z, *ys, lw["w_branch"])
    dys, dwb, own_sums = [], [], None
    for r in range(3):
        if scatter_own and r == 0:
            late = SHARDED_NAMES[GATHER_SPLIT:]
            own = [_shard_split(g[n], n).astype(BF16) for n in late]
            dy, from_sibling = fused_mm(f"d_branch_{r}", [dbs[r]], [_x_spec()], _cast_prologue, lw["w_branch"][r],
                                        tn=1024, out_dtype=BF16, nt=True, side=("scatter_cores", own))
            own_sums = _pair_sums(own, from_sibling, late)
        else:
            dy = fused_mm(f"d_branch_{r}", [dbs[r]], [_x_spec()], _cast_prologue, lw["w_branch"][r], tn=1024,
                          out_dtype=BF16, nt=True)
        dys.append(dy)
        dwb.append(mm_tn(f"d_w_branch_{r}", ys[r], dbs[r]))
    g["w_branch"] = jnp.stack(dwb)
    sc = _ssm_consts(lw)
    (du, dbb_re, dbb_im, dct_re, dct_im, da_re, da_im, g["ssm_d"], g["ssm_w_glu"]), received = ssm_bwd(
        dys[0], z, res["ypre"], res["h_re"], res["h_im"], res["hrow_re"], res["hrow_im"], sc["bbt_re"], sc["bbt_im"], sc["c_re"], sc["c_im"],
        sc["a_re"], sc["a_im"], lw["ssm_d"], lw["ssm_w_glu"], lw["ssm_w_glu_t"],
        side=None if pending is None else ("scatter_chips", sums))
    g["ssm_c_re"] = _blockdiag_extract(jnp.swapaxes(dct_re, 1, 2))
    g["ssm_c_im"] = _blockdiag_extract(jnp.swapaxes(dct_im, 1, 2))
    (g["ssm_lambda_re"], g["ssm_lambda_im"], dlog_dt, g["bt_re"], g["bt_im"]) = ssm_disc_bwd(
        lw["ssm_lambda_re"], lw["ssm_lambda_im"], lw["ssm_log_dt"].reshape(SSM_GROUPS, 1), lw["bt_re"], lw["bt_im"],
        da_re.reshape(_GN), da_im.reshape(_GN),
        _blockdiag_extract(dbb_re), _blockdiag_extract(dbb_im))
    g["ssm_log_dt"] = dlog_dt.reshape(SSM_GROUPS)
    dconv, g["conv_w"] = conv_bwd(dys[1], z, lw["conv_w"])
    (dq, dkv_cur, dkv_prev, g["dbias"], g["attn_sinks"]), own_received = attn_bwd(
        dys[2], z, bias, lw["attn_sinks"], side=None if own_sums is None else ("scatter_chips", own_sums))
    dkv = kv_shift_add(dkv_cur, dkv_prev)
    pieces = [dgates[0], dgates[1], dgates[2], du, dconv, dq, dkv]
    g["w_in"] = jnp.concatenate(mm_tn_multi("d_w_in_gates", res["h"], pieces[:3])
                                + mm_tn_multi("d_w_in_branches", res["h"], pieces[3:]), axis=1)
    dx0, g["norm_mix"] = mm_norm_bwd("d_in_proj", pieces, [_row_spec(pc.shape[1], tm=half) for pc in pieces], lw["w_in"],
                                     res["x"], lw["norm_mix"], dx1, tm=half)
    return dx0, g, (sums, received), (own_sums, own_received)


def adamw(name, parts, w, m, v):
    n, R, C = parts.shape
    tr = _pick(R, (512, 256, 128, 64, 32, 16, 8))

    def body(p_ref, w_ref, m_ref, v_ref, g_ref, d_ref, nm_ref, nv_ref):
        gsum = p_ref[0].astype(F32)
        for k in range(1, n):
            gsum = gsum + p_ref[k].astype(F32)
        mn = ADAM_B1 * m_ref[...] + (1.0 - ADAM_B1) * gsum
        vn = ADAM_B2 * v_ref[...] + (1.0 - ADAM_B2) * jnp.square(gsum)
        m_hat = mn / (1.0 - ADAM_B1 ** ADAM_STEP)
        v_hat = vn / (1.0 - ADAM_B2 ** ADAM_STEP)
        g_ref[...] = gsum
        d_ref[...] = -ADAM_LR * (m_hat / (jnp.sqrt(v_hat) + ADAM_EPS) + ADAM_WD * w_ref[...])
        nm_ref[...] = mn
        nv_ref[...] = vn

    blk = pl.BlockSpec((tr, C), lambda i: (i, 0))
    return pl.pallas_call(
        body, name=name, grid=(R // tr,), in_specs=[pl.BlockSpec((n, tr, C), lambda i: (0, i, 0)), blk, blk, blk],
        out_specs=[blk] * 4, out_shape=[jax.ShapeDtypeStruct((R, C), F32)] * 4, compiler_params=_cparams("parallel"),
    )(parts, w, m, v)


_ANY = pl.BlockSpec(memory_space=pl.ANY)


def _coords():
    return lax.axis_index("x"), lax.axis_index("y"), lax.axis_index("c")


def _chip_peers(x, y):
    return [(1 - x, y), (x, 1 - y), (1 - x, 1 - y)]


def _gather_chips_copies(x_refs, out_refs, send_sems, recv_sems):
    x, y, c = _coords()
    me = 2 * x + y
    peers = _chip_peers(x, y)

    def copy(i, k, slot):
        return pltpu.make_async_remote_copy(
            src_ref=x_refs[i], dst_ref=out_refs[i].at[slot], send_sem=send_sems.at[3 * i + k],
            recv_sem=recv_sems.at[3 * i + k], device_id=(*peers[k], c), device_id_type=MESH)

    n = len(x_refs)
    sends = [copy(i, k, me) for i in range(n) for k in range(3)]
    recvs = [copy(i, k, 2 * px + py) for i in range(n) for k, (px, py) in enumerate(peers)]
    return sends, recvs


def _gather_cores_copies(x_refs, out_refs, send_sems, recv_sems):
    x, y, c = _coords()

    def copy(i, slot):
        return pltpu.make_async_remote_copy(
            src_ref=x_refs[i], dst_ref=out_refs[i].at[slot], send_sem=send_sems.at[i], recv_sem=recv_sems.at[i],
            device_id=(x, y, 1 - c), device_id_type=MESH)

    n = len(x_refs)
    return [copy(i, c) for i in range(n)], [copy(i, 1 - c) for i in range(n)]


def _scatter_cores_copies(x_refs, out_refs, send_sems, recv_sems):
    x, y, c = _coords()
    copies = [pltpu.make_async_remote_copy(
        src_ref=x_refs[i].at[1 - c], dst_ref=out_refs[i], send_sem=send_sems.at[i], recv_sem=recv_sems.at[i],
        device_id=(x, y, 1 - c), device_id_type=MESH) for i in range(len(x_refs))]
    return copies, copies


def _scatter_chips_copies(x_refs, out_refs, send_sems, recv_sems):
    x, y, c = _coords()
    me = 2 * x + y
    peers = _chip_peers(x, y)

    def copy(i, k, src_slot, dst_slot):
        return pltpu.make_async_remote_copy(
            src_ref=x_refs[i].at[src_slot], dst_ref=out_refs[i].at[dst_slot], send_sem=send_sems.at[3 * i + k],
            recv_sem=recv_sems.at[3 * i + k], device_id=(*peers[k], c), device_id_type=MESH)

    n = len(x_refs)
    sends = [copy(i, k, 2 * px + py, me) for i in range(n) for k, (px, py) in enumerate(peers)]
    recvs = [copy(i, k, me, 2 * px + py) for i in range(n) for k, (px, py) in enumerate(peers)]
    return sends, recvs


_EXCHANGES = {
    "gather_chips": (lambda b: (4,) + b.shape, 3, _gather_chips_copies),
    "gather_cores": (lambda b: (2,) + b.shape, 1, _gather_cores_copies),
    "scatter_cores": (lambda b: b.shape[1:], 1, _scatter_cores_copies),
    "scatter_chips": (lambda b: b.shape, 3, _scatter_chips_copies),
}


def exchange(name, kind, blks):
    out_shape_of, per_block, make_copies = _EXCHANGES[kind]
    n = len(blks)

    def body(*refs):
        sends, recvs = make_copies(refs[:n], refs[n:2 * n], refs[2 * n], refs[2 * n + 1])
        for cp in sends:
            cp.start()
        for cp in recvs:
            cp.wait_recv()
        for cp in sends:
            cp.wait_send()

    return pl.pallas_call(
        body, name=name, in_specs=[_ANY] * n, out_specs=[_ANY] * n,
        out_shape=[jax.ShapeDtypeStruct(out_shape_of(b), b.dtype) for b in blks],
        scratch_shapes=[pltpu.SemaphoreType.DMA((per_block * n,)), pltpu.SemaphoreType.DMA((per_block * n,))],
    )(*blks)


def _put_slot(buf, block, idx):
    return lax.dynamic_update_slice(buf, block[None].astype(buf.dtype), (idx,) + (0,) * block.ndim)


def pair_sum(name, mine, theirs):
    _, R, C = mine.shape
    tr = _pick(R, (1024, 512, 256, 128, 64, 32, 16))
    c_idx = lax.axis_index("c").astype(jnp.int32).reshape(1)

    def body(c_ref, a_ref, b_ref, o_ref):
        o_ref[...] = (a_ref[0].astype(F32) + b_ref[...].astype(F32)).astype(BF16)

    return pl.pallas_call(
        body, name=name,
        grid_spec=pltpu.PrefetchScalarGridSpec(
            num_scalar_prefetch=1, grid=(R // tr,),
            in_specs=[pl.BlockSpec((1, tr, C), lambda i, c: (c[0], i, 0)), pl.BlockSpec((tr, C), lambda i, c: (i, 0))],
            out_specs=pl.BlockSpec((tr, C), lambda i, c: (i, 0))),
        out_shape=jax.ShapeDtypeStruct(theirs.shape, BF16), compiler_params=_cparams("parallel"),
    )(c_idx, mine, theirs)


SHARDED = {
    "w_in": ((D_MODEL, IN_WIDTH), 2), "ssm_w_glu": ((BRANCH, BRANCH), 1), "conv_w": ((3, BRANCH), 2),
    "w_branch": ((3, BRANCH, D_MODEL), 3), "w_out": ((D_MODEL, D_MODEL), 1), "w_ffn_in": ((D_MODEL, 2 * FFN_HIDDEN), 2),
    "w_ffn_out": ((FFN_HIDDEN, D_MODEL), 1), "w_ple_gate": ((D_MODEL, D_MODEL), 1), "w_ple_proj": ((PLE_DIM, D_MODEL), 2),
}
SMALL = ["rel_bias", "norm_mix", "ssm_lambda_re", "ssm_lambda_im", "ssm_b_re", "ssm_b_im", "ssm_c_re", "ssm_c_im", "ssm_d",
         "ssm_log_dt", "attn_sinks", "norm_ffn", "norm_ple", "norm_final"]
SHARDED_NAMES = list(SHARDED)
GATHER_SPLIT = 4
WEIGHTS = ["rel_bias", "norm_mix", "w_in", "ssm_lambda_re", "ssm_lambda_im", "ssm_b_re", "ssm_b_im", "ssm_c_re", "ssm_c_im",
           "ssm_d", "ssm_log_dt", "ssm_w_glu", "conv_w", "attn_sinks", "w_branch", "w_out", "norm_ffn", "w_ffn_in",
           "w_ffn_out", "norm_ple", "w_ple_gate", "w_ple_proj", "norm_final"]


def _pad_to(flat, n):
    return jnp.pad(flat, [(0, 0)] * (flat.ndim - 1) + [(0, n - flat.shape[-1])])


def _unshard(g8, name):
    axis = SHARDED[name][1] - 1
    shard = g8.shape[2:]
    b = g8.reshape((2, 2, 2) + shard)
    b = jnp.moveaxis(b, (1, 2, 0), (axis, axis + 1, axis + 2))
    full = list(shard)
    full[axis] *= N_DEV
    return b.reshape(full)


def _shard_split(full, name):
    axis = SHARDED[name][1] - 1
    dims = list(full.shape)
    dims[axis:axis + 1] = [2, 2, 2, dims[axis] // N_DEV]
    b = jnp.moveaxis(full.reshape(dims), (axis, axis + 1, axis + 2), (1, 2, 0))
    return b.reshape((2, 4) + b.shape[3:])


def _small_sizes(shapes):
    return [-(-int(np.prod(shapes[n])) // 128) * 128 for n in SMALL]


def pack_small(vals, shapes, extra):
    segs = [_pad_to(vals[n].reshape(-1).astype(F32), s) for n, s in zip(SMALL, _small_sizes(shapes))]
    segs.append(_pad_to(extra.reshape(-1), 128))
    flat = jnp.concatenate(segs)
    rows = -(-flat.shape[0] // (128 * 8)) * 8
    return _pad_to(flat, rows * 128).reshape(rows, 128)


def unpack_small(packed, shapes):
    flat = packed.reshape(-1)
    out, off = {}, 0
    for n, s in zip(SMALL, _small_sizes(shapes)):
        out[n] = flat[off:off + int(np.prod(shapes[n]))].reshape(shapes[n])
        off += s
    return out, flat[off]


def _layer_weights(gathered, small, p, i):
    full = {n: _unshard(g, n) for n, g in zip(SHARDED, gathered)}
    w_in = full["w_in"]
    w_in_p = jnp.concatenate([w_in[:, 2816:], w_in[:, :2560], w_in[:, 2560:2816]], axis=-1)
    return dict(
        w_in=w_in_p,
        ssm_w_glu=full["ssm_w_glu"], ssm_w_glu_t=full["ssm_w_glu"].T,
        conv_w=full["conv_w"],
        w_branch=full["w_branch"], w_out=full["w_out"], w_ffn_in=full["w_ffn_in"], w_ffn_out=full["w_ffn_out"],
        w_ple_gate=full["w_ple_gate"],
        w_ple_proj=full["w_ple_proj"],
        norm_mix=small["norm_mix"][i][None, :], norm_ffn=small["norm_ffn"][i][None, :],
        norm_ple=small["norm_ple"][i][None, :],
        ssm_lambda_re=small["ssm_lambda_re"][i], ssm_lambda_im=small["ssm_lambda_im"][i],
        ssm_log_dt=small["ssm_log_dt"][i],
        bt_re=jnp.swapaxes(small["ssm_b_re"][i], 1, 2), bt_im=jnp.swapaxes(small["ssm_b_im"][i], 1, 2),
        ssm_c_re=small["ssm_c_re"][i], ssm_c_im=small["ssm_c_im"][i], ssm_d=small["ssm_d"][i][None, :],
        attn_sinks=small["attn_sinks"][i], p=p[i],
    )


def matrix_grads(g):
    w_in_g = g["w_in"]
    return dict(
        w_in=jnp.concatenate([w_in_g[:, 3072:5632], w_in_g[:, 5632:], w_in_g[:, :3072]], axis=-1),
        ssm_w_glu=g["ssm_w_glu"], conv_w=jnp.sum(g["conv_w"].reshape(3, 8, BRANCH), axis=1),
        w_branch=g["w_branch"], w_out=g["w_out"], w_ffn_in=g["w_ffn_in"], w_ffn_out=g["w_ffn_out"],
        w_ple_gate=g["w_ple_gate"], w_ple_proj=g["w_ple_proj"])


def small_grads(per_layer, dg_final):
    keys = ("dbias", "norm_mix", "ssm_lambda_re", "ssm_lambda_im", "bt_re", "bt_im", "ssm_c_re", "ssm_c_im", "ssm_d",
            "ssm_log_dt", "attn_sinks", "norm_ffn", "norm_ple")
    g = {k: jnp.stack([gl[k] for gl in per_layer]) for k in keys}
    drel = rel_bias_bwd(g["dbias"].reshape(DEPTH, N_Q_HEADS, BLOCK * 2 * BLOCK)).T
    return dict(
        rel_bias=drel, norm_mix=jnp.sum(g["norm_mix"], axis=1), ssm_lambda_re=g["ssm_lambda_re"],
        ssm_lambda_im=g["ssm_lambda_im"], ssm_b_re=jnp.swapaxes(g["bt_re"], 2, 3), ssm_b_im=jnp.swapaxes(g["bt_im"], 2, 3),
        ssm_c_re=g["ssm_c_re"], ssm_c_im=g["ssm_c_im"], ssm_d=g["ssm_d"][:, 0, :], ssm_log_dt=g["ssm_log_dt"],
        attn_sinks=g["attn_sinks"][:, :, 0], norm_ffn=jnp.sum(g["norm_ffn"], axis=1), norm_ple=jnp.sum(g["norm_ple"], axis=1),
        norm_final=jnp.sum(dg_final, axis=0))


def kernel(x, p, rel_bias, norm_mix, w_in, ssm_lambda_re, ssm_lambda_im, ssm_b_re, ssm_b_im, ssm_c_re, ssm_c_im, ssm_d, ssm_log_dt, ssm_w_glu, conv_w, attn_sinks, w_branch, w_out, norm_ffn, w_ffn_in, w_ffn_out, norm_ple, w_ple_gate, w_ple_proj, norm_final, loss_target, m_rel_bias, m_norm_mix, m_w_in, m_ssm_lambda_re, m_ssm_lambda_im, m_ssm_b_re, m_ssm_b_im, m_ssm_c_re, m_ssm_c_im, m_ssm_d, m_ssm_log_dt, m_ssm_w_glu, m_conv_w, m_attn_sinks, m_w_branch, m_w_out, m_norm_ffn, m_w_ffn_in, m_w_ffn_out, m_norm_ple, m_w_ple_gate, m_w_ple_proj, m_norm_final, v_rel_bias, v_norm_mix, v_w_in, v_ssm_lambda_re, v_ssm_lambda_im, v_ssm_b_re, v_ssm_b_im, v_ssm_c_re, v_ssm_c_im, v_ssm_d, v_ssm_log_dt, v_ssm_w_glu, v_conv_w, v_attn_sinks, v_w_branch, v_w_out, v_norm_ffn, v_w_ffn_in, v_w_ffn_out, v_norm_ple, v_w_ple_gate, v_w_ple_proj, v_norm_final):
    args = dict(locals())
    w = {n: args[n] for n in WEIGHTS}
    m = {n: args["m_" + n] for n in WEIGHTS}
    v = {n: args["v_" + n] for n in WEIGHTS}
    shapes = {n: w[n].shape for n in SMALL}

    x_i, y_i, c_i = _coords()
    chip = 2 * x_i + y_i
    place = (chip, c_i)
    small = {n: w[n] for n in SMALL}

    def all_gather(tag, blks):
        g4 = exchange(f"gather_{tag}_chips", "gather_chips", blks)
        g4 = [_put_slot(g, b, chip) for g, b in zip(g4, blks)]
        g8 = exchange(f"gather_{tag}_cores", "gather_cores", g4)
        return [_put_slot(g, b, c_i) for g, b in zip(g8, g4)]

    def shards_of(layer):
        return [w[n][layer] if n == "conv_w" else w[n][layer].astype(BF16) for n in SHARDED]

    bias = rel_bias_fwd(small["rel_bias"].T).reshape(N_Q_HEADS, BLOCK, 2 * BLOCK)
    xs, layers, res = x[0], [], []
    gathered = all_gather("w", shards_of(0))
    for layer in range(DEPTH):
        layers.append(_layer_weights(gathered, small, p[:, 0], layer))
        nxt = shards_of(layer + 1) if layer + 1 < DEPTH else None
        xs, res_l, gathered = layer_fwd(xs, layers[layer], bias, nxt, place)
        res.append(res_l)
    grad_x, loss_parts, dg_final = loss_head(xs, small["norm_final"][None, :], loss_target[0])

    per_layer, reduced, pending = [None] * DEPTH, [None] * DEPTH, None
    for layer in reversed(range(DEPTH)):
        grad_x, per_layer[layer], done, own_done = layer_bwd(grad_x, layers[layer], res[layer], bias, pending,
                                                             scatter_own=layer == 0)
        if pending is not None:
            reduced[layer + 1] = done
        mg = matrix_grads(per_layer[layer])
        pending = [_shard_split(mg[n], n).astype(BF16) for n in SHARDED]
    early = pending[:GATHER_SPLIT]
    sums = _pair_sums(early, exchange("scatter_g_cores", "scatter_cores", early), SHARDED_NAMES[:GATHER_SPLIT])
    received = exchange("scatter_g_chips", "scatter_chips", sums)
    reduced[0] = (list(sums) + list(own_done[0]), list(received) + list(own_done[1]))

    outs = ({}, {}, {}, {})
    for k, name in enumerate(SHARDED):
        parts = jnp.stack([_put_slot(rcv[k], lax.dynamic_index_in_dim(sm[k], chip, 0, keepdims=False), chip)
                           for sm, rcv in reduced], axis=1)
        cols = parts.shape[-1]
        res4 = adamw("adamw_" + name, parts.reshape(4, -1, cols), w[name].reshape(-1, cols), m[name].reshape(-1, cols),
                     v[name].reshape(-1, cols))
        for d, o in zip(outs, res4):
            d[name] = o.reshape(w[name].shape)

    small_local = pack_small(small_grads(per_layer, dg_final), shapes, jnp.sum(loss_parts))
    small_all = all_gather("s", [small_local])[0]
    zero = jnp.zeros((1,), F32)
    res4 = adamw("adamw_small", small_all.reshape(N_DEV, small_local.shape[0], 128),
                 pack_small({n: w[n] for n in SMALL}, shapes, zero), pack_small({n: m[n] for n in SMALL}, shapes, zero),
                 pack_small({n: v[n] for n in SMALL}, shapes, zero))
    loss = None
    for d, r in zip(outs, res4):
        vals, extra = unpack_small(r, shapes)
        d.update(vals)
        if loss is None:
            loss = extra

    return (loss, grad_x[None], *[d[n] for d in outs for n in WEIGHTS])
```

```python
import functools
import math

import numpy as np
import jax
import jax.numpy as jnp
from jax import lax
from jax.experimental import pallas as pl
from jax.experimental.pallas import tpu as pltpu

F32 = jnp.float32
BF16 = jnp.bfloat16
MESH = pl.DeviceIdType.MESH

D_MODEL = 1024
DEPTH = 4
PLE_DIM = 256
BRANCH = 512
SSM_GROUPS = 32
SSM_GROUP = 16
SSM_STATE = 64
SSM_LANES = SSM_GROUPS * SSM_STATE
SSM_SUB = 4
SUB_IN = BRANCH // SSM_SUB
SUB_ST = SSM_LANES // SSM_SUB
HEAD_DIM = 64
N_Q_HEADS = 8
N_KV_HEADS = 2
GQA_GROUP = 4
KV_WIDTH = 2 * N_KV_HEADS * HEAD_DIM
WINDOW = 128
BLOCK = 128
ATTN_SCALE = 1.0 / math.sqrt(HEAD_DIM)
REL_BUCKETS = 32
REL_MAX_DIST = 128
FFN_HIDDEN = 2816
RMS_EPS = 1e-6
IN_WIDTH = 5888
N_DEV = 8

ADAM_LR = 0.001
ADAM_B1 = 0.9
ADAM_B2 = 0.999
ADAM_EPS = 1e-08
ADAM_WD = 0.01
ADAM_STEP = 10

COL_U = 3072
COL_KV = 5632
NEG = -1e30

SCAN_T = 256
TM = 512
TM_W = 1024
VMEM_LIMIT = 52 * 1024 * 1024


def _cparams(*sem):
    return pltpu.CompilerParams(dimension_semantics=sem, vmem_limit_bytes=VMEM_LIMIT)


def _full(shape):
    n = len(shape)
    return pl.BlockSpec(shape, lambda *_: (0,) * n)


def _pick(n, cands):
    for c in cands:
        if n % c == 0:
            return c
    return n


def _call(body, *, name, steps, in_specs, out_specs, out_shape, scratch, args, side=None):
    in_specs, out_specs, out_shape = list(in_specs), list(out_specs), list(out_shape)
    scratch, args = list(scratch), list(args)
    n_in, n_out, n_scr = len(in_specs), len(out_specs), len(scratch)
    n = 0
    if side is not None:
        kind, blks = side
        out_shape_of, per_block, make_copies = _EXCHANGES[kind]
        n = len(blks)
        inner = body

        def body(*refs):
            ins, sx = refs[:n_in], refs[n_in:n_in + n]
            outs, so = refs[n_in + n:n_in + n + n_out], refs[n_in + n + n_out:n_in + 2 * n + n_out]
            scr = refs[n_in + 2 * n + n_out:n_in + 2 * n + n_out + n_scr]
            send_sems, recv_sems = refs[-2:]

            @pl.when(pl.program_id(0) == 0)
            def _():
                sends, _ = make_copies(sx, so, send_sems, recv_sems)
                for cp in sends:
                    cp.start()

            inner(*ins, *outs, *scr)

            @pl.when(pl.program_id(0) == steps - 1)
            def _():
                sends, recvs = make_copies(sx, so, send_sems, recv_sems)
                for cp in recvs:
                    cp.wait_recv()
                for cp in sends:
                    cp.wait_send()

        in_specs += [_ANY] * n
        args += list(blks)
        out_specs += [_ANY] * n
        out_shape += [jax.ShapeDtypeStruct(out_shape_of(b), b.dtype) for b in blks]
        scratch += [pltpu.SemaphoreType.DMA((per_block * n,)), pltpu.SemaphoreType.DMA((per_block * n,))]
    outs = pl.pallas_call(
        body, name=name, grid=(steps,), in_specs=in_specs, out_specs=out_specs, out_shape=out_shape,
        scratch_shapes=scratch, compiler_params=_cparams("arbitrary"),
    )(*args)
    return outs[:n_out], outs[n_out:]


def _dot(a, b):
    return jnp.dot(a, b, preferred_element_type=F32)


def _dot_tn(a, b):
    return lax.dot_general(a, b, (((0,), (0,)), ((), ())), preferred_element_type=F32)


def _dot_nt(a, b):
    return lax.dot_general(a, b, (((1,), (1,)), ((), ())), preferred_element_type=F32)


def _rms(x, g):
    r = lax.rsqrt(jnp.mean(x * x, axis=-1, keepdims=True) + RMS_EPS)
    return x * r * g


def fused_mm(name, ins, in_specs, prologue, w, *, tn, out_dtype, res=None, extras=(), side=None, nt=False):
    S = ins[0].shape[0]
    N, K = w.shape if nt else w.shape[::-1]
    tn = min(tn, N)
    n_in, n_ex = len(ins), len(extras)

    def body(*refs):
        in_refs = refs[:n_in]
        w_ref = refs[n_in]
        pos = n_in + 1
        res_ref = None
        if res is not None:
            res_ref = refs[pos]
            pos += 1
        o_ref = refs[pos]
        ex_refs = refs[pos + 1:pos + 1 + n_ex]
        a_scr = refs[-1]
        out = prologue(*[r[...] for r in in_refs])
        a_scr[...] = out[0]
        for r, e in zip(ex_refs, out[1:]):
            r[...] = e.astype(r.dtype)
        for j in range(N // tn):
            cs = slice(j * tn, (j + 1) * tn)
            acc = _dot_nt(a_scr[...], w_ref[cs, :]) if nt else _dot(a_scr[...], w_ref[:, cs])
            if res_ref is not None:
                acc = acc + res_ref[:, cs]
            o_ref[:, cs] = acc.astype(o_ref.dtype)

    specs = list(in_specs) + [pl.BlockSpec(w.shape, lambda i: (0, 0), pipeline_mode=pl.Buffered(1))]
    args = list(ins) + [w]
    if res is not None:
        specs.append(pl.BlockSpec((TM, N), lambda i: (i, 0)))
        args.append(res)
    out_shape = [jax.ShapeDtypeStruct((S, N), out_dtype)]
    out_specs = [pl.BlockSpec((TM, N), lambda i: (i, 0))]
    for cols, dt in extras:
        out_shape.append(jax.ShapeDtypeStruct((S, cols), dt))
        out_specs.append(pl.BlockSpec((TM, cols), lambda i: (i, 0)))
    outs, side_outs = _call(body, name=name, steps=S // TM, in_specs=specs, out_specs=out_specs, out_shape=out_shape,
                            scratch=[pltpu.VMEM((TM, K), BF16)], args=args, side=side)
    result = outs if n_ex else outs[0]
    return result if side is None else (result, side_outs)


def _row_spec(cols, blk=0, tm=TM):
    return pl.BlockSpec((tm, cols), lambda i: (i, blk))


def mm_norm_bwd(name, ins, in_specs, w, x, g, dres, *, tm, pre=None, extras=()):
    S = x.shape[0]
    n = len(ins)

    def body(*refs):
        x_ref, g_ref, dres_ref, w_ref, dx_ref, dg_ref = refs[n:n + 6]

        @pl.when(pl.program_id(0) == 0)
        def _():
            dg_ref[...] = jnp.zeros_like(dg_ref)

        tiles = [r[...] for r in refs[:n]]
        if pre is not None:
            tiles, extra_tiles = pre(*tiles)
            for r, e in zip(refs[n + 6:], extra_tiles):
                r[...] = e.astype(r.dtype)
        tiles = [t.astype(BF16) for t in tiles]
        a = tiles[0] if len(tiles) == 1 else jnp.concatenate(tiles, axis=1)
        dh = _dot_nt(a, w_ref[...])
        xv = x_ref[...]
        r = lax.rsqrt(jnp.mean(xv * xv, axis=-1, keepdims=True) + RMS_EPS)
        xhat = xv * r
        dxhat = dh * g_ref[...]
        dx_ref[...] = dres_ref[...] + r * (dxhat - xhat * jnp.mean(dxhat * xhat, axis=-1, keepdims=True))
        dg_ref[...] += jnp.sum((dh * xhat).reshape(tm // 8, 8, D_MODEL), axis=0)

    row = _row_spec(D_MODEL, tm=tm)
    return pl.pallas_call(
        body, name=name, grid=(S // tm,),
        in_specs=list(in_specs)
        + [row, _full((1, D_MODEL)), row, pl.BlockSpec(w.shape, lambda i: (0, 0), pipeline_mode=pl.Buffered(1))],
        out_specs=[row, _full((8, D_MODEL))] + [_row_spec(cols, tm=tm) for cols, _ in extras],
        out_shape=[jax.ShapeDtypeStruct((S, D_MODEL), F32), jax.ShapeDtypeStruct((8, D_MODEL), F32)]
        + [jax.ShapeDtypeStruct((S, cols), dt) for cols, dt in extras],
        compiler_params=_cparams("arbitrary"),
    )(*ins, x, g, dres, w)


def mm_tn_multi(name, a, pieces):
    S, K = a.shape
    n = len(pieces)

    def body(a_ref, *refs):
        @pl.when(pl.program_id(0) == 0)
        def _():
            for o_ref in refs[n:]:
                o_ref[...] = jnp.zeros_like(o_ref)

        at = a_ref[...].astype(BF16)
        for p_ref, o_ref in zip(refs[:n], refs[n:]):
            o_ref[...] += _dot_tn(at, p_ref[...].astype(BF16))

    return pl.pallas_call(
        body, name=name, grid=(S // TM_W,),
        in_specs=[_row_spec(K, tm=TM_W)] + [_row_spec(p.shape[1], tm=TM_W) for p in pieces],
        out_specs=[_full((K, p.shape[1])) for p in pieces],
        out_shape=[jax.ShapeDtypeStruct((K, p.shape[1]), F32) for p in pieces],
        compiler_params=_cparams("arbitrary"),
    )(a, *pieces)


def mm_tn(name, a, b):
    S, K = a.shape
    N = b.shape[1]
    tk = _pick(K, (1024, 1408, 512, 256))
    tn = _pick(N, (1024, 1408, 1536, 512, 256))

    def body(a_ref, b_ref, o_ref):
        @pl.when(pl.program_id(2) == 0)
        def _():
            o_ref[...] = jnp.zeros_like(o_ref)

        o_ref[...] += _dot_tn(a_ref[...].astype(BF16), b_ref[...].astype(BF16))

    return pl.pallas_call(
        body, name=name, grid=(K // tk, N // tn, S // TM_W),
        in_specs=[pl.BlockSpec((TM_W, tk), lambda k, n, s: (s, k)), pl.BlockSpec((TM_W, tn), lambda k, n, s: (s, n))],
        out_specs=pl.BlockSpec((tk, tn), lambda k, n, s: (k, n)),
        out_shape=jax.ShapeDtypeStruct((K, N), F32),
        compiler_params=_cparams("parallel", "parallel", "arbitrary"),
    )(a, b)


def _swiglu_bwd_pre(dact, h1, h2):
    h1 = h1.astype(F32)
    h2 = h2.astype(F32)
    da = dact.astype(F32)
    sg = jax.nn.sigmoid(h1)
    halves = [(da * h2 * sg * (1.0 + h1 * (1.0 - sg))).astype(BF16), (da * h1 * sg).astype(BF16)]
    return halves, halves


def _ple_bwd_pre(dx, a_pre, pp):
    pg = jax.nn.sigmoid(a_pre.astype(F32))
    da = (dx * pp.astype(F32) * pg * (1.0 - pg)).astype(BF16)
    return [da], [da, (dx * pg).astype(BF16)]


def ple_combine(x2, a_pre, pp):
    S = x2.shape[0]

    def body(x_ref, a_ref, p_ref, o_ref):
        o_ref[...] = x_ref[...] + jax.nn.sigmoid(a_ref[...].astype(F32)) * p_ref[...].astype(F32)

    row = pl.BlockSpec((TM, D_MODEL), lambda i: (i, 0))
    return pl.pallas_call(
        body, name="ple_combine", grid=(S // TM,), in_specs=[row, row, row], out_specs=row,
        out_shape=jax.ShapeDtypeStruct((S, D_MODEL), F32), compiler_params=_cparams("parallel"),
    )(x2, a_pre, pp)


def _gate_specs(tn, nn):
    return [pl.BlockSpec((TM, tn), functools.partial(lambda i, j, r: (i, r * nn + j), r=r)) for r in range(3)]


def merge_fwd(z, y_ssm, y_conv, y_attn, wb):
    S = z.shape[0]
    tn = 512
    nn = D_MODEL // tn

    def body(g0, g1, g2, y0, y1, y2, w_ref, o_ref):
        acc = jnp.zeros((TM, tn), F32)
        for r, (g_ref, y_ref) in enumerate(((g0, y0), (g1, y1), (g2, y2))):
            acc += jax.nn.sigmoid(g_ref[...].astype(F32)) * _dot(y_ref[...], w_ref[r])
        o_ref[...] = acc.astype(BF16)

    y_spec = pl.BlockSpec((TM, BRANCH), lambda i, j: (i, 0))
    return pl.pallas_call(
        body, name="merge_fwd", grid=(S // TM, nn),
        in_specs=_gate_specs(tn, nn) + [y_spec] * 3 + [pl.BlockSpec((3, BRANCH, tn), lambda i, j: (0, 0, j))],
        out_specs=pl.BlockSpec((TM, tn), lambda i, j: (i, j)),
        out_shape=jax.ShapeDtypeStruct((S, D_MODEL), BF16), compiler_params=_cparams("parallel", "parallel"),
    )(z, z, z, y_ssm, y_conv, y_attn, wb)


def merge_bwd(dmerged, z, y_ssm, y_conv, y_attn, wb, side=None):
    S = z.shape[0]

    def body(dm_ref, g0, g1, g2, y0, y1, y2, w_ref, dg0, dg1, dg2, db0, db1, db2, dy0, dy1, dy2):
        dm = dm_ref[...].astype(F32)
        rows = ((g0, y0, dg0, db0, dy0), (g1, y1, dg1, db1, dy1), (g2, y2, dg2, db2, dy2))
        for r, (g_ref, y_ref, dg_ref, db_ref, dy_ref) in enumerate(rows):
            sg = jax.nn.sigmoid(g_ref[...].astype(F32))
            b = _dot(y_ref[...], w_ref[r])
            dg_ref[...] = (dm * b * sg * (1.0 - sg)).astype(BF16)
            db = (dm * sg).astype(BF16)
            db_ref[...] = db
            dy_ref[...] = _dot_nt(db, w_ref[r]).astype(BF16)

    wide = pl.BlockSpec((TM, D_MODEL), lambda i: (i, 0))
    y_spec = pl.BlockSpec((TM, BRANCH), lambda i: (i, 0))
    gates = [pl.BlockSpec((TM, D_MODEL), functools.partial(lambda i, r: (i, r), r=r)) for r in range(3)]
    outs, side_outs = _call(
        body, name="merge_bwd", steps=S // TM,
        in_specs=[wide] + gates + [y_spec] * 3
        + [pl.BlockSpec((3, BRANCH, D_MODEL), lambda i: (0, 0, 0), pipeline_mode=pl.Buffered(1))],
        out_specs=[wide] * 6 + [y_spec] * 3,
        out_shape=[jax.ShapeDtypeStruct((S, D_MODEL), BF16)] * 6 + [jax.ShapeDtypeStruct((S, BRANCH), BF16)] * 3,
        scratch=[], args=(dmerged, z, z, z, y_ssm, y_conv, y_attn, wb), side=side)
    return (outs[:3], outs[3:6], outs[6:]), side_outs


def _shift_down(v, halo, k):
    rolled = pltpu.roll(v, k, 0)
    h = pltpu.roll(halo, k, 0)
    row = lax.broadcasted_iota(jnp.int32, v.shape, 0)
    head = jnp.concatenate([h, jnp.zeros((v.shape[0] - 8, v.shape[1]), v.dtype)], axis=0)
    return jnp.where(row < k, head, rolled)


def _shift_up(v, halo, k):
    n = v.shape[0]
    rolled = pltpu.roll(v, n - k, 0)
    h = pltpu.roll(halo, 8 - k, 0)
    row = lax.broadcasted_iota(jnp.int32, v.shape, 0)
    tail = jnp.concatenate([jnp.zeros((n - 8, v.shape[1]), v.dtype), h], axis=0)
    return jnp.where(row >= n - k, tail, rolled)


def _conv_specs():
    rb = TM // 8
    c0 = COL_U // BRANCH

    def cur(k):
        return pl.BlockSpec((TM, BRANCH), lambda i: (i, c0 + k))

    def prev(k):
        return pl.BlockSpec((8, BRANCH), lambda i: (jnp.maximum(i * rb - 1, 0), c0 + k))

    return [cur(1), cur(2), cur(3), prev(2), prev(3)]


def conv_fwd(z, conv_w):
    S = z.shape[0]

    def body(cb_ref, cc_ref, cx_ref, pc_ref, px_ref, w_ref, o_ref):
        first = pl.program_id(0) == 0
        v = cc_ref[...].astype(F32) * cx_ref[...].astype(F32)
        pv = jnp.where(first, 0.0, pc_ref[...].astype(F32) * px_ref[...].astype(F32))
        w = w_ref[...]
        y = w[2:3] * v + w[1:2] * _shift_down(v, pv, 1) + w[0:1] * _shift_down(v, pv, 2)
        o_ref[...] = (cb_ref[...].astype(F32) * y).astype(BF16)

    return pl.pallas_call(
        body, name="conv_fwd", grid=(S // TM,), in_specs=_conv_specs() + [_full((3, BRANCH))],
        out_specs=pl.BlockSpec((TM, BRANCH), lambda i: (i, 0)),
        out_shape=jax.ShapeDtypeStruct((S, BRANCH), BF16), compiler_params=_cparams("parallel"),
    )(z, z, z, z, z, conv_w)


def conv_bwd(dy, z, conv_w):
    S = z.shape[0]
    rb = TM // 8
    nt = S // TM
    c0 = COL_U // BRANCH

    def body(dy_ref, cb_ref, cc_ref, cx_ref, pc_ref, px_ref, ndy_ref, ncb_ref, w_ref, o_ref, dw_ref):
        i = pl.program_id(0)

        @pl.when(i == 0)
        def _():
            dw_ref[...] = jnp.zeros_like(dw_ref)

        cb = cb_ref[...].astype(F32)
        cc = cc_ref[...].astype(F32)
        cx = cx_ref[...].astype(F32)
        dyv = dy_ref[...].astype(F32)
        v = cc * cx
        pv = jnp.where(i == 0, 0.0, pc_ref[...].astype(F32) * px_ref[...].astype(F32))
        v1 = _shift_down(v, pv, 1)
        v2 = _shift_down(v, pv, 2)
        w = w_ref[...]
        conv = w[2:3] * v + w[1:2] * v1 + w[0:1] * v2
        dc = dyv * cb
        ndc = jnp.where(i == nt - 1, 0.0, ndy_ref[...].astype(F32) * ncb_ref[...].astype(F32))
        dv = w[2:3] * dc + w[1:2] * _shift_up(dc, ndc, 1) + w[0:1] * _shift_up(dc, ndc, 2)
        o_ref[:, 0:BRANCH] = (dyv * conv).astype(BF16)
        o_ref[:, BRANCH:2 * BRANCH] = (dv * cx).astype(BF16)
        o_ref[:, 2 * BRANCH:3 * BRANCH] = (dv * cc).astype(BF16)
        for k, vk in enumerate((v2, v1, v)):
            dw_ref[8 * k:8 * k + 8, :] += jnp.sum((dc * vk).reshape(rb, 8, BRANCH), axis=0)

    nxt = jnp.minimum

    return pl.pallas_call(
        body, name="conv_bwd", grid=(nt,),
        in_specs=[pl.BlockSpec((TM, BRANCH), lambda i: (i, 0))] + _conv_specs()
        + [pl.BlockSpec((8, BRANCH), lambda i: (nxt((i + 1) * rb, S // 8 - 1), 0)),
           pl.BlockSpec((8, BRANCH), lambda i: (nxt((i + 1) * rb, S // 8 - 1), c0 + 1)),
           _full((3, BRANCH))],
        out_specs=[pl.BlockSpec((TM, 3 * BRANCH), lambda i: (i, 0)), _full((24, BRANCH))],
        out_shape=[jax.ShapeDtypeStruct((S, 3 * BRANCH), BF16), jax.ShapeDtypeStruct((24, BRANCH), F32)],
        compiler_params=_cparams("arbitrary"),
    )(dy, z, z, z, z, z, dy, z, conv_w)


def _bucket_onehot_t():
    qi = np.arange(BLOCK)[:, None]
    kj = np.arange(2 * BLOCK)[None, :]
    dist = np.clip(qi + BLOCK - kj, 0, REL_MAX_DIST - 1)
    exact = REL_BUCKETS // 2
    df = np.maximum(dist, 1).astype(np.float32)
    large = exact + (np.log(df / np.float32(exact)) / np.float32(math.log(REL_MAX_DIST / exact))
                     * np.float32(REL_BUCKETS - exact)).astype(np.int32)
    large = np.minimum(large, REL_BUCKETS - 1)
    bucket = np.where(dist < exact, dist, large).reshape(-1)
    return (np.arange(REL_BUCKETS)[:, None] == bucket[None, :]).astype(np.float32)


def rel_bias_fwd(rel_bias_t):
    n = BLOCK * 2 * BLOCK

    def body(r_ref, oh_ref, o_ref):
        o_ref[...] = jnp.dot(r_ref[...], oh_ref[...], precision=lax.Precision.HIGHEST, preferred_element_type=F32)

    return pl.pallas_call(
        body, name="rel_bias_fwd", grid=(1,), in_specs=[_full((N_Q_HEADS, REL_BUCKETS)), _full((REL_BUCKETS, n))],
        out_specs=_full((N_Q_HEADS, n)), out_shape=jax.ShapeDtypeStruct((N_Q_HEADS, n), F32),
        compiler_params=_cparams("arbitrary"),
    )(rel_bias_t, jnp.asarray(_bucket_onehot_t()))


def rel_bias_bwd(dbias):
    n_l = dbias.shape[0]
    n = BLOCK * 2 * BLOCK

    def body(d_ref, oh_ref, o_ref):
        tot = d_ref[0]
        for l in range(1, n_l):
            tot = tot + d_ref[l]
        o_ref[...] = lax.dot_general(tot, oh_ref[...], (((1,), (1,)), ((), ())), precision=lax.Precision.HIGHEST,
                                     preferred_element_type=F32)

    return pl.pallas_call(
        body, name="rel_bias_bwd", grid=(1,), in_specs=[_full((n_l, N_Q_HEADS, n)), _full((REL_BUCKETS, n))],
        out_specs=_full((N_Q_HEADS, REL_BUCKETS)), out_shape=jax.ShapeDtypeStruct((N_Q_HEADS, REL_BUCKETS), F32),
        compiler_params=_cparams("arbitrary"),
    )(dbias, jnp.asarray(_bucket_onehot_t()))


def _attn_valid(first):
    qi = lax.broadcasted_iota(jnp.int32, (BLOCK, 2 * BLOCK), 0)
    kj = lax.broadcasted_iota(jnp.int32, (BLOCK, 2 * BLOCK), 1)
    dist = qi + BLOCK - kj
    return (dist >= 0) & (dist < WINDOW) & (jnp.logical_not(first) | (kj >= BLOCK))


def _attn_weights(qh, kcat, bias_h, valid, sink):
    s = _dot_nt(qh, kcat) * ATTN_SCALE + bias_h
    s = jnp.where(valid, s, NEG)
    m = jnp.maximum(jnp.max(s, axis=-1, keepdims=True), sink)
    p = jnp.exp(s - m)
    esink = jnp.exp(sink - m)
    inv = 1.0 / (jnp.sum(p, axis=-1, keepdims=True) + esink)
    return p * inv, esink * inv


def _kv_heads(kvp, kvc, hk):
    ks = slice(hk * HEAD_DIM, (hk + 1) * HEAD_DIM)
    vs = slice(KV_WIDTH // 2 + hk * HEAD_DIM, KV_WIDTH // 2 + (hk + 1) * HEAD_DIM)
    return jnp.concatenate([kvp[:, ks], kvc[:, ks]], axis=0), jnp.concatenate([kvp[:, vs], kvc[:, vs]], axis=0)


def _attn_specs():
    cq = (COL_U + 4 * BRANCH) // BRANCH
    ckv = COL_KV // KV_WIDTH
    return [pl.BlockSpec((BLOCK, BRANCH), lambda n: (n, cq)),
            pl.BlockSpec((BLOCK, KV_WIDTH), lambda n: (n, ckv)),
            pl.BlockSpec((BLOCK, KV_WIDTH), lambda n: (jnp.maximum(n - 1, 0), ckv)),
            _full((N_Q_HEADS, BLOCK, 2 * BLOCK)),
            pl.BlockSpec(memory_space=pltpu.SMEM)]


def attn_fwd(z, bias, sinks, side=None):
    S = z.shape[0]

    def body(q_ref, kvc_ref, kvp_ref, b_ref, sink_ref, o_ref):
        valid = _attn_valid(pl.program_id(0) == 0)
        q = q_ref[...]
        kvc = kvc_ref[...]
        kvp = kvp_ref[...]
        outs = []
        for hk in range(N_KV_HEADS):
            kcat, vcat = _kv_heads(kvp, kvc, hk)
            for g in range(GQA_GROUP):
                h = hk * GQA_GROUP + g
                w, _ = _attn_weights(q[:, h * HEAD_DIM:(h + 1) * HEAD_DIM], kcat, b_ref[h], valid, sink_ref[h])
                outs.append(_dot(w.astype(BF16), vcat))
        o_ref[...] = jnp.concatenate(outs, axis=1).astype(BF16)

    outs, side_outs = _call(
        body, name="attn_fwd", steps=S // BLOCK, in_specs=_attn_specs(),
        out_specs=[pl.BlockSpec((BLOCK, BRANCH), lambda n: (n, 0))],
        out_shape=[jax.ShapeDtypeStruct((S, BRANCH), BF16)], scratch=[], args=(z, z, z, bias, sinks), side=side)
    return outs[0], side_outs


def attn_bwd(do, z, bias, sinks, side=None):
    S = z.shape[0]

    def body(do_ref, q_ref, kvc_ref, kvp_ref, bt_ref, sink_ref, dq_ref, dc_ref, dp_ref, db_ref, ds_ref):
        first = pl.program_id(0) == 0

        @pl.when(first)
        def _():
            db_ref[...] = jnp.zeros_like(db_ref)
            ds_ref[...] = jnp.zeros_like(ds_ref)

        kj = lax.broadcasted_iota(jnp.int32, (2 * BLOCK, BLOCK), 0)
        dist = lax.broadcasted_iota(jnp.int32, (2 * BLOCK, BLOCK), 1) + BLOCK - kj
        valid = (dist >= 0) & (dist < WINDOW) & (jnp.logical_not(first) | (kj >= BLOCK))
        valid4 = jnp.concatenate([valid] * GQA_GROUP, axis=1)
        q = q_ref[...]
        kvc = kvc_ref[...]
        kvp = kvp_ref[...]
        dov = do_ref[...]
        dqs, dks, dvs = [], [], []
        for hk in range(N_KV_HEADS):
            kcat, vcat = _kv_heads(kvp, kvc, hk)
            heads = range(hk * GQA_GROUP, (hk + 1) * GQA_GROUP)
            q4 = jnp.concatenate([q[:, h * HEAD_DIM:(h + 1) * HEAD_DIM] for h in heads], axis=0)
            do4 = jnp.concatenate([dov[:, h * HEAD_DIM:(h + 1) * HEAD_DIM] for h in heads], axis=0)
            bias4 = jnp.concatenate([bt_ref[h] for h in heads], axis=1)
            sink4 = jnp.concatenate([jnp.full((1, BLOCK), sink_ref[h], F32) for h in heads], axis=1)
            s = jnp.where(valid4, _dot_nt(kcat, q4) * ATTN_SCALE + bias4, NEG)
            m = jnp.maximum(jnp.max(s, axis=0, keepdims=True), sink4)
            p = jnp.exp(s - m)
            esink = jnp.exp(sink4 - m)
            inv = 1.0 / (jnp.sum(p, axis=0, keepdims=True) + esink)
            w = p * inv
            dvs.append(_dot(w.astype(BF16), do4))
            dw = _dot_nt(vcat, do4)
            delta = jnp.sum(w * dw, axis=0, keepdims=True)
            ds = w * (dw - delta)
            dsink = -(esink * inv) * delta
            for g, h in enumerate(heads):
                lanes = slice(g * BLOCK, (g + 1) * BLOCK)
                db_ref[h] += ds[:, lanes]
                ds_ref[h:h + 1, :] += jnp.broadcast_to(jnp.sum(dsink[:, lanes], axis=1, keepdims=True), (1, BLOCK))
            dsb = (ds * ATTN_SCALE).astype(BF16)
            dks.append(_dot(dsb, q4))
            dq4 = _dot_tn(dsb, kcat)
            dqs += [dq4[g * BLOCK:(g + 1) * BLOCK] for g in range(GQA_GROUP)]
        dq_ref[...] = jnp.concatenate(dqs, axis=1).astype(BF16)
        both = jnp.concatenate(dks + dvs, axis=1)
        dp_ref[...] = both[:BLOCK]
        dc_ref[...] = both[BLOCK:]

    blk = pl.BlockSpec((BLOCK, BRANCH), lambda n: (n, 0))
    kvb = pl.BlockSpec((BLOCK, KV_WIDTH), lambda n: (n, 0))
    keys_first = (N_Q_HEADS, 2 * BLOCK, BLOCK)
    specs = _attn_specs()
    specs[3] = _full(keys_first)
    (dq, dkv_cur, dkv_prev, dbias_t, dsinks), side_outs = _call(
        body, name="attn_bwd", steps=S // BLOCK, in_specs=[blk] + specs,
        out_specs=[blk, kvb, kvb, _full(keys_first), _full((N_Q_HEADS, BLOCK))],
        out_shape=[jax.ShapeDtypeStruct((S, BRANCH), BF16), jax.ShapeDtypeStruct((S, KV_WIDTH), F32),
                   jax.ShapeDtypeStruct((S, KV_WIDTH), F32), jax.ShapeDtypeStruct(keys_first, F32),
                   jax.ShapeDtypeStruct((N_Q_HEADS, BLOCK), F32)],
        scratch=[], args=(do, z, z, z, jnp.swapaxes(bias, 1, 2), sinks), side=side)
    return (dq, dkv_cur, dkv_prev, jnp.swapaxes(dbias_t, 1, 2), dsinks), side_outs


def kv_shift_add(dcur, dprev):
    S = dcur.shape[0]
    nt = S // TM
    per_tile = TM // BLOCK

    def body(c_ref, p_ref, n_ref, o_ref):
        nxt = jnp.where(pl.program_id(0) == nt - 1, 0.0, n_ref[...])
        o_ref[...] = (c_ref[...] + jnp.concatenate([p_ref[BLOCK:, :], nxt], axis=0)).astype(BF16)

    tile = pl.BlockSpec((TM, KV_WIDTH), lambda i: (i, 0))
    return pl.pallas_call(
        body, name="kv_shift_add", grid=(nt,),
        in_specs=[tile, tile,
                  pl.BlockSpec((BLOCK, KV_WIDTH), lambda i: (jnp.minimum((i + 1) * per_tile, S // BLOCK - 1), 0))],
        out_specs=tile, out_shape=jax.ShapeDtypeStruct((S, KV_WIDTH), BF16), compiler_params=_cparams("parallel"),
    )(dcur, dprev, dprev)


def _ssm_disc(lam_re, lam_im, log_dt, bt_re, bt_im):
    dt = jnp.exp(log_dt)
    mag = jnp.exp(lam_re * dt)
    ang = lam_im * dt
    a_re = mag * jnp.cos(ang)
    a_im = mag * jnp.sin(ang)
    den = lam_re * lam_re + lam_im * lam_im
    nr = a_re - 1.0
    coef_re = (nr * lam_re + a_im * lam_im) / den
    coef_im = (a_im * lam_re - nr * lam_im) / den
    bb_re = coef_re[:, None, :] * bt_re - coef_im[:, None, :] * bt_im
    bb_im = coef_re[:, None, :] * bt_im + coef_im[:, None, :] * bt_re
    return a_re, a_im, bb_re, bb_im


_GN = (SSM_GROUPS, SSM_STATE)
_GPN = (SSM_GROUPS, SSM_GROUP, SSM_STATE)


def ssm_disc_fwd(lam_re, lam_im, log_dt, bt_re, bt_im):
    def body(lr_ref, li_ref, dt_ref, br_ref, bi_ref, ar_ref, ai_ref, bbr_ref, bbi_ref):
        a_re, a_im, bb_re, bb_im = _ssm_disc(lr_ref[...], li_ref[...], dt_ref[...], br_ref[...], bi_ref[...])
        ar_ref[...] = a_re
        ai_ref[...] = a_im
        bbr_ref[...] = bb_re
        bbi_ref[...] = bb_im

    return pl.pallas_call(
        body, name="ssm_disc_fwd", grid=(1,),
        in_specs=[_full(_GN), _full(_GN), _full((SSM_GROUPS, 1)), _full(_GPN), _full(_GPN)],
        out_specs=[_full(_GN), _full(_GN), _full(_GPN), _full(_GPN)],
        out_shape=[jax.ShapeDtypeStruct(s, F32) for s in (_GN, _GN, _GPN, _GPN)],
        compiler_params=_cparams("arbitrary"),
    )(lam_re, lam_im, log_dt, bt_re, bt_im)


def ssm_disc_bwd(lam_re, lam_im, log_dt, bt_re, bt_im, da_re, da_im, dbb_re, dbb_im):
    def body(lr_ref, li_ref, dt_ref, br_ref, bi_ref, dar_ref, dai_ref, dbr_ref, dbi_ref, o_lr, o_li, o_dt, o_br, o_bi):
        prim = (lr_ref[...], li_ref[...], dt_ref[...], br_ref[...], bi_ref[...])
        _, vjp = jax.vjp(_ssm_disc, *prim)
        grads = vjp((dar_ref[...], dai_ref[...], dbr_ref[...], dbi_ref[...]))
        for r, v in zip((o_lr, o_li, o_dt, o_br, o_bi), grads):
            r[...] = v

    shapes = (_GN, _GN, (SSM_GROUPS, 1), _GPN, _GPN)
    return pl.pallas_call(
        body, name="ssm_disc_bwd", grid=(1,),
        in_specs=[_full(s) for s in shapes + (_GN, _GN, _GPN, _GPN)],
        out_specs=[_full(s) for s in shapes], out_shape=[jax.ShapeDtypeStruct(s, F32) for s in shapes],
        compiler_params=_cparams("arbitrary"),
    )(lam_re, lam_im, log_dt, bt_re, bt_im, da_re, da_im, dbb_re, dbb_im)


LANE_GROUPS = SSM_LANES // 128
SUB_GROUPS = SUB_ST // 128
_TM_SHAPE = (LANE_GROUPS, 128)


def _step_rows(t):
    return pl.ds(pl.multiple_of(t * LANE_GROUPS, LANE_GROUPS), LANE_GROUPS)


def _group_rows(j):
    return pl.ds(j, SCAN_T, stride=LANE_GROUPS)


def _store_sub(ref, j, val):
    for k in range(SUB_GROUPS):
        ref[_group_rows(j * SUB_GROUPS + k), :] = val[:, k * 128:(k + 1) * 128]


def _load_sub(ref, j):
    return jnp.concatenate([ref[_group_rows(j * SUB_GROUPS + k), :] for k in range(SUB_GROUPS)], axis=1)


_SUB_SHAPE_IN = (SSM_SUB, SUB_IN, SUB_ST)
_SUB_SHAPE_OUT = (SSM_SUB, SUB_ST, SUB_IN)


def ssm_fwd(z, bb_re, bb_im, ct_re, ct_im, a_re, a_im, d_skip, wglu, side=None):
    S = z.shape[0]
    cu = COL_U // BRANCH

    def body(u_ref, bbr_ref, bbi_ref, ctr_ref, cti_ref, ar_ref, ai_ref, d_ref, wg_ref,
             y_ref, ypre_ref, hr_ref, hi_ref, hrow_r, hrow_i, bur, bui, car_r, car_i):
        @pl.when(pl.program_id(0) == 0)
        def _():
            car_r[...] = jnp.zeros_like(car_r)
            car_i[...] = jnp.zeros_like(car_i)

        u = u_ref[...]
        for j in range(SSM_SUB):
            uj = u[:, j * SUB_IN:(j + 1) * SUB_IN]
            _store_sub(bur, j, _dot(uj, bbr_ref[j]))
            _store_sub(bui, j, _dot(uj, bbi_ref[j]))
        ar = ar_ref[...]
        ai = ai_ref[...]

        def step(t, carry):
            hr, hi = carry
            rows = _step_rows(t)
            nhr = ar * hr - ai * hi + bur[rows, :]
            nhi = ar * hi + ai * hr + bui[rows, :]
            hr_ref[rows, :] = nhr
            hi_ref[rows, :] = nhi
            return nhr, nhi

        hr, hi = lax.fori_loop(0, SCAN_T, step, (car_r[...], car_i[...]), unroll=8)
        car_r[...] = hr
        car_i[...] = hi
        ys = []
        for j in range(SSM_SUB):
            cs = slice(j * SUB_ST, (j + 1) * SUB_ST)
            hrow_r[:, cs] = _load_sub(hr_ref, j).astype(BF16)
            hrow_i[:, cs] = _load_sub(hi_ref, j).astype(BF16)
            ys.append(_dot(hrow_r[:, cs], ctr_ref[j]) - _dot(hrow_i[:, cs], cti_ref[j]))
        ypre = jnp.concatenate(ys, axis=1) + d_ref[...] * u.astype(F32)
        ypre_ref[...] = ypre
        g = jax.nn.gelu(ypre)
        y_ref[...] = (g * jax.nn.sigmoid(_dot(g.astype(BF16), wg_ref[...]))).astype(BF16)

    row = pl.BlockSpec((SCAN_T, BRANCH), lambda i: (i, 0))
    st = pl.BlockSpec((SCAN_T * LANE_GROUPS, 128), lambda i: (i, 0))
    wide = pl.BlockSpec((SCAN_T, SSM_LANES), lambda i: (i, 0))
    return _call(
        body, name="ssm_fwd", steps=S // SCAN_T,
        in_specs=[pl.BlockSpec((SCAN_T, BRANCH), lambda i: (i, cu)), _full(_SUB_SHAPE_IN), _full(_SUB_SHAPE_IN),
                  _full(_SUB_SHAPE_OUT), _full(_SUB_SHAPE_OUT), _full(_TM_SHAPE), _full(_TM_SHAPE), _full((1, BRANCH)),
                  _full((BRANCH, BRANCH))],
        out_specs=[row, row, st, st, wide, wide],
        out_shape=[jax.ShapeDtypeStruct((S, BRANCH), BF16), jax.ShapeDtypeStruct((S, BRANCH), F32),
                   jax.ShapeDtypeStruct((S * LANE_GROUPS, 128), F32), jax.ShapeDtypeStruct((S * LANE_GROUPS, 128), F32),
                   jax.ShapeDtypeStruct((S, SSM_LANES), BF16), jax.ShapeDtypeStruct((S, SSM_LANES), BF16)],
        scratch=[pltpu.VMEM((SCAN_T * LANE_GROUPS, 128), F32), pltpu.VMEM((SCAN_T * LANE_GROUPS, 128), F32),
                 pltpu.VMEM(_TM_SHAPE, F32), pltpu.VMEM(_TM_SHAPE, F32)],
        args=(z, bb_re, bb_im, ct_re, ct_im, a_re, a_im, d_skip, wglu), side=side)


def ssm_bwd(dy, z, ypre, h_re, h_im, hrow_re, hrow_im, bbt_re, bbt_im, c_re, c_im, a_re, a_im, d_skip, wglu, wglu_t,
            side=None):
    S = z.shape[0]
    nt = S // SCAN_T
    cu = COL_U // BRANCH

    def body(dy_ref, u_ref, ypre_ref, hr_ref, hi_ref, hpr_ref, hpi_ref, hrow_r, hrow_i, bbr_ref, bbi_ref, cr_ref, ci_ref,
             ar_ref, ai_ref, d_ref, wg_ref, wgt_ref,
             du_ref, dbbr_ref, dbbi_ref, dctr_ref, dcti_ref, dar_ref, dai_ref, dd_ref, dwg_ref,
             lr_scr, li_scr, car_r, car_i):
        step = pl.program_id(0)

        @pl.when(step == 0)
        def _():
            for r in (dbbr_ref, dbbi_ref, dctr_ref, dcti_ref, dar_ref, dai_ref, dd_ref, dwg_ref, car_r, car_i):
                r[...] = jnp.zeros_like(r)

        u = u_ref[...]
        uf = u.astype(F32)
        dyv = dy_ref[...].astype(F32)
        g, gelu_vjp = jax.vjp(jax.nn.gelu, ypre_ref[...])
        gb = g.astype(BF16)
        sg = jax.nn.sigmoid(_dot(gb, wg_ref[...]))
        dgl = (dyv * g * sg * (1.0 - sg)).astype(BF16)
        dwg_ref[...] += _dot_tn(gb, dgl)
        dg = dyv * sg + _dot(dgl, wgt_ref[...])
        dypre = gelu_vjp(dg)[0]
        dd_ref[...] += jnp.sum(dypre * uf, axis=0, keepdims=True)
        dyb = dypre.astype(BF16)
        for j in range(SSM_SUB):
            dyj = dyb[:, j * SUB_IN:(j + 1) * SUB_IN]
            _store_sub(lr_scr, j, _dot(dyj, cr_ref[j]))
            _store_sub(li_scr, j, -_dot(dyj, ci_ref[j]))
            cs = slice(j * SUB_ST, (j + 1) * SUB_ST)
            dctr_ref[j] += _dot_tn(hrow_r[:, cs], dyj)
            dcti_ref[j] -= _dot_tn(hrow_i[:, cs], dyj)

        ar = ar_ref[...]
        ai = ai_ref[...]

        def adjoint(lr, li, rows):
            nlr = ar * lr + ai * li + lr_scr[rows, :]
            nli = ar * li - ai * lr + li_scr[rows, :]
            lr_scr[rows, :] = nlr
            li_scr[rows, :] = nli
            return nlr, nli

        def back(k, carry):
            lr, li, acc_r, acc_i = carry
            t = SCAN_T - 1 - k
            lr, li = adjoint(lr, li, _step_rows(t))
            hpr = hr_ref[_step_rows(t - 1), :]
            hpi = hi_ref[_step_rows(t - 1), :]
            return lr, li, acc_r + lr * hpr + li * hpi, acc_i + li * hpr - lr * hpi

        zero = jnp.zeros(_TM_SHAPE, F32)
        lr, li, acc_r, acc_i = lax.fori_loop(0, SCAN_T - 1, back, (car_r[...], car_i[...], zero, zero), unroll=8)
        lr, li = adjoint(lr, li, pl.ds(0, LANE_GROUPS))
        car_r[...] = lr
        car_i[...] = li
        first_tile = step == nt - 1
        hpr = jnp.where(first_tile, 0.0, hpr_ref[...])
        hpi = jnp.where(first_tile, 0.0, hpi_ref[...])
        dar_ref[...] += acc_r + lr * hpr + li * hpi
        dai_ref[...] += acc_i + li * hpr - lr * hpi

        dus = []
        for j in range(SSM_SUB):
            lrb = _load_sub(lr_scr, j).astype(BF16)
            lib = _load_sub(li_scr, j).astype(BF16)
            uj = u[:, j * SUB_IN:(j + 1) * SUB_IN]
            dus.append(_dot(lrb, bbr_ref[j]) + _dot(lib, bbi_ref[j]))
            dbbr_ref[j] += _dot_tn(uj, lrb)
            dbbi_ref[j] += _dot_tn(uj, lib)
        du_ref[...] = (jnp.concatenate(dus, axis=1) + dypre * d_ref[...]).astype(BF16)

    def rev(i):
        return nt - 1 - i

    row = pl.BlockSpec((SCAN_T, BRANCH), lambda i: (rev(i), 0))
    st = pl.BlockSpec((SCAN_T * LANE_GROUPS, 128), lambda i: (rev(i), 0))
    before = pl.BlockSpec(_TM_SHAPE, lambda i: (jnp.maximum(rev(i) * SCAN_T - 1, 0), 0))
    wide = pl.BlockSpec((SCAN_T, SSM_LANES), lambda i: (rev(i), 0))
    tm = _full(_TM_SHAPE)
    return _call(
        body, name="ssm_bwd", steps=nt,
        in_specs=[row, pl.BlockSpec((SCAN_T, BRANCH), lambda i: (rev(i), cu)), row, st, st, before, before, wide, wide,
                  _full(_SUB_SHAPE_OUT), _full(_SUB_SHAPE_OUT), _full(_SUB_SHAPE_IN), _full(_SUB_SHAPE_IN),
                  tm, tm, _full((1, BRANCH)), _full((BRANCH, BRANCH)), _full((BRANCH, BRANCH))],
        out_specs=[row, _full(_SUB_SHAPE_IN), _full(_SUB_SHAPE_IN), _full(_SUB_SHAPE_OUT), _full(_SUB_SHAPE_OUT),
                   tm, tm, _full((1, BRANCH)), _full((BRANCH, BRANCH))],
        out_shape=[jax.ShapeDtypeStruct((S, BRANCH), BF16), jax.ShapeDtypeStruct(_SUB_SHAPE_IN, F32),
                   jax.ShapeDtypeStruct(_SUB_SHAPE_IN, F32), jax.ShapeDtypeStruct(_SUB_SHAPE_OUT, F32),
                   jax.ShapeDtypeStruct(_SUB_SHAPE_OUT, F32), jax.ShapeDtypeStruct(_TM_SHAPE, F32),
                   jax.ShapeDtypeStruct(_TM_SHAPE, F32), jax.ShapeDtypeStruct((1, BRANCH), F32),
                   jax.ShapeDtypeStruct((BRANCH, BRANCH), F32)],
        scratch=[pltpu.VMEM((SCAN_T * LANE_GROUPS, 128), F32), pltpu.VMEM((SCAN_T * LANE_GROUPS, 128), F32),
                 pltpu.VMEM(_TM_SHAPE, F32), pltpu.VMEM(_TM_SHAPE, F32)],
        args=(dy, z, ypre, h_re, h_im, h_re, h_im, hrow_re, hrow_im, bbt_re, bbt_im, c_re, c_im, a_re, a_im, d_skip, wglu,
              wglu_t),
        side=side)


def _blockdiag(x):
    gs = SSM_GROUPS // SSM_SUB
    x = x.reshape(SSM_SUB, gs, SSM_GROUP, SSM_STATE)
    eye = jnp.eye(gs, dtype=x.dtype)
    return (x[:, :, :, None, :] * eye[None, :, None, :, None]).reshape(SSM_SUB, SUB_IN, SUB_ST)


def _blockdiag_extract(x):
    gs = SSM_GROUPS // SSM_SUB
    x = x.reshape(SSM_SUB, gs, SSM_GROUP, gs, SSM_STATE)
    eye = jnp.eye(gs, dtype=x.dtype)
    return jnp.sum(x * eye[None, :, None, :, None], axis=3).reshape(SSM_GROUPS, SSM_GROUP, SSM_STATE)


def loss_head(x, g, target):
    S = x.shape[0]

    def body(x_ref, g_ref, t_ref, dx_ref, loss_ref, dg_ref):
        @pl.when(pl.program_id(0) == 0)
        def _():
            loss_ref[...] = jnp.zeros_like(loss_ref)
            dg_ref[...] = jnp.zeros_like(dg_ref)

        xv = x_ref[...]
        gv = g_ref[...]
        r = lax.rsqrt(jnp.mean(xv * xv, axis=-1, keepdims=True) + RMS_EPS)
        xhat = xv * r
        err = xhat * gv - t_ref[...]
        loss_ref[...] += jnp.sum((err * err).reshape(TM // 8, 8, D_MODEL), axis=0) * (0.5 / D_MODEL)
        dy = err * (1.0 / D_MODEL)
        dxhat = dy * gv
        dx_ref[...] = r * (dxhat - xhat * jnp.mean(dxhat * xhat, axis=-1, keepdims=True))
        dg_ref[...] += jnp.sum((dy * xhat).reshape(TM // 8, 8, D_MODEL), axis=0)

    row = pl.BlockSpec((TM, D_MODEL), lambda i: (i, 0))
    acc = _full((8, D_MODEL))
    return pl.pallas_call(
        body, name="loss_head", grid=(S // TM,), in_specs=[row, _full((1, D_MODEL)), row],
        out_specs=[row, acc, acc],
        out_shape=[jax.ShapeDtypeStruct((S, D_MODEL), F32), jax.ShapeDtypeStruct((8, D_MODEL), F32),
                   jax.ShapeDtypeStruct((8, D_MODEL), F32)],
        compiler_params=_cparams("arbitrary"),
    )(x, g, target)


def _x_spec():
    return pl.BlockSpec((TM, D_MODEL), lambda i: (i, 0))


def _g_spec():
    return pl.BlockSpec((1, D_MODEL), lambda i: (0, 0))


def _norm_prologue(x, g):
    h = _rms(x, g).astype(BF16)
    return h, h


def _cast_prologue(x):
    return (x.astype(BF16),)


def _swiglu_prologue(h1, h2):
    a = h1.astype(F32)
    act = (a * jax.nn.sigmoid(a) * h2.astype(F32)).astype(BF16)
    return act, act


def _ssm_consts(lw):
    a_re, a_im, bbt_re, bbt_im = ssm_disc_fwd(
        lw["ssm_lambda_re"], lw["ssm_lambda_im"], lw["ssm_log_dt"].reshape(SSM_GROUPS, 1), lw["bt_re"], lw["bt_im"])
    bb_re = _blockdiag(bbt_re).astype(BF16)
    bb_im = _blockdiag(bbt_im).astype(BF16)
    c_re = _blockdiag(lw["ssm_c_re"]).astype(BF16)
    c_im = _blockdiag(lw["ssm_c_im"]).astype(BF16)
    return dict(
        a_re=a_re.reshape(_TM_SHAPE), a_im=a_im.reshape(_TM_SHAPE),
        bb_re=bb_re, bb_im=bb_im, bbt_re=jnp.swapaxes(bb_re, 1, 2), bbt_im=jnp.swapaxes(bb_im, 1, 2),
        c_re=c_re, c_im=c_im, ct_re=jnp.swapaxes(c_re, 1, 2), ct_im=jnp.swapaxes(c_im, 1, 2))


def layer_fwd(x, lw, bias, next_shards=None, place=None):
    sides = (None, None) if next_shards is None else (("gather_chips", next_shards[:GATHER_SPLIT]),
                                                      ("gather_chips", next_shards[GATHER_SPLIT:]))
    out = fused_mm("in_proj", [x, lw["norm_mix"]], [_x_spec(), _g_spec()], _norm_prologue, lw["w_in"], tn=2944,
                   out_dtype=BF16, extras=((D_MODEL, BF16),), side=sides[0])
    (z, h), g4a = (out, []) if next_shards is None else out
    sc = _ssm_consts(lw)
    (y_ssm, ypre, h_re, h_im, hrow_re, hrow_im), g4b = ssm_fwd(
        z, sc["bb_re"], sc["bb_im"], sc["ct_re"], sc["ct_im"], sc["a_re"], sc["a_im"], lw["ssm_d"], lw["ssm_w_glu"],
        side=sides[1])
    y_conv = conv_fwd(z, lw["conv_w"])
    if next_shards is not None:
        g4 = [_put_slot(g, b, place[0]) for g, b in zip(list(g4a) + list(g4b), next_shards)]
    y_attn, g8 = attn_fwd(z, bias, lw["attn_sinks"], side=None if next_shards is None else ("gather_cores", g4))
    next_gathered = None if next_shards is None else [_put_slot(g, b, place[1]) for g, b in zip(g8, g4)]
    merged = merge_fwd(z, y_ssm, y_conv, y_attn, lw["w_branch"])
    x1 = fused_mm("out_proj", [merged], [_x_spec()], _cast_prologue, lw["w_out"], tn=1024, out_dtype=F32, res=x)
    hf, hn1 = fused_mm("ffn_in", [x1, lw["norm_ffn"]], [_x_spec(), _g_spec()], _norm_prologue, lw["w_ffn_in"], tn=2816,
                       out_dtype=BF16, extras=((D_MODEL, BF16),))
    x2, act = fused_mm("ffn_out", [hf, hf], [_row_spec(FFN_HIDDEN, 0), _row_spec(FFN_HIDDEN, 1)], _swiglu_prologue,
                       lw["w_ffn_out"], tn=1024, out_dtype=F32, res=x1, extras=((FFN_HIDDEN, BF16),))
    a_pre, hn2 = fused_mm("ple_gate", [x2, lw["norm_ple"]], [_x_spec(), _g_spec()], _norm_prologue, lw["w_ple_gate"],
                          tn=1024, out_dtype=BF16, extras=((D_MODEL, BF16),))
    pp = fused_mm("ple_proj", [lw["p"]], [_row_spec(PLE_DIM)], _cast_prologue, lw["w_ple_proj"], tn=1024, out_dtype=BF16)
    x3 = ple_combine(x2, a_pre, pp)
    res = dict(x=x, z=z, h=h, y_ssm=y_ssm, ypre=ypre, h_re=h_re, h_im=h_im, hrow_re=hrow_re, hrow_im=hrow_im, y_conv=y_conv, y_attn=y_attn, merged=merged,
               x1=x1, hf=hf, hn1=hn1, act=act, x2=x2, a_pre=a_pre, hn2=hn2, pp=pp)
    return x3, res, next_gathered


def _pair_sums(split, from_sibling, names):
    return [pair_sum("pair_sum_" + n, a.reshape(2, -1, a.shape[-1]), b.reshape(-1, b.shape[-1])).reshape(b.shape)
            for n, a, b in zip(names, split, from_sibling)]


def layer_bwd(dx3, lw, res, bias, pending=None, scatter_own=False):
    g = {}
    wide = ((D_MODEL, BF16), (D_MODEL, BF16))
    dx2, g["norm_ple"], da, dpp = mm_norm_bwd(
        "d_ple_gate", [dx3, res["a_pre"], res["pp"]], [_row_spec(D_MODEL)] * 3, lw["w_ple_gate"], res["x2"],
        lw["norm_ple"], dx3, tm=TM, pre=_ple_bwd_pre, extras=wide)
    g["w_ple_proj"] = mm_tn("d_w_ple_proj", lw["p"], dpp)
    g["w_ple_gate"] = mm_tn("d_w_ple_gate", res["hn2"], da)
    sums = None
    if pending is None:
        dact = fused_mm("d_ffn_out", [dx2], [_x_spec()], _cast_prologue, lw["w_ffn_out"], tn=1408, out_dtype=BF16,
                        nt=True)
    else:
        dact, from_sibling = fused_mm("d_ffn_out", [dx2], [_x_spec()], _cast_prologue, lw["w_ffn_out"], tn=1408,
                                      out_dtype=BF16, side=("scatter_cores", pending), nt=True)
        sums = _pair_sums(pending, from_sibling, SHARDED_NAMES)
    g["w_ffn_out"] = mm_tn("d_w_ffn_out", res["act"], dx2)
    half = TM // 2
    dx1, g["norm_ffn"], dh1, dh2 = mm_norm_bwd(
        "d_ffn_in", [dact, res["hf"], res["hf"]],
        [_row_spec(FFN_HIDDEN, 0, half), _row_spec(FFN_HIDDEN, 0, half), _row_spec(FFN_HIDDEN, 1, half)],
        lw["w_ffn_in"], res["x1"], lw["norm_ffn"], dx2, tm=half, pre=_swiglu_bwd_pre,
        extras=((FFN_HIDDEN, BF16), (FFN_HIDDEN, BF16)))
    g["w_ffn_in"] = jnp.concatenate([mm_tn("d_w_ffn_in_a", res["hn1"], dh1), mm_tn("d_w_ffn_in_b", res["hn1"], dh2)],
                                    axis=1)
    dmerged = fused_mm("d_out_proj", [dx1], [_x_spec()], _cast_prologue, lw["w_out"], tn=1024, out_dtype=BF16, nt=True)
    g["w_out"] = mm_tn("d_w_out", res["merged"], dx1)
    z = res["z"]
    ys = (res["y_ssm"], res["y_conv"], res["y_attn"])
    own_sums = None
    if scatter_own:
        late = SHARDED_NAMES[GATHER_SPLIT:]
        own = [_shard_split(g[n], n).astype(BF16) for n in late]
        (dgates, dbs, dys), from_sibling = merge_bwd(dmerged, z, *ys, lw["w_branch"], side=("scatter_cores", own))
        own_sums = _pair_sums(own, from_sibling, late)
    else:
        (dgates, dbs, dys), _ = merge_bwd(dmerged, z, *ys, lw["w_branch"])
    g["w_branch"] = jnp.stack([mm_tn(f"d_w_branch_{r}", ys[r], dbs[r]) for r in range(3)])
    sc = _ssm_consts(lw)
    (du, dbb_re, dbb_im, dct_re, dct_im, da_re, da_im, g["ssm_d"], g["ssm_w_glu"]), received = ssm_bwd(
        dys[0], z, res["ypre"], res["h_re"], res["h_im"], res["hrow_re"], res["hrow_im"], sc["bbt_re"], sc["bbt_im"], sc["c_re"], sc["c_im"],
        sc["a_re"], sc["a_im"], lw["ssm_d"], lw["ssm_w_glu"], lw["ssm_w_glu_t"],
        side=None if pending is None else ("scatter_chips", sums))
    g["ssm_c_re"] = _blockdiag_extract(jnp.swapaxes(dct_re, 1, 2))
    g["ssm_c_im"] = _blockdiag_extract(jnp.swapaxes(dct_im, 1, 2))
    (g["ssm_lambda_re"], g["ssm_lambda_im"], dlog_dt, g["bt_re"], g["bt_im"]) = ssm_disc_bwd(
        lw["ssm_lambda_re"], lw["ssm_lambda_im"], lw["ssm_log_dt"].reshape(SSM_GROUPS, 1), lw["bt_re"], lw["bt_im"],
        da_re.reshape(_GN), da_im.reshape(_GN),
        _blockdiag_extract(dbb_re), _blockdiag_extract(dbb_im))
    g["ssm_log_dt"] = dlog_dt.reshape(SSM_GROUPS)
    dconv, g["conv_w"] = conv_bwd(dys[1], z, lw["conv_w"])
    (dq, dkv_cur, dkv_prev, g["dbias"], g["attn_sinks"]), own_received = attn_bwd(
        dys[2], z, bias, lw["attn_sinks"], side=None if own_sums is None else ("scatter_chips", own_sums))
    dkv = kv_shift_add(dkv_cur, dkv_prev)
    pieces = [dgates[0], dgates[1], dgates[2], du, dconv, dq, dkv]
    g["w_in"] = jnp.concatenate(mm_tn_multi("d_w_in_gates", res["h"], pieces[:3])
                                + mm_tn_multi("d_w_in_branches", res["h"], pieces[3:]), axis=1)
    dx0, g["norm_mix"] = mm_norm_bwd("d_in_proj", pieces, [_row_spec(pc.shape[1], tm=half) for pc in pieces], lw["w_in"],
                                     res["x"], lw["norm_mix"], dx1, tm=half)
    return dx0, g, (sums, received), (own_sums, own_received)


def adamw(name, parts, w, m, v):
    n, R, C = parts.shape
    tr = _pick(R, (512, 256, 128, 64, 32, 16, 8))

    def body(p_ref, w_ref, m_ref, v_ref, g_ref, d_ref, nm_ref, nv_ref):
        gsum = p_ref[0].astype(F32)
        for k in range(1, n):
            gsum = gsum + p_ref[k].astype(F32)
        mn = ADAM_B1 * m_ref[...] + (1.0 - ADAM_B1) * gsum
        vn = ADAM_B2 * v_ref[...] + (1.0 - ADAM_B2) * jnp.square(gsum)
        m_hat = mn / (1.0 - ADAM_B1 ** ADAM_STEP)
        v_hat = vn / (1.0 - ADAM_B2 ** ADAM_STEP)
        g_ref[...] = gsum
        d_ref[...] = -ADAM_LR * (m_hat / (jnp.sqrt(v_hat) + ADAM_EPS) + ADAM_WD * w_ref[...])
        nm_ref[...] = mn
        nv_ref[...] = vn

    blk = pl.BlockSpec((tr, C), lambda i: (i, 0))
    return pl.pallas_call(
        body, name=name, grid=(R // tr,), in_specs=[pl.BlockSpec((n, tr, C), lambda i: (0, i, 0)), blk, blk, blk],
        out_specs=[blk] * 4, out_shape=[jax.ShapeDtypeStruct((R, C), F32)] * 4, compiler_params=_cparams("parallel"),
    )(parts, w, m, v)


_ANY = pl.BlockSpec(memory_space=pl.ANY)


def _coords():
    return lax.axis_index("x"), lax.axis_index("y"), lax.axis_index("c")


def _chip_peers(x, y):
    return [(1 - x, y), (x, 1 - y), (1 - x, 1 - y)]


def _gather_chips_copies(x_refs, out_refs, send_sems, recv_sems):
    x, y, c = _coords()
    me = 2 * x + y
    peers = _chip_peers(x, y)

    def copy(i, k, slot):
        return pltpu.make_async_remote_copy(
            src_ref=x_refs[i], dst_ref=out_refs[i].at[slot], send_sem=send_sems.at[3 * i + k],
            recv_sem=recv_sems.at[3 * i + k], device_id=(*peers[k], c), device_id_type=MESH)

    n = len(x_refs)
    sends = [copy(i, k, me) for i in range(n) for k in range(3)]
    recvs = [copy(i, k, 2 * px + py) for i in range(n) for k, (px, py) in enumerate(peers)]
    return sends, recvs


def _gather_cores_copies(x_refs, out_refs, send_sems, recv_sems):
    x, y, c = _coords()

    def copy(i, slot):
        return pltpu.make_async_remote_copy(
            src_ref=x_refs[i], dst_ref=out_refs[i].at[slot], send_sem=send_sems.at[i], recv_sem=recv_sems.at[i],
            device_id=(x, y, 1 - c), device_id_type=MESH)

    n = len(x_refs)
    return [copy(i, c) for i in range(n)], [copy(i, 1 - c) for i in range(n)]


def _scatter_cores_copies(x_refs, out_refs, send_sems, recv_sems):
    x, y, c = _coords()
    copies = [pltpu.make_async_remote_copy(
        src_ref=x_refs[i].at[1 - c], dst_ref=out_refs[i], send_sem=send_sems.at[i], recv_sem=recv_sems.at[i],
        device_id=(x, y, 1 - c), device_id_type=MESH) for i in range(len(x_refs))]
    return copies, copies


def _scatter_chips_copies(x_refs, out_refs, send_sems, recv_sems):
    x, y, c = _coords()
    me = 2 * x + y
    peers = _chip_peers(x, y)

    def copy(i, k, src_slot, dst_slot):
        return pltpu.make_async_remote_copy(
            src_ref=x_refs[i].at[src_slot], dst_ref=out_refs[i].at[dst_slot], send_sem=send_sems.at[3 * i + k],
            recv_sem=recv_sems.at[3 * i + k], device_id=(*peers[k], c), device_id_type=MESH)

    n = len(x_refs)
    sends = [copy(i, k, 2 * px + py, me) for i in range(n) for k, (px, py) in enumerate(peers)]
    recvs = [copy(i, k, me, 2 * px + py) for i in range(n) for k, (px, py) in enumerate(peers)]
    return sends, recvs


_EXCHANGES = {
    "gather_chips": (lambda b: (4,) + b.shape, 3, _gather_chips_copies),
    "gather_cores": (lambda b: (2,) + b.shape, 1, _gather_cores_copies),
    "scatter_cores": (lambda b: b.shape[1:], 1, _scatter_cores_copies),
    "scatter_chips": (lambda b: b.shape, 3, _scatter_chips_copies),
}


def exchange(name, kind, blks):
    out_shape_of, per_block, make_copies = _EXCHANGES[kind]
    n = len(blks)

    def body(*refs):
        sends, recvs = make_copies(refs[:n], refs[n:2 * n], refs[2 * n], refs[2 * n + 1])
        for cp in sends:
            cp.start()
        for cp in recvs:
            cp.wait_recv()
        for cp in sends:
            cp.wait_send()

    return pl.pallas_call(
        body, name=name, in_specs=[_ANY] * n, out_specs=[_ANY] * n,
        out_shape=[jax.ShapeDtypeStruct(out_shape_of(b), b.dtype) for b in blks],
        scratch_shapes=[pltpu.SemaphoreType.DMA((per_block * n,)), pltpu.SemaphoreType.DMA((per_block * n,))],
    )(*blks)


def _put_slot(buf, block, idx):
    return lax.dynamic_update_slice(buf, block[None].astype(buf.dtype), (idx,) + (0,) * block.ndim)


def pair_sum(name, mine, theirs):
    _, R, C = mine.shape
    tr = _pick(R, (1024, 512, 256, 128, 64, 32, 16))
    c_idx = lax.axis_index("c").astype(jnp.int32).reshape(1)

    def body(c_ref, a_ref, b_ref, o_ref):
        o_ref[...] = (a_ref[0].astype(F32) + b_ref[...].astype(F32)).astype(BF16)

    return pl.pallas_call(
        body, name=name,
        grid_spec=pltpu.PrefetchScalarGridSpec(
            num_scalar_prefetch=1, grid=(R // tr,),
            in_specs=[pl.BlockSpec((1, tr, C), lambda i, c: (c[0], i, 0)), pl.BlockSpec((tr, C), lambda i, c: (i, 0))],
            out_specs=pl.BlockSpec((tr, C), lambda i, c: (i, 0))),
        out_shape=jax.ShapeDtypeStruct(theirs.shape, BF16), compiler_params=_cparams("parallel"),
    )(c_idx, mine, theirs)


SHARDED = {
    "w_in": ((D_MODEL, IN_WIDTH), 2), "ssm_w_glu": ((BRANCH, BRANCH), 1), "conv_w": ((3, BRANCH), 2),
    "w_branch": ((3, BRANCH, D_MODEL), 3), "w_out": ((D_MODEL, D_MODEL), 1), "w_ffn_in": ((D_MODEL, 2 * FFN_HIDDEN), 2),
    "w_ffn_out": ((FFN_HIDDEN, D_MODEL), 1), "w_ple_gate": ((D_MODEL, D_MODEL), 1), "w_ple_proj": ((PLE_DIM, D_MODEL), 2),
}
SMALL = ["rel_bias", "norm_mix", "ssm_lambda_re", "ssm_lambda_im", "ssm_b_re", "ssm_b_im", "ssm_c_re", "ssm_c_im", "ssm_d",
         "ssm_log_dt", "attn_sinks", "norm_ffn", "norm_ple", "norm_final"]
SHARDED_NAMES = list(SHARDED)
GATHER_SPLIT = 4
WEIGHTS = ["rel_bias", "norm_mix", "w_in", "ssm_lambda_re", "ssm_lambda_im", "ssm_b_re", "ssm_b_im", "ssm_c_re", "ssm_c_im",
           "ssm_d", "ssm_log_dt", "ssm_w_glu", "conv_w", "attn_sinks", "w_branch", "w_out", "norm_ffn", "w_ffn_in",
           "w_ffn_out", "norm_ple", "w_ple_gate", "w_ple_proj", "norm_final"]


def _pad_to(flat, n):
    return jnp.pad(flat, [(0, 0)] * (flat.ndim - 1) + [(0, n - flat.shape[-1])])


def _unshard(g8, name):
    axis = SHARDED[name][1] - 1
    shard = g8.shape[2:]
    b = g8.reshape((2, 2, 2) + shard)
    b = jnp.moveaxis(b, (1, 2, 0), (axis, axis + 1, axis + 2))
    full = list(shard)
    full[axis] *= N_DEV
    return b.reshape(full)


def _shard_split(full, name):
    axis = SHARDED[name][1] - 1
    dims = list(full.shape)
    dims[axis:axis + 1] = [2, 2, 2, dims[axis] // N_DEV]
    b = jnp.moveaxis(full.reshape(dims), (axis, axis + 1, axis + 2), (1, 2, 0))
    return b.reshape((2, 4) + b.shape[3:])


def _small_sizes(shapes):
    return [-(-int(np.prod(shapes[n])) // 128) * 128 for n in SMALL]


def pack_small(vals, shapes, extra):
    segs = [_pad_to(vals[n].reshape(-1).astype(F32), s) for n, s in zip(SMALL, _small_sizes(shapes))]
    segs.append(_pad_to(extra.reshape(-1), 128))
    flat = jnp.concatenate(segs)
    rows = -(-flat.shape[0] // (128 * 8)) * 8
    return _pad_to(flat, rows * 128).reshape(rows, 128)


def unpack_small(packed, shapes):
    flat = packed.reshape(-1)
    out, off = {}, 0
    for n, s in zip(SMALL, _small_sizes(shapes)):
        out[n] = flat[off:off + int(np.prod(shapes[n]))].reshape(shapes[n])
        off += s
    return out, flat[off]


def _layer_weights(gathered, small, p, i):
    full = {n: _unshard(g, n) for n, g in zip(SHARDED, gathered)}
    w_in = full["w_in"]
    w_in_p = jnp.concatenate([w_in[:, 2816:], w_in[:, :2560], w_in[:, 2560:2816]], axis=-1)
    return dict(
        w_in=w_in_p,
        ssm_w_glu=full["ssm_w_glu"], ssm_w_glu_t=full["ssm_w_glu"].T,
        conv_w=full["conv_w"],
        w_branch=full["w_branch"], w_out=full["w_out"], w_ffn_in=full["w_ffn_in"], w_ffn_out=full["w_ffn_out"],
        w_ple_gate=full["w_ple_gate"],
        w_ple_proj=full["w_ple_proj"],
        norm_mix=small["norm_mix"][i][None, :], norm_ffn=small["norm_ffn"][i][None, :],
        norm_ple=small["norm_ple"][i][None, :],
        ssm_lambda_re=small["ssm_lambda_re"][i], ssm_lambda_im=small["ssm_lambda_im"][i],
        ssm_log_dt=small["ssm_log_dt"][i],
        bt_re=jnp.swapaxes(small["ssm_b_re"][i], 1, 2), bt_im=jnp.swapaxes(small["ssm_b_im"][i], 1, 2),
        ssm_c_re=small["ssm_c_re"][i], ssm_c_im=small["ssm_c_im"][i], ssm_d=small["ssm_d"][i][None, :],
        attn_sinks=small["attn_sinks"][i], p=p[i],
    )


def matrix_grads(g):
    w_in_g = g["w_in"]
    return dict(
        w_in=jnp.concatenate([w_in_g[:, 3072:5632], w_in_g[:, 5632:], w_in_g[:, :3072]], axis=-1),
        ssm_w_glu=g["ssm_w_glu"], conv_w=jnp.sum(g["conv_w"].reshape(3, 8, BRANCH), axis=1),
        w_branch=g["w_branch"], w_out=g["w_out"], w_ffn_in=g["w_ffn_in"], w_ffn_out=g["w_ffn_out"],
        w_ple_gate=g["w_ple_gate"], w_ple_proj=g["w_ple_proj"])


def small_grads(per_layer, dg_final):
    keys = ("dbias", "norm_mix", "ssm_lambda_re", "ssm_lambda_im", "bt_re", "bt_im", "ssm_c_re", "ssm_c_im", "ssm_d",
            "ssm_log_dt", "attn_sinks", "norm_ffn", "norm_ple")
    g = {k: jnp.stack([gl[k] for gl in per_layer]) for k in keys}
    drel = rel_bias_bwd(g["dbias"].reshape(DEPTH, N_Q_HEADS, BLOCK * 2 * BLOCK)).T
    return dict(
        rel_bias=drel, norm_mix=jnp.sum(g["norm_mix"], axis=1), ssm_lambda_re=g["ssm_lambda_re"],
        ssm_lambda_im=g["ssm_lambda_im"], ssm_b_re=jnp.swapaxes(g["bt_re"], 2, 3), ssm_b_im=jnp.swapaxes(g["bt_im"], 2, 3),
        ssm_c_re=g["ssm_c_re"], ssm_c_im=g["ssm_c_im"], ssm_d=g["ssm_d"][:, 0, :], ssm_log_dt=g["ssm_log_dt"],
        attn_sinks=g["attn_sinks"][:, :, 0], norm_ffn=jnp.sum(g["norm_ffn"], axis=1), norm_ple=jnp.sum(g["norm_ple"], axis=1),
        norm_final=jnp.sum(dg_final, axis=0))


def kernel(x, p, rel_bias, norm_mix, w_in, ssm_lambda_re, ssm_lambda_im, ssm_b_re, ssm_b_im, ssm_c_re, ssm_c_im, ssm_d, ssm_log_dt, ssm_w_glu, conv_w, attn_sinks, w_branch, w_out, norm_ffn, w_ffn_in, w_ffn_out, norm_ple, w_ple_gate, w_ple_proj, norm_final, loss_target, m_rel_bias, m_norm_mix, m_w_in, m_ssm_lambda_re, m_ssm_lambda_im, m_ssm_b_re, m_ssm_b_im, m_ssm_c_re, m_ssm_c_im, m_ssm_d, m_ssm_log_dt, m_ssm_w_glu, m_conv_w, m_attn_sinks, m_w_branch, m_w_out, m_norm_ffn, m_w_ffn_in, m_w_ffn_out, m_norm_ple, m_w_ple_gate, m_w_ple_proj, m_norm_final, v_rel_bias, v_norm_mix, v_w_in, v_ssm_lambda_re, v_ssm_lambda_im, v_ssm_b_re, v_ssm_b_im, v_ssm_c_re, v_ssm_c_im, v_ssm_d, v_ssm_log_dt, v_ssm_w_glu, v_conv_w, v_attn_sinks, v_w_branch, v_w_out, v_norm_ffn, v_w_ffn_in, v_w_ffn_out, v_norm_ple, v_w_ple_gate, v_w_ple_proj, v_norm_final):
    args = dict(locals())
    w = {n: args[n] for n in WEIGHTS}
    m = {n: args["m_" + n] for n in WEIGHTS}
    v = {n: args["v_" + n] for n in WEIGHTS}
    shapes = {n: w[n].shape for n in SMALL}

    x_i, y_i, c_i = _coords()
    chip = 2 * x_i + y_i
    place = (chip, c_i)
    small = {n: w[n] for n in SMALL}

    def all_gather(tag, blks):
        g4 = exchange(f"gather_{tag}_chips", "gather_chips", blks)
        g4 = [_put_slot(g, b, chip) for g, b in zip(g4, blks)]
        g8 = exchange(f"gather_{tag}_cores", "gather_cores", g4)
        return [_put_slot(g, b, c_i) for g, b in zip(g8, g4)]

    def shards_of(layer):
        return [w[n][layer] if n == "conv_w" else w[n][layer].astype(BF16) for n in SHARDED]

    bias = rel_bias_fwd(small["rel_bias"].T).reshape(N_Q_HEADS, BLOCK, 2 * BLOCK)
    xs, layers, res = x[0], [], []
    gathered = all_gather("w", shards_of(0))
    for layer in range(DEPTH):
        layers.append(_layer_weights(gathered, small, p[:, 0], layer))
        nxt = shards_of(layer + 1) if layer + 1 < DEPTH else None
        xs, res_l, gathered = layer_fwd(xs, layers[layer], bias, nxt, place)
        res.append(res_l)
    grad_x, loss_parts, dg_final = loss_head(xs, small["norm_final"][None, :], loss_target[0])

    per_layer, reduced, pending = [None] * DEPTH, [None] * DEPTH, None
    for layer in reversed(range(DEPTH)):
        grad_x, per_layer[layer], done, own_done = layer_bwd(grad_x, layers[layer], res[layer], bias, pending,
                                                             scatter_own=layer == 0)
        if pending is not None:
            reduced[layer + 1] = done
        mg = matrix_grads(per_layer[layer])
        pending = [_shard_split(mg[n], n).astype(BF16) for n in SHARDED]
    early = pending[:GATHER_SPLIT]
    sums = _pair_sums(early, exchange("scatter_g_cores", "scatter_cores", early), SHARDED_NAMES[:GATHER_SPLIT])
    received = exchange("scatter_g_chips", "scatter_chips", sums)
    reduced[0] = (list(sums) + list(own_done[0]), list(received) + list(own_done[1]))

    outs = ({}, {}, {}, {})
    for k, name in enumerate(SHARDED):
        parts = jnp.stack([_put_slot(rcv[k], lax.dynamic_index_in_dim(sm[k], chip, 0, keepdims=False), chip)
                           for sm, rcv in reduced], axis=1)
        cols = parts.shape[-1]
        res4 = adamw("adamw_" + name, parts.reshape(4, -1, cols), w[name].reshape(-1, cols), m[name].reshape(-1, cols),
                     v[name].reshape(-1, cols))
        for d, o in zip(outs, res4):
            d[name] = o.reshape(w[name].shape)

    small_local = pack_small(small_grads(per_layer, dg_final), shapes, jnp.sum(loss_parts))
    small_all = all_gather("s", [small_local])[0]
    zero = jnp.zeros((1,), F32)
    res4 = adamw("adamw_small", small_all.reshape(N_DEV, small_local.shape[0], 128),
                 pack_small({n: w[n] for n in SMALL}, shapes, zero), pack_small({n: m[n] for n in SMALL}, shapes, zero),
                 pack_small({n: v[n] for n in SMALL}, shapes, zero))
    loss = None
    for d, r in zip(outs, res4):
        vals, extra = unpack_small(r, shapes)
        d.update(vals)
        if loss is None:
            loss = extra

    return (loss, grad_x[None], *[d[n] for d in outs for n in WEIGHTS])
```

```python
import functools
import math

import numpy as np
import jax
import jax.numpy as jnp
from jax import lax
from jax.experimental import pallas as pl
from jax.experimental.pallas import tpu as pltpu

F32 = jnp.float32
BF16 = jnp.bfloat16
MESH = pl.DeviceIdType.MESH

D_MODEL = 1024
DEPTH = 4
PLE_DIM = 256
BRANCH = 512
SSM_GROUPS = 32
SSM_GROUP = 16
SSM_STATE = 64
SSM_LANES = SSM_GROUPS * SSM_STATE
SSM_SUB = 4
SUB_IN = BRANCH // SSM_SUB
SUB_ST = SSM_LANES // SSM_SUB
HEAD_DIM = 64
N_Q_HEADS = 8
N_KV_HEADS = 2
GQA_GROUP = 4
KV_WIDTH = 2 * N_KV_HEADS * HEAD_DIM
WINDOW = 128
BLOCK = 128
ATTN_SCALE = 1.0 / math.sqrt(HEAD_DIM)
REL_BUCKETS = 32
REL_MAX_DIST = 128
FFN_HIDDEN = 2816
RMS_EPS = 1e-6
IN_WIDTH = 5888
N_DEV = 8

ADAM_LR = 0.001
ADAM_B1 = 0.9
ADAM_B2 = 0.999
ADAM_EPS = 1e-08
ADAM_WD = 0.01
ADAM_STEP = 10

COL_U = 3072
COL_KV = 5632
NEG = -1e30

SCAN_T = 256
TM = 512
TM_W = 1024
VMEM_LIMIT = 52 * 1024 * 1024


def _cparams(*sem):
    return pltpu.CompilerParams(dimension_semantics=sem, vmem_limit_bytes=VMEM_LIMIT)


def _full(shape):
    n = len(shape)
    return pl.BlockSpec(shape, lambda *_: (0,) * n)


def _pick(n, cands):
    for c in cands:
        if n % c == 0:
            return c
    return n


def _call(body, *, name, steps, in_specs, out_specs, out_shape, scratch, args, side=None):
    in_specs, out_specs, out_shape = list(in_specs), list(out_specs), list(out_shape)
    scratch, args = list(scratch), list(args)
    n_in, n_out, n_scr = len(in_specs), len(out_specs), len(scratch)
    n = 0
    if side is not None:
        kind, blks = side
        out_shape_of, per_block, make_copies = _EXCHANGES[kind]
        n = len(blks)
        inner = body

        def body(*refs):
            ins, sx = refs[:n_in], refs[n_in:n_in + n]
            outs, so = refs[n_in + n:n_in + n + n_out], refs[n_in + n + n_out:n_in + 2 * n + n_out]
            scr = refs[n_in + 2 * n + n_out:n_in + 2 * n + n_out + n_scr]
            send_sems, recv_sems = refs[-2:]

            @pl.when(pl.program_id(0) == 0)
            def _():
                sends, _ = make_copies(sx, so, send_sems, recv_sems)
                for cp in sends:
                    cp.start()

            inner(*ins, *outs, *scr)

            @pl.when(pl.program_id(0) == steps - 1)
            def _():
                sends, recvs = make_copies(sx, so, send_sems, recv_sems)
                for cp in recvs:
                    cp.wait_recv()
                for cp in sends:
                    cp.wait_send()

        in_specs += [_ANY] * n
        args += list(blks)
        out_specs += [_ANY] * n
        out_shape += [jax.ShapeDtypeStruct(out_shape_of(b), b.dtype) for b in blks]
        scratch += [pltpu.SemaphoreType.DMA((per_block * n,)), pltpu.SemaphoreType.DMA((per_block * n,))]
    outs = pl.pallas_call(
        body, name=name, grid=(steps,), in_specs=in_specs, out_specs=out_specs, out_shape=out_shape,
        scratch_shapes=scratch, compiler_params=_cparams("arbitrary"),
    )(*args)
    return outs[:n_out], outs[n_out:]


def _dot(a, b):
    return jnp.dot(a, b, preferred_element_type=F32)


def _dot_tn(a, b):
    return lax.dot_general(a, b, (((0,), (0,)), ((), ())), preferred_element_type=F32)


def _dot_nt(a, b):
    return lax.dot_general(a, b, (((1,), (1,)), ((), ())), preferred_element_type=F32)


def _rms(x, g):
    r = lax.rsqrt(jnp.mean(x * x, axis=-1, keepdims=True) + RMS_EPS)
    return x * r * g


def fused_mm(name, ins, in_specs, prologue, w, *, tn, out_dtype, res=None, extras=(), side=None, nt=False):
    S = ins[0].shape[0]
    N, K = w.shape if nt else w.shape[::-1]
    tn = min(tn, N)
    n_in, n_ex = len(ins), len(extras)

    def body(*refs):
        in_refs = refs[:n_in]
        w_ref = refs[n_in]
        pos = n_in + 1
        res_ref = None
        if res is not None:
            res_ref = refs[pos]
            pos += 1
        o_ref = refs[pos]
        ex_refs = refs[pos + 1:pos + 1 + n_ex]
        a_scr = refs[-1]
        out = prologue(*[r[...] for r in in_refs])
        a_scr[...] = out[0]
        for r, e in zip(ex_refs, out[1:]):
            r[...] = e.astype(r.dtype)
        for j in range(N // tn):
            cs = slice(j * tn, (j + 1) * tn)
            acc = _dot_nt(a_scr[...], w_ref[cs, :]) if nt else _dot(a_scr[...], w_ref[:, cs])
            if res_ref is not None:
                acc = acc + res_ref[:, cs]
            o_ref[:, cs] = acc.astype(o_ref.dtype)

    specs = list(in_specs) + [pl.BlockSpec(w.shape, lambda i: (0, 0), pipeline_mode=pl.Buffered(1))]
    args = list(ins) + [w]
    if res is not None:
        specs.append(pl.BlockSpec((TM, N), lambda i: (i, 0)))
        args.append(res)
    out_shape = [jax.ShapeDtypeStruct((S, N), out_dtype)]
    out_specs = [pl.BlockSpec((TM, N), lambda i: (i, 0))]
    for cols, dt in extras:
        out_shape.append(jax.ShapeDtypeStruct((S, cols), dt))
        out_specs.append(pl.BlockSpec((TM, cols), lambda i: (i, 0)))
    outs, side_outs = _call(body, name=name, steps=S // TM, in_specs=specs, out_specs=out_specs, out_shape=out_shape,
                            scratch=[pltpu.VMEM((TM, K), BF16)], args=args, side=side)
    result = outs if n_ex else outs[0]
    return result if side is None else (result, side_outs)


def _row_spec(cols, blk=0, tm=TM):
    return pl.BlockSpec((tm, cols), lambda i: (i, blk))


def mm_norm_bwd(name, ins, in_specs, w, x, g, dres, *, tm, pre=None, extras=()):
    S = x.shape[0]
    n = len(ins)

    def body(*refs):
        x_ref, g_ref, dres_ref, w_ref, dx_ref, dg_ref = refs[n:n + 6]

        @pl.when(pl.program_id(0) == 0)
        def _():
            dg_ref[...] = jnp.zeros_like(dg_ref)

        tiles = [r[...] for r in refs[:n]]
        if pre is not None:
            tiles, extra_tiles = pre(*tiles)
            for r, e in zip(refs[n + 6:], extra_tiles):
                r[...] = e.astype(r.dtype)
        tiles = [t.astype(BF16) for t in tiles]
        a = tiles[0] if len(tiles) == 1 else jnp.concatenate(tiles, axis=1)
        dh = _dot_nt(a, w_ref[...])
        xv = x_ref[...]
        r = lax.rsqrt(jnp.mean(xv * xv, axis=-1, keepdims=True) + RMS_EPS)
        xhat = xv * r
        dxhat = dh * g_ref[...]
        dx_ref[...] = dres_ref[...] + r * (dxhat - xhat * jnp.mean(dxhat * xhat, axis=-1, keepdims=True))
        dg_ref[...] += jnp.sum((dh * xhat).reshape(tm // 8, 8, D_MODEL), axis=0)

    row = _row_spec(D_MODEL, tm=tm)
    return pl.pallas_call(
        body, name=name, grid=(S // tm,),
        in_specs=list(in_specs)
        + [row, _full((1, D_MODEL)), row, pl.BlockSpec(w.shape, lambda i: (0, 0), pipeline_mode=pl.Buffered(1))],
        out_specs=[row, _full((8, D_MODEL))] + [_row_spec(cols, tm=tm) for cols, _ in extras],
        out_shape=[jax.ShapeDtypeStruct((S, D_MODEL), F32), jax.ShapeDtypeStruct((8, D_MODEL), F32)]
        + [jax.ShapeDtypeStruct((S, cols), dt) for cols, dt in extras],
        compiler_params=_cparams("arbitrary"),
    )(*ins, x, g, dres, w)


def mm_tn_multi(name, a, pieces):
    S, K = a.shape
    n = len(pieces)

    def body(a_ref, *refs):
        @pl.when(pl.program_id(0) == 0)
        def _():
            for o_ref in refs[n:]:
                o_ref[...] = jnp.zeros_like(o_ref)

        at = a_ref[...].astype(BF16)
        for p_ref, o_ref in zip(refs[:n], refs[n:]):
            o_ref[...] += _dot_tn(at, p_ref[...].astype(BF16))

    return pl.pallas_call(
        body, name=name, grid=(S // TM_W,),
        in_specs=[_row_spec(K, tm=TM_W)] + [_row_spec(p.shape[1], tm=TM_W) for p in pieces],
        out_specs=[_full((K, p.shape[1])) for p in pieces],
        out_shape=[jax.ShapeDtypeStruct((K, p.shape[1]), F32) for p in pieces],
        compiler_params=_cparams("arbitrary"),
    )(a, *pieces)


def mm_tn(name, a, b):
    S, K = a.shape
    N = b.shape[1]
    tk = _pick(K, (1024, 1408, 512, 256))
    tn = _pick(N, (1024, 1408, 1536, 512, 256))

    def body(a_ref, b_ref, o_ref):
        @pl.when(pl.program_id(2) == 0)
        def _():
            o_ref[...] = jnp.zeros_like(o_ref)

        o_ref[...] += _dot_tn(a_ref[...].astype(BF16), b_ref[...].astype(BF16))

    return pl.pallas_call(
        body, name=name, grid=(K // tk, N // tn, S // TM_W),
        in_specs=[pl.BlockSpec((TM_W, tk), lambda k, n, s: (s, k)), pl.BlockSpec((TM_W, tn), lambda k, n, s: (s, n))],
        out_specs=pl.BlockSpec((tk, tn), lambda k, n, s: (k, n)),
        out_shape=jax.ShapeDtypeStruct((K, N), F32),
        compiler_params=_cparams("parallel", "parallel", "arbitrary"),
    )(a, b)


def _swiglu_bwd_pre(dact, h1, h2):
    h1 = h1.astype(F32)
    h2 = h2.astype(F32)
    da = dact.astype(F32)
    sg = jax.nn.sigmoid(h1)
    halves = [(da * h2 * sg * (1.0 + h1 * (1.0 - sg))).astype(BF16), (da * h1 * sg).astype(BF16)]
    return halves, halves


def _ple_bwd_pre(dx, a_pre, pp):
    pg = jax.nn.sigmoid(a_pre.astype(F32))
    da = (dx * pp.astype(F32) * pg * (1.0 - pg)).astype(BF16)
    return [da], [da, (dx * pg).astype(BF16)]


def ple_combine(x2, a_pre, pp):
    S = x2.shape[0]

    def body(x_ref, a_ref, p_ref, o_ref):
        o_ref[...] = x_ref[...] + jax.nn.sigmoid(a_ref[...].astype(F32)) * p_ref[...].astype(F32)

    row = pl.BlockSpec((TM, D_MODEL), lambda i: (i, 0))
    return pl.pallas_call(
        body, name="ple_combine", grid=(S // TM,), in_specs=[row, row, row], out_specs=row,
        out_shape=jax.ShapeDtypeStruct((S, D_MODEL), F32), compiler_params=_cparams("parallel"),
    )(x2, a_pre, pp)


def merge_fwd(z, y_ssm, y_conv, y_attn, wb, w_out, x):
    S = z.shape[0]

    def body(g0, g1, g2, y0, y1, y2, w_ref, wo_ref, x_ref, m_ref, o_ref):
        acc = jnp.zeros((TM, D_MODEL), F32)
        for r, (g_ref, y_ref) in enumerate(((g0, y0), (g1, y1), (g2, y2))):
            acc += jax.nn.sigmoid(g_ref[...].astype(F32)) * _dot(y_ref[...], w_ref[r])
        merged = acc.astype(BF16)
        m_ref[...] = merged
        o_ref[...] = x_ref[...] + _dot(merged, wo_ref[...])

    wide = pl.BlockSpec((TM, D_MODEL), lambda i: (i, 0))
    y_spec = pl.BlockSpec((TM, BRANCH), lambda i: (i, 0))
    gates = [pl.BlockSpec((TM, D_MODEL), functools.partial(lambda i, r: (i, r), r=r)) for r in range(3)]
    return pl.pallas_call(
        body, name="merge_fwd", grid=(S // TM,),
        in_specs=gates + [y_spec] * 3
        + [pl.BlockSpec((3, BRANCH, D_MODEL), lambda i: (0, 0, 0), pipeline_mode=pl.Buffered(1)),
           pl.BlockSpec((D_MODEL, D_MODEL), lambda i: (0, 0), pipeline_mode=pl.Buffered(1)), wide],
        out_specs=[wide, wide],
        out_shape=[jax.ShapeDtypeStruct((S, D_MODEL), BF16), jax.ShapeDtypeStruct((S, D_MODEL), F32)],
        compiler_params=_cparams("parallel"),
    )(z, z, z, y_ssm, y_conv, y_attn, wb, w_out, x)


def merge_bwd(dmerged, z, y_ssm, y_conv, y_attn, wb, side=None):
    S = z.shape[0]

    def body(dm_ref, g0, g1, g2, y0, y1, y2, w_ref, dg0, dg1, dg2, db0, db1, db2, dy0, dy1, dy2):
        dm = dm_ref[...].astype(F32)
        rows = ((g0, y0, dg0, db0, dy0), (g1, y1, dg1, db1, dy1), (g2, y2, dg2, db2, dy2))
        for r, (g_ref, y_ref, dg_ref, db_ref, dy_ref) in enumerate(rows):
            sg = jax.nn.sigmoid(g_ref[...].astype(F32))
            b = _dot(y_ref[...], w_ref[r])
            dg_ref[...] = (dm * b * sg * (1.0 - sg)).astype(BF16)
            db = (dm * sg).astype(BF16)
            db_ref[...] = db
            dy_ref[...] = _dot_nt(db, w_ref[r]).astype(BF16)

    wide = pl.BlockSpec((TM, D_MODEL), lambda i: (i, 0))
    y_spec = pl.BlockSpec((TM, BRANCH), lambda i: (i, 0))
    gates = [pl.BlockSpec((TM, D_MODEL), functools.partial(lambda i, r: (i, r), r=r)) for r in range(3)]
    outs, side_outs = _call(
        body, name="merge_bwd", steps=S // TM,
        in_specs=[wide] + gates + [y_spec] * 3
        + [pl.BlockSpec((3, BRANCH, D_MODEL), lambda i: (0, 0, 0), pipeline_mode=pl.Buffered(1))],
        out_specs=[wide] * 6 + [y_spec] * 3,
        out_shape=[jax.ShapeDtypeStruct((S, D_MODEL), BF16)] * 6 + [jax.ShapeDtypeStruct((S, BRANCH), BF16)] * 3,
        scratch=[], args=(dmerged, z, z, z, y_ssm, y_conv, y_attn, wb), side=side)
    return (outs[:3], outs[3:6], outs[6:]), side_outs


def _shift_down(v, halo, k):
    rolled = pltpu.roll(v, k, 0)
    h = pltpu.roll(halo, k, 0)
    row = lax.broadcasted_iota(jnp.int32, v.shape, 0)
    head = jnp.concatenate([h, jnp.zeros((v.shape[0] - 8, v.shape[1]), v.dtype)], axis=0)
    return jnp.where(row < k, head, rolled)


def _shift_up(v, halo, k):
    n = v.shape[0]
    rolled = pltpu.roll(v, n - k, 0)
    h = pltpu.roll(halo, 8 - k, 0)
    row = lax.broadcasted_iota(jnp.int32, v.shape, 0)
    tail = jnp.concatenate([jnp.zeros((n - 8, v.shape[1]), v.dtype), h], axis=0)
    return jnp.where(row >= n - k, tail, rolled)


def _conv_specs():
    rb = TM // 8
    c0 = COL_U // BRANCH

    def cur(k):
        return pl.BlockSpec((TM, BRANCH), lambda i: (i, c0 + k))

    def prev(k):
        return pl.BlockSpec((8, BRANCH), lambda i: (jnp.maximum(i * rb - 1, 0), c0 + k))

    return [cur(1), cur(2), cur(3), prev(2), prev(3)]


def conv_fwd(z, conv_w):
    S = z.shape[0]

    def body(cb_ref, cc_ref, cx_ref, pc_ref, px_ref, w_ref, o_ref):
        first = pl.program_id(0) == 0
        v = cc_ref[...].astype(F32) * cx_ref[...].astype(F32)
        pv = jnp.where(first, 0.0, pc_ref[...].astype(F32) * px_ref[...].astype(F32))
        w = w_ref[...]
        y = w[2:3] * v + w[1:2] * _shift_down(v, pv, 1) + w[0:1] * _shift_down(v, pv, 2)
        o_ref[...] = (cb_ref[...].astype(F32) * y).astype(BF16)

    return pl.pallas_call(
        body, name="conv_fwd", grid=(S // TM,), in_specs=_conv_specs() + [_full((3, BRANCH))],
        out_specs=pl.BlockSpec((TM, BRANCH), lambda i: (i, 0)),
        out_shape=jax.ShapeDtypeStruct((S, BRANCH), BF16), compiler_params=_cparams("parallel"),
    )(z, z, z, z, z, conv_w)


def conv_bwd(dy, z, conv_w):
    S = z.shape[0]
    rb = TM // 8
    nt = S // TM
    c0 = COL_U // BRANCH

    def body(dy_ref, cb_ref, cc_ref, cx_ref, pc_ref, px_ref, ndy_ref, ncb_ref, w_ref, o_ref, dw_ref):
        i = pl.program_id(0)

        @pl.when(i == 0)
        def _():
            dw_ref[...] = jnp.zeros_like(dw_ref)

        cb = cb_ref[...].astype(F32)
        cc = cc_ref[...].astype(F32)
        cx = cx_ref[...].astype(F32)
        dyv = dy_ref[...].astype(F32)
        v = cc * cx
        pv = jnp.where(i == 0, 0.0, pc_ref[...].astype(F32) * px_ref[...].astype(F32))
        v1 = _shift_down(v, pv, 1)
        v2 = _shift_down(v, pv, 2)
        w = w_ref[...]
        conv = w[2:3] * v + w[1:2] * v1 + w[0:1] * v2
        dc = dyv * cb
        ndc = jnp.where(i == nt - 1, 0.0, ndy_ref[...].astype(F32) * ncb_ref[...].astype(F32))
        dv = w[2:3] * dc + w[1:2] * _shift_up(dc, ndc, 1) + w[0:1] * _shift_up(dc, ndc, 2)
        o_ref[:, 0:BRANCH] = (dyv * conv).astype(BF16)
        o_ref[:, BRANCH:2 * BRANCH] = (dv * cx).astype(BF16)
        o_ref[:, 2 * BRANCH:3 * BRANCH] = (dv * cc).astype(BF16)
        for k, vk in enumerate((v2, v1, v)):
            dw_ref[8 * k:8 * k + 8, :] += jnp.sum((dc * vk).reshape(rb, 8, BRANCH), axis=0)

    nxt = jnp.minimum

    return pl.pallas_call(
        body, name="conv_bwd", grid=(nt,),
        in_specs=[pl.BlockSpec((TM, BRANCH), lambda i: (i, 0))] + _conv_specs()
        + [pl.BlockSpec((8, BRANCH), lambda i: (nxt((i + 1) * rb, S // 8 - 1), 0)),
           pl.BlockSpec((8, BRANCH), lambda i: (nxt((i + 1) * rb, S // 8 - 1), c0 + 1)),
           _full((3, BRANCH))],
        out_specs=[pl.BlockSpec((TM, 3 * BRANCH), lambda i: (i, 0)), _full((24, BRANCH))],
        out_shape=[jax.ShapeDtypeStruct((S, 3 * BRANCH), BF16), jax.ShapeDtypeStruct((24, BRANCH), F32)],
        compiler_params=_cparams("arbitrary"),
    )(dy, z, z, z, z, z, dy, z, conv_w)


def _bucket_onehot_t():
    qi = np.arange(BLOCK)[:, None]
    kj = np.arange(2 * BLOCK)[None, :]
    dist = np.clip(qi + BLOCK - kj, 0, REL_MAX_DIST - 1)
    exact = REL_BUCKETS // 2
    df = np.maximum(dist, 1).astype(np.float32)
    large = exact + (np.log(df / np.float32(exact)) / np.float32(math.log(REL_MAX_DIST / exact))
                     * np.float32(REL_BUCKETS - exact)).astype(np.int32)
    large = np.minimum(large, REL_BUCKETS - 1)
    bucket = np.where(dist < exact, dist, large).reshape(-1)
    return (np.arange(REL_BUCKETS)[:, None] == bucket[None, :]).astype(np.float32)


def rel_bias_fwd(rel_bias_t):
    n = BLOCK * 2 * BLOCK

    def body(r_ref, oh_ref, o_ref):
        o_ref[...] = jnp.dot(r_ref[...], oh_ref[...], precision=lax.Precision.HIGHEST, preferred_element_type=F32)

    return pl.pallas_call(
        body, name="rel_bias_fwd", grid=(1,), in_specs=[_full((N_Q_HEADS, REL_BUCKETS)), _full((REL_BUCKETS, n))],
        out_specs=_full((N_Q_HEADS, n)), out_shape=jax.ShapeDtypeStruct((N_Q_HEADS, n), F32),
        compiler_params=_cparams("arbitrary"),
    )(rel_bias_t, jnp.asarray(_bucket_onehot_t()))


def rel_bias_bwd(dbias):
    n_l = dbias.shape[0]
    n = BLOCK * 2 * BLOCK

    def body(d_ref, oh_ref, o_ref):
        tot = d_ref[0]
        for l in range(1, n_l):
            tot = tot + d_ref[l]
        o_ref[...] = lax.dot_general(tot, oh_ref[...], (((1,), (1,)), ((), ())), precision=lax.Precision.HIGHEST,
                                     preferred_element_type=F32)

    return pl.pallas_call(
        body, name="rel_bias_bwd", grid=(1,), in_specs=[_full((n_l, N_Q_HEADS, n)), _full((REL_BUCKETS, n))],
        out_specs=_full((N_Q_HEADS, REL_BUCKETS)), out_shape=jax.ShapeDtypeStruct((N_Q_HEADS, REL_BUCKETS), F32),
        compiler_params=_cparams("arbitrary"),
    )(dbias, jnp.asarray(_bucket_onehot_t()))


def _attn_valid(first):
    qi = lax.broadcasted_iota(jnp.int32, (BLOCK, 2 * BLOCK), 0)
    kj = lax.broadcasted_iota(jnp.int32, (BLOCK, 2 * BLOCK), 1)
    dist = qi + BLOCK - kj
    return (dist >= 0) & (dist < WINDOW) & (jnp.logical_not(first) | (kj >= BLOCK))


def _attn_weights(qh, kcat, bias_h, valid, sink):
    s = _dot_nt(qh, kcat) * ATTN_SCALE + bias_h
    s = jnp.where(valid, s, NEG)
    m = jnp.maximum(jnp.max(s, axis=-1, keepdims=True), sink)
    p = jnp.exp(s - m)
    esink = jnp.exp(sink - m)
    inv = 1.0 / (jnp.sum(p, axis=-1, keepdims=True) + esink)
    return p * inv, esink * inv


def _kv_heads(kvp, kvc, hk):
    ks = slice(hk * HEAD_DIM, (hk + 1) * HEAD_DIM)
    vs = slice(KV_WIDTH // 2 + hk * HEAD_DIM, KV_WIDTH // 2 + (hk + 1) * HEAD_DIM)
    return jnp.concatenate([kvp[:, ks], kvc[:, ks]], axis=0), jnp.concatenate([kvp[:, vs], kvc[:, vs]], axis=0)


def _attn_specs():
    cq = (COL_U + 4 * BRANCH) // BRANCH
    ckv = COL_KV // KV_WIDTH
    return [pl.BlockSpec((BLOCK, BRANCH), lambda n: (n, cq)),
            pl.BlockSpec((BLOCK, KV_WIDTH), lambda n: (n, ckv)),
            pl.BlockSpec((BLOCK, KV_WIDTH), lambda n: (jnp.maximum(n - 1, 0), ckv)),
            _full((N_Q_HEADS, BLOCK, 2 * BLOCK)),
            pl.BlockSpec(memory_space=pltpu.SMEM)]


def attn_fwd(z, bias, sinks, side=None):
    S = z.shape[0]

    def body(q_ref, kvc_ref, kvp_ref, b_ref, sink_ref, o_ref):
        valid = _attn_valid(pl.program_id(0) == 0)
        q = q_ref[...]
        kvc = kvc_ref[...]
        kvp = kvp_ref[...]
        outs = []
        for hk in range(N_KV_HEADS):
            kcat, vcat = _kv_heads(kvp, kvc, hk)
            for g in range(GQA_GROUP):
                h = hk * GQA_GROUP + g
                w, _ = _attn_weights(q[:, h * HEAD_DIM:(h + 1) * HEAD_DIM], kcat, b_ref[h], valid, sink_ref[h])
                outs.append(_dot(w.astype(BF16), vcat))
        o_ref[...] = jnp.concatenate(outs, axis=1).astype(BF16)

    outs, side_outs = _call(
        body, name="attn_fwd", steps=S // BLOCK, in_specs=_attn_specs(),
        out_specs=[pl.BlockSpec((BLOCK, BRANCH), lambda n: (n, 0))],
        out_shape=[jax.ShapeDtypeStruct((S, BRANCH), BF16)], scratch=[], args=(z, z, z, bias, sinks), side=side)
    return outs[0], side_outs


def attn_bwd(do, z, bias, sinks, side=None):
    S = z.shape[0]

    def body(do_ref, q_ref, kvc_ref, kvp_ref, bt_ref, sink_ref, dq_ref, dc_ref, dp_ref, db_ref, ds_ref):
        first = pl.program_id(0) == 0

        @pl.when(first)
        def _():
            db_ref[...] = jnp.zeros_like(db_ref)
            ds_ref[...] = jnp.zeros_like(ds_ref)

        kj = lax.broadcasted_iota(jnp.int32, (2 * BLOCK, BLOCK), 0)
        dist = lax.broadcasted_iota(jnp.int32, (2 * BLOCK, BLOCK), 1) + BLOCK - kj
        valid = (dist >= 0) & (dist < WINDOW) & (jnp.logical_not(first) | (kj >= BLOCK))
        valid4 = jnp.concatenate([valid] * GQA_GROUP, axis=1)
        q = q_ref[...]
        kvc = kvc_ref[...]
        kvp = kvp_ref[...]
        dov = do_ref[...]
        dqs, dks, dvs = [], [], []
        for hk in range(N_KV_HEADS):
            kcat, vcat = _kv_heads(kvp, kvc, hk)
            heads = range(hk * GQA_GROUP, (hk + 1) * GQA_GROUP)
            q4 = jnp.concatenate([q[:, h * HEAD_DIM:(h + 1) * HEAD_DIM] for h in heads], axis=0)
            do4 = jnp.concatenate([dov[:, h * HEAD_DIM:(h + 1) * HEAD_DIM] for h in heads], axis=0)
            bias4 = jnp.concatenate([bt_ref[h] for h in heads], axis=1)
            sink4 = jnp.concatenate([jnp.full((1, BLOCK), sink_ref[h], F32) for h in heads], axis=1)
            s = jnp.where(valid4, _dot_nt(kcat, q4) * ATTN_SCALE + bias4, NEG)
            m = jnp.maximum(jnp.max(s, axis=0, keepdims=True), sink4)
            p = jnp.exp(s - m)
            esink = jnp.exp(sink4 - m)
            inv = 1.0 / (jnp.sum(p, axis=0, keepdims=True) + esink)
            w = p * inv
            dvs.append(_dot(w.astype(BF16), do4))
            dw = _dot_nt(vcat, do4)
            delta = jnp.sum(w * dw, axis=0, keepdims=True)
            ds = w * (dw - delta)
            dsink = -(esink * inv) * delta
            for g, h in enumerate(heads):
                lanes = slice(g * BLOCK, (g + 1) * BLOCK)
                db_ref[h] += ds[:, lanes]
                ds_ref[h:h + 1, :] += jnp.broadcast_to(jnp.sum(dsink[:, lanes], axis=1, keepdims=True), (1, BLOCK))
            dsb = (ds * ATTN_SCALE).astype(BF16)
            dks.append(_dot(dsb, q4))
            dq4 = _dot_tn(dsb, kcat)
            dqs += [dq4[g * BLOCK:(g + 1) * BLOCK] for g in range(GQA_GROUP)]
        dq_ref[...] = jnp.concatenate(dqs, axis=1).astype(BF16)
        both = jnp.concatenate(dks + dvs, axis=1)
        dp_ref[...] = both[:BLOCK]
        dc_ref[...] = both[BLOCK:]

    blk = pl.BlockSpec((BLOCK, BRANCH), lambda n: (n, 0))
    kvb = pl.BlockSpec((BLOCK, KV_WIDTH), lambda n: (n, 0))
    keys_first = (N_Q_HEADS, 2 * BLOCK, BLOCK)
    specs = _attn_specs()
    specs[3] = _full(keys_first)
    (dq, dkv_cur, dkv_prev, dbias_t, dsinks), side_outs = _call(
        body, name="attn_bwd", steps=S // BLOCK, in_specs=[blk] + specs,
        out_specs=[blk, kvb, kvb, _full(keys_first), _full((N_Q_HEADS, BLOCK))],
        out_shape=[jax.ShapeDtypeStruct((S, BRANCH), BF16), jax.ShapeDtypeStruct((S, KV_WIDTH), F32),
                   jax.ShapeDtypeStruct((S, KV_WIDTH), F32), jax.ShapeDtypeStruct(keys_first, F32),
                   jax.ShapeDtypeStruct((N_Q_HEADS, BLOCK), F32)],
        scratch=[], args=(do, z, z, z, jnp.swapaxes(bias, 1, 2), sinks), side=side)
    return (dq, dkv_cur, dkv_prev, jnp.swapaxes(dbias_t, 1, 2), dsinks), side_outs


def kv_shift_add(dcur, dprev):
    S = dcur.shape[0]
    nt = S // TM
    per_tile = TM // BLOCK

    def body(c_ref, p_ref, n_ref, o_ref):
        nxt = jnp.where(pl.program_id(0) == nt - 1, 0.0, n_ref[...])
        o_ref[...] = (c_ref[...] + jnp.concatenate([p_ref[BLOCK:, :], nxt], axis=0)).astype(BF16)

    tile = pl.BlockSpec((TM, KV_WIDTH), lambda i: (i, 0))
    return pl.pallas_call(
        body, name="kv_shift_add", grid=(nt,),
        in_specs=[tile, tile,
                  pl.BlockSpec((BLOCK, KV_WIDTH), lambda i: (jnp.minimum((i + 1) * per_tile, S // BLOCK - 1), 0))],
        out_specs=tile, out_shape=jax.ShapeDtypeStruct((S, KV_WIDTH), BF16), compiler_params=_cparams("parallel"),
    )(dcur, dprev, dprev)


def _ssm_disc(lam_re, lam_im, log_dt, bt_re, bt_im):
    dt = jnp.exp(log_dt)
    mag = jnp.exp(lam_re * dt)
    ang = lam_im * dt
    a_re = mag * jnp.cos(ang)
    a_im = mag * jnp.sin(ang)
    den = lam_re * lam_re + lam_im * lam_im
    nr = a_re - 1.0
    coef_re = (nr * lam_re + a_im * lam_im) / den
    coef_im = (a_im * lam_re - nr * lam_im) / den
    bb_re = coef_re[:, None, :] * bt_re - coef_im[:, None, :] * bt_im
    bb_im = coef_re[:, None, :] * bt_im + coef_im[:, None, :] * bt_re
    return a_re, a_im, bb_re, bb_im


_GN = (SSM_GROUPS, SSM_STATE)
_GPN = (SSM_GROUPS, SSM_GROUP, SSM_STATE)


def ssm_disc_fwd(lam_re, lam_im, log_dt, bt_re, bt_im):
    def body(lr_ref, li_ref, dt_ref, br_ref, bi_ref, ar_ref, ai_ref, bbr_ref, bbi_ref):
        a_re, a_im, bb_re, bb_im = _ssm_disc(lr_ref[...], li_ref[...], dt_ref[...], br_ref[...], bi_ref[...])
        ar_ref[...] = a_re
        ai_ref[...] = a_im
        bbr_ref[...] = bb_re
        bbi_ref[...] = bb_im

    return pl.pallas_call(
        body, name="ssm_disc_fwd", grid=(1,),
        in_specs=[_full(_GN), _full(_GN), _full((SSM_GROUPS, 1)), _full(_GPN), _full(_GPN)],
        out_specs=[_full(_GN), _full(_GN), _full(_GPN), _full(_GPN)],
        out_shape=[jax.ShapeDtypeStruct(s, F32) for s in (_GN, _GN, _GPN, _GPN)],
        compiler_params=_cparams("arbitrary"),
    )(lam_re, lam_im, log_dt, bt_re, bt_im)


def ssm_disc_bwd(lam_re, lam_im, log_dt, bt_re, bt_im, da_re, da_im, dbb_re, dbb_im):
    def body(lr_ref, li_ref, dt_ref, br_ref, bi_ref, dar_ref, dai_ref, dbr_ref, dbi_ref, o_lr, o_li, o_dt, o_br, o_bi):
        prim = (lr_ref[...], li_ref[...], dt_ref[...], br_ref[...], bi_ref[...])
        _, vjp = jax.vjp(_ssm_disc, *prim)
        grads = vjp((dar_ref[...], dai_ref[...], dbr_ref[...], dbi_ref[...]))
        for r, v in zip((o_lr, o_li, o_dt, o_br, o_bi), grads):
            r[...] = v

    shapes = (_GN, _GN, (SSM_GROUPS, 1), _GPN, _GPN)
    return pl.pallas_call(
        body, name="ssm_disc_bwd", grid=(1,),
        in_specs=[_full(s) for s in shapes + (_GN, _GN, _GPN, _GPN)],
        out_specs=[_full(s) for s in shapes], out_shape=[jax.ShapeDtypeStruct(s, F32) for s in shapes],
        compiler_params=_cparams("arbitrary"),
    )(lam_re, lam_im, log_dt, bt_re, bt_im, da_re, da_im, dbb_re, dbb_im)


LANE_GROUPS = SSM_LANES // 128
SUB_GROUPS = SUB_ST // 128
_TM_SHAPE = (LANE_GROUPS, 128)


def _step_rows(t):
    return pl.ds(pl.multiple_of(t * LANE_GROUPS, LANE_GROUPS), LANE_GROUPS)


def _group_rows(j):
    return pl.ds(j, SCAN_T, stride=LANE_GROUPS)


def _store_sub(ref, j, val):
    for k in range(SUB_GROUPS):
        ref[_group_rows(j * SUB_GROUPS + k), :] = val[:, k * 128:(k + 1) * 128]


def _load_sub(ref, j):
    return jnp.concatenate([ref[_group_rows(j * SUB_GROUPS + k), :] for k in range(SUB_GROUPS)], axis=1)


_SUB_SHAPE_IN = (SSM_SUB, SUB_IN, SUB_ST)
_SUB_SHAPE_OUT = (SSM_SUB, SUB_ST, SUB_IN)


def ssm_fwd(z, bb_re, bb_im, ct_re, ct_im, a_re, a_im, d_skip, wglu, side=None):
    S = z.shape[0]
    cu = COL_U // BRANCH

    def body(u_ref, bbr_ref, bbi_ref, ctr_ref, cti_ref, ar_ref, ai_ref, d_ref, wg_ref,
             y_ref, ypre_ref, hr_ref, hi_ref, hrow_r, hrow_i, bur, bui, car_r, car_i):
        @pl.when(pl.program_id(0) == 0)
        def _():
            car_r[...] = jnp.zeros_like(car_r)
            car_i[...] = jnp.zeros_like(car_i)

        u = u_ref[...]
        for j in range(SSM_SUB):
            uj = u[:, j * SUB_IN:(j + 1) * SUB_IN]
            _store_sub(bur, j, _dot(uj, bbr_ref[j]))
            _store_sub(bui, j, _dot(uj, bbi_ref[j]))
        ar = ar_ref[...]
        ai = ai_ref[...]

        def step(t, carry):
            hr, hi = carry
            rows = _step_rows(t)
            nhr = ar * hr - ai * hi + bur[rows, :]
            nhi = ar * hi + ai * hr + bui[rows, :]
            hr_ref[rows, :] = nhr
            hi_ref[rows, :] = nhi
            return nhr, nhi

        hr, hi = lax.fori_loop(0, SCAN_T, step, (car_r[...], car_i[...]), unroll=8)
        car_r[...] = hr
        car_i[...] = hi
        ys = []
        for j in range(SSM_SUB):
            cs = slice(j * SUB_ST, (j + 1) * SUB_ST)
            hrow_r[:, cs] = _load_sub(hr_ref, j).astype(BF16)
            hrow_i[:, cs] = _load_sub(hi_ref, j).astype(BF16)
            ys.append(_dot(hrow_r[:, cs], ctr_ref[j]) - _dot(hrow_i[:, cs], cti_ref[j]))
        ypre = jnp.concatenate(ys, axis=1) + d_ref[...] * u.astype(F32)
        ypre_ref[...] = ypre
        g = jax.nn.gelu(ypre)
        y_ref[...] = (g * jax.nn.sigmoid(_dot(g.astype(BF16), wg_ref[...]))).astype(BF16)

    row = pl.BlockSpec((SCAN_T, BRANCH), lambda i: (i, 0))
    st = pl.BlockSpec((SCAN_T * LANE_GROUPS, 128), lambda i: (i, 0))
    wide = pl.BlockSpec((SCAN_T, SSM_LANES), lambda i: (i, 0))
    return _call(
        body, name="ssm_fwd", steps=S // SCAN_T,
        in_specs=[pl.BlockSpec((SCAN_T, BRANCH), lambda i: (i, cu)), _full(_SUB_SHAPE_IN), _full(_SUB_SHAPE_IN),
                  _full(_SUB_SHAPE_OUT), _full(_SUB_SHAPE_OUT), _full(_TM_SHAPE), _full(_TM_SHAPE), _full((1, BRANCH)),
                  _full((BRANCH, BRANCH))],
        out_specs=[row, row, st, st, wide, wide],
        out_shape=[jax.ShapeDtypeStruct((S, BRANCH), BF16), jax.ShapeDtypeStruct((S, BRANCH), F32),
                   jax.ShapeDtypeStruct((S * LANE_GROUPS, 128), F32), jax.ShapeDtypeStruct((S * LANE_GROUPS, 128), F32),
                   jax.ShapeDtypeStruct((S, SSM_LANES), BF16), jax.ShapeDtypeStruct((S, SSM_LANES), BF16)],
        scratch=[pltpu.VMEM((SCAN_T * LANE_GROUPS, 128), F32), pltpu.VMEM((SCAN_T * LANE_GROUPS, 128), F32),
                 pltpu.VMEM(_TM_SHAPE, F32), pltpu.VMEM(_TM_SHAPE, F32)],
        args=(z, bb_re, bb_im, ct_re, ct_im, a_re, a_im, d_skip, wglu), side=side)


def ssm_bwd(dy, z, ypre, h_re, h_im, hrow_re, hrow_im, bbt_re, bbt_im, c_re, c_im, a_re, a_im, d_skip, wglu, wglu_t,
            side=None):
    S = z.shape[0]
    nt = S // SCAN_T
    cu = COL_U // BRANCH

    def body(dy_ref, u_ref, ypre_ref, hr_ref, hi_ref, hpr_ref, hpi_ref, hrow_r, hrow_i, bbr_ref, bbi_ref, cr_ref, ci_ref,
             ar_ref, ai_ref, d_ref, wg_ref, wgt_ref,
             du_ref, dbbr_ref, dbbi_ref, dctr_ref, dcti_ref, dar_ref, dai_ref, dd_ref, dwg_ref,
             lr_scr, li_scr, car_r, car_i):
        step = pl.program_id(0)

        @pl.when(step == 0)
        def _():
            for r in (dbbr_ref, dbbi_ref, dctr_ref, dcti_ref, dar_ref, dai_ref, dd_ref, dwg_ref, car_r, car_i):
                r[...] = jnp.zeros_like(r)

        u = u_ref[...]
        uf = u.astype(F32)
        dyv = dy_ref[...].astype(F32)
        g, gelu_vjp = jax.vjp(jax.nn.gelu, ypre_ref[...])
        gb = g.astype(BF16)
        sg = jax.nn.sigmoid(_dot(gb, wg_ref[...]))
        dgl = (dyv * g * sg * (1.0 - sg)).astype(BF16)
        dwg_ref[...] += _dot_tn(gb, dgl)
        dg = dyv * sg + _dot(dgl, wgt_ref[...])
        dypre = gelu_vjp(dg)[0]
        dd_ref[...] += jnp.sum(dypre * uf, axis=0, keepdims=True)
        dyb = dypre.astype(BF16)
        for j in range(SSM_SUB):
            dyj = dyb[:, j * SUB_IN:(j + 1) * SUB_IN]
            _store_sub(lr_scr, j, _dot(dyj, cr_ref[j]))
            _store_sub(li_scr, j, -_dot(dyj, ci_ref[j]))
            cs = slice(j * SUB_ST, (j + 1) * SUB_ST)
            dctr_ref[j] += _dot_tn(hrow_r[:, cs], dyj)
            dcti_ref[j] -= _dot_tn(hrow_i[:, cs], dyj)

        ar = ar_ref[...]
        ai = ai_ref[...]

        def adjoint(lr, li, rows):
            nlr = ar * lr + ai * li + lr_scr[rows, :]
            nli = ar * li - ai * lr + li_scr[rows, :]
            lr_scr[rows, :] = nlr
            li_scr[rows, :] = nli
            return nlr, nli

        def back(k, carry):
            lr, li, acc_r, acc_i = carry
            t = SCAN_T - 1 - k
            lr, li = adjoint(lr, li, _step_rows(t))
            hpr = hr_ref[_step_rows(t - 1), :]
            hpi = hi_ref[_step_rows(t - 1), :]
            return lr, li, acc_r + lr * hpr + li * hpi, acc_i + li * hpr - lr * hpi

        zero = jnp.zeros(_TM_SHAPE, F32)
        lr, li, acc_r, acc_i = lax.fori_loop(0, SCAN_T - 1, back, (car_r[...], car_i[...], zero, zero), unroll=8)
        lr, li = adjoint(lr, li, pl.ds(0, LANE_GROUPS))
        car_r[...] = lr
        car_i[...] = li
        first_tile = step == nt - 1
        hpr = jnp.where(first_tile, 0.0, hpr_ref[...])
        hpi = jnp.where(first_tile, 0.0, hpi_ref[...])
        dar_ref[...] += acc_r + lr * hpr + li * hpi
        dai_ref[...] += acc_i + li * hpr - lr * hpi

        dus = []
        for j in range(SSM_SUB):
            lrb = _load_sub(lr_scr, j).astype(BF16)
            lib = _load_sub(li_scr, j).astype(BF16)
            uj = u[:, j * SUB_IN:(j + 1) * SUB_IN]
            dus.append(_dot(lrb, bbr_ref[j]) + _dot(lib, bbi_ref[j]))
            dbbr_ref[j] += _dot_tn(uj, lrb)
            dbbi_ref[j] += _dot_tn(uj, lib)
        du_ref[...] = (jnp.concatenate(dus, axis=1) + dypre * d_ref[...]).astype(BF16)

    def rev(i):
        return nt - 1 - i

    row = pl.BlockSpec((SCAN_T, BRANCH), lambda i: (rev(i), 0))
    st = pl.BlockSpec((SCAN_T * LANE_GROUPS, 128), lambda i: (rev(i), 0))
    before = pl.BlockSpec(_TM_SHAPE, lambda i: (jnp.maximum(rev(i) * SCAN_T - 1, 0), 0))
    wide = pl.BlockSpec((SCAN_T, SSM_LANES), lambda i: (rev(i), 0))
    tm = _full(_TM_SHAPE)
    return _call(
        body, name="ssm_bwd", steps=nt,
        in_specs=[row, pl.BlockSpec((SCAN_T, BRANCH), lambda i: (rev(i), cu)), row, st, st, before, before, wide, wide,
                  _full(_SUB_SHAPE_OUT), _full(_SUB_SHAPE_OUT), _full(_SUB_SHAPE_IN), _full(_SUB_SHAPE_IN),
                  tm, tm, _full((1, BRANCH)), _full((BRANCH, BRANCH)), _full((BRANCH, BRANCH))],
        out_specs=[row, _full(_SUB_SHAPE_IN), _full(_SUB_SHAPE_IN), _full(_SUB_SHAPE_OUT), _full(_SUB_SHAPE_OUT),
                   tm, tm, _full((1, BRANCH)), _full((BRANCH, BRANCH))],
        out_shape=[jax.ShapeDtypeStruct((S, BRANCH), BF16), jax.ShapeDtypeStruct(_SUB_SHAPE_IN, F32),
                   jax.ShapeDtypeStruct(_SUB_SHAPE_IN, F32), jax.ShapeDtypeStruct(_SUB_SHAPE_OUT, F32),
                   jax.ShapeDtypeStruct(_SUB_SHAPE_OUT, F32), jax.ShapeDtypeStruct(_TM_SHAPE, F32),
                   jax.ShapeDtypeStruct(_TM_SHAPE, F32), jax.ShapeDtypeStruct((1, BRANCH), F32),
                   jax.ShapeDtypeStruct((BRANCH, BRANCH), F32)],
        scratch=[pltpu.VMEM((SCAN_T * LANE_GROUPS, 128), F32), pltpu.VMEM((SCAN_T * LANE_GROUPS, 128), F32),
                 pltpu.VMEM(_TM_SHAPE, F32), pltpu.VMEM(_TM_SHAPE, F32)],
        args=(dy, z, ypre, h_re, h_im, h_re, h_im, hrow_re, hrow_im, bbt_re, bbt_im, c_re, c_im, a_re, a_im, d_skip, wglu,
              wglu_t),
        side=side)


def _blockdiag(x):
    gs = SSM_GROUPS // SSM_SUB
    x = x.reshape(SSM_SUB, gs, SSM_GROUP, SSM_STATE)
    eye = jnp.eye(gs, dtype=x.dtype)
    return (x[:, :, :, None, :] * eye[None, :, None, :, None]).reshape(SSM_SUB, SUB_IN, SUB_ST)


def _blockdiag_extract(x):
    gs = SSM_GROUPS // SSM_SUB
    x = x.reshape(SSM_SUB, gs, SSM_GROUP, gs, SSM_STATE)
    eye = jnp.eye(gs, dtype=x.dtype)
    return jnp.sum(x * eye[None, :, None, :, None], axis=3).reshape(SSM_GROUPS, SSM_GROUP, SSM_STATE)


def loss_head(x, g, target):
    S = x.shape[0]

    def body(x_ref, g_ref, t_ref, dx_ref, loss_ref, dg_ref):
        @pl.when(pl.program_id(0) == 0)
        def _():
            loss_ref[...] = jnp.zeros_like(loss_ref)
            dg_ref[...] = jnp.zeros_like(dg_ref)

        xv = x_ref[...]
        gv = g_ref[...]
        r = lax.rsqrt(jnp.mean(xv * xv, axis=-1, keepdims=True) + RMS_EPS)
        xhat = xv * r
        err = xhat * gv - t_ref[...]
        loss_ref[...] += jnp.sum((err * err).reshape(TM // 8, 8, D_MODEL), axis=0) * (0.5 / D_MODEL)
        dy = err * (1.0 / D_MODEL)
        dxhat = dy * gv
        dx_ref[...] = r * (dxhat - xhat * jnp.mean(dxhat * xhat, axis=-1, keepdims=True))
        dg_ref[...] += jnp.sum((dy * xhat).reshape(TM // 8, 8, D_MODEL), axis=0)

    row = pl.BlockSpec((TM, D_MODEL), lambda i: (i, 0))
    acc = _full((8, D_MODEL))
    return pl.pallas_call(
        body, name="loss_head", grid=(S // TM,), in_specs=[row, _full((1, D_MODEL)), row],
        out_specs=[row, acc, acc],
        out_shape=[jax.ShapeDtypeStruct((S, D_MODEL), F32), jax.ShapeDtypeStruct((8, D_MODEL), F32),
                   jax.ShapeDtypeStruct((8, D_MODEL), F32)],
        compiler_params=_cparams("arbitrary"),
    )(x, g, target)


def _x_spec():
    return pl.BlockSpec((TM, D_MODEL), lambda i: (i, 0))


def _g_spec():
    return pl.BlockSpec((1, D_MODEL), lambda i: (0, 0))


def _norm_prologue(x, g):
    h = _rms(x, g).astype(BF16)
    return h, h


def _cast_prologue(x):
    return (x.astype(BF16),)


def _swiglu_prologue(h1, h2):
    a = h1.astype(F32)
    act = (a * jax.nn.sigmoid(a) * h2.astype(F32)).astype(BF16)
    return act, act


def _ssm_consts(lw):
    a_re, a_im, bbt_re, bbt_im = ssm_disc_fwd(
        lw["ssm_lambda_re"], lw["ssm_lambda_im"], lw["ssm_log_dt"].reshape(SSM_GROUPS, 1), lw["bt_re"], lw["bt_im"])
    bb_re = _blockdiag(bbt_re).astype(BF16)
    bb_im = _blockdiag(bbt_im).astype(BF16)
    c_re = _blockdiag(lw["ssm_c_re"]).astype(BF16)
    c_im = _blockdiag(lw["ssm_c_im"]).astype(BF16)
    return dict(
        a_re=a_re.reshape(_TM_SHAPE), a_im=a_im.reshape(_TM_SHAPE),
        bb_re=bb_re, bb_im=bb_im, bbt_re=jnp.swapaxes(bb_re, 1, 2), bbt_im=jnp.swapaxes(bb_im, 1, 2),
        c_re=c_re, c_im=c_im, ct_re=jnp.swapaxes(c_re, 1, 2), ct_im=jnp.swapaxes(c_im, 1, 2))


def layer_fwd(x, lw, bias, next_shards=None, place=None):
    sides = (None, None) if next_shards is None else (("gather_chips", next_shards[:GATHER_SPLIT]),
                                                      ("gather_chips", next_shards[GATHER_SPLIT:]))
    out = fused_mm("in_proj", [x, lw["norm_mix"]], [_x_spec(), _g_spec()], _norm_prologue, lw["w_in"], tn=2944,
                   out_dtype=BF16, extras=((D_MODEL, BF16),), side=sides[0])
    (z, h), g4a = (out, []) if next_shards is None else out
    sc = _ssm_consts(lw)
    (y_ssm, ypre, h_re, h_im, hrow_re, hrow_im), g4b = ssm_fwd(
        z, sc["bb_re"], sc["bb_im"], sc["ct_re"], sc["ct_im"], sc["a_re"], sc["a_im"], lw["ssm_d"], lw["ssm_w_glu"],
        side=sides[1])
    y_conv = conv_fwd(z, lw["conv_w"])
    if next_shards is not None:
        g4 = [_put_slot(g, b, place[0]) for g, b in zip(list(g4a) + list(g4b), next_shards)]
    y_attn, g8 = attn_fwd(z, bias, lw["attn_sinks"], side=None if next_shards is None else ("gather_cores", g4))
    next_gathered = None if next_shards is None else [_put_slot(g, b, place[1]) for g, b in zip(g8, g4)]
    merged, x1 = merge_fwd(z, y_ssm, y_conv, y_attn, lw["w_branch"], lw["w_out"], x)
    hf, hn1 = fused_mm("ffn_in", [x1, lw["norm_ffn"]], [_x_spec(), _g_spec()], _norm_prologue, lw["w_ffn_in"], tn=2816,
                       out_dtype=BF16, extras=((D_MODEL, BF16),))
    x2, act = fused_mm("ffn_out", [hf, hf], [_row_spec(FFN_HIDDEN, 0), _row_spec(FFN_HIDDEN, 1)], _swiglu_prologue,
                       lw["w_ffn_out"], tn=1024, out_dtype=F32, res=x1, extras=((FFN_HIDDEN, BF16),))
    a_pre, hn2 = fused_mm("ple_gate", [x2, lw["norm_ple"]], [_x_spec(), _g_spec()], _norm_prologue, lw["w_ple_gate"],
                          tn=1024, out_dtype=BF16, extras=((D_MODEL, BF16),))
    pp = fused_mm("ple_proj", [lw["p"]], [_row_spec(PLE_DIM)], _cast_prologue, lw["w_ple_proj"], tn=1024, out_dtype=BF16)
    x3 = ple_combine(x2, a_pre, pp)
    res = dict(x=x, z=z, h=h, y_ssm=y_ssm, ypre=ypre, h_re=h_re, h_im=h_im, hrow_re=hrow_re, hrow_im=hrow_im, y_conv=y_conv, y_attn=y_attn, merged=merged,
               x1=x1, hf=hf, hn1=hn1, act=act, x2=x2, a_pre=a_pre, hn2=hn2, pp=pp)
    return x3, res, next_gathered


def _pair_sums(split, from_sibling, names):
    return [pair_sum("pair_sum_" + n, a.reshape(2, -1, a.shape[-1]), b.reshape(-1, b.shape[-1])).reshape(b.shape)
            for n, a, b in zip(names, split, from_sibling)]


def layer_bwd(dx3, lw, res, bias, pending=None, scatter_own=False):
    g = {}
    wide = ((D_MODEL, BF16), (D_MODEL, BF16))
    dx2, g["norm_ple"], da, dpp = mm_norm_bwd(
        "d_ple_gate", [dx3, res["a_pre"], res["pp"]], [_row_spec(D_MODEL)] * 3, lw["w_ple_gate"], res["x2"],
        lw["norm_ple"], dx3, tm=TM, pre=_ple_bwd_pre, extras=wide)
    g["w_ple_proj"] = mm_tn("d_w_ple_proj", lw["p"], dpp)
    g["w_ple_gate"] = mm_tn("d_w_ple_gate", res["hn2"], da)
    sums = None
    if pending is None:
        dact = fused_mm("d_ffn_out", [dx2], [_x_spec()], _cast_prologue, lw["w_ffn_out"], tn=1408, out_dtype=BF16,
                        nt=True)
    else:
        dact, from_sibling = fused_mm("d_ffn_out", [dx2], [_x_spec()], _cast_prologue, lw["w_ffn_out"], tn=1408,
                                      out_dtype=BF16, side=("scatter_cores", pending), nt=True)
        sums = _pair_sums(pending, from_sibling, SHARDED_NAMES)
    g["w_ffn_out"] = mm_tn("d_w_ffn_out", res["act"], dx2)
    half = TM // 2
    dx1, g["norm_ffn"], dh1, dh2 = mm_norm_bwd(
        "d_ffn_in", [dact, res["hf"], res["hf"]],
        [_row_spec(FFN_HIDDEN, 0, half), _row_spec(FFN_HIDDEN, 0, half), _row_spec(FFN_HIDDEN, 1, half)],
        lw["w_ffn_in"], res["x1"], lw["norm_ffn"], dx2, tm=half, pre=_swiglu_bwd_pre,
        extras=((FFN_HIDDEN, BF16), (FFN_HIDDEN, BF16)))
    g["w_ffn_in"] = jnp.concatenate([mm_tn("d_w_ffn_in_a", res["hn1"], dh1), mm_tn("d_w_ffn_in_b", res["hn1"], dh2)],
                                    axis=1)
    dmerged = fused_mm("d_out_proj", [dx1], [_x_spec()], _cast_prologue, lw["w_out"], tn=1024, out_dtype=BF16, nt=True)
    g["w_out"] = mm_tn("d_w_out", res["merged"], dx1)
    z = res["z"]
    ys = (res["y_ssm"], res["y_conv"], res["y_attn"])
    own_sums = None
    if scatter_own:
        late = SHARDED_NAMES[GATHER_SPLIT:]
        own = [_shard_split(g[n], n).astype(BF16) for n in late]
        (dgates, dbs, dys), from_sibling = merge_bwd(dmerged, z, *ys, lw["w_branch"], side=("scatter_cores", own))
        own_sums = _pair_sums(own, from_sibling, late)
    else:
        (dgates, dbs, dys), _ = merge_bwd(dmerged, z, *ys, lw["w_branch"])
    g["w_branch"] = jnp.stack([mm_tn(f"d_w_branch_{r}", ys[r], dbs[r]) for r in range(3)])
    sc = _ssm_consts(lw)
    (du, dbb_re, dbb_im, dct_re, dct_im, da_re, da_im, g["ssm_d"], g["ssm_w_glu"]), received = ssm_bwd(
        dys[0], z, res["ypre"], res["h_re"], res["h_im"], res["hrow_re"], res["hrow_im"], sc["bbt_re"], sc["bbt_im"], sc["c_re"], sc["c_im"],
        sc["a_re"], sc["a_im"], lw["ssm_d"], lw["ssm_w_glu"], lw["ssm_w_glu_t"],
        side=None if pending is None else ("scatter_chips", sums))
    g["ssm_c_re"] = _blockdiag_extract(jnp.swapaxes(dct_re, 1, 2))
    g["ssm_c_im"] = _blockdiag_extract(jnp.swapaxes(dct_im, 1, 2))
    (g["ssm_lambda_re"], g["ssm_lambda_im"], dlog_dt, g["bt_re"], g["bt_im"]) = ssm_disc_bwd(
        lw["ssm_lambda_re"], lw["ssm_lambda_im"], lw["ssm_log_dt"].reshape(SSM_GROUPS, 1), lw["bt_re"], lw["bt_im"],
        da_re.reshape(_GN), da_im.reshape(_GN),
        _blockdiag_extract(dbb_re), _blockdiag_extract(dbb_im))
    g["ssm_log_dt"] = dlog_dt.reshape(SSM_GROUPS)
    dconv, g["conv_w"] = conv_bwd(dys[1], z, lw["conv_w"])
    (dq, dkv_cur, dkv_prev, g["dbias"], g["attn_sinks"]), own_received = attn_bwd(
        dys[2], z, bias, lw["attn_sinks"], side=None if own_sums is None else ("scatter_chips", own_sums))
    dkv = kv_shift_add(dkv_cur, dkv_prev)
    pieces = [dgates[0], dgates[1], dgates[2], du, dconv, dq, dkv]
    g["w_in"] = jnp.concatenate(mm_tn_multi("d_w_in_gates", res["h"], pieces[:3])
                                + mm_tn_multi("d_w_in_branches", res["h"], pieces[3:]), axis=1)
    dx0, g["norm_mix"] = mm_norm_bwd("d_in_proj", pieces, [_row_spec(pc.shape[1], tm=half) for pc in pieces], lw["w_in"],
                                     res["x"], lw["norm_mix"], dx1, tm=half)
    return dx0, g, (sums, received), (own_sums, own_received)


def adamw(name, parts, w, m, v):
    n, R, C = parts.shape
    tr = _pick(R, (512, 256, 128, 64, 32, 16, 8))

    def body(p_ref, w_ref, m_ref, v_ref, g_ref, d_ref, nm_ref, nv_ref):
        gsum = p_ref[0].astype(F32)
        for k in range(1, n):
            gsum = gsum + p_ref[k].astype(F32)
        mn = ADAM_B1 * m_ref[...] + (1.0 - ADAM_B1) * gsum
        vn = ADAM_B2 * v_ref[...] + (1.0 - ADAM_B2) * jnp.square(gsum)
        m_hat = mn / (1.0 - ADAM_B1 ** ADAM_STEP)
        v_hat = vn / (1.0 - ADAM_B2 ** ADAM_STEP)
        g_ref[...] = gsum
        d_ref[...] = -ADAM_LR * (m_hat / (jnp.sqrt(v_hat) + ADAM_EPS) + ADAM_WD * w_ref[...])
        nm_ref[...] = mn
        nv_ref[...] = vn

    blk = pl.BlockSpec((tr, C), lambda i: (i, 0))
    return pl.pallas_call(
        body, name=name, grid=(R // tr,), in_specs=[pl.BlockSpec((n, tr, C), lambda i: (0, i, 0)), blk, blk, blk],
        out_specs=[blk] * 4, out_shape=[jax.ShapeDtypeStruct((R, C), F32)] * 4, compiler_params=_cparams("parallel"),
    )(parts, w, m, v)


_ANY = pl.BlockSpec(memory_space=pl.ANY)


def _coords():
    return lax.axis_index("x"), lax.axis_index("y"), lax.axis_index("c")


def _chip_peers(x, y):
    return [(1 - x, y), (x, 1 - y), (1 - x, 1 - y)]


def _gather_chips_copies(x_refs, out_refs, send_sems, recv_sems):
    x, y, c = _coords()
    me = 2 * x + y
    peers = _chip_peers(x, y)

    def copy(i, k, slot):
        return pltpu.make_async_remote_copy(
            src_ref=x_refs[i], dst_ref=out_refs[i].at[slot], send_sem=send_sems.at[3 * i + k],
            recv_sem=recv_sems.at[3 * i + k], device_id=(*peers[k], c), device_id_type=MESH)

    n = len(x_refs)
    sends = [copy(i, k, me) for i in range(n) for k in range(3)]
    recvs = [copy(i, k, 2 * px + py) for i in range(n) for k, (px, py) in enumerate(peers)]
    return sends, recvs


def _gather_cores_copies(x_refs, out_refs, send_sems, recv_sems):
    x, y, c = _coords()

    def copy(i, slot):
        return pltpu.make_async_remote_copy(
            src_ref=x_refs[i], dst_ref=out_refs[i].at[slot], send_sem=send_sems.at[i], recv_sem=recv_sems.at[i],
            device_id=(x, y, 1 - c), device_id_type=MESH)

    n = len(x_refs)
    return [copy(i, c) for i in range(n)], [copy(i, 1 - c) for i in range(n)]


def _scatter_cores_copies(x_refs, out_refs, send_sems, recv_sems):
    x, y, c = _coords()
    copies = [pltpu.make_async_remote_copy(
        src_ref=x_refs[i].at[1 - c], dst_ref=out_refs[i], send_sem=send_sems.at[i], recv_sem=recv_sems.at[i],
        device_id=(x, y, 1 - c), device_id_type=MESH) for i in range(len(x_refs))]
    return copies, copies


def _scatter_chips_copies(x_refs, out_refs, send_sems, recv_sems):
    x, y, c = _coords()
    me = 2 * x + y
    peers = _chip_peers(x, y)

    def copy(i, k, src_slot, dst_slot):
        return pltpu.make_async_remote_copy(
            src_ref=x_refs[i].at[src_slot], dst_ref=out_refs[i].at[dst_slot], send_sem=send_sems.at[3 * i + k],
            recv_sem=recv_sems.at[3 * i + k], device_id=(*peers[k], c), device_id_type=MESH)

    n = len(x_refs)
    sends = [copy(i, k, 2 * px + py, me) for i in range(n) for k, (px, py) in enumerate(peers)]
    recvs = [copy(i, k, me, 2 * px + py) for i in range(n) for k, (px, py) in enumerate(peers)]
    return sends, recvs


_EXCHANGES = {
    "gather_chips": (lambda b: (4,) + b.shape, 3, _gather_chips_copies),
    "gather_cores": (lambda b: (2,) + b.shape, 1, _gather_cores_copies),
    "scatter_cores": (lambda b: b.shape[1:], 1, _scatter_cores_copies),
    "scatter_chips": (lambda b: b.shape, 3, _scatter_chips_copies),
}


def exchange(name, kind, blks):
    out_shape_of, per_block, make_copies = _EXCHANGES[kind]
    n = len(blks)

    def body(*refs):
        sends, recvs = make_copies(refs[:n], refs[n:2 * n], refs[2 * n], refs[2 * n + 1])
        for cp in sends:
            cp.start()
        for cp in recvs:
            cp.wait_recv()
        for cp in sends:
            cp.wait_send()

    return pl.pallas_call(
        body, name=name, in_specs=[_ANY] * n, out_specs=[_ANY] * n,
        out_shape=[jax.ShapeDtypeStruct(out_shape_of(b), b.dtype) for b in blks],
        scratch_shapes=[pltpu.SemaphoreType.DMA((per_block * n,)), pltpu.SemaphoreType.DMA((per_block * n,))],
    )(*blks)


def _put_slot(buf, block, idx):
    return lax.dynamic_update_slice(buf, block[None].astype(buf.dtype), (idx,) + (0,) * block.ndim)


def pair_sum(name, mine, theirs):
    _, R, C = mine.shape
    tr = _pick(R, (1024, 512, 256, 128, 64, 32, 16))
    c_idx = lax.axis_index("c").astype(jnp.int32).reshape(1)

    def body(c_ref, a_ref, b_ref, o_ref):
        o_ref[...] = (a_ref[0].astype(F32) + b_ref[...].astype(F32)).astype(BF16)

    return pl.pallas_call(
        body, name=name,
        grid_spec=pltpu.PrefetchScalarGridSpec(
            num_scalar_prefetch=1, grid=(R // tr,),
            in_specs=[pl.BlockSpec((1, tr, C), lambda i, c: (c[0], i, 0)), pl.BlockSpec((tr, C), lambda i, c: (i, 0))],
            out_specs=pl.BlockSpec((tr, C), lambda i, c: (i, 0))),
        out_shape=jax.ShapeDtypeStruct(theirs.shape, BF16), compiler_params=_cparams("parallel"),
    )(c_idx, mine, theirs)


SHARDED = {
    "w_in": ((D_MODEL, IN_WIDTH), 2), "ssm_w_glu": ((BRANCH, BRANCH), 1), "conv_w": ((3, BRANCH), 2),
    "w_branch": ((3, BRANCH, D_MODEL), 3), "w_out": ((D_MODEL, D_MODEL), 1), "w_ffn_in": ((D_MODEL, 2 * FFN_HIDDEN), 2),
    "w_ffn_out": ((FFN_HIDDEN, D_MODEL), 1), "w_ple_gate": ((D_MODEL, D_MODEL), 1), "w_ple_proj": ((PLE_DIM, D_MODEL), 2),
}
SMALL = ["rel_bias", "norm_mix", "ssm_lambda_re", "ssm_lambda_im", "ssm_b_re", "ssm_b_im", "ssm_c_re", "ssm_c_im", "ssm_d",
         "ssm_log_dt", "attn_sinks", "norm_ffn", "norm_ple", "norm_final"]
SHARDED_NAMES = list(SHARDED)
GATHER_SPLIT = 4
WEIGHTS = ["rel_bias", "norm_mix", "w_in", "ssm_lambda_re", "ssm_lambda_im", "ssm_b_re", "ssm_b_im", "ssm_c_re", "ssm_c_im",
           "ssm_d", "ssm_log_dt", "ssm_w_glu", "conv_w", "attn_sinks", "w_branch", "w_out", "norm_ffn", "w_ffn_in",
           "w_ffn_out", "norm_ple", "w_ple_gate", "w_ple_proj", "norm_final"]


def _pad_to(flat, n):
    return jnp.pad(flat, [(0, 0)] * (flat.ndim - 1) + [(0, n - flat.shape[-1])])


def _unshard(g8, name):
    axis = SHARDED[name][1] - 1
    shard = g8.shape[2:]
    b = g8.reshape((2, 2, 2) + shard)
    b = jnp.moveaxis(b, (1, 2, 0), (axis, axis + 1, axis + 2))
    full = list(shard)
    full[axis] *= N_DEV
    return b.reshape(full)


def _shard_split(full, name):
    axis = SHARDED[name][1] - 1
    dims = list(full.shape)
    dims[axis:axis + 1] = [2, 2, 2, dims[axis] // N_DEV]
    b = jnp.moveaxis(full.reshape(dims), (axis, axis + 1, axis + 2), (1, 2, 0))
    return b.reshape((2, 4) + b.shape[3:])


def _small_sizes(shapes):
    return [-(-int(np.prod(shapes[n])) // 128) * 128 for n in SMALL]


def pack_small(vals, shapes, extra):
    segs = [_pad_to(vals[n].reshape(-1).astype(F32), s) for n, s in zip(SMALL, _small_sizes(shapes))]
    segs.append(_pad_to(extra.reshape(-1), 128))
    flat = jnp.concatenate(segs)
    rows = -(-flat.shape[0] // (128 * 8)) * 8
    return _pad_to(flat, rows * 128).reshape(rows, 128)


def unpack_small(packed, shapes):
    flat = packed.reshape(-1)
    out, off = {}, 0
    for n, s in zip(SMALL, _small_sizes(shapes)):
        out[n] = flat[off:off + int(np.prod(shapes[n]))].reshape(shapes[n])
        off += s
    return out, flat[off]


def _layer_weights(gathered, small, p, i):
    full = {n: _unshard(g, n) for n, g in zip(SHARDED, gathered)}
    w_in = full["w_in"]
    w_in_p = jnp.concatenate([w_in[:, 2816:], w_in[:, :2560], w_in[:, 2560:2816]], axis=-1)
    return dict(
        w_in=w_in_p,
        ssm_w_glu=full["ssm_w_glu"], ssm_w_glu_t=full["ssm_w_glu"].T,
        conv_w=full["conv_w"],
        w_branch=full["w_branch"], w_out=full["w_out"], w_ffn_in=full["w_ffn_in"], w_ffn_out=full["w_ffn_out"],
        w_ple_gate=full["w_ple_gate"],
        w_ple_proj=full["w_ple_proj"],
        norm_mix=small["norm_mix"][i][None, :], norm_ffn=small["norm_ffn"][i][None, :],
        norm_ple=small["norm_ple"][i][None, :],
        ssm_lambda_re=small["ssm_lambda_re"][i], ssm_lambda_im=small["ssm_lambda_im"][i],
        ssm_log_dt=small["ssm_log_dt"][i],
        bt_re=jnp.swapaxes(small["ssm_b_re"][i], 1, 2), bt_im=jnp.swapaxes(small["ssm_b_im"][i], 1, 2),
        ssm_c_re=small["ssm_c_re"][i], ssm_c_im=small["ssm_c_im"][i], ssm_d=small["ssm_d"][i][None, :],
        attn_sinks=small["attn_sinks"][i], p=p[i],
    )


def matrix_grads(g):
    w_in_g = g["w_in"]
    return dict(
        w_in=jnp.concatenate([w_in_g[:, 3072:5632], w_in_g[:, 5632:], w_in_g[:, :3072]], axis=-1),
        ssm_w_glu=g["ssm_w_glu"], conv_w=jnp.sum(g["conv_w"].reshape(3, 8, BRANCH), axis=1),
        w_branch=g["w_branch"], w_out=g["w_out"], w_ffn_in=g["w_ffn_in"], w_ffn_out=g["w_ffn_out"],
        w_ple_gate=g["w_ple_gate"], w_ple_proj=g["w_ple_proj"])


def small_grads(per_layer, dg_final):
    keys = ("dbias", "norm_mix", "ssm_lambda_re", "ssm_lambda_im", "bt_re", "bt_im", "ssm_c_re", "ssm_c_im", "ssm_d",
            "ssm_log_dt", "attn_sinks", "norm_ffn", "norm_ple")
    g = {k: jnp.stack([gl[k] for gl in per_layer]) for k in keys}
    drel = rel_bias_bwd(g["dbias"].reshape(DEPTH, N_Q_HEADS, BLOCK * 2 * BLOCK)).T
    return dict(
        rel_bias=drel, norm_mix=jnp.sum(g["norm_mix"], axis=1), ssm_lambda_re=g["ssm_lambda_re"],
        ssm_lambda_im=g["ssm_lambda_im"], ssm_b_re=jnp.swapaxes(g["bt_re"], 2, 3), ssm_b_im=jnp.swapaxes(g["bt_im"], 2, 3),
        ssm_c_re=g["ssm_c_re"], ssm_c_im=g["ssm_c_im"], ssm_d=g["ssm_d"][:, 0, :], ssm_log_dt=g["ssm_log_dt"],
        attn_sinks=g["attn_sinks"][:, :, 0], norm_ffn=jnp.sum(g["norm_ffn"], axis=1), norm_ple=jnp.sum(g["norm_ple"], axis=1),
        norm_final=jnp.sum(dg_final, axis=0))


def kernel(x, p, rel_bias, norm_mix, w_in, ssm_lambda_re, ssm_lambda_im, ssm_b_re, ssm_b_im, ssm_c_re, ssm_c_im, ssm_d, ssm_log_dt, ssm_w_glu, conv_w, attn_sinks, w_branch, w_out, norm_ffn, w_ffn_in, w_ffn_out, norm_ple, w_ple_gate, w_ple_proj, norm_final, loss_target, m_rel_bias, m_norm_mix, m_w_in, m_ssm_lambda_re, m_ssm_lambda_im, m_ssm_b_re, m_ssm_b_im, m_ssm_c_re, m_ssm_c_im, m_ssm_d, m_ssm_log_dt, m_ssm_w_glu, m_conv_w, m_attn_sinks, m_w_branch, m_w_out, m_norm_ffn, m_w_ffn_in, m_w_ffn_out, m_norm_ple, m_w_ple_gate, m_w_ple_proj, m_norm_final, v_rel_bias, v_norm_mix, v_w_in, v_ssm_lambda_re, v_ssm_lambda_im, v_ssm_b_re, v_ssm_b_im, v_ssm_c_re, v_ssm_c_im, v_ssm_d, v_ssm_log_dt, v_ssm_w_glu, v_conv_w, v_attn_sinks, v_w_branch, v_w_out, v_norm_ffn, v_w_ffn_in, v_w_ffn_out, v_norm_ple, v_w_ple_gate, v_w_ple_proj, v_norm_final):
    args = dict(locals())
    w = {n: args[n] for n in WEIGHTS}
    m = {n: args["m_" + n] for n in WEIGHTS}
    v = {n: args["v_" + n] for n in WEIGHTS}
    shapes = {n: w[n].shape for n in SMALL}

    x_i, y_i, c_i = _coords()
    chip = 2 * x_i + y_i
    place = (chip, c_i)
    small = {n: w[n] for n in SMALL}

    def all_gather(tag, blks):
        g4 = exchange(f"gather_{tag}_chips", "gather_chips", blks)
        g4 = [_put_slot(g, b, chip) for g, b in zip(g4, blks)]
        g8 = exchange(f"gather_{tag}_cores", "gather_cores", g4)
        return [_put_slot(g, b, c_i) for g, b in zip(g8, g4)]

    def shards_of(layer):
        return [w[n][layer] if n == "conv_w" else w[n][layer].astype(BF16) for n in SHARDED]

    bias = rel_bias_fwd(small["rel_bias"].T).reshape(N_Q_HEADS, BLOCK, 2 * BLOCK)
    xs, layers, res = x[0], [], []
    gathered = all_gather("w", shards_of(0))
    for layer in range(DEPTH):
        layers.append(_layer_weights(gathered, small, p[:, 0], layer))
        nxt = shards_of(layer + 1) if layer + 1 < DEPTH else None
        xs, res_l, gathered = layer_fwd(xs, layers[layer], bias, nxt, place)
        res.append(res_l)
    grad_x, loss_parts, dg_final = loss_head(xs, small["norm_final"][None, :], loss_target[0])

    per_layer, reduced, pending = [None] * DEPTH, [None] * DEPTH, None
    for layer in reversed(range(DEPTH)):
        grad_x, per_layer[layer], done, own_done = layer_bwd(grad_x, layers[layer], res[layer], bias, pending,
                                                             scatter_own=layer == 0)
        if pending is not None:
            reduced[layer + 1] = done
        mg = matrix_grads(per_layer[layer])
        pending = [_shard_split(mg[n], n).astype(BF16) for n in SHARDED]
    early = pending[:GATHER_SPLIT]
    sums = _pair_sums(early, exchange("scatter_g_cores", "scatter_cores", early), SHARDED_NAMES[:GATHER_SPLIT])
    received = exchange("scatter_g_chips", "scatter_chips", sums)
    reduced[0] = (list(sums) + list(own_done[0]), list(received) + list(own_done[1]))

    outs = ({}, {}, {}, {})
    for k, name in enumerate(SHARDED):
        parts = jnp.stack([_put_slot(rcv[k], lax.dynamic_index_in_dim(sm[k], chip, 0, keepdims=False), chip)
                           for sm, rcv in reduced], axis=1)
        cols = parts.shape[-1]
        res4 = adamw("adamw_" + name, parts.reshape(4, -1, cols), w[name].reshape(-1, cols), m[name].reshape(-1, cols),
                     v[name].reshape(-1, cols))
        for d, o in zip(outs, res4):
            d[name] = o.reshape(w[name].shape)

    small_local = pack_small(small_grads(per_layer, dg_final), shapes, jnp.sum(loss_parts))
    small_all = all_gather("s", [small_local])[0]
    zero = jnp.zeros((1,), F32)
    res4 = adamw("adamw_small", small_all.reshape(N_DEV, small_local.shape[0], 128),
                 pack_small({n: w[n] for n in SMALL}, shapes, zero), pack_small({n: m[n] for n in SMALL}, shapes, zero),
                 pack_small({n: v[n] for n in SMALL}, shapes, zero))
    loss = None
    for d, r in zip(outs, res4):
        vals, extra = unpack_small(r, shapes)
        d.update(vals)
        if loss is None:
            loss = extra

    return (loss, grad_x[None], *[d[n] for d in outs for n in WEIGHTS])
```

```python
import functools
import math

import numpy as np
import jax
import jax.numpy as jnp
from jax import lax
from jax.experimental import pallas as pl
from jax.experimental.pallas import tpu as pltpu

F32 = jnp.float32
BF16 = jnp.bfloat16
MESH = pl.DeviceIdType.MESH

D_MODEL = 1024
DEPTH = 4
PLE_DIM = 256
BRANCH = 512
SSM_GROUPS = 32
SSM_GROUP = 16
SSM_STATE = 64
SSM_LANES = SSM_GROUPS * SSM_STATE
SSM_SUB = 4
SUB_IN = BRANCH // SSM_SUB
SUB_ST = SSM_LANES // SSM_SUB
HEAD_DIM = 64
N_Q_HEADS = 8
N_KV_HEADS = 2
GQA_GROUP = 4
KV_WIDTH = 2 * N_KV_HEADS * HEAD_DIM
WINDOW = 128
BLOCK = 128
ATTN_SCALE = 1.0 / math.sqrt(HEAD_DIM)
REL_BUCKETS = 32
REL_MAX_DIST = 128
FFN_HIDDEN = 2816
RMS_EPS = 1e-6
IN_WIDTH = 5888
N_DEV = 8

ADAM_LR = 0.001
ADAM_B1 = 0.9
ADAM_B2 = 0.999
ADAM_EPS = 1e-08
ADAM_WD = 0.01
ADAM_STEP = 10

COL_U = 3072
COL_KV = 5632
NEG = -1e30

SCAN_T = 256
TM = 512
TM_W = 1024
VMEM_LIMIT = 52 * 1024 * 1024


def _cparams(*sem):
    return pltpu.CompilerParams(dimension_semantics=sem, vmem_limit_bytes=VMEM_LIMIT)


def _full(shape):
    n = len(shape)
    return pl.BlockSpec(shape, lambda *_: (0,) * n)


def _pick(n, cands):
    for c in cands:
        if n % c == 0:
            return c
    return n


def _call(body, *, name, steps, in_specs, out_specs, out_shape, scratch, args, side=None):
    in_specs, out_specs, out_shape = list(in_specs), list(out_specs), list(out_shape)
    scratch, args = list(scratch), list(args)
    n_in, n_out, n_scr = len(in_specs), len(out_specs), len(scratch)
    n = 0
    if side is not None:
        kind, blks = side
        out_shape_of, per_block, make_copies = _EXCHANGES[kind]
        n = len(blks)
        inner = body

        def body(*refs):
            ins, sx = refs[:n_in], refs[n_in:n_in + n]
            outs, so = refs[n_in + n:n_in + n + n_out], refs[n_in + n + n_out:n_in + 2 * n + n_out]
            scr = refs[n_in + 2 * n + n_out:n_in + 2 * n + n_out + n_scr]
            send_sems, recv_sems = refs[-2:]

            @pl.when(pl.program_id(0) == 0)
            def _():
                sends, _ = make_copies(sx, so, send_sems, recv_sems)
                for cp in sends:
                    cp.start()

            inner(*ins, *outs, *scr)

            @pl.when(pl.program_id(0) == steps - 1)
            def _():
                sends, recvs = make_copies(sx, so, send_sems, recv_sems)
                for cp in recvs:
                    cp.wait_recv()
                for cp in sends:
                    cp.wait_send()

        in_specs += [_ANY] * n
        args += list(blks)
        out_specs += [_ANY] * n
        out_shape += [jax.ShapeDtypeStruct(out_shape_of(b), b.dtype) for b in blks]
        scratch += [pltpu.SemaphoreType.DMA((per_block * n,)), pltpu.SemaphoreType.DMA((per_block * n,))]
    outs = pl.pallas_call(
        body, name=name, grid=(steps,), in_specs=in_specs, out_specs=out_specs, out_shape=out_shape,
        scratch_shapes=scratch, compiler_params=_cparams("arbitrary"),
    )(*args)
    return outs[:n_out], outs[n_out:]


def _dot(a, b):
    return jnp.dot(a, b, preferred_element_type=F32)


def _dot_tn(a, b):
    return lax.dot_general(a, b, (((0,), (0,)), ((), ())), preferred_element_type=F32)


def _dot_nt(a, b):
    return lax.dot_general(a, b, (((1,), (1,)), ((), ())), preferred_element_type=F32)


def _rms(x, g):
    r = lax.rsqrt(jnp.mean(x * x, axis=-1, keepdims=True) + RMS_EPS)
    return x * r * g


def fused_mm(name, ins, in_specs, prologue, w, *, tn, out_dtype, res=None, extras=(), side=None, nt=False):
    S = ins[0].shape[0]
    N, K = w.shape if nt else w.shape[::-1]
    tn = min(tn, N)
    n_in, n_ex = len(ins), len(extras)

    def body(*refs):
        in_refs = refs[:n_in]
        w_ref = refs[n_in]
        pos = n_in + 1
        res_ref = None
        if res is not None:
            res_ref = refs[pos]
            pos += 1
        o_ref = refs[pos]
        ex_refs = refs[pos + 1:pos + 1 + n_ex]
        a_scr = refs[-1]
        out = prologue(*[r[...] for r in in_refs])
        a_scr[...] = out[0]
        for r, e in zip(ex_refs, out[1:]):
            r[...] = e.astype(r.dtype)
        for j in range(N // tn):
            cs = slice(j * tn, (j + 1) * tn)
            acc = _dot_nt(a_scr[...], w_ref[cs, :]) if nt else _dot(a_scr[...], w_ref[:, cs])
            if res_ref is not None:
                acc = acc + res_ref[:, cs]
            o_ref[:, cs] = acc.astype(o_ref.dtype)

    specs = list(in_specs) + [pl.BlockSpec(w.shape, lambda i: (0, 0), pipeline_mode=pl.Buffered(1))]
    args = list(ins) + [w]
    if res is not None:
        specs.append(pl.BlockSpec((TM, N), lambda i: (i, 0)))
        args.append(res)
    out_shape = [jax.ShapeDtypeStruct((S, N), out_dtype)]
    out_specs = [pl.BlockSpec((TM, N), lambda i: (i, 0))]
    for cols, dt in extras:
        out_shape.append(jax.ShapeDtypeStruct((S, cols), dt))
        out_specs.append(pl.BlockSpec((TM, cols), lambda i: (i, 0)))
    outs, side_outs = _call(body, name=name, steps=S // TM, in_specs=specs, out_specs=out_specs, out_shape=out_shape,
                            scratch=[pltpu.VMEM((TM, K), BF16)], args=args, side=side)
    result = outs if n_ex else outs[0]
    return result if side is None else (result, side_outs)


def _row_spec(cols, blk=0, tm=TM):
    return pl.BlockSpec((tm, cols), lambda i: (i, blk))


def mm_norm_bwd(name, ins, in_specs, w, x, g, dres, *, tm, pre=None, extras=()):
    S = x.shape[0]
    n = len(ins)

    def body(*refs):
        x_ref, g_ref, dres_ref, w_ref, dx_ref, dg_ref = refs[n:n + 6]

        @pl.when(pl.program_id(0) == 0)
        def _():
            dg_ref[...] = jnp.zeros_like(dg_ref)

        tiles = [r[...] for r in refs[:n]]
        if pre is not None:
            tiles, extra_tiles = pre(*tiles)
            for r, e in zip(refs[n + 6:], extra_tiles):
                r[...] = e.astype(r.dtype)
        tiles = [t.astype(BF16) for t in tiles]
        a = tiles[0] if len(tiles) == 1 else jnp.concatenate(tiles, axis=1)
        dh = _dot_nt(a, w_ref[...])
        xv = x_ref[...]
        r = lax.rsqrt(jnp.mean(xv * xv, axis=-1, keepdims=True) + RMS_EPS)
        xhat = xv * r
        dxhat = dh * g_ref[...]
        dx_ref[...] = dres_ref[...] + r * (dxhat - xhat * jnp.mean(dxhat * xhat, axis=-1, keepdims=True))
        dg_ref[...] += jnp.sum((dh * xhat).reshape(tm // 8, 8, D_MODEL), axis=0)

    row = _row_spec(D_MODEL, tm=tm)
    return pl.pallas_call(
        body, name=name, grid=(S // tm,),
        in_specs=list(in_specs)
        + [row, _full((1, D_MODEL)), row, pl.BlockSpec(w.shape, lambda i: (0, 0), pipeline_mode=pl.Buffered(1))],
        out_specs=[row, _full((8, D_MODEL))] + [_row_spec(cols, tm=tm) for cols, _ in extras],
        out_shape=[jax.ShapeDtypeStruct((S, D_MODEL), F32), jax.ShapeDtypeStruct((8, D_MODEL), F32)]
        + [jax.ShapeDtypeStruct((S, cols), dt) for cols, dt in extras],
        compiler_params=_cparams("arbitrary"),
    )(*ins, x, g, dres, w)


def mm_tn_multi(name, a, pieces):
    S, K = a.shape
    n = len(pieces)

    def body(a_ref, *refs):
        @pl.when(pl.program_id(0) == 0)
        def _():
            for o_ref in refs[n:]:
                o_ref[...] = jnp.zeros_like(o_ref)

        at = a_ref[...].astype(BF16)
        for p_ref, o_ref in zip(refs[:n], refs[n:]):
            o_ref[...] += _dot_tn(at, p_ref[...].astype(BF16))

    return pl.pallas_call(
        body, name=name, grid=(S // TM_W,),
        in_specs=[_row_spec(K, tm=TM_W)] + [_row_spec(p.shape[1], tm=TM_W) for p in pieces],
        out_specs=[_full((K, p.shape[1])) for p in pieces],
        out_shape=[jax.ShapeDtypeStruct((K, p.shape[1]), F32) for p in pieces],
        compiler_params=_cparams("arbitrary"),
    )(a, *pieces)


def mm_tn(name, a, b):
    S, K = a.shape
    N = b.shape[1]
    tk = _pick(K, (1024, 1408, 512, 256))
    tn = _pick(N, (1024, 1408, 1536, 512, 256))

    def body(a_ref, b_ref, o_ref):
        @pl.when(pl.program_id(2) == 0)
        def _():
            o_ref[...] = jnp.zeros_like(o_ref)

        o_ref[...] += _dot_tn(a_ref[...].astype(BF16), b_ref[...].astype(BF16))

    return pl.pallas_call(
        body, name=name, grid=(K // tk, N // tn, S // TM_W),
        in_specs=[pl.BlockSpec((TM_W, tk), lambda k, n, s: (s, k)), pl.BlockSpec((TM_W, tn), lambda k, n, s: (s, n))],
        out_specs=pl.BlockSpec((tk, tn), lambda k, n, s: (k, n)),
        out_shape=jax.ShapeDtypeStruct((K, N), F32),
        compiler_params=_cparams("parallel", "parallel", "arbitrary"),
    )(a, b)


def _swiglu_bwd_pre(dact, h1, h2):
    h1 = h1.astype(F32)
    h2 = h2.astype(F32)
    da = dact.astype(F32)
    sg = jax.nn.sigmoid(h1)
    halves = [(da * h2 * sg * (1.0 + h1 * (1.0 - sg))).astype(BF16), (da * h1 * sg).astype(BF16)]
    return halves, halves


def _ple_bwd_pre(dx, a_pre, pp):
    pg = jax.nn.sigmoid(a_pre.astype(F32))
    da = (dx * pp.astype(F32) * pg * (1.0 - pg)).astype(BF16)
    return [da], [da, (dx * pg).astype(BF16)]


def ple_out(x2, a_pre, p, w_pp):
    S = x2.shape[0]

    def body(x_ref, a_ref, p_ref, w_ref, pp_ref, o_ref):
        pp = _dot(p_ref[...].astype(BF16), w_ref[...]).astype(BF16)
        pp_ref[...] = pp
        o_ref[...] = x_ref[...] + jax.nn.sigmoid(a_ref[...].astype(F32)) * pp.astype(F32)

    row = pl.BlockSpec((TM, D_MODEL), lambda i: (i, 0))
    return pl.pallas_call(
        body, name="ple_out", grid=(S // TM,),
        in_specs=[row, row, pl.BlockSpec((TM, PLE_DIM), lambda i: (i, 0)), _full((PLE_DIM, D_MODEL))],
        out_specs=[row, row],
        out_shape=[jax.ShapeDtypeStruct((S, D_MODEL), BF16), jax.ShapeDtypeStruct((S, D_MODEL), F32)],
        compiler_params=_cparams("parallel"),
    )(x2, a_pre, p, w_pp)


def merge_fwd(z, y_ssm, y_conv, y_attn, wb, w_out, x):
    S = z.shape[0]

    def body(g0, g1, g2, y0, y1, y2, w_ref, wo_ref, x_ref, m_ref, o_ref):
        acc = jnp.zeros((TM, D_MODEL), F32)
        for r, (g_ref, y_ref) in enumerate(((g0, y0), (g1, y1), (g2, y2))):
            acc += jax.nn.sigmoid(g_ref[...].astype(F32)) * _dot(y_ref[...], w_ref[r])
        merged = acc.astype(BF16)
        m_ref[...] = merged
        o_ref[...] = x_ref[...] + _dot(merged, wo_ref[...])

    wide = pl.BlockSpec((TM, D_MODEL), lambda i: (i, 0))
    y_spec = pl.BlockSpec((TM, BRANCH), lambda i: (i, 0))
    gates = [pl.BlockSpec((TM, D_MODEL), functools.partial(lambda i, r: (i, r), r=r)) for r in range(3)]
    return pl.pallas_call(
        body, name="merge_fwd", grid=(S // TM,),
        in_specs=gates + [y_spec] * 3
        + [pl.BlockSpec((3, BRANCH, D_MODEL), lambda i: (0, 0, 0), pipeline_mode=pl.Buffered(1)),
           pl.BlockSpec((D_MODEL, D_MODEL), lambda i: (0, 0), pipeline_mode=pl.Buffered(1)), wide],
        out_specs=[wide, wide],
        out_shape=[jax.ShapeDtypeStruct((S, D_MODEL), BF16), jax.ShapeDtypeStruct((S, D_MODEL), F32)],
        compiler_params=_cparams("parallel"),
    )(z, z, z, y_ssm, y_conv, y_attn, wb, w_out, x)


def merge_bwd(dmerged, z, y_ssm, y_conv, y_attn, wb, side=None):
    S = z.shape[0]

    def body(dm_ref, g0, g1, g2, y0, y1, y2, w_ref, dg0, dg1, dg2, db0, db1, db2, dy0, dy1, dy2):
        dm = dm_ref[...].astype(F32)
        rows = ((g0, y0, dg0, db0, dy0), (g1, y1, dg1, db1, dy1), (g2, y2, dg2, db2, dy2))
        for r, (g_ref, y_ref, dg_ref, db_ref, dy_ref) in enumerate(rows):
            sg = jax.nn.sigmoid(g_ref[...].astype(F32))
            b = _dot(y_ref[...], w_ref[r])
            dg_ref[...] = (dm * b * sg * (1.0 - sg)).astype(BF16)
            db = (dm * sg).astype(BF16)
            db_ref[...] = db
            dy_ref[...] = _dot_nt(db, w_ref[r]).astype(BF16)

    wide = pl.BlockSpec((TM, D_MODEL), lambda i: (i, 0))
    y_spec = pl.BlockSpec((TM, BRANCH), lambda i: (i, 0))
    gates = [pl.BlockSpec((TM, D_MODEL), functools.partial(lambda i, r: (i, r), r=r)) for r in range(3)]
    outs, side_outs = _call(
        body, name="merge_bwd", steps=S // TM,
        in_specs=[wide] + gates + [y_spec] * 3
        + [pl.BlockSpec((3, BRANCH, D_MODEL), lambda i: (0, 0, 0), pipeline_mode=pl.Buffered(1))],
        out_specs=[wide] * 6 + [y_spec] * 3,
        out_shape=[jax.ShapeDtypeStruct((S, D_MODEL), BF16)] * 6 + [jax.ShapeDtypeStruct((S, BRANCH), BF16)] * 3,
        scratch=[], args=(dmerged, z, z, z, y_ssm, y_conv, y_attn, wb), side=side)
    return (outs[:3], outs[3:6], outs[6:]), side_outs


def _shift_down(v, halo, k):
    rolled = pltpu.roll(v, k, 0)
    h = pltpu.roll(halo, k, 0)
    row = lax.broadcasted_iota(jnp.int32, v.shape, 0)
    head = jnp.concatenate([h, jnp.zeros((v.shape[0] - 8, v.shape[1]), v.dtype)], axis=0)
    return jnp.where(row < k, head, rolled)


def _shift_up(v, halo, k):
    n = v.shape[0]
    rolled = pltpu.roll(v, n - k, 0)
    h = pltpu.roll(halo, 8 - k, 0)
    row = lax.broadcasted_iota(jnp.int32, v.shape, 0)
    tail = jnp.concatenate([jnp.zeros((n - 8, v.shape[1]), v.dtype), h], axis=0)
    return jnp.where(row >= n - k, tail, rolled)


def _conv_specs():
    rb = TM // 8
    c0 = COL_U // BRANCH

    def cur(k):
        return pl.BlockSpec((TM, BRANCH), lambda i: (i, c0 + k))

    def prev(k):
        return pl.BlockSpec((8, BRANCH), lambda i: (jnp.maximum(i * rb - 1, 0), c0 + k))

    return [cur(1), cur(2), cur(3), prev(2), prev(3)]


def conv_fwd(z, conv_w):
    S = z.shape[0]

    def body(cb_ref, cc_ref, cx_ref, pc_ref, px_ref, w_ref, o_ref):
        first = pl.program_id(0) == 0
        v = cc_ref[...].astype(F32) * cx_ref[...].astype(F32)
        pv = jnp.where(first, 0.0, pc_ref[...].astype(F32) * px_ref[...].astype(F32))
        w = w_ref[...]
        y = w[2:3] * v + w[1:2] * _shift_down(v, pv, 1) + w[0:1] * _shift_down(v, pv, 2)
        o_ref[...] = (cb_ref[...].astype(F32) * y).astype(BF16)

    return pl.pallas_call(
        body, name="conv_fwd", grid=(S // TM,), in_specs=_conv_specs() + [_full((3, BRANCH))],
        out_specs=pl.BlockSpec((TM, BRANCH), lambda i: (i, 0)),
        out_shape=jax.ShapeDtypeStruct((S, BRANCH), BF16), compiler_params=_cparams("parallel"),
    )(z, z, z, z, z, conv_w)


def conv_bwd(dy, z, conv_w):
    S = z.shape[0]
    rb = TM // 8
    nt = S // TM
    c0 = COL_U // BRANCH

    def body(dy_ref, cb_ref, cc_ref, cx_ref, pc_ref, px_ref, ndy_ref, ncb_ref, w_ref, o_ref, dw_ref):
        i = pl.program_id(0)

        @pl.when(i == 0)
        def _():
            dw_ref[...] = jnp.zeros_like(dw_ref)

        cb = cb_ref[...].astype(F32)
        cc = cc_ref[...].astype(F32)
        cx = cx_ref[...].astype(F32)
        dyv = dy_ref[...].astype(F32)
        v = cc * cx
        pv = jnp.where(i == 0, 0.0, pc_ref[...].astype(F32) * px_ref[...].astype(F32))
        v1 = _shift_down(v, pv, 1)
        v2 = _shift_down(v, pv, 2)
        w = w_ref[...]
        conv = w[2:3] * v + w[1:2] * v1 + w[0:1] * v2
        dc = dyv * cb
        ndc = jnp.where(i == nt - 1, 0.0, ndy_ref[...].astype(F32) * ncb_ref[...].astype(F32))
        dv = w[2:3] * dc + w[1:2] * _shift_up(dc, ndc, 1) + w[0:1] * _shift_up(dc, ndc, 2)
        o_ref[:, 0:BRANCH] = (dyv * conv).astype(BF16)
        o_ref[:, BRANCH:2 * BRANCH] = (dv * cx).astype(BF16)
        o_ref[:, 2 * BRANCH:3 * BRANCH] = (dv * cc).astype(BF16)
        for k, vk in enumerate((v2, v1, v)):
            dw_ref[8 * k:8 * k + 8, :] += jnp.sum((dc * vk).reshape(rb, 8, BRANCH), axis=0)

    nxt = jnp.minimum

    return pl.pallas_call(
        body, name="conv_bwd", grid=(nt,),
        in_specs=[pl.BlockSpec((TM, BRANCH), lambda i: (i, 0))] + _conv_specs()
        + [pl.BlockSpec((8, BRANCH), lambda i: (nxt((i + 1) * rb, S // 8 - 1), 0)),
           pl.BlockSpec((8, BRANCH), lambda i: (nxt((i + 1) * rb, S // 8 - 1), c0 + 1)),
           _full((3, BRANCH))],
        out_specs=[pl.BlockSpec((TM, 3 * BRANCH), lambda i: (i, 0)), _full((24, BRANCH))],
        out_shape=[jax.ShapeDtypeStruct((S, 3 * BRANCH), BF16), jax.ShapeDtypeStruct((24, BRANCH), F32)],
        compiler_params=_cparams("arbitrary"),
    )(dy, z, z, z, z, z, dy, z, conv_w)


def _bucket_onehot_t():
    qi = np.arange(BLOCK)[:, None]
    kj = np.arange(2 * BLOCK)[None, :]
    dist = np.clip(qi + BLOCK - kj, 0, REL_MAX_DIST - 1)
    exact = REL_BUCKETS // 2
    df = np.maximum(dist, 1).astype(np.float32)
    large = exact + (np.log(df / np.float32(exact)) / np.float32(math.log(REL_MAX_DIST / exact))
                     * np.float32(REL_BUCKETS - exact)).astype(np.int32)
    large = np.minimum(large, REL_BUCKETS - 1)
    bucket = np.where(dist < exact, dist, large).reshape(-1)
    return (np.arange(REL_BUCKETS)[:, None] == bucket[None, :]).astype(np.float32)


def rel_bias_fwd(rel_bias_t):
    n = BLOCK * 2 * BLOCK

    def body(r_ref, oh_ref, o_ref):
        o_ref[...] = jnp.dot(r_ref[...], oh_ref[...], precision=lax.Precision.HIGHEST, preferred_element_type=F32)

    return pl.pallas_call(
        body, name="rel_bias_fwd", grid=(1,), in_specs=[_full((N_Q_HEADS, REL_BUCKETS)), _full((REL_BUCKETS, n))],
        out_specs=_full((N_Q_HEADS, n)), out_shape=jax.ShapeDtypeStruct((N_Q_HEADS, n), F32),
        compiler_params=_cparams("arbitrary"),
    )(rel_bias_t, jnp.asarray(_bucket_onehot_t()))


def rel_bias_bwd(dbias):
    n_l = dbias.shape[0]
    n = BLOCK * 2 * BLOCK

    def body(d_ref, oh_ref, o_ref):
        tot = d_ref[0]
        for l in range(1, n_l):
            tot = tot + d_ref[l]
        o_ref[...] = lax.dot_general(tot, oh_ref[...], (((1,), (1,)), ((), ())), precision=lax.Precision.HIGHEST,
                                     preferred_element_type=F32)

    return pl.pallas_call(
        body, name="rel_bias_bwd", grid=(1,), in_specs=[_full((n_l, N_Q_HEADS, n)), _full((REL_BUCKETS, n))],
        out_specs=_full((N_Q_HEADS, REL_BUCKETS)), out_shape=jax.ShapeDtypeStruct((N_Q_HEADS, REL_BUCKETS), F32),
        compiler_params=_cparams("arbitrary"),
    )(dbias, jnp.asarray(_bucket_onehot_t()))


def _attn_valid(first):
    qi = lax.broadcasted_iota(jnp.int32, (BLOCK, 2 * BLOCK), 0)
    kj = lax.broadcasted_iota(jnp.int32, (BLOCK, 2 * BLOCK), 1)
    dist = qi + BLOCK - kj
    return (dist >= 0) & (dist < WINDOW) & (jnp.logical_not(first) | (kj >= BLOCK))


def _attn_weights(qh, kcat, bias_h, valid, sink):
    s = _dot_nt(qh, kcat) * ATTN_SCALE + bias_h
    s = jnp.where(valid, s, NEG)
    m = jnp.maximum(jnp.max(s, axis=-1, keepdims=True), sink)
    p = jnp.exp(s - m)
    esink = jnp.exp(sink - m)
    inv = 1.0 / (jnp.sum(p, axis=-1, keepdims=True) + esink)
    return p * inv, esink * inv


def _kv_heads(kvp, kvc, hk):
    ks = slice(hk * HEAD_DIM, (hk + 1) * HEAD_DIM)
    vs = slice(KV_WIDTH // 2 + hk * HEAD_DIM, KV_WIDTH // 2 + (hk + 1) * HEAD_DIM)
    return jnp.concatenate([kvp[:, ks], kvc[:, ks]], axis=0), jnp.concatenate([kvp[:, vs], kvc[:, vs]], axis=0)


def _attn_specs():
    cq = (COL_U + 4 * BRANCH) // BRANCH
    ckv = COL_KV // KV_WIDTH
    return [pl.BlockSpec((BLOCK, BRANCH), lambda n: (n, cq)),
            pl.BlockSpec((BLOCK, KV_WIDTH), lambda n: (n, ckv)),
            pl.BlockSpec((BLOCK, KV_WIDTH), lambda n: (jnp.maximum(n - 1, 0), ckv)),
            _full((N_Q_HEADS, BLOCK, 2 * BLOCK)),
            pl.BlockSpec(memory_space=pltpu.SMEM)]


def attn_fwd(z, bias, sinks, side=None):
    S = z.shape[0]

    def body(q_ref, kvc_ref, kvp_ref, b_ref, sink_ref, o_ref):
        valid = _attn_valid(pl.program_id(0) == 0)
        q = q_ref[...]
        kvc = kvc_ref[...]
        kvp = kvp_ref[...]
        outs = []
        for hk in range(N_KV_HEADS):
            kcat, vcat = _kv_heads(kvp, kvc, hk)
            for g in range(GQA_GROUP):
                h = hk * GQA_GROUP + g
                w, _ = _attn_weights(q[:, h * HEAD_DIM:(h + 1) * HEAD_DIM], kcat, b_ref[h], valid, sink_ref[h])
                outs.append(_dot(w.astype(BF16), vcat))
        o_ref[...] = jnp.concatenate(outs, axis=1).astype(BF16)

    outs, side_outs = _call(
        body, name="attn_fwd", steps=S // BLOCK, in_specs=_attn_specs(),
        out_specs=[pl.BlockSpec((BLOCK, BRANCH), lambda n: (n, 0))],
        out_shape=[jax.ShapeDtypeStruct((S, BRANCH), BF16)], scratch=[], args=(z, z, z, bias, sinks), side=side)
    return outs[0], side_outs


def attn_bwd(do, z, bias, sinks, side=None):
    S = z.shape[0]

    def body(do_ref, q_ref, kvc_ref, kvp_ref, bt_ref, sink_ref, dq_ref, dc_ref, dp_ref, db_ref, ds_ref):
        first = pl.program_id(0) == 0

        @pl.when(first)
        def _():
            db_ref[...] = jnp.zeros_like(db_ref)
            ds_ref[...] = jnp.zeros_like(ds_ref)

        kj = lax.broadcasted_iota(jnp.int32, (2 * BLOCK, BLOCK), 0)
        dist = lax.broadcasted_iota(jnp.int32, (2 * BLOCK, BLOCK), 1) + BLOCK - kj
        valid = (dist >= 0) & (dist < WINDOW) & (jnp.logical_not(first) | (kj >= BLOCK))
        valid4 = jnp.concatenate([valid] * GQA_GROUP, axis=1)
        q = q_ref[...]
        kvc = kvc_ref[...]
        kvp = kvp_ref[...]
        dov = do_ref[...]
        dqs, dks, dvs = [], [], []
        for hk in range(N_KV_HEADS):
            kcat, vcat = _kv_heads(kvp, kvc, hk)
            heads = range(hk * GQA_GROUP, (hk + 1) * GQA_GROUP)
            q4 = jnp.concatenate([q[:, h * HEAD_DIM:(h + 1) * HEAD_DIM] for h in heads], axis=0)
            do4 = jnp.concatenate([dov[:, h * HEAD_DIM:(h + 1) * HEAD_DIM] for h in heads], axis=0)
            bias4 = jnp.concatenate([bt_ref[h] for h in heads], axis=1)
            sink4 = jnp.concatenate([jnp.full((1, BLOCK), sink_ref[h], F32) for h in heads], axis=1)
            s = jnp.where(valid4, _dot_nt(kcat, q4) * ATTN_SCALE + bias4, NEG)
            m = jnp.maximum(jnp.max(s, axis=0, keepdims=True), sink4)
            p = jnp.exp(s - m)
            esink = jnp.exp(sink4 - m)
            inv = 1.0 / (jnp.sum(p, axis=0, keepdims=True) + esink)
            w = p * inv
            dvs.append(_dot(w.astype(BF16), do4))
            dw = _dot_nt(vcat, do4)
            delta = jnp.sum(w * dw, axis=0, keepdims=True)
            ds = w * (dw - delta)
            dsink = -(esink * inv) * delta
            for g, h in enumerate(heads):
                lanes = slice(g * BLOCK, (g + 1) * BLOCK)
                db_ref[h] += ds[:, lanes]
                ds_ref[h:h + 1, :] += jnp.broadcast_to(jnp.sum(dsink[:, lanes], axis=1, keepdims=True), (1, BLOCK))
            dsb = (ds * ATTN_SCALE).astype(BF16)
            dks.append(_dot(dsb, q4))
            dq4 = _dot_tn(dsb, kcat)
            dqs += [dq4[g * BLOCK:(g + 1) * BLOCK] for g in range(GQA_GROUP)]
        dq_ref[...] = jnp.concatenate(dqs, axis=1).astype(BF16)
        both = jnp.concatenate(dks + dvs, axis=1)
        dp_ref[...] = both[:BLOCK]
        dc_ref[...] = both[BLOCK:]

    blk = pl.BlockSpec((BLOCK, BRANCH), lambda n: (n, 0))
    kvb = pl.BlockSpec((BLOCK, KV_WIDTH), lambda n: (n, 0))
    keys_first = (N_Q_HEADS, 2 * BLOCK, BLOCK)
    specs = _attn_specs()
    specs[3] = _full(keys_first)
    (dq, dkv_cur, dkv_prev, dbias_t, dsinks), side_outs = _call(
        body, name="attn_bwd", steps=S // BLOCK, in_specs=[blk] + specs,
        out_specs=[blk, kvb, kvb, _full(keys_first), _full((N_Q_HEADS, BLOCK))],
        out_shape=[jax.ShapeDtypeStruct((S, BRANCH), BF16), jax.ShapeDtypeStruct((S, KV_WIDTH), F32),
                   jax.ShapeDtypeStruct((S, KV_WIDTH), F32), jax.ShapeDtypeStruct(keys_first, F32),
                   jax.ShapeDtypeStruct((N_Q_HEADS, BLOCK), F32)],
        scratch=[], args=(do, z, z, z, jnp.swapaxes(bias, 1, 2), sinks), side=side)
    return (dq, dkv_cur, dkv_prev, jnp.swapaxes(dbias_t, 1, 2), dsinks), side_outs


def kv_shift_add(dcur, dprev):
    S = dcur.shape[0]
    nt = S // TM
    per_tile = TM // BLOCK

    def body(c_ref, p_ref, n_ref, o_ref):
        nxt = jnp.where(pl.program_id(0) == nt - 1, 0.0, n_ref[...])
        o_ref[...] = (c_ref[...] + jnp.concatenate([p_ref[BLOCK:, :], nxt], axis=0)).astype(BF16)

    tile = pl.BlockSpec((TM, KV_WIDTH), lambda i: (i, 0))
    return pl.pallas_call(
        body, name="kv_shift_add", grid=(nt,),
        in_specs=[tile, tile,
                  pl.BlockSpec((BLOCK, KV_WIDTH), lambda i: (jnp.minimum((i + 1) * per_tile, S // BLOCK - 1), 0))],
        out_specs=tile, out_shape=jax.ShapeDtypeStruct((S, KV_WIDTH), BF16), compiler_params=_cparams("parallel"),
    )(dcur, dprev, dprev)


def _ssm_disc(lam_re, lam_im, log_dt, bt_re, bt_im):
    dt = jnp.exp(log_dt)
    mag = jnp.exp(lam_re * dt)
    ang = lam_im * dt
    a_re = mag * jnp.cos(ang)
    a_im = mag * jnp.sin(ang)
    den = lam_re * lam_re + lam_im * lam_im
    nr = a_re - 1.0
    coef_re = (nr * lam_re + a_im * lam_im) / den
    coef_im = (a_im * lam_re - nr * lam_im) / den
    bb_re = coef_re[:, None, :] * bt_re - coef_im[:, None, :] * bt_im
    bb_im = coef_re[:, None, :] * bt_im + coef_im[:, None, :] * bt_re
    return a_re, a_im, bb_re, bb_im


_GN = (SSM_GROUPS, SSM_STATE)
_GPN = (SSM_GROUPS, SSM_GROUP, SSM_STATE)


def ssm_disc_fwd(lam_re, lam_im, log_dt, bt_re, bt_im):
    def body(lr_ref, li_ref, dt_ref, br_ref, bi_ref, ar_ref, ai_ref, bbr_ref, bbi_ref):
        a_re, a_im, bb_re, bb_im = _ssm_disc(lr_ref[...], li_ref[...], dt_ref[...], br_ref[...], bi_ref[...])
        ar_ref[...] = a_re
        ai_ref[...] = a_im
        bbr_ref[...] = bb_re
        bbi_ref[...] = bb_im

    return pl.pallas_call(
        body, name="ssm_disc_fwd", grid=(1,),
        in_specs=[_full(_GN), _full(_GN), _full((SSM_GROUPS, 1)), _full(_GPN), _full(_GPN)],
        out_specs=[_full(_GN), _full(_GN), _full(_GPN), _full(_GPN)],
        out_shape=[jax.ShapeDtypeStruct(s, F32) for s in (_GN, _GN, _GPN, _GPN)],
        compiler_params=_cparams("arbitrary"),
    )(lam_re, lam_im, log_dt, bt_re, bt_im)


def ssm_disc_bwd(lam_re, lam_im, log_dt, bt_re, bt_im, da_re, da_im, dbb_re, dbb_im):
    def body(lr_ref, li_ref, dt_ref, br_ref, bi_ref, dar_ref, dai_ref, dbr_ref, dbi_ref, o_lr, o_li, o_dt, o_br, o_bi):
        prim = (lr_ref[...], li_ref[...], dt_ref[...], br_ref[...], bi_ref[...])
        _, vjp = jax.vjp(_ssm_disc, *prim)
        grads = vjp((dar_ref[...], dai_ref[...], dbr_ref[...], dbi_ref[...]))
        for r, v in zip((o_lr, o_li, o_dt, o_br, o_bi), grads):
            r[...] = v

    shapes = (_GN, _GN, (SSM_GROUPS, 1), _GPN, _GPN)
    return pl.pallas_call(
        body, name="ssm_disc_bwd", grid=(1,),
        in_specs=[_full(s) for s in shapes + (_GN, _GN, _GPN, _GPN)],
        out_specs=[_full(s) for s in shapes], out_shape=[jax.ShapeDtypeStruct(s, F32) for s in shapes],
        compiler_params=_cparams("arbitrary"),
    )(lam_re, lam_im, log_dt, bt_re, bt_im, da_re, da_im, dbb_re, dbb_im)


LANE_GROUPS = SSM_LANES // 128
SUB_GROUPS = SUB_ST // 128
_TM_SHAPE = (LANE_GROUPS, 128)


def _step_rows(t):
    return pl.ds(pl.multiple_of(t * LANE_GROUPS, LANE_GROUPS), LANE_GROUPS)


def _group_rows(j):
    return pl.ds(j, SCAN_T, stride=LANE_GROUPS)


def _store_sub(ref, j, val):
    for k in range(SUB_GROUPS):
        ref[_group_rows(j * SUB_GROUPS + k), :] = val[:, k * 128:(k + 1) * 128]


def _load_sub(ref, j):
    return jnp.concatenate([ref[_group_rows(j * SUB_GROUPS + k), :] for k in range(SUB_GROUPS)], axis=1)


_SUB_SHAPE_IN = (SSM_SUB, SUB_IN, SUB_ST)
_SUB_SHAPE_OUT = (SSM_SUB, SUB_ST, SUB_IN)


def ssm_fwd(z, bb_re, bb_im, ct_re, ct_im, a_re, a_im, d_skip, wglu, side=None):
    S = z.shape[0]
    cu = COL_U // BRANCH

    def body(u_ref, bbr_ref, bbi_ref, ctr_ref, cti_ref, ar_ref, ai_ref, d_ref, wg_ref,
             y_ref, ypre_ref, hr_ref, hi_ref, hrow_r, hrow_i, bur, bui, car_r, car_i):
        @pl.when(pl.program_id(0) == 0)
        def _():
            car_r[...] = jnp.zeros_like(car_r)
            car_i[...] = jnp.zeros_like(car_i)

        u = u_ref[...]
        for j in range(SSM_SUB):
            uj = u[:, j * SUB_IN:(j + 1) * SUB_IN]
            _store_sub(bur, j, _dot(uj, bbr_ref[j]))
            _store_sub(bui, j, _dot(uj, bbi_ref[j]))
        ar = ar_ref[...]
        ai = ai_ref[...]

        def step(t, carry):
            hr, hi = carry
            rows = _step_rows(t)
            nhr = ar * hr - ai * hi + bur[rows, :]
            nhi = ar * hi + ai * hr + bui[rows, :]
            hr_ref[rows, :] = nhr
            hi_ref[rows, :] = nhi
            return nhr, nhi

        hr, hi = lax.fori_loop(0, SCAN_T, step, (car_r[...], car_i[...]), unroll=8)
        car_r[...] = hr
        car_i[...] = hi
        ys = []
        for j in range(SSM_SUB):
            cs = slice(j * SUB_ST, (j + 1) * SUB_ST)
            hrow_r[:, cs] = _load_sub(hr_ref, j).astype(BF16)
            hrow_i[:, cs] = _load_sub(hi_ref, j).astype(BF16)
            ys.append(_dot(hrow_r[:, cs], ctr_ref[j]) - _dot(hrow_i[:, cs], cti_ref[j]))
        ypre = jnp.concatenate(ys, axis=1) + d_ref[...] * u.astype(F32)
        ypre_ref[...] = ypre
        g = jax.nn.gelu(ypre)
        y_ref[...] = (g * jax.nn.sigmoid(_dot(g.astype(BF16), wg_ref[...]))).astype(BF16)

    row = pl.BlockSpec((SCAN_T, BRANCH), lambda i: (i, 0))
    st = pl.BlockSpec((SCAN_T * LANE_GROUPS, 128), lambda i: (i, 0))
    wide = pl.BlockSpec((SCAN_T, SSM_LANES), lambda i: (i, 0))
    return _call(
        body, name="ssm_fwd", steps=S // SCAN_T,
        in_specs=[pl.BlockSpec((SCAN_T, BRANCH), lambda i: (i, cu)), _full(_SUB_SHAPE_IN), _full(_SUB_SHAPE_IN),
                  _full(_SUB_SHAPE_OUT), _full(_SUB_SHAPE_OUT), _full(_TM_SHAPE), _full(_TM_SHAPE), _full((1, BRANCH)),
                  _full((BRANCH, BRANCH))],
        out_specs=[row, row, st, st, wide, wide],
        out_shape=[jax.ShapeDtypeStruct((S, BRANCH), BF16), jax.ShapeDtypeStruct((S, BRANCH), F32),
                   jax.ShapeDtypeStruct((S * LANE_GROUPS, 128), F32), jax.ShapeDtypeStruct((S * LANE_GROUPS, 128), F32),
                   jax.ShapeDtypeStruct((S, SSM_LANES), BF16), jax.ShapeDtypeStruct((S, SSM_LANES), BF16)],
        scratch=[pltpu.VMEM((SCAN_T * LANE_GROUPS, 128), F32), pltpu.VMEM((SCAN_T * LANE_GROUPS, 128), F32),
                 pltpu.VMEM(_TM_SHAPE, F32), pltpu.VMEM(_TM_SHAPE, F32)],
        args=(z, bb_re, bb_im, ct_re, ct_im, a_re, a_im, d_skip, wglu), side=side)


def ssm_bwd(dy, z, ypre, h_re, h_im, hrow_re, hrow_im, bbt_re, bbt_im, c_re, c_im, a_re, a_im, d_skip, wglu, wglu_t,
            side=None):
    S = z.shape[0]
    nt = S // SCAN_T
    cu = COL_U // BRANCH

    def body(dy_ref, u_ref, ypre_ref, hr_ref, hi_ref, hpr_ref, hpi_ref, hrow_r, hrow_i, bbr_ref, bbi_ref, cr_ref, ci_ref,
             ar_ref, ai_ref, d_ref, wg_ref, wgt_ref,
             du_ref, dbbr_ref, dbbi_ref, dctr_ref, dcti_ref, dar_ref, dai_ref, dd_ref, dwg_ref,
             lr_scr, li_scr, car_r, car_i):
        step = pl.program_id(0)

        @pl.when(step == 0)
        def _():
            for r in (dbbr_ref, dbbi_ref, dctr_ref, dcti_ref, dar_ref, dai_ref, dd_ref, dwg_ref, car_r, car_i):
                r[...] = jnp.zeros_like(r)

        u = u_ref[...]
        uf = u.astype(F32)
        dyv = dy_ref[...].astype(F32)
        g, gelu_vjp = jax.vjp(jax.nn.gelu, ypre_ref[...])
        gb = g.astype(BF16)
        sg = jax.nn.sigmoid(_dot(gb, wg_ref[...]))
        dgl = (dyv * g * sg * (1.0 - sg)).astype(BF16)
        dwg_ref[...] += _dot_tn(gb, dgl)
        dg = dyv * sg + _dot(dgl, wgt_ref[...])
        dypre = gelu_vjp(dg)[0]
        dd_ref[...] += jnp.sum(dypre * uf, axis=0, keepdims=True)
        dyb = dypre.astype(BF16)
        for j in range(SSM_SUB):
            dyj = dyb[:, j * SUB_IN:(j + 1) * SUB_IN]
            _store_sub(lr_scr, j, _dot(dyj, cr_ref[j]))
            _store_sub(li_scr, j, -_dot(dyj, ci_ref[j]))
            cs = slice(j * SUB_ST, (j + 1) * SUB_ST)
            dctr_ref[j] += _dot_tn(hrow_r[:, cs], dyj)
            dcti_ref[j] -= _dot_tn(hrow_i[:, cs], dyj)

        ar = ar_ref[...]
        ai = ai_ref[...]

        def adjoint(lr, li, rows):
            nlr = ar * lr + ai * li + lr_scr[rows, :]
            nli = ar * li - ai * lr + li_scr[rows, :]
            lr_scr[rows, :] = nlr
            li_scr[rows, :] = nli
            return nlr, nli

        def back(k, carry):
            lr, li, acc_r, acc_i = carry
            t = SCAN_T - 1 - k
            lr, li = adjoint(lr, li, _step_rows(t))
            hpr = hr_ref[_step_rows(t - 1), :]
            hpi = hi_ref[_step_rows(t - 1), :]
            return lr, li, acc_r + lr * hpr + li * hpi, acc_i + li * hpr - lr * hpi

        zero = jnp.zeros(_TM_SHAPE, F32)
        lr, li, acc_r, acc_i = lax.fori_loop(0, SCAN_T - 1, back, (car_r[...], car_i[...], zero, zero), unroll=8)
        lr, li = adjoint(lr, li, pl.ds(0, LANE_GROUPS))
        car_r[...] = lr
        car_i[...] = li
        first_tile = step == nt - 1
        hpr = jnp.where(first_tile, 0.0, hpr_ref[...])
        hpi = jnp.where(first_tile, 0.0, hpi_ref[...])
        dar_ref[...] += acc_r + lr * hpr + li * hpi
        dai_ref[...] += acc_i + li * hpr - lr * hpi

        dus = []
        for j in range(SSM_SUB):
            lrb = _load_sub(lr_scr, j).astype(BF16)
            lib = _load_sub(li_scr, j).astype(BF16)
            uj = u[:, j * SUB_IN:(j + 1) * SUB_IN]
            dus.append(_dot(lrb, bbr_ref[j]) + _dot(lib, bbi_ref[j]))
            dbbr_ref[j] += _dot_tn(uj, lrb)
            dbbi_ref[j] += _dot_tn(uj, lib)
        du_ref[...] = (jnp.concatenate(dus, axis=1) + dypre * d_ref[...]).astype(BF16)

    def rev(i):
        return nt - 1 - i

    row = pl.BlockSpec((SCAN_T, BRANCH), lambda i: (rev(i), 0))
    st = pl.BlockSpec((SCAN_T * LANE_GROUPS, 128), lambda i: (rev(i), 0))
    before = pl.BlockSpec(_TM_SHAPE, lambda i: (jnp.maximum(rev(i) * SCAN_T - 1, 0), 0))
    wide = pl.BlockSpec((SCAN_T, SSM_LANES), lambda i: (rev(i), 0))
    tm = _full(_TM_SHAPE)
    return _call(
        body, name="ssm_bwd", steps=nt,
        in_specs=[row, pl.BlockSpec((SCAN_T, BRANCH), lambda i: (rev(i), cu)), row, st, st, before, before, wide, wide,
                  _full(_SUB_SHAPE_OUT), _full(_SUB_SHAPE_OUT), _full(_SUB_SHAPE_IN), _full(_SUB_SHAPE_IN),
                  tm, tm, _full((1, BRANCH)), _full((BRANCH, BRANCH)), _full((BRANCH, BRANCH))],
        out_specs=[row, _full(_SUB_SHAPE_IN), _full(_SUB_SHAPE_IN), _full(_SUB_SHAPE_OUT), _full(_SUB_SHAPE_OUT),
                   tm, tm, _full((1, BRANCH)), _full((BRANCH, BRANCH))],
        out_shape=[jax.ShapeDtypeStruct((S, BRANCH), BF16), jax.ShapeDtypeStruct(_SUB_SHAPE_IN, F32),
                   jax.ShapeDtypeStruct(_SUB_SHAPE_IN, F32), jax.ShapeDtypeStruct(_SUB_SHAPE_OUT, F32),
                   jax.ShapeDtypeStruct(_SUB_SHAPE_OUT, F32), jax.ShapeDtypeStruct(_TM_SHAPE, F32),
                   jax.ShapeDtypeStruct(_TM_SHAPE, F32), jax.ShapeDtypeStruct((1, BRANCH), F32),
                   jax.ShapeDtypeStruct((BRANCH, BRANCH), F32)],
        scratch=[pltpu.VMEM((SCAN_T * LANE_GROUPS, 128), F32), pltpu.VMEM((SCAN_T * LANE_GROUPS, 128), F32),
                 pltpu.VMEM(_TM_SHAPE, F32), pltpu.VMEM(_TM_SHAPE, F32)],
        args=(dy, z, ypre, h_re, h_im, h_re, h_im, hrow_re, hrow_im, bbt_re, bbt_im, c_re, c_im, a_re, a_im, d_skip, wglu,
              wglu_t),
        side=side)


def _blockdiag(x):
    gs = SSM_GROUPS // SSM_SUB
    x = x.reshape(SSM_SUB, gs, SSM_GROUP, SSM_STATE)
    eye = jnp.eye(gs, dtype=x.dtype)
    return (x[:, :, :, None, :] * eye[None, :, None, :, None]).reshape(SSM_SUB, SUB_IN, SUB_ST)


def _blockdiag_extract(x):
    gs = SSM_GROUPS // SSM_SUB
    x = x.reshape(SSM_SUB, gs, SSM_GROUP, gs, SSM_STATE)
    eye = jnp.eye(gs, dtype=x.dtype)
    return jnp.sum(x * eye[None, :, None, :, None], axis=3).reshape(SSM_GROUPS, SSM_GROUP, SSM_STATE)


def loss_head(x, g, target):
    S = x.shape[0]

    def body(x_ref, g_ref, t_ref, dx_ref, loss_ref, dg_ref):
        @pl.when(pl.program_id(0) == 0)
        def _():
            loss_ref[...] = jnp.zeros_like(loss_ref)
            dg_ref[...] = jnp.zeros_like(dg_ref)

        xv = x_ref[...]
        gv = g_ref[...]
        r = lax.rsqrt(jnp.mean(xv * xv, axis=-1, keepdims=True) + RMS_EPS)
        xhat = xv * r
        err = xhat * gv - t_ref[...]
        loss_ref[...] += jnp.sum((err * err).reshape(TM // 8, 8, D_MODEL), axis=0) * (0.5 / D_MODEL)
        dy = err * (1.0 / D_MODEL)
        dxhat = dy * gv
        dx_ref[...] = r * (dxhat - xhat * jnp.mean(dxhat * xhat, axis=-1, keepdims=True))
        dg_ref[...] += jnp.sum((dy * xhat).reshape(TM // 8, 8, D_MODEL), axis=0)

    row = pl.BlockSpec((TM, D_MODEL), lambda i: (i, 0))
    acc = _full((8, D_MODEL))
    return pl.pallas_call(
        body, name="loss_head", grid=(S // TM,), in_specs=[row, _full((1, D_MODEL)), row],
        out_specs=[row, acc, acc],
        out_shape=[jax.ShapeDtypeStruct((S, D_MODEL), F32), jax.ShapeDtypeStruct((8, D_MODEL), F32),
                   jax.ShapeDtypeStruct((8, D_MODEL), F32)],
        compiler_params=_cparams("arbitrary"),
    )(x, g, target)


def _x_spec():
    return pl.BlockSpec((TM, D_MODEL), lambda i: (i, 0))


def _g_spec():
    return pl.BlockSpec((1, D_MODEL), lambda i: (0, 0))


def _norm_prologue(x, g):
    h = _rms(x, g).astype(BF16)
    return h, h


def _cast_prologue(x):
    return (x.astype(BF16),)


def _swiglu_prologue(h1, h2):
    a = h1.astype(F32)
    act = (a * jax.nn.sigmoid(a) * h2.astype(F32)).astype(BF16)
    return act, act


def _ssm_consts(lw):
    a_re, a_im, bbt_re, bbt_im = ssm_disc_fwd(
        lw["ssm_lambda_re"], lw["ssm_lambda_im"], lw["ssm_log_dt"].reshape(SSM_GROUPS, 1), lw["bt_re"], lw["bt_im"])
    bb_re = _blockdiag(bbt_re).astype(BF16)
    bb_im = _blockdiag(bbt_im).astype(BF16)
    c_re = _blockdiag(lw["ssm_c_re"]).astype(BF16)
    c_im = _blockdiag(lw["ssm_c_im"]).astype(BF16)
    return dict(
        a_re=a_re.reshape(_TM_SHAPE), a_im=a_im.reshape(_TM_SHAPE),
        bb_re=bb_re, bb_im=bb_im, bbt_re=jnp.swapaxes(bb_re, 1, 2), bbt_im=jnp.swapaxes(bb_im, 1, 2),
        c_re=c_re, c_im=c_im, ct_re=jnp.swapaxes(c_re, 1, 2), ct_im=jnp.swapaxes(c_im, 1, 2))


def layer_fwd(x, lw, bias, next_shards=None, place=None):
    sides = (None, None) if next_shards is None else (("gather_chips", next_shards[:GATHER_SPLIT]),
                                                      ("gather_chips", next_shards[GATHER_SPLIT:]))
    out = fused_mm("in_proj", [x, lw["norm_mix"]], [_x_spec(), _g_spec()], _norm_prologue, lw["w_in"], tn=2944,
                   out_dtype=BF16, extras=((D_MODEL, BF16),), side=sides[0])
    (z, h), g4a = (out, []) if next_shards is None else out
    sc = _ssm_consts(lw)
    (y_ssm, ypre, h_re, h_im, hrow_re, hrow_im), g4b = ssm_fwd(
        z, sc["bb_re"], sc["bb_im"], sc["ct_re"], sc["ct_im"], sc["a_re"], sc["a_im"], lw["ssm_d"], lw["ssm_w_glu"],
        side=sides[1])
    y_conv = conv_fwd(z, lw["conv_w"])
    if next_shards is not None:
        g4 = [_put_slot(g, b, place[0]) for g, b in zip(list(g4a) + list(g4b), next_shards)]
    y_attn, g8 = attn_fwd(z, bias, lw["attn_sinks"], side=None if next_shards is None else ("gather_cores", g4))
    next_gathered = None if next_shards is None else [_put_slot(g, b, place[1]) for g, b in zip(g8, g4)]
    merged, x1 = merge_fwd(z, y_ssm, y_conv, y_attn, lw["w_branch"], lw["w_out"], x)
    hf, hn1 = fused_mm("ffn_in", [x1, lw["norm_ffn"]], [_x_spec(), _g_spec()], _norm_prologue, lw["w_ffn_in"], tn=2816,
                       out_dtype=BF16, extras=((D_MODEL, BF16),))
    x2, act = fused_mm("ffn_out", [hf, hf], [_row_spec(FFN_HIDDEN, 0), _row_spec(FFN_HIDDEN, 1)], _swiglu_prologue,
                       lw["w_ffn_out"], tn=1024, out_dtype=F32, res=x1, extras=((FFN_HIDDEN, BF16),))
    a_pre, hn2 = fused_mm("ple_gate", [x2, lw["norm_ple"]], [_x_spec(), _g_spec()], _norm_prologue, lw["w_ple_gate"],
                          tn=1024, out_dtype=BF16, extras=((D_MODEL, BF16),))
    pp, x3 = ple_out(x2, a_pre, lw["p"], lw["w_ple_proj"])
    res = dict(x=x, z=z, h=h, y_ssm=y_ssm, ypre=ypre, h_re=h_re, h_im=h_im, hrow_re=hrow_re, hrow_im=hrow_im, y_conv=y_conv, y_attn=y_attn, merged=merged,
               x1=x1, hf=hf, hn1=hn1, act=act, x2=x2, a_pre=a_pre, hn2=hn2, pp=pp)
    return x3, res, next_gathered


def _pair_sums(split, from_sibling, names):
    return [pair_sum("pair_sum_" + n, a.reshape(2, -1, a.shape[-1]), b.reshape(-1, b.shape[-1])).reshape(b.shape)
            for n, a, b in zip(names, split, from_sibling)]


def layer_bwd(dx3, lw, res, bias, pending=None, scatter_own=False):
    g = {}
    wide = ((D_MODEL, BF16), (D_MODEL, BF16))
    dx2, g["norm_ple"], da, dpp = mm_norm_bwd(
        "d_ple_gate", [dx3, res["a_pre"], res["pp"]], [_row_spec(D_MODEL)] * 3, lw["w_ple_gate"], res["x2"],
        lw["norm_ple"], dx3, tm=TM, pre=_ple_bwd_pre, extras=wide)
    g["w_ple_proj"] = mm_tn("d_w_ple_proj", lw["p"], dpp)
    g["w_ple_gate"] = mm_tn("d_w_ple_gate", res["hn2"], da)
    sums = None
    if pending is None:
        dact = fused_mm("d_ffn_out", [dx2], [_x_spec()], _cast_prologue, lw["w_ffn_out"], tn=1408, out_dtype=BF16,
                        nt=True)
    else:
        dact, from_sibling = fused_mm("d_ffn_out", [dx2], [_x_spec()], _cast_prologue, lw["w_ffn_out"], tn=1408,
                                      out_dtype=BF16, side=("scatter_cores", pending), nt=True)
        sums = _pair_sums(pending, from_sibling, SHARDED_NAMES)
    g["w_ffn_out"] = mm_tn("d_w_ffn_out", res["act"], dx2)
    half = TM // 2
    dx1, g["norm_ffn"], dh1, dh2 = mm_norm_bwd(
        "d_ffn_in", [dact, res["hf"], res["hf"]],
        [_row_spec(FFN_HIDDEN, 0, half), _row_spec(FFN_HIDDEN, 0, half), _row_spec(FFN_HIDDEN, 1, half)],
        lw["w_ffn_in"], res["x1"], lw["norm_ffn"], dx2, tm=half, pre=_swiglu_bwd_pre,
        extras=((FFN_HIDDEN, BF16), (FFN_HIDDEN, BF16)))
    g["w_ffn_in"] = jnp.concatenate([mm_tn("d_w_ffn_in_a", res["hn1"], dh1), mm_tn("d_w_ffn_in_b", res["hn1"], dh2)],
                                    axis=1)
    dmerged = fused_mm("d_out_proj", [dx1], [_x_spec()], _cast_prologue, lw["w_out"], tn=1024, out_dtype=BF16, nt=True)
    g["w_out"] = mm_tn("d_w_out", res["merged"], dx1)
    z = res["z"]
    ys = (res["y_ssm"], res["y_conv"], res["y_attn"])
    own_sums = None
    if scatter_own:
        late = SHARDED_NAMES[GATHER_SPLIT:]
        own = [_shard_split(g[n], n).astype(BF16) for n in late]
        (dgates, dbs, dys), from_sibling = merge_bwd(dmerged, z, *ys, lw["w_branch"], side=("scatter_cores", own))
        own_sums = _pair_sums(own, from_sibling, late)
    else:
        (dgates, dbs, dys), _ = merge_bwd(dmerged, z, *ys, lw["w_branch"])
    g["w_branch"] = jnp.stack([mm_tn(f"d_w_branch_{r}", ys[r], dbs[r]) for r in range(3)])
    sc = _ssm_consts(lw)
    (du, dbb_re, dbb_im, dct_re, dct_im, da_re, da_im, g["ssm_d"], g["ssm_w_glu"]), received = ssm_bwd(
        dys[0], z, res["ypre"], res["h_re"], res["h_im"], res["hrow_re"], res["hrow_im"], sc["bbt_re"], sc["bbt_im"], sc["c_re"], sc["c_im"],
        sc["a_re"], sc["a_im"], lw["ssm_d"], lw["ssm_w_glu"], lw["ssm_w_glu_t"],
        side=None if pending is None else ("scatter_chips", sums))
    g["ssm_c_re"] = _blockdiag_extract(jnp.swapaxes(dct_re, 1, 2))
    g["ssm_c_im"] = _blockdiag_extract(jnp.swapaxes(dct_im, 1, 2))
    (g["ssm_lambda_re"], g["ssm_lambda_im"], dlog_dt, g["bt_re"], g["bt_im"]) = ssm_disc_bwd(
        lw["ssm_lambda_re"], lw["ssm_lambda_im"], lw["ssm_log_dt"].reshape(SSM_GROUPS, 1), lw["bt_re"], lw["bt_im"],
        da_re.reshape(_GN), da_im.reshape(_GN),
        _blockdiag_extract(dbb_re), _blockdiag_extract(dbb_im))
    g["ssm_log_dt"] = dlog_dt.reshape(SSM_GROUPS)
    dconv, g["conv_w"] = conv_bwd(dys[1], z, lw["conv_w"])
    (dq, dkv_cur, dkv_prev, g["dbias"], g["attn_sinks"]), own_received = attn_bwd(
        dys[2], z, bias, lw["attn_sinks"], side=None if own_sums is None else ("scatter_chips", own_sums))
    dkv = kv_shift_add(dkv_cur, dkv_prev)
    pieces = [dgates[0], dgates[1], dgates[2], du, dconv, dq, dkv]
    g["w_in"] = jnp.concatenate(mm_tn_multi("d_w_in_gates", res["h"], pieces[:3])
                                + mm_tn_multi("d_w_in_branches", res["h"], pieces[3:]), axis=1)
    dx0, g["norm_mix"] = mm_norm_bwd("d_in_proj", pieces, [_row_spec(pc.shape[1], tm=half) for pc in pieces], lw["w_in"],
                                     res["x"], lw["norm_mix"], dx1, tm=half)
    return dx0, g, (sums, received), (own_sums, own_received)


def adamw(name, parts, w, m, v):
    n, R, C = parts.shape
    tr = _pick(R, (512, 256, 128, 64, 32, 16, 8))

    def body(p_ref, w_ref, m_ref, v_ref, g_ref, d_ref, nm_ref, nv_ref):
        gsum = p_ref[0].astype(F32)
        for k in range(1, n):
            gsum = gsum + p_ref[k].astype(F32)
        mn = ADAM_B1 * m_ref[...] + (1.0 - ADAM_B1) * gsum
        vn = ADAM_B2 * v_ref[...] + (1.0 - ADAM_B2) * jnp.square(gsum)
        m_hat = mn / (1.0 - ADAM_B1 ** ADAM_STEP)
        v_hat = vn / (1.0 - ADAM_B2 ** ADAM_STEP)
        g_ref[...] = gsum
        d_ref[...] = -ADAM_LR * (m_hat / (jnp.sqrt(v_hat) + ADAM_EPS) + ADAM_WD * w_ref[...])
        nm_ref[...] = mn
        nv_ref[...] = vn

    blk = pl.BlockSpec((tr, C), lambda i: (i, 0))
    return pl.pallas_call(
        body, name=name, grid=(R // tr,), in_specs=[pl.BlockSpec((n, tr, C), lambda i: (0, i, 0)), blk, blk, blk],
        out_specs=[blk] * 4, out_shape=[jax.ShapeDtypeStruct((R, C), F32)] * 4, compiler_params=_cparams("parallel"),
    )(parts, w, m, v)


_ANY = pl.BlockSpec(memory_space=pl.ANY)


def _coords():
    return lax.axis_index("x"), lax.axis_index("y"), lax.axis_index("c")


def _chip_peers(x, y):
    return [(1 - x, y), (x, 1 - y), (1 - x, 1 - y)]


def _gather_chips_copies(x_refs, out_refs, send_sems, recv_sems):
    x, y, c = _coords()
    me = 2 * x + y
    peers = _chip_peers(x, y)

    def copy(i, k, slot):
        return pltpu.make_async_remote_copy(
            src_ref=x_refs[i], dst_ref=out_refs[i].at[slot], send_sem=send_sems.at[3 * i + k],
            recv_sem=recv_sems.at[3 * i + k], device_id=(*peers[k], c), device_id_type=MESH)

    n = len(x_refs)
    sends = [copy(i, k, me) for i in range(n) for k in range(3)]
    recvs = [copy(i, k, 2 * px + py) for i in range(n) for k, (px, py) in enumerate(peers)]
    return sends, recvs


def _gather_cores_copies(x_refs, out_refs, send_sems, recv_sems):
    x, y, c = _coords()

    def copy(i, slot):
        return pltpu.make_async_remote_copy(
            src_ref=x_refs[i], dst_ref=out_refs[i].at[slot], send_sem=send_sems.at[i], recv_sem=recv_sems.at[i],
            device_id=(x, y, 1 - c), device_id_type=MESH)

    n = len(x_refs)
    return [copy(i, c) for i in range(n)], [copy(i, 1 - c) for i in range(n)]


def _scatter_cores_copies(x_refs, out_refs, send_sems, recv_sems):
    x, y, c = _coords()
    copies = [pltpu.make_async_remote_copy(
        src_ref=x_refs[i].at[1 - c], dst_ref=out_refs[i], send_sem=send_sems.at[i], recv_sem=recv_sems.at[i],
        device_id=(x, y, 1 - c), device_id_type=MESH) for i in range(len(x_refs))]
    return copies, copies


def _scatter_chips_copies(x_refs, out_refs, send_sems, recv_sems):
    x, y, c = _coords()
    me = 2 * x + y
    peers = _chip_peers(x, y)

    def copy(i, k, src_slot, dst_slot):
        return pltpu.make_async_remote_copy(
            src_ref=x_refs[i].at[src_slot], dst_ref=out_refs[i].at[dst_slot], send_sem=send_sems.at[3 * i + k],
            recv_sem=recv_sems.at[3 * i + k], device_id=(*peers[k], c), device_id_type=MESH)

    n = len(x_refs)
    sends = [copy(i, k, 2 * px + py, me) for i in range(n) for k, (px, py) in enumerate(peers)]
    recvs = [copy(i, k, me, 2 * px + py) for i in range(n) for k, (px, py) in enumerate(peers)]
    return sends, recvs


_EXCHANGES = {
    "gather_chips": (lambda b: (4,) + b.shape, 3, _gather_chips_copies),
    "gather_cores": (lambda b: (2,) + b.shape, 1, _gather_cores_copies),
    "scatter_cores": (lambda b: b.shape[1:], 1, _scatter_cores_copies),
    "scatter_chips": (lambda b: b.shape, 3, _scatter_chips_copies),
}


def exchange(name, kind, blks):
    out_shape_of, per_block, make_copies = _EXCHANGES[kind]
    n = len(blks)

    def body(*refs):
        sends, recvs = make_copies(refs[:n], refs[n:2 * n], refs[2 * n], refs[2 * n + 1])
        for cp in sends:
            cp.start()
        for cp in recvs:
            cp.wait_recv()
        for cp in sends:
            cp.wait_send()

    return pl.pallas_call(
        body, name=name, in_specs=[_ANY] * n, out_specs=[_ANY] * n,
        out_shape=[jax.ShapeDtypeStruct(out_shape_of(b), b.dtype) for b in blks],
        scratch_shapes=[pltpu.SemaphoreType.DMA((per_block * n,)), pltpu.SemaphoreType.DMA((per_block * n,))],
    )(*blks)


def _put_slot(buf, block, idx):
    return lax.dynamic_update_slice(buf, block[None].astype(buf.dtype), (idx,) + (0,) * block.ndim)


def pair_sum(name, mine, theirs):
    _, R, C = mine.shape
    tr = _pick(R, (1024, 512, 256, 128, 64, 32, 16))
    c_idx = lax.axis_index("c").astype(jnp.int32).reshape(1)

    def body(c_ref, a_ref, b_ref, o_ref):
        o_ref[...] = (a_ref[0].astype(F32) + b_ref[...].astype(F32)).astype(BF16)

    return pl.pallas_call(
        body, name=name,
        grid_spec=pltpu.PrefetchScalarGridSpec(
            num_scalar_prefetch=1, grid=(R // tr,),
            in_specs=[pl.BlockSpec((1, tr, C), lambda i, c: (c[0], i, 0)), pl.BlockSpec((tr, C), lambda i, c: (i, 0))],
            out_specs=pl.BlockSpec((tr, C), lambda i, c: (i, 0))),
        out_shape=jax.ShapeDtypeStruct(theirs.shape, BF16), compiler_params=_cparams("parallel"),
    )(c_idx, mine, theirs)


SHARDED = {
    "w_in": ((D_MODEL, IN_WIDTH), 2), "ssm_w_glu": ((BRANCH, BRANCH), 1), "conv_w": ((3, BRANCH), 2),
    "w_branch": ((3, BRANCH, D_MODEL), 3), "w_out": ((D_MODEL, D_MODEL), 1), "w_ffn_in": ((D_MODEL, 2 * FFN_HIDDEN), 2),
    "w_ffn_out": ((FFN_HIDDEN, D_MODEL), 1), "w_ple_gate": ((D_MODEL, D_MODEL), 1), "w_ple_proj": ((PLE_DIM, D_MODEL), 2),
}
SMALL = ["rel_bias", "norm_mix", "ssm_lambda_re", "ssm_lambda_im", "ssm_b_re", "ssm_b_im", "ssm_c_re", "ssm_c_im", "ssm_d",
         "ssm_log_dt", "attn_sinks", "norm_ffn", "norm_ple", "norm_final"]
SHARDED_NAMES = list(SHARDED)
GATHER_SPLIT = 4
WEIGHTS = ["rel_bias", "norm_mix", "w_in", "ssm_lambda_re", "ssm_lambda_im", "ssm_b_re", "ssm_b_im", "ssm_c_re", "ssm_c_im",
           "ssm_d", "ssm_log_dt", "ssm_w_glu", "conv_w", "attn_sinks", "w_branch", "w_out", "norm_ffn", "w_ffn_in",
           "w_ffn_out", "norm_ple", "w_ple_gate", "w_ple_proj", "norm_final"]


def _pad_to(flat, n):
    return jnp.pad(flat, [(0, 0)] * (flat.ndim - 1) + [(0, n - flat.shape[-1])])


def _unshard(g8, name):
    axis = SHARDED[name][1] - 1
    shard = g8.shape[2:]
    b = g8.reshape((2, 2, 2) + shard)
    b = jnp.moveaxis(b, (1, 2, 0), (axis, axis + 1, axis + 2))
    full = list(shard)
    full[axis] *= N_DEV
    return b.reshape(full)


def _shard_split(full, name):
    axis = SHARDED[name][1] - 1
    dims = list(full.shape)
    dims[axis:axis + 1] = [2, 2, 2, dims[axis] // N_DEV]
    b = jnp.moveaxis(full.reshape(dims), (axis, axis + 1, axis + 2), (1, 2, 0))
    return b.reshape((2, 4) + b.shape[3:])


def _small_sizes(shapes):
    return [-(-int(np.prod(shapes[n])) // 128) * 128 for n in SMALL]


def pack_small(vals, shapes, extra):
    segs = [_pad_to(vals[n].reshape(-1).astype(F32), s) for n, s in zip(SMALL, _small_sizes(shapes))]
    segs.append(_pad_to(extra.reshape(-1), 128))
    flat = jnp.concatenate(segs)
    rows = -(-flat.shape[0] // (128 * 8)) * 8
    return _pad_to(flat, rows * 128).reshape(rows, 128)


def unpack_small(packed, shapes):
    flat = packed.reshape(-1)
    out, off = {}, 0
    for n, s in zip(SMALL, _small_sizes(shapes)):
        out[n] = flat[off:off + int(np.prod(shapes[n]))].reshape(shapes[n])
        off += s
    return out, flat[off]


def _layer_weights(gathered, small, p, i):
    full = {n: _unshard(g, n) for n, g in zip(SHARDED, gathered)}
    w_in = full["w_in"]
    w_in_p = jnp.concatenate([w_in[:, 2816:], w_in[:, :2560], w_in[:, 2560:2816]], axis=-1)
    return dict(
        w_in=w_in_p,
        ssm_w_glu=full["ssm_w_glu"], ssm_w_glu_t=full["ssm_w_glu"].T,
        conv_w=full["conv_w"],
        w_branch=full["w_branch"], w_out=full["w_out"], w_ffn_in=full["w_ffn_in"], w_ffn_out=full["w_ffn_out"],
        w_ple_gate=full["w_ple_gate"],
        w_ple_proj=full["w_ple_proj"],
        norm_mix=small["norm_mix"][i][None, :], norm_ffn=small["norm_ffn"][i][None, :],
        norm_ple=small["norm_ple"][i][None, :],
        ssm_lambda_re=small["ssm_lambda_re"][i], ssm_lambda_im=small["ssm_lambda_im"][i],
        ssm_log_dt=small["ssm_log_dt"][i],
        bt_re=jnp.swapaxes(small["ssm_b_re"][i], 1, 2), bt_im=jnp.swapaxes(small["ssm_b_im"][i], 1, 2),
        ssm_c_re=small["ssm_c_re"][i], ssm_c_im=small["ssm_c_im"][i], ssm_d=small["ssm_d"][i][None, :],
        attn_sinks=small["attn_sinks"][i], p=p[i],
    )


def matrix_grads(g):
    w_in_g = g["w_in"]
    return dict(
        w_in=jnp.concatenate([w_in_g[:, 3072:5632], w_in_g[:, 5632:], w_in_g[:, :3072]], axis=-1),
        ssm_w_glu=g["ssm_w_glu"], conv_w=jnp.sum(g["conv_w"].reshape(3, 8, BRANCH), axis=1),
        w_branch=g["w_branch"], w_out=g["w_out"], w_ffn_in=g["w_ffn_in"], w_ffn_out=g["w_ffn_out"],
        w_ple_gate=g["w_ple_gate"], w_ple_proj=g["w_ple_proj"])


def small_grads(per_layer, dg_final):
    keys = ("dbias", "norm_mix", "ssm_lambda_re", "ssm_lambda_im", "bt_re", "bt_im", "ssm_c_re", "ssm_c_im", "ssm_d",
            "ssm_log_dt", "attn_sinks", "norm_ffn", "norm_ple")
    g = {k: jnp.stack([gl[k] for gl in per_layer]) for k in keys}
    drel = rel_bias_bwd(g["dbias"].reshape(DEPTH, N_Q_HEADS, BLOCK * 2 * BLOCK)).T
    return dict(
        rel_bias=drel, norm_mix=jnp.sum(g["norm_mix"], axis=1), ssm_lambda_re=g["ssm_lambda_re"],
        ssm_lambda_im=g["ssm_lambda_im"], ssm_b_re=jnp.swapaxes(g["bt_re"], 2, 3), ssm_b_im=jnp.swapaxes(g["bt_im"], 2, 3),
        ssm_c_re=g["ssm_c_re"], ssm_c_im=g["ssm_c_im"], ssm_d=g["ssm_d"][:, 0, :], ssm_log_dt=g["ssm_log_dt"],
        attn_sinks=g["attn_sinks"][:, :, 0], norm_ffn=jnp.sum(g["norm_ffn"], axis=1), norm_ple=jnp.sum(g["norm_ple"], axis=1),
        norm_final=jnp.sum(dg_final, axis=0))


def kernel(x, p, rel_bias, norm_mix, w_in, ssm_lambda_re, ssm_lambda_im, ssm_b_re, ssm_b_im, ssm_c_re, ssm_c_im, ssm_d, ssm_log_dt, ssm_w_glu, conv_w, attn_sinks, w_branch, w_out, norm_ffn, w_ffn_in, w_ffn_out, norm_ple, w_ple_gate, w_ple_proj, norm_final, loss_target, m_rel_bias, m_norm_mix, m_w_in, m_ssm_lambda_re, m_ssm_lambda_im, m_ssm_b_re, m_ssm_b_im, m_ssm_c_re, m_ssm_c_im, m_ssm_d, m_ssm_log_dt, m_ssm_w_glu, m_conv_w, m_attn_sinks, m_w_branch, m_w_out, m_norm_ffn, m_w_ffn_in, m_w_ffn_out, m_norm_ple, m_w_ple_gate, m_w_ple_proj, m_norm_final, v_rel_bias, v_norm_mix, v_w_in, v_ssm_lambda_re, v_ssm_lambda_im, v_ssm_b_re, v_ssm_b_im, v_ssm_c_re, v_ssm_c_im, v_ssm_d, v_ssm_log_dt, v_ssm_w_glu, v_conv_w, v_attn_sinks, v_w_branch, v_w_out, v_norm_ffn, v_w_ffn_in, v_w_ffn_out, v_norm_ple, v_w_ple_gate, v_w_ple_proj, v_norm_final):
    args = dict(locals())
    w = {n: args[n] for n in WEIGHTS}
    m = {n: args["m_" + n] for n in WEIGHTS}
    v = {n: args["v_" + n] for n in WEIGHTS}
    shapes = {n: w[n].shape for n in SMALL}

    x_i, y_i, c_i = _coords()
    chip = 2 * x_i + y_i
    place = (chip, c_i)
    small = {n: w[n] for n in SMALL}

    def all_gather(tag, blks):
        g4 = exchange(f"gather_{tag}_chips", "gather_chips", blks)
        g4 = [_put_slot(g, b, chip) for g, b in zip(g4, blks)]
        g8 = exchange(f"gather_{tag}_cores", "gather_cores", g4)
        return [_put_slot(g, b, c_i) for g, b in zip(g8, g4)]

    def shards_of(layer):
        return [w[n][layer] if n == "conv_w" else w[n][layer].astype(BF16) for n in SHARDED]

    bias = rel_bias_fwd(small["rel_bias"].T).reshape(N_Q_HEADS, BLOCK, 2 * BLOCK)
    xs, layers, res = x[0], [], []
    gathered = all_gather("w", shards_of(0))
    for layer in range(DEPTH):
        layers.append(_layer_weights(gathered, small, p[:, 0], layer))
        nxt = shards_of(layer + 1) if layer + 1 < DEPTH else None
        xs, res_l, gathered = layer_fwd(xs, layers[layer], bias, nxt, place)
        res.append(res_l)
    grad_x, loss_parts, dg_final = loss_head(xs, small["norm_final"][None, :], loss_target[0])

    per_layer, reduced, pending = [None] * DEPTH, [None] * DEPTH, None
    for layer in reversed(range(DEPTH)):
        grad_x, per_layer[layer], done, own_done = layer_bwd(grad_x, layers[layer], res[layer], bias, pending,
                                                             scatter_own=layer == 0)
        if pending is not None:
            reduced[layer + 1] = done
        mg = matrix_grads(per_layer[layer])
        pending = [_shard_split(mg[n], n).astype(BF16) for n in SHARDED]
    early = pending[:GATHER_SPLIT]
    sums = _pair_sums(early, exchange("scatter_g_cores", "scatter_cores", early), SHARDED_NAMES[:GATHER_SPLIT])
    received = exchange("scatter_g_chips", "scatter_chips", sums)
    reduced[0] = (list(sums) + list(own_done[0]), list(received) + list(own_done[1]))

    outs = ({}, {}, {}, {})
    for k, name in enumerate(SHARDED):
        parts = jnp.stack([_put_slot(rcv[k], lax.dynamic_index_in_dim(sm[k], chip, 0, keepdims=False), chip)
                           for sm, rcv in reduced], axis=1)
        cols = parts.shape[-1]
        res4 = adamw("adamw_" + name, parts.reshape(4, -1, cols), w[name].reshape(-1, cols), m[name].reshape(-1, cols),
                     v[name].reshape(-1, cols))
        for d, o in zip(outs, res4):
            d[name] = o.reshape(w[name].shape)

    small_local = pack_small(small_grads(per_layer, dg_final), shapes, jnp.sum(loss_parts))
    small_all = all_gather("s", [small_local])[0]
    zero = jnp.zeros((1,), F32)
    res4 = adamw("adamw_small", small_all.reshape(N_DEV, small_local.shape[0], 128),
                 pack_small({n: w[n] for n in SMALL}, shapes, zero), pack_small({n: m[n] for n in SMALL}, shapes, zero),
                 pack_small({n: v[n] for n in SMALL}, shapes, zero))
    loss = None
    for d, r in zip(outs, res4):
        vals, extra = unpack_small(r, shapes)
        d.update(vals)
        if loss is None:
            loss = extra

    return (loss, grad_x[None], *[d[n] for d in outs for n in WEIGHTS])
```
